```python
import math
import jax, jax.numpy as jnp
from jax import lax
import numpy as np

D_MODEL = 1024
BATCH = 2
SEQ = 8192
DEPTH = 4
DEC_BATCH = 32
DEC_SEQ = 4
PAST_LEN = 8192
PAGE_SIZE = 128

HEAD_DIM = 64
NSA_HEADS = 8
NSA_KV_HEADS = 2
NSA_REP = NSA_HEADS // NSA_KV_HEADS
CMP_STRIDE = 16
CMP_BLOCK = 2 * CMP_STRIDE
CMP_HIDDEN = 128
SEL_BLOCK = 64
N_SEL = 16
WINDOW = 512
Q_BLOCK = 128
SEL_BONUS = 1.0e4
GLA_HEADS = 4
GLA_DK = 32
GLA_DV = 64
GLA_GATE_RANK = 16
GLA_GATE_TEMP = 16.0
GLA_CHUNK = 64
GM_GROUPS = 4
GM_CH = 64
GM_CHUNK = 128
D_FF = 2816
ROPE_THETA = 10000.0
EPS = 1e-6

NSA_Q = NSA_HEADS * HEAD_DIM
NSA_KV = NSA_KV_HEADS * HEAD_DIM
GLA_QK = GLA_HEADS * GLA_DK
GLA_V = GLA_HEADS * GLA_DV
GM_W = GM_GROUPS * GM_CH
MIX_OUT = NSA_Q + GLA_V + GM_W
IN_SPLITS = (NSA_Q, 6 * NSA_KV, 3 * NSA_HEADS, GLA_QK, GLA_QK, GLA_V, GLA_GATE_RANK, GLA_V, GM_W, GM_W)
D_IN = sum(IN_SPLITS)

kernel_name = 'hymba_nsa_gla_gmlp_macaron_step'


def _rms(x, g):
    xf = x.astype(jnp.float32)
    y = xf * lax.rsqrt(jnp.mean(xf * xf, axis=-1, keepdims=True) + EPS)
    return (y * g.astype(jnp.float32)).astype(x.dtype)


def _layernorm(x, g, b):
    xf = x.astype(jnp.float32)
    mu = jnp.mean(xf, axis=-1, keepdims=True)
    var = jnp.mean(jnp.square(xf - mu), axis=-1, keepdims=True)
    y = (xf - mu) * lax.rsqrt(var + EPS) * g.astype(jnp.float32) + b.astype(jnp.float32)
    return y.astype(x.dtype)


def _rope(x, pos):
    half = HEAD_DIM // 2
    inv = 1.0 / (ROPE_THETA ** (jnp.arange(half, dtype=jnp.float32) * (2.0 / HEAD_DIM)))
    ang = pos.astype(jnp.float32)[:, None] * inv[None, :]
    cos = jnp.cos(ang)[:, None, :]
    sin = jnp.sin(ang)[:, None, :]
    xf = x.astype(jnp.float32)
    x1, x2 = xf[..., :half], xf[..., half:]
    return jnp.concatenate([x1 * cos - x2 * sin, x2 * cos + x1 * sin], axis=-1).astype(x.dtype)


def _masked_probs(s, mask):
    s = jnp.where(mask, s.astype(jnp.float32), -jnp.inf)
    m = jnp.max(s, axis=-1, keepdims=True)
    m = jnp.where(jnp.isfinite(m), m, 0.0)
    e = jnp.exp(s - m)
    return e / jnp.maximum(jnp.sum(e, axis=-1, keepdims=True), 1e-30)


def _swiglu(h, w_gu, w_d):
    g, u = jnp.split(h @ w_gu, 2, axis=-1)
    return (jax.nn.silu(g) * u) @ w_d


def _compress(kr, pos_emb, w1, w2):
    B, T, G, Dh = kr.shape
    n_c = (T - CMP_BLOCK) // CMP_STRIDE + 1
    ch = kr[:, :(n_c + 1) * CMP_STRIDE].reshape(B, n_c + 1, CMP_STRIDE, G, Dh)
    blk = jnp.concatenate([ch[:, :-1], ch[:, 1:]], axis=2) + pos_emb[None, None, :, None, :]
    flat = jnp.moveaxis(blk, 3, 2).reshape(B, n_c, G, CMP_BLOCK * Dh)
    return jax.nn.gelu(flat @ w1) @ w2


def _nsa(q, q_rot, gates, kv_all, win_all, pos0, win_pos0, qk_norm, cmp_pos, cmp_w1, cmp_w2):
    B, Tq = q.shape[:2]
    G, R = NSA_KV_HEADS, NSA_REP
    T_tot = kv_all.shape[1]
    scale = HEAD_DIM ** -0.5
    kc = _rms(_compress(kv_all[:, :, 0], cmp_pos[0], cmp_w1[0], cmp_w2[0]), qk_norm[1])
    vc = _compress(kv_all[:, :, 1], cmp_pos[1], cmp_w1[1], cmp_w2[1])
    n_c = kc.shape[1]
    cmp_end = jnp.arange(n_c) * CMP_STRIDE + CMP_BLOCK - 1
    n_s = -(-T_tot // SEL_BLOCK)
    sel = jnp.pad(kv_all[:, :, 2:4], ((0, 0), (0, n_s * SEL_BLOCK - T_tot), (0, 0), (0, 0), (0, 0)))
    sel = sel.reshape(B, n_s, SEL_BLOCK, 2, G, HEAD_DIM)
    ks_blk = jnp.transpose(sel[:, :, :, 0], (0, 3, 1, 2, 4))
    vs_blk = jnp.transpose(sel[:, :, :, 1], (0, 3, 1, 2, 4))
    ci = jnp.arange(n_c)[:, None]
    sj = jnp.arange(n_s)
    cover = ((ci * CMP_STRIDE <= sj[None, :] * SEL_BLOCK + SEL_BLOCK - 1)
             & (ci * CMP_STRIDE + CMP_BLOCK - 1 >= sj[None, :] * SEL_BLOCK)).astype(jnp.float32)
    n_top = min(N_SEL, n_s)
    kw = jnp.pad(win_all, ((0, 0), (WINDOW, 0), (0, 0), (0, 0), (0, 0)))
    qbs = Q_BLOCK if Tq % Q_BLOCK == 0 else Tq
    nb = Tq // qbs
    bi = jnp.arange(B)[:, None, None, None]
    gi = jnp.arange(G)[None, :, None, None]

    def to_blocks(a):
        return jnp.moveaxis(a.reshape(B, nb, qbs, *a.shape[2:]), 1, 0)

    def block(args):
        qb, qrb, gb, ib = args
        qb = qb.reshape(B, qbs, G, R, HEAD_DIM)
        qrb = qrb.reshape(B, qbs, G, R, HEAD_DIM)
        p0 = pos0 + ib * qbs
        t = p0 + jnp.arange(qbs)
        s = jnp.einsum('bqgrd,bngd->bqgrn', qb, kc) * scale
        p_c = _masked_probs(s, (cmp_end[None, :] <= t[:, None])[None, :, None, None, :])
        o_c = jnp.einsum('bqgrn,bngd->bqgrd', p_c, vc)
        imp = jnp.einsum('bqgn,ns->bqgs', p_c.sum(axis=3), cover)
        bt = t // SEL_BLOCK
        allowed = sj[None, :] * SEL_BLOCK <= t[:, None]
        forced = (sj[None, :] == 0) | (sj[None, :] == bt[:, None]) | (sj[None, :] == bt[:, None] - 1)
        score = jnp.where(allowed[None, :, None, :], imp + jnp.where(forced, SEL_BONUS, 0.0)[None, :, None, :], -jnp.inf)
        top_v, top_i = lax.top_k(score, n_top)
        idx = jnp.transpose(top_i, (0, 2, 1, 3))
        valid = jnp.isfinite(jnp.transpose(top_v, (0, 2, 1, 3)))
        kg = ks_blk[bi, gi, idx]
        vg = vs_blk[bi, gi, idx]
        s = jnp.einsum('bqgrd,bgqksd->bqgrks', qrb, kg) * scale
        kpos = idx[..., None] * SEL_BLOCK + jnp.arange(SEL_BLOCK)
        msk = valid[..., None] & (kpos <= t[None, None, :, None, None])
        msk = jnp.transpose(msk, (0, 2, 1, 3, 4))[:, :, :, None]
        p_s = _masked_probs(s.reshape(*s.shape[:4], -1), msk.reshape(*msk.shape[:4], -1)).reshape(s.shape)
        o_s = jnp.einsum('bqgrks,bgqksd->bqgrd', p_s, vg)
        kwb = lax.dynamic_slice_in_dim(kw, p0 - win_pos0, WINDOW + qbs, axis=1)
        s = jnp.einsum('bqgrd,bkgd->bqgrk', qrb, kwb[:, :, 0]) * scale
        apos = p0 - WINDOW + jnp.arange(WINDOW + qbs)
        diff = t[:, None] - apos[None, :]
        wm = (apos[None, :] >= win_pos0) & (diff >= 0) & (diff < WINDOW)
        p_w = _masked_probs(s, wm[None, :, None, None, :])
        o_w = jnp.einsum('bqgrk,bkgd->bqgrd', p_w, kwb[:, :, 1])
        o = jnp.stack([o_c, o_s, o_w], axis=-1)
        o = jnp.sum(o * gb.reshape(B, qbs, G, R, 1, 3), axis=-1)
        return o.reshape(B, qbs, NSA_HEADS, HEAD_DIM)

    out = lax.map(block, (to_blocks(q), to_blocks(q_rot), to_blocks(gates), jnp.arange(nb)))
    return jnp.moveaxis(out, 0, 1).reshape(B, Tq, NSA_Q)


def _gla(q, k, v, log_a, S0):
    B, T, H, DK = q.shape
    C = GLA_CHUNK if T % GLA_CHUNK == 0 else T
    n = T // C
    tril = jnp.tril(jnp.ones((C, C), dtype=bool))

    def chunks(a):
        return jnp.moveaxis(a.reshape(B, n, C, *a.shape[2:]), 1, 0)

    def step(S, inp):
        qc, kc, vc, gc = inp
        b = jnp.cumsum(gc, axis=1)
        inter = jnp.einsum('bthk,bhkv->bthv', qc * jnp.exp(b), S)
        d = jnp.where(tril[None, :, :, None, None], b[:, :, None] - b[:, None, :], -jnp.inf)
        att = jnp.einsum('bthk,bshk,btshk->bths', qc, kc, jnp.exp(d))
        o = inter + jnp.einsum('bths,bshv->bthv', att, vc)
        bl = b[:, -1]
        S = jnp.exp(bl)[..., None] * S + jnp.einsum('bshk,bshv->bhkv', kc * jnp.exp(bl[:, None] - b), vc)
        return S, o

    S_T, o = lax.scan(step, S0.astype(jnp.float32), (chunks(q), chunks(k), chunks(v), chunks(log_a)))
    return jnp.moveaxis(o, 0, 1).reshape(B, T, H, v.shape[-1]), S_T


def _spatial(vn, ws, bs):
    B, T, G, C = vn.shape
    n = -(-T // GM_CHUNK)
    vp = jnp.pad(vn, ((0, 0), (0, n * GM_CHUNK - T), (0, 0), (0, 0))).reshape(B, n, GM_CHUNK, G, C)
    wm = ws * jnp.tril(jnp.ones((GM_CHUNK, GM_CHUNK), dtype=ws.dtype))
    z = jnp.einsum('gts,bnsgc->bntgc', wm, vp) + bs.T[None, None, :, :, None]
    return z.reshape(B, n * GM_CHUNK, G, C)[:, :T]


def _mixer(h, pos0, past_kv, past_win, S0, w_in, w_out, qk_norm, cmp_pos, cmp_w1, cmp_w2,
           gla_gw, gla_gb, gla_norm, gm_ln, gm_ws, gm_b):
    B, T, _ = h.shape
    cuts = [int(c) for c in np.cumsum(IN_SPLITS)[:-1]]
    q_a, kv_a, gate_a, q_b, k_b, v_b, lr_b, r_b, u_c, v_c = jnp.split(h @ w_in, cuts, axis=-1)
    pos = pos0 + jnp.arange(T)
    q = _rms(q_a.reshape(B, T, NSA_HEADS, HEAD_DIM), qk_norm[0])
    kv = kv_a.reshape(B, T, 6, NSA_KV_HEADS, HEAD_DIM)
    k_sel = _rope(_rms(kv[:, :, 2], qk_norm[2]), pos)
    k_win = _rope(_rms(kv[:, :, 4], qk_norm[3]), pos)
    new_kv = jnp.stack([kv[:, :, 0], kv[:, :, 1], k_sel, kv[:, :, 3]], axis=2)
    new_win = jnp.stack([k_win, kv[:, :, 5]], axis=2)
    if past_kv is None:
        kv_all, win_all, win_pos0, n_keep = new_kv, new_win, 0, min(WINDOW, T)
    else:
        kv_all = jnp.concatenate([past_kv, new_kv], axis=1)
        win_all = jnp.concatenate([past_win, new_win], axis=1)
        win_pos0, n_keep = pos0 - past_win.shape[1], past_win.shape[1]
    gates = jax.nn.sigmoid(gate_a.astype(jnp.float32)).reshape(B, T, NSA_HEADS, 3)
    o_a = _nsa(q, _rope(q, pos), gates, kv_all, win_all, pos0, win_pos0, qk_norm, cmp_pos, cmp_w1, cmp_w2)
    win_state = win_all[:, win_all.shape[1] - n_keep:]
    qg = q_b.reshape(B, T, GLA_HEADS, GLA_DK) * (GLA_DK ** -0.5)
    kg = k_b.reshape(B, T, GLA_HEADS, GLA_DK)
    vg = v_b.reshape(B, T, GLA_HEADS, GLA_DV)
    logit = (lr_b @ gla_gw + gla_gb).astype(jnp.float32)
    log_a = (jax.nn.log_sigmoid(logit) / GLA_GATE_TEMP).reshape(B, T, GLA_HEADS, GLA_DK)
    if S0 is None:
        S0 = jnp.zeros((B, GLA_HEADS, GLA_DK, GLA_DV), jnp.float32)
    o_b, S_T = _gla(qg, kg, vg, log_a, S0)
    o_b = _rms(o_b, gla_norm).reshape(B, T, GLA_V) * jax.nn.silu(r_b.astype(jnp.float32))
    u = jax.nn.gelu(u_c)
    vn = _layernorm(jax.nn.gelu(v_c), gm_ln[0], gm_ln[1]).reshape(B, T, GM_GROUPS, GM_CH)
    o_c = u * _spatial(vn, gm_ws, gm_b).reshape(B, T, GM_W)
    y = jnp.concatenate([o_a.astype(h.dtype), o_b.astype(h.dtype), o_c.astype(h.dtype)], axis=-1) @ w_out
    return y, new_kv, win_state, S_T, vn


def _layer(x, pos0, past_kv, past_win, S0, ln, f_gu, f_d, *mix_w):
    x = x + 0.5 * _swiglu(_rms(x, ln[0]), f_gu[0], f_d[0])
    y, new_kv, win_state, S_T, vn = _mixer(_rms(x, ln[1]), pos0, past_kv, past_win, S0, *mix_w)
    x = x + y
    x = x + 0.5 * _swiglu(_rms(x, ln[2]), f_gu[1], f_d[1])
    return x, new_kv, win_state, S_T, vn


def setup_inputs(seed: int = 0) -> dict:
    key = jax.random.key(seed)
    ks = jax.random.split(key, 24)
    f32 = jnp.float32
    n_pages = PAST_LEN // PAGE_SIZE
    n_used = DEC_BATCH * n_pages
    n_phys = n_used + max(1, n_used // 4)
    w_buf = min(WINDOW, PAST_LEN)
    page_table = jax.random.permutation(ks[0], n_phys)[:n_used].reshape(DEC_BATCH, n_pages).astype(jnp.int32)
    nrm = lambda k, s, sc: sc * jax.random.normal(k, s, f32)
    return {
        'x_prompt': nrm(ks[1], (BATCH, SEQ, D_MODEL), 1.0),
        'x_sample': nrm(ks[2], (DEC_BATCH, DEC_SEQ, D_MODEL), 1.0),
        'cache_kv': nrm(ks[3], (DEPTH, n_phys, PAGE_SIZE, 4, NSA_KV_HEADS, HEAD_DIM), 1.0),
        'cache_win_kv': nrm(ks[4], (DEPTH, DEC_BATCH, w_buf, 2, NSA_KV_HEADS, HEAD_DIM), 1.0),
        'state_gla': nrm(ks[5], (DEPTH, DEC_BATCH, GLA_HEADS, GLA_DK, GLA_DV), 0.3),
        'page_table': page_table,
        'ln_gains': 1.0 + nrm(ks[6], (DEPTH, 3, D_MODEL), 0.02),
        'ffn_w_gate_up': nrm(ks[7], (DEPTH, 2, D_MODEL, 2 * D_FF), D_MODEL ** -0.5),
        'ffn_w_down': nrm(ks[8], (DEPTH, 2, D_FF, D_MODEL), D_FF ** -0.5),
        'w_in': nrm(ks[9], (DEPTH, D_MODEL, D_IN), D_MODEL ** -0.5),
        'w_out': nrm(ks[10], (DEPTH, MIX_OUT, D_MODEL), MIX_OUT ** -0.5),
        'nsa_qk_norm': 1.0 + nrm(ks[11], (DEPTH, 4, HEAD_DIM), 0.02),
        'nsa_cmp_pos': nrm(ks[12], (DEPTH, 2, CMP_BLOCK, HEAD_DIM), 0.1),
        'nsa_cmp_w1': nrm(ks[13], (DEPTH, 2, CMP_BLOCK * HEAD_DIM, CMP_HIDDEN), (CMP_BLOCK * HEAD_DIM) ** -0.5),
        'nsa_cmp_w2': nrm(ks[14], (DEPTH, 2, CMP_HIDDEN, HEAD_DIM), CMP_HIDDEN ** -0.5),
        'gla_gate_w': nrm(ks[15], (DEPTH, GLA_GATE_RANK, GLA_QK), GLA_GATE_RANK ** -0.5),
        'gla_gate_b': nrm(ks[16], (DEPTH, GLA_QK), 0.02),
        'gla_norm': 1.0 + nrm(ks[17], (DEPTH, GLA_DV), 0.02),
        'gm_ln': jnp.stack([1.0 + nrm(ks[18], (DEPTH, GM_W), 0.02), nrm(ks[19], (DEPTH, GM_W), 0.02)], axis=1),
        'gm_ws': nrm(ks[20], (DEPTH, GM_GROUPS, GM_CHUNK, GM_CHUNK), GM_CHUNK ** -0.5),
        'gm_b': 1.0 + nrm(ks[21], (DEPTH, GM_GROUPS, GM_CHUNK), 0.02),
    }


def reference(x_prompt, x_sample, cache_kv, cache_win_kv, state_gla, page_table, ln_gains, ffn_w_gate_up,
              ffn_w_down, w_in, w_out, nsa_qk_norm, nsa_cmp_pos, nsa_cmp_w1, nsa_cmp_w2, gla_gate_w, gla_gate_b,
              gla_norm, gm_ln, gm_ws, gm_b):
    xp, xs = x_prompt, x_sample
    nb_dec, n_pages = page_table.shape
    kv_p, win_p, gla_p, kv_s, win_s, gla_s, gmv_s = [], [], [], [], [], [], []
    for l in range(DEPTH):
        mix_w = (w_in[l], w_out[l], nsa_qk_norm[l], nsa_cmp_pos[l], nsa_cmp_w1[l], nsa_cmp_w2[l],
                 gla_gate_w[l], gla_gate_b[l], gla_norm[l], gm_ln[l], gm_ws[l], gm_b[l])
        xp, nkv, nwin, ns, _ = _layer(xp, 0, None, None, None, ln_gains[l], ffn_w_gate_up[l], ffn_w_down[l], *mix_w)
        kv_p.append(nkv); win_p.append(nwin); gla_p.append(ns)
        past = cache_kv[l][page_table].reshape(nb_dec, n_pages * PAGE_SIZE, 4, NSA_KV_HEADS, HEAD_DIM)
        xs, nkv, nwin, ns, vn = _layer(xs, past.shape[1], past, cache_win_kv[l], state_gla[l], ln_gains[l],
                                       ffn_w_gate_up[l], ffn_w_down[l], *mix_w)
        kv_s.append(nkv); win_s.append(nwin); gla_s.append(ns); gmv_s.append(vn)
    return (xp, xs, jnp.stack(kv_p), jnp.stack(win_p), jnp.stack(gla_p), jnp.stack(kv_s), jnp.stack(win_s),
            jnp.stack(gla_s), jnp.stack(gmv_s))
```

```python
import functools
import math

import numpy as np
import jax
import jax.numpy as jnp
from jax import lax
from jax.experimental import pallas as pl
from jax.experimental.pallas import tpu as pltpu

F32 = jnp.float32
BF16 = jnp.bfloat16

D_MODEL = 1024
HEAD_DIM = 64
NSA_HEADS = 8
NSA_KV_HEADS = 2
NSA_REP = NSA_HEADS // NSA_KV_HEADS
CMP_STRIDE = 16
CMP_BLOCK = 2 * CMP_STRIDE
CMP_HIDDEN = 128
SEL_BLOCK = 64
N_SEL = 16
WINDOW = 512
Q_BLOCK = 128
SEL_BONUS = 1.0e4
GLA_HEADS = 4
GLA_DK = 32
GLA_DV = 64
GLA_GATE_RANK = 16
GLA_GATE_TEMP = 16.0
GLA_CHUNK = 64
GM_GROUPS = 4
GM_CH = 64
GM_CHUNK = 128
D_FF = 2816
ROPE_THETA = 10000.0
EPS = 1e-6

NSA_Q = NSA_HEADS * HEAD_DIM
NSA_KV = NSA_KV_HEADS * HEAD_DIM
GLA_QK = GLA_HEADS * GLA_DK
GLA_V = GLA_HEADS * GLA_DV
GM_W = GM_GROUPS * GM_CH
MIX_OUT = NSA_Q + GLA_V + GM_W
IN_SPLITS = (NSA_Q, 6 * NSA_KV, 3 * NSA_HEADS, GLA_QK, GLA_QK, GLA_V, GLA_GATE_RANK, GLA_V, GM_W, GM_W)
IN_PADDED = tuple(-(-s // 128) * 128 for s in IN_SPLITS)
IN_OFFS = tuple(int(v) for v in np.cumsum((0,) + IN_PADDED))
D_IN_PAD = IN_OFFS[-1]

LANES = 128
NEG_BIG = -1.0e30
VMEM_LIMIT = 56 * 1024 * 1024


def _cparams(sem):
    return pltpu.CompilerParams(dimension_semantics=sem, vmem_limit_bytes=VMEM_LIMIT)


def _gelu(x):
    c = np.float32(np.sqrt(2.0 / np.pi))
    return x * (0.5 * (1.0 + jnp.tanh(c * (x + 0.044715 * (x * x * x)))))


def _sigmoid(x):
    return 1.0 / (1.0 + jnp.exp(-x))


def _dot(a, b):
    return jnp.dot(a, b, preferred_element_type=F32)


def _dot_nt(a, b):
    return lax.dot_general(a, b, (((1,), (1,)), ((), ())), preferred_element_type=F32)


def _seg_mean_sq(x, sm):
    sq = x * x
    hi = sq.astype(BF16)
    lo = (sq - hi.astype(F32)).astype(BF16)
    outs = []
    for c in range(x.shape[1] // LANES):
        sl = slice(c * LANES, (c + 1) * LANES)
        outs.append(_dot(hi[:, sl], sm) + _dot(lo[:, sl], sm))
    return outs[0] if len(outs) == 1 else jnp.concatenate(outs, axis=1)


def _seg_rms(x, gain, sm):
    return x * lax.rsqrt(_seg_mean_sq(x, sm) + EPS) * gain


def _tile_lanes(a, w):
    n = w // a.shape[1]
    return a if n == 1 else jnp.concatenate([a] * n, axis=1)


def _rope(x, cos, sin_signed):
    w = x.shape[1]
    lane = lax.broadcasted_iota(jnp.int32, x.shape, 1)
    fwd = pltpu.roll(x, w - HEAD_DIM // 2, axis=1)
    bwd = pltpu.roll(x, HEAD_DIM // 2, axis=1)
    partner = jnp.where((lane % HEAD_DIM) < HEAD_DIM // 2, fwd, bwd)
    return x * _tile_lanes(cos, w) + partner * _tile_lanes(sin_signed, w)


def _ffn_kernel(x_ref, g_ref, wg_ref, wu_ref, wd_ref, o_ref, h_scr, acc_scr, *, nj):
    j = pl.program_id(1)

    @pl.when(j == 0)
    def _():
        x = x_ref[...]
        ms = jnp.mean(x * x, axis=-1, keepdims=True)
        h_scr[...] = (x * lax.rsqrt(ms + EPS) * g_ref[...]).astype(BF16)
        acc_scr[...] = jnp.zeros_like(acc_scr)

    h = h_scr[...]
    g = _dot(h, wg_ref[...])
    u = _dot(h, wu_ref[...])
    a = (g * _sigmoid(g)) * u
    acc_scr[...] += _dot(a.astype(BF16), wd_ref[...])

    @pl.when(j == nj - 1)
    def _():
        o_ref[...] = x_ref[...] + 0.5 * acc_scr[...]


def _ffn(x, gain, w_gu, w_d, li, *, tm, tf):
    m = x.shape[0]
    nj = D_FF // tf
    return pl.pallas_call(
        functools.partial(_ffn_kernel, nj=nj),
        out_shape=jax.ShapeDtypeStruct((m, D_MODEL), F32),
        grid=(m // tm, nj),
        in_specs=[
            pl.BlockSpec((tm, D_MODEL), lambda i, j: (i, 0)),
            pl.BlockSpec((1, D_MODEL), lambda i, j: (0, 0)),
            pl.BlockSpec((None, D_MODEL, tf), lambda i, j: (li, 0, j)),
            pl.BlockSpec((None, D_MODEL, tf), lambda i, j: (li, 0, j + nj)),
            pl.BlockSpec((None, tf, D_MODEL), lambda i, j: (li, j, 0)),
        ],
        out_specs=pl.BlockSpec((tm, D_MODEL), lambda i, j: (i, 0)),
        scratch_shapes=[pltpu.VMEM((tm, D_MODEL), BF16), pltpu.VMEM((tm, D_MODEL), F32)],
        compiler_params=_cparams(("parallel", "arbitrary")),
        name="ffn",
    )(x, gain, w_gu, w_gu, w_d)


def _group_padded(arr, h):
    c = arr[:, (h // 2) * LANES:(h // 2 + 1) * LANES]
    g = h // NSA_REP
    if (h % 2) != g:
        c = pltpu.roll(c, HEAD_DIM, axis=1)
    lane = lax.broadcasted_iota(jnp.int32, c.shape, 1)
    keep = (lane >= g * HEAD_DIM) & (lane < (g + 1) * HEAD_DIM)
    return jnp.where(keep, c, 0.0)


def _inproj_kernel(x_ref, ln_ref, w_ref, gq_ref, gks_ref, gkw_ref, cos_ref, sin_ref, gw_ref, gb_ref,
                   lng_ref, lnb_ref, sm_ref, *outs, tm, attn_layout):
    (newkv_ref, newwin_ref, gates_ref, qg_ref, kg_ref, la_ref, vg_ref, rs_ref, u_ref, vn_ref) = outs[:10]
    x = x_ref[...]
    ms = jnp.mean(x * x, axis=-1, keepdims=True)
    h = (x * lax.rsqrt(ms + EPS) * ln_ref[...]).astype(BF16)
    p = _dot(h, w_ref[...])
    sm = sm_ref[...]
    cos = cos_ref[...]
    sin = sin_ref[...]
    o = IN_OFFS

    def seg(i, a=0, b=None):
        b = IN_PADDED[i] if b is None else b
        return p[:, o[i] + a:o[i] + b]

    qn = _seg_rms(seg(0), gq_ref[...], sm)
    qr = _rope(qn, cos, sin)
    kv = [seg(1, LANES * j, LANES * (j + 1)) for j in range(6)]
    ksel = _rope(_seg_rms(kv[2], gks_ref[...], sm), cos, sin)
    kwin = _rope(_seg_rms(kv[4], gkw_ref[...], sm), cos, sin)
    newkv_ref[:, 0:LANES] = kv[0]
    newkv_ref[:, LANES:2 * LANES] = kv[1]
    newkv_ref[:, 2 * LANES:3 * LANES] = ksel
    newkv_ref[:, 3 * LANES:4 * LANES] = kv[3]
    newwin_ref[:, 0:LANES] = kwin
    newwin_ref[:, LANES:2 * LANES] = kv[5]
    gates = _sigmoid(seg(2))
    gates_ref[...] = gates
    qg_ref[...] = seg(3) * np.float32(GLA_DK ** -0.5)
    kg_ref[...] = seg(4)
    vg_ref[...] = seg(5)
    logit = _dot(seg(6).astype(BF16), gw_ref[...]) + gb_ref[...]
    log_sig = jnp.minimum(logit, 0.0) - jnp.log1p(jnp.exp(-jnp.abs(logit)))
    la_ref[...] = log_sig * np.float32(1.0 / GLA_GATE_TEMP)
    r = seg(7)
    rs_ref[...] = r * _sigmoid(r)
    u_ref[...] = _gelu(seg(8))
    v = _gelu(seg(9))
    mu = jnp.mean(v, axis=-1, keepdims=True)
    var = jnp.mean(jnp.square(v - mu), axis=-1, keepdims=True)
    vn_ref[...] = (v - mu) * lax.rsqrt(var + EPS) * lng_ref[...] + lnb_ref[...]

    if attn_layout:
        (qs_ref, qrs_ref, gt_ref, kselr_ref, vselt_ref, kwinr_ref, vwint_ref, kcmp_ref, vcmp_ref) = outs[10:]
        scale = np.float32(HEAD_DIM ** -0.5)
        qs = qn * scale
        qrs = qr * scale
        for hh in range(NSA_HEADS):
            a = _group_padded(qs, hh).astype(BF16)
            b = _group_padded(qrs, hh).astype(BF16)
            for rb in range(tm // Q_BLOCK):
                qs_ref[rb, hh] = a[rb * Q_BLOCK:(rb + 1) * Q_BLOCK]
                qrs_ref[rb, hh] = b[rb * Q_BLOCK:(rb + 1) * Q_BLOCK]
        for rb in range(tm // Q_BLOCK):
            gt_ref[rb] = gates[rb * Q_BLOCK:(rb + 1) * Q_BLOCK].T
        kselr_ref[...] = ksel.astype(BF16)
        vselt_ref[...] = kv[3].T.astype(BF16)
        kwinr_ref[...] = kwin.astype(BF16)
        vwint_ref[...] = kv[5].T.astype(BF16)
        kcmp_ref[...] = kv[0]
        vcmp_ref[...] = kv[1]
    else:
        qn_ref, qr_ref = outs[10:]
        qn_ref[...] = qn
        qr_ref[...] = qr


def _inproj(x, lw, cos_t, sin_t, *, tm, attn_layout, batch, seq):
    m = x.shape[0]
    nt = m // tm
    ntab = cos_t.shape[0] // tm
    row = lambda w: pl.BlockSpec((tm, w), lambda i: (i, 0))
    full = lambda a: pl.BlockSpec(a.shape, lambda i: (0,) * a.ndim)
    ins = [x, lw["ln1"], lw["w_in"], lw["gq"], lw["gks"], lw["gkw"], cos_t, sin_t, lw["gla_gw"], lw["gla_gb"],
           lw["gm_lng"], lw["gm_lnb"], lw["sm"]]
    in_specs = [row(D_MODEL), full(lw["ln1"]), full(lw["w_in"]), full(lw["gq"]), full(lw["gks"]), full(lw["gkw"]),
                pl.BlockSpec((tm, LANES), lambda i: (i % ntab, 0)), pl.BlockSpec((tm, LANES), lambda i: (i % ntab, 0)),
                full(lw["gla_gw"]), full(lw["gla_gb"]), full(lw["gm_lng"]), full(lw["gm_lnb"]), full(lw["sm"])]
    widths = [512, 256, 128, 128, 128, 128, 256, 256, 256, 256]
    out_shape = [jax.ShapeDtypeStruct((m, w), F32) for w in widths]
    out_specs = [row(w) for w in widths]
    if attn_layout:
        nqb = m // Q_BLOCK
        rpb = tm // Q_BLOCK
        tpb = seq // tm
        out_shape += [jax.ShapeDtypeStruct((nqb, NSA_HEADS, Q_BLOCK, LANES), BF16)] * 2
        out_specs += [pl.BlockSpec((rpb, NSA_HEADS, Q_BLOCK, LANES), lambda i: (i, 0, 0, 0))] * 2
        out_shape += [jax.ShapeDtypeStruct((nqb, LANES, Q_BLOCK), F32)]
        out_specs += [pl.BlockSpec((rpb, LANES, Q_BLOCK), lambda i: (i, 0, 0))]
        rowmaj = (jax.ShapeDtypeStruct((m, LANES), BF16), row(LANES))
        trans = (jax.ShapeDtypeStruct((batch, LANES, seq), BF16),
                 pl.BlockSpec((None, LANES, tm), lambda i: (i // tpb, 0, i % tpb)))
        for sh, sp in (rowmaj, trans, rowmaj, trans):
            out_shape.append(sh)
            out_specs.append(sp)
        out_shape += [jax.ShapeDtypeStruct((m, LANES), F32)] * 2
        out_specs += [row(LANES)] * 2
    else:
        out_shape += [jax.ShapeDtypeStruct((m, NSA_Q), F32)] * 2
        out_specs += [row(NSA_Q)] * 2
    return pl.pallas_call(
        functools.partial(_inproj_kernel, tm=tm, attn_layout=attn_layout),
        out_shape=out_shape,
        grid=(nt,),
        in_specs=in_specs,
        out_specs=out_specs,
        compiler_params=_cparams(("parallel",)),
        name="inproj",
    )(*ins)


def _compress_kernel(zk_ref, zkn_ref, zv_ref, zvn_ref, pos_ref, wak_ref, wbk_ref, wav_ref, wbv_ref,
                     w2k_ref, w2v_ref, gkc_ref, sm_ref, kc_ref, vct_ref):
    pos = pos_ref[...]

    def one(z_ref, zn_ref, pa, pb, wa_ref, wb_ref, w2_ref):
        a = _dot((z_ref[...] + pa).astype(BF16), wa_ref[...])
        b = _dot((zn_ref[...] + pb).astype(BF16), wb_ref[...])
        return _dot(_gelu(a + b).astype(BF16), w2_ref[...])

    ck = one(zk_ref, zkn_ref, pos[0:1], pos[1:2], wak_ref, wbk_ref, w2k_ref)
    cv = one(zv_ref, zvn_ref, pos[2:3], pos[3:4], wav_ref, wbv_ref, w2v_ref)
    kc_ref[...] = _seg_rms(ck, gkc_ref[...], sm_ref[...]).astype(BF16)
    vct_ref[...] = cv.T.astype(BF16)


def _compress(zk, zkn, zv, zvn, lw, *, rows):
    b, nc, kdim = zk.shape
    zspec = pl.BlockSpec((None, rows, kdim), lambda i, j: (i, j, 0))
    full = lambda a: pl.BlockSpec(a.shape, lambda i, j: (0,) * a.ndim)
    ws = [lw["cmp_pos"], lw["cmp_wak"], lw["cmp_wbk"], lw["cmp_wav"], lw["cmp_wbv"], lw["cmp_w2k"], lw["cmp_w2v"],
          lw["gkc"], lw["sm"]]
    return pl.pallas_call(
        _compress_kernel,
        out_shape=[jax.ShapeDtypeStruct((b, nc, LANES), BF16), jax.ShapeDtypeStruct((b, LANES, nc), BF16)],
        grid=(b, nc // rows),
        in_specs=[zspec] * 4 + [full(w) for w in ws],
        out_specs=[pl.BlockSpec((None, rows, LANES), lambda i, j: (i, j, 0)),
                   pl.BlockSpec((None, LANES, rows), lambda i, j: (i, 0, j))],
        compiler_params=_cparams(("parallel", "parallel")),
        name="nsa_compress",
    )(zk, zkn, zv, zvn, *ws)


def _nsa_kernel(qs_ref, qrs_ref, gt_ref, kc_ref, vct_ref, cov_ref, ksel_ref, vselt_ref, kwin_ref, vwint_ref,
                o_ref, score_scr, sel_scr, *, nc, ns, kb_keys):
    ib = pl.program_id(1)
    p0 = ib * Q_BLOCK
    nl = NSA_REP * Q_BLOCK
    t_row = p0 + lax.broadcasted_iota(jnp.int32, (1, nl), 1) % Q_BLOCK
    t_q = p0 + lax.broadcasted_iota(jnp.int32, (1, Q_BLOCK), 1)
    gt = gt_ref[...]
    n_top = min(N_SEL, ns)
    per_kb = kb_keys // SEL_BLOCK

    def flash_step(carry, s, mask, vt):
        m, l, acc = carry
        sm_ = jnp.where(mask, s, NEG_BIG)
        m_new = jnp.maximum(m, jnp.max(sm_, axis=0, keepdims=True))
        p = jnp.where(mask, jnp.exp(sm_ - m_new), 0.0)
        alpha = jnp.exp(m - m_new)
        l = alpha * l + jnp.sum(p, axis=0, keepdims=True)
        acc = alpha * acc + _dot(vt, p.astype(BF16))
        return m_new, l, acc

    init = (jnp.full((1, nl), NEG_BIG, F32), jnp.zeros((1, nl), F32), jnp.zeros((HEAD_DIM, nl), F32))

    for g in range(NSA_KV_HEADS):
        rows = slice(g * HEAD_DIM, (g + 1) * HEAD_DIM)
        q4 = qs_ref[NSA_REP * g:NSA_REP * (g + 1)].reshape(nl, LANES)
        q4r = qrs_ref[NSA_REP * g:NSA_REP * (g + 1)].reshape(nl, LANES)

        s = _dot_nt(kc_ref[...], q4)
        ci = lax.broadcasted_iota(jnp.int32, (nc, 1), 0)
        cmask = (ci * CMP_STRIDE + (CMP_BLOCK - 1) <= t_row) & (ci < nc - 1)
        s = jnp.where(cmask, s, -jnp.inf)
        m = jnp.max(s, axis=0, keepdims=True)
        m = jnp.where(m == -jnp.inf, 0.0, m)
        e = jnp.exp(s - m)
        l = jnp.sum(e, axis=0, keepdims=True)
        p = e * (1.0 / jnp.maximum(l, 1e-30))
        o_c = _dot(vct_ref[rows, :], p.astype(BF16))

        psum = p[:, 0:Q_BLOCK]
        for r in range(1, NSA_REP):
            psum = psum + p[:, r * Q_BLOCK:(r + 1) * Q_BLOCK]
        imp = _dot(cov_ref[...], psum.astype(BF16))
        sj = lax.broadcasted_iota(jnp.int32, (ns, 1), 0)
        bt = t_q // SEL_BLOCK
        allowed = sj * SEL_BLOCK <= t_q
        forced = (sj == 0) | (sj == bt) | (sj == bt - 1)
        score = jnp.where(allowed, imp + jnp.where(forced, np.float32(SEL_BONUS), 0.0), -jnp.inf)
        score_scr[...] = score

        def rank_body(k, rank):
            row = score_scr[pl.ds(k, 1), :]
            ahead = (row > score) | ((row == score) & (sj > k))
            return rank + jnp.where(ahead, 1, 0)

        n_live = jnp.minimum((p0 + Q_BLOCK + SEL_BLOCK - 1) // SEL_BLOCK, ns)
        rank = lax.fori_loop(0, n_live, rank_body, jnp.zeros((ns, Q_BLOCK), jnp.int32))
        sel_scr[...] = jnp.where((rank < n_top) & (score > -jnp.inf), 1.0, 0.0)

        def sel_body(kb, carry):
            k0 = pl.multiple_of(kb * kb_keys, kb_keys)
            s = _dot_nt(ksel_ref[pl.ds(k0, kb_keys), :], q4r)
            kpos = k0 + lax.broadcasted_iota(jnp.int32, (kb_keys, 1), 0)
            parts = [jnp.broadcast_to(sel_scr[pl.ds(kb * per_kb + i, 1), :], (SEL_BLOCK, Q_BLOCK))
                     for i in range(per_kb)]
            selm = jnp.concatenate(parts, axis=0)
            selm = jnp.concatenate([selm] * NSA_REP, axis=1)
            mask = (selm > 0.0) & (kpos <= t_row)
            return flash_step(carry, s, mask, vselt_ref[rows, pl.ds(k0, kb_keys)])

        n_kb = (p0 + Q_BLOCK + kb_keys - 1) // kb_keys
        _, l_s, acc_s = lax.fori_loop(0, n_kb, sel_body, init)
        o_s = acc_s * (1.0 / jnp.maximum(l_s, 1e-30))

        def win_body(wb, carry):
            k0 = pl.multiple_of(wb * Q_BLOCK, Q_BLOCK)
            s = _dot_nt(kwin_ref[pl.ds(k0, Q_BLOCK), :], q4r)
            kpos = k0 + lax.broadcasted_iota(jnp.int32, (Q_BLOCK, 1), 0)
            diff = t_row - kpos
            mask = (diff >= 0) & (diff < WINDOW)
            return flash_step(carry, s, mask, vwint_ref[rows, pl.ds(k0, Q_BLOCK)])

        w_lo = jnp.maximum(ib - WINDOW // Q_BLOCK, 0)
        _, l_w, acc_w = lax.fori_loop(w_lo, ib + 1, win_body, init)
        o_w = acc_w * (1.0 / jnp.maximum(l_w, 1e-30))

        def gate_row(jb):
            return jnp.concatenate(
                [gt[(NSA_REP * g + r) * 3 + jb:(NSA_REP * g + r) * 3 + jb + 1, :] for r in range(NSA_REP)], axis=1)

        o = o_c * gate_row(0) + o_s * gate_row(1) + o_w * gate_row(2)
        for pr in range(NSA_REP // 2):
            blk = jnp.concatenate([o[:, (2 * pr) * Q_BLOCK:(2 * pr + 1) * Q_BLOCK],
                                   o[:, (2 * pr + 1) * Q_BLOCK:(2 * pr + 2) * Q_BLOCK]], axis=0)
            c0 = g * NSA_REP * HEAD_DIM + pr * LANES
            o_ref[:, c0:c0 + LANES] = blk.T


def _nsa_prompt(qs, qrs, gt, kc, vct, cov, ksel_r, vsel_t, kwin_r, vwin_t, *, batch, seq):
    nb = seq // Q_BLOCK
    nc = kc.shape[1]
    ns = cov.shape[0]
    kb_keys = min(256, seq)
    per_b3 = lambda a: pl.BlockSpec((None,) + a.shape[1:], lambda b, i: (b, 0, 0))
    return pl.pallas_call(
        functools.partial(_nsa_kernel, nc=nc, ns=ns, kb_keys=kb_keys),
        out_shape=jax.ShapeDtypeStruct((batch * seq, NSA_Q), F32),
        grid=(batch, nb),
        in_specs=[
            pl.BlockSpec((None, NSA_HEADS, Q_BLOCK, LANES), lambda b, i: (b * nb + i, 0, 0, 0)),
            pl.BlockSpec((None, NSA_HEADS, Q_BLOCK, LANES), lambda b, i: (b * nb + i, 0, 0, 0)),
            pl.BlockSpec((None, LANES, Q_BLOCK), lambda b, i: (b * nb + i, 0, 0)),
            per_b3(kc), per_b3(vct),
            pl.BlockSpec(cov.shape, lambda b, i: (0, 0)),
            per_b3(ksel_r), per_b3(vsel_t), per_b3(kwin_r), per_b3(vwin_t),
        ],
        out_specs=pl.BlockSpec((Q_BLOCK, NSA_Q), lambda b, i: (b * nb + i, 0)),
        scratch_shapes=[pltpu.VMEM((ns, Q_BLOCK), F32), pltpu.VMEM((ns, Q_BLOCK), F32)],
        compiler_params=_cparams(("parallel", "arbitrary")),
        name="nsa_prompt",
    )(qs, qrs, gt, kc, vct, cov, ksel_r, vsel_t, kwin_r, vwin_t)


def _gla_rows(c):
    offs, n = [], 0
    for s in range(c):
        t0 = (s // 8) * 8
        offs.append((n, t0))
        n += c - t0
    return offs, n


def _gla_kernel(q_ref, k_ref, la_ref, v_ref, rs_ref, gn_ref, tril_ref, bm_ref, bmask_ref, sm_ref, s0_ref,
                o_ref, sout_ref, s_scr, prod_scr, res_scr, *, c, nchunks):
    ci = pl.program_id(1)

    @pl.when(ci == 0)
    def _():
        s_scr[...] = s0_ref[...]
        prod_scr[...] = jnp.zeros_like(prod_scr)

    q = q_ref[...]
    k = k_ref[...]
    v = v_ref[...]
    la = la_ref[...]
    if c >= 8:
        b = jnp.dot(tril_ref[...], la, preferred_element_type=F32, precision=lax.Precision.HIGHEST)
    else:
        rows_b = [la[0:1]]
        for t in range(1, c):
            rows_b.append(rows_b[-1] + la[t:t + 1])
        b = jnp.concatenate(rows_b, axis=0)
    state = s_scr[...]
    inter = _dot_nt((q * jnp.exp(b)).astype(BF16), state.astype(BF16))

    offs, _ = _gla_rows(c)
    tt = lax.broadcasted_iota(jnp.int32, (c, 1), 0)
    for s in range(c):
        r0, t0 = offs[s]
        d = b[t0:] - b[s:s + 1]
        e = jnp.exp(jnp.where(tt[t0:] >= s, d, -jnp.inf))
        prod_scr[r0:r0 + c - t0, :] = (q[t0:] * k[s:s + 1] * e).astype(BF16)
    res_scr[...] = _dot(prod_scr[...], bm_ref[...])
    pieces = []
    for tb in range(0, c, 8):
        hi = min(tb + 8, c)
        acc = inter[tb:hi]
        for s in range(hi):
            r0, t0 = offs[s]
            if t0 <= tb:
                acc = acc + res_scr[r0 + tb - t0:r0 + hi - t0, :] * v[s:s + 1]
        pieces.append(acc)
    o = pieces[0] if len(pieces) == 1 else jnp.concatenate(pieces, axis=0)

    bl = b[c - 1:c]
    kd = (k * jnp.exp(bl - b)).astype(BF16)
    upd = _dot(v.T.astype(BF16), kd)
    new_state = jnp.exp(bl) * state + upd * bmask_ref[...]
    s_scr[...] = new_state

    o_ref[...] = _seg_rms(o, gn_ref[...], sm_ref[...]) * rs_ref[...]

    @pl.when(ci == nchunks - 1)
    def _():
        sout_ref[...] = new_state


def _gla(qg, kg, la, vg, rs, s0_bd, lw, *, batch, seq, c):
    nchunks = seq // c
    _, npack = _gla_rows(c)
    npad = -(-npack // 16) * 16
    blk = lambda w: pl.BlockSpec((None, c, w), lambda b, i: (b, i, 0))
    full = lambda a: pl.BlockSpec(a.shape, lambda b, i: (0,) * a.ndim)
    tril = jnp.tril(jnp.ones((max(c, 8), max(c, 8)), F32))
    consts = [lw["gla_gn"], tril, lw["gla_bm"], lw["gla_bmask"], lw["sm"]]
    st = pl.BlockSpec((None, GLA_V, GLA_QK), lambda b, i: (b, 0, 0))
    return pl.pallas_call(
        functools.partial(_gla_kernel, c=c, nchunks=nchunks),
        out_shape=[jax.ShapeDtypeStruct((batch, seq, GLA_V), F32), jax.ShapeDtypeStruct((batch, GLA_V, GLA_QK), F32)],
        grid=(batch, nchunks),
        in_specs=[blk(GLA_QK), blk(GLA_QK), blk(GLA_QK), blk(GLA_V), blk(GLA_V)] + [full(a) for a in consts] + [st],
        out_specs=[blk(GLA_V), st],
        scratch_shapes=[pltpu.VMEM((GLA_V, GLA_QK), F32), pltpu.VMEM((npad, GLA_QK), BF16),
                        pltpu.VMEM((npad, GLA_V), F32)],
        compiler_params=_cparams(("parallel", "arbitrary")),
        name="gla",
    )(qg, kg, la, vg, rs, *consts, s0_bd)


def _outproj_kernel(x_ref, oa_ref, ob_ref, u_ref, vn_ref, ws_ref, bias_ref, wo_ref, o_ref, *, tm):
    lane = lax.broadcasted_iota(jnp.int32, (GM_CHUNK, GM_W), 1)
    tri = (lax.broadcasted_iota(jnp.int32, (GM_CHUNK, GM_CHUNK), 0)
           >= lax.broadcasted_iota(jnp.int32, (GM_CHUNK, GM_CHUNK), 1))
    zs = []
    for cb in range(tm // GM_CHUNK):
        vn = vn_ref[cb * GM_CHUNK:(cb + 1) * GM_CHUNK, :]
        z = bias_ref[...]
        for g in range(GM_GROUPS):
            wm = jnp.where(tri, ws_ref[g], 0.0).astype(BF16)
            vg = jnp.where((lane >= g * GM_CH) & (lane < (g + 1) * GM_CH), vn, 0.0).astype(BF16)
            z = z + _dot(wm, vg)
        zs.append(z)
    z = zs[0] if len(zs) == 1 else jnp.concatenate(zs, axis=0)
    oc = u_ref[...] * z
    y = _dot(oa_ref[...].astype(BF16), wo_ref[0:NSA_Q, :])
    y = y + _dot(ob_ref[...].astype(BF16), wo_ref[NSA_Q:NSA_Q + GLA_V, :])
    y = y + _dot(oc.astype(BF16), wo_ref[NSA_Q + GLA_V:MIX_OUT, :])
    o_ref[...] = x_ref[...] + y


def _outproj(x, oa, ob, u, vn, ws, bias, wo, li, *, tm):
    m = x.shape[0]
    row = lambda w: pl.BlockSpec((tm, w), lambda i: (i, 0))
    return pl.pallas_call(
        functools.partial(_outproj_kernel, tm=tm),
        out_shape=jax.ShapeDtypeStruct((m, D_MODEL), F32),
        grid=(m // tm,),
        in_specs=[row(D_MODEL), row(NSA_Q), row(GLA_V), row(GM_W), row(GM_W),
                  pl.BlockSpec(ws.shape, lambda i: (0, 0, 0)), pl.BlockSpec(bias.shape, lambda i: (0, 0)),
                  pl.BlockSpec((None, MIX_OUT, D_MODEL), lambda i: (li, 0, 0))],
        out_specs=row(D_MODEL),
        compiler_params=_cparams(("parallel",)),
        name="outproj",
    )(x, oa, ob, u, vn, ws, bias, wo)


def _j_rms(x, g):
    y = x * lax.rsqrt(jnp.mean(x * x, axis=-1, keepdims=True) + EPS)
    return y * g


def _j_masked_probs(s, mask):
    s = jnp.where(mask, s, -jnp.inf)
    m = jnp.max(s, axis=-1, keepdims=True)
    m = jnp.where(jnp.isfinite(m), m, 0.0)
    e = jnp.exp(s - m)
    return e / jnp.maximum(jnp.sum(e, axis=-1, keepdims=True), 1e-30)


def _j_compress(kr, pos_emb, w1, w2):
    B, T, G, Dh = kr.shape
    n_c = (T - CMP_BLOCK) // CMP_STRIDE + 1
    ch = kr[:, :(n_c + 1) * CMP_STRIDE].reshape(B, n_c + 1, CMP_STRIDE, G, Dh)
    blk = jnp.concatenate([ch[:, :-1], ch[:, 1:]], axis=2) + pos_emb[None, None, :, None, :]
    flat = jnp.moveaxis(blk, 3, 2).reshape(B, n_c, G, CMP_BLOCK * Dh)
    return jax.nn.gelu(flat @ w1) @ w2


def _j_nsa(q, q_rot, gates, kv_all, win_all, pos0, win_pos0, qk_norm, cmp_pos, cmp_w1, cmp_w2):
    B, Tq = q.shape[:2]
    G, R = NSA_KV_HEADS, NSA_REP
    T_tot = kv_all.shape[1]
    scale = HEAD_DIM ** -0.5
    kc = _j_rms(_j_compress(kv_all[:, :, 0], cmp_pos[0], cmp_w1[0], cmp_w2[0]), qk_norm[1])
    vc = _j_compress(kv_all[:, :, 1], cmp_pos[1], cmp_w1[1], cmp_w2[1])
    n_c = kc.shape[1]
    cmp_end = jnp.arange(n_c) * CMP_STRIDE + CMP_BLOCK - 1
    n_s = -(-T_tot // SEL_BLOCK)
    sel = jnp.pad(kv_all[:, :, 2:4], ((0, 0), (0, n_s * SEL_BLOCK - T_tot), (0, 0), (0, 0), (0, 0)))
    sel = sel.reshape(B, n_s, SEL_BLOCK, 2, G, HEAD_DIM)
    ks_blk = jnp.transpose(sel[:, :, :, 0], (0, 3, 1, 2, 4))
    vs_blk = jnp.transpose(sel[:, :, :, 1], (0, 3, 1, 2, 4))
    ci = jnp.arange(n_c)[:, None]
    sj = jnp.arange(n_s)
    cover = ((ci * CMP_STRIDE <= sj[None, :] * SEL_BLOCK + SEL_BLOCK - 1)
             & (ci * CMP_STRIDE + CMP_BLOCK - 1 >= sj[None, :] * SEL_BLOCK)).astype(F32)
    n_top = min(N_SEL, n_s)
    kw = jnp.pad(win_all, ((0, 0), (WINDOW, 0), (0, 0), (0, 0), (0, 0)))
    qbs = Tq
    bi = jnp.arange(B)[:, None, None, None]
    gi = jnp.arange(G)[None, :, None, None]
    qb = q.reshape(B, qbs, G, R, HEAD_DIM)
    qrb = q_rot.reshape(B, qbs, G, R, HEAD_DIM)
    p0 = pos0
    t = p0 + jnp.arange(qbs)
    s = jnp.einsum('bqgrd,bngd->bqgrn', qb, kc) * scale
    p_c = _j_masked_probs(s, (cmp_end[None, :] <= t[:, None])[None, :, None, None, :])
    o_c = jnp.einsum('bqgrn,bngd->bqgrd', p_c, vc)
    imp = jnp.einsum('bqgn,ns->bqgs', p_c.sum(axis=3), cover)
    bt = t // SEL_BLOCK
    allowed = sj[None, :] * SEL_BLOCK <= t[:, None]
    forced = (sj[None, :] == 0) | (sj[None, :] == bt[:, None]) | (sj[None, :] == bt[:, None] - 1)
    score = jnp.where(allowed[None, :, None, :], imp + jnp.where(forced, SEL_BONUS, 0.0)[None, :, None, :], -jnp.inf)
    top_v, top_i = lax.top_k(score, n_top)
    idx = jnp.transpose(top_i, (0, 2, 1, 3))
    valid = jnp.isfinite(jnp.transpose(top_v, (0, 2, 1, 3)))
    kg = ks_blk[bi, gi, idx]
    vg = vs_blk[bi, gi, idx]
    s = jnp.einsum('bqgrd,bgqksd->bqgrks', qrb, kg) * scale
    kpos = idx[..., None] * SEL_BLOCK + jnp.arange(SEL_BLOCK)
    msk = valid[..., None] & (kpos <= t[None, None, :, None, None])
    msk = jnp.transpose(msk, (0, 2, 1, 3, 4))[:, :, :, None]
    p_s = _j_masked_probs(s.reshape(*s.shape[:4], -1), msk.reshape(*msk.shape[:4], -1)).reshape(s.shape)
    o_s = jnp.einsum('bqgrks,bgqksd->bqgrd', p_s, vg)
    kwb = lax.dynamic_slice_in_dim(kw, p0 - win_pos0, WINDOW + qbs, axis=1)
    s = jnp.einsum('bqgrd,bkgd->bqgrk', qrb, kwb[:, :, 0]) * scale
    apos = p0 - WINDOW + jnp.arange(WINDOW + qbs)
    diff = t[:, None] - apos[None, :]
    wm = (apos[None, :] >= win_pos0) & (diff >= 0) & (diff < WINDOW)
    p_w = _j_masked_probs(s, wm[None, :, None, None, :])
    o_w = jnp.einsum('bqgrk,bkgd->bqgrd', p_w, kwb[:, :, 1])
    o = jnp.stack([o_c, o_s, o_w], axis=-1)
    o = jnp.sum(o * gates.reshape(B, qbs, G, R, 1, 3), axis=-1)
    return o.reshape(B, Tq, NSA_Q)


def _j_gla(q, k, v, log_a, S0):
    B, T, H, DK = q.shape
    tril = jnp.tril(jnp.ones((T, T), dtype=bool))
    b = jnp.cumsum(log_a, axis=1)
    inter = jnp.einsum('bthk,bhkv->bthv', q * jnp.exp(b), S0)
    d = jnp.where(tril[None, :, :, None, None], b[:, :, None] - b[:, None, :], -jnp.inf)
    att = jnp.einsum('bthk,bshk,btshk->bths', q, k, jnp.exp(d))
    o = inter + jnp.einsum('bths,bshv->bthv', att, v)
    bl = b[:, -1]
    S = jnp.exp(bl)[..., None] * S0 + jnp.einsum('bshk,bshv->bhkv', k * jnp.exp(bl[:, None] - b), v)
    return o, S


def _pad_cols(w):
    cuts = np.cumsum((0,) + IN_SPLITS)
    parts = []
    for i, (n, p) in enumerate(zip(IN_SPLITS, IN_PADDED)):
        seg = w[..., cuts[i]:cuts[i] + n]
        if p != n:
            seg = jnp.pad(seg, [(0, 0)] * (w.ndim - 1) + [(0, p - n)])
        parts.append(seg)
    return jnp.concatenate(parts, axis=-1)


def _rope_tables(pos):
    half = HEAD_DIM // 2
    inv = 1.0 / (ROPE_THETA ** (jnp.arange(half, dtype=F32) * (2.0 / HEAD_DIM)))
    ang = pos.astype(F32)[:, None] * inv[None, :]
    cos = jnp.cos(ang)
    sin = jnp.sin(ang)
    cos_f = jnp.concatenate([cos, cos, cos, cos], axis=1)
    sin_f = jnp.concatenate([-sin, sin, -sin, sin], axis=1)
    return cos_f, sin_f


def _cover_t(seq):
    nc = seq // CMP_STRIDE
    ns = seq // SEL_BLOCK
    ci = np.arange(nc)[None, :]
    sj = np.arange(ns)[:, None]
    cov = ((ci * CMP_STRIDE <= sj * SEL_BLOCK + SEL_BLOCK - 1)
           & (ci * CMP_STRIDE + CMP_BLOCK - 1 >= sj * SEL_BLOCK) & (ci < nc - 1))
    return jnp.asarray(cov, dtype=BF16)


def _layer_weights(l, ln_gains, w_in_p, nsa_qk_norm, nsa_cmp_pos, nsa_cmp_w1, nsa_cmp_w2, gla_gate_w, gla_gate_b,
                   gla_norm, gm_ln, gm_ws, gm_b):
    eye2 = jnp.eye(NSA_KV_HEADS, dtype=F32)
    lw = {"ln0": ln_gains[l, 0][None], "ln1": ln_gains[l, 1][None], "ln2": ln_gains[l, 2][None], "w_in": w_in_p[l]}
    lw["gq"] = jnp.tile(nsa_qk_norm[l, 0], NSA_HEADS)[None]
    lw["gkc"] = jnp.tile(nsa_qk_norm[l, 1], NSA_KV_HEADS)[None]
    lw["gks"] = jnp.tile(nsa_qk_norm[l, 2], NSA_KV_HEADS)[None]
    lw["gkw"] = jnp.tile(nsa_qk_norm[l, 3], NSA_KV_HEADS)[None]
    seg = (np.arange(LANES)[:, None] // HEAD_DIM) == (np.arange(LANES)[None, :] // HEAD_DIM)
    lw["sm"] = jnp.asarray(seg * (1.0 / HEAD_DIM), dtype=BF16)
    lw["gla_gw"] = jnp.pad(gla_gate_w[l], ((0, LANES - GLA_GATE_RANK), (0, 0))).astype(BF16)
    lw["gla_gb"] = gla_gate_b[l][None]
    lw["gm_lng"] = gm_ln[l, 0][None]
    lw["gm_lnb"] = gm_ln[l, 1][None]
    pos_rows = []
    for c, nm in ((0, "k"), (1, "v")):
        w1 = nsa_cmp_w1[l, c].reshape(CMP_BLOCK, HEAD_DIM, CMP_HIDDEN)
        for half, tag in ((w1[:CMP_STRIDE], "a"), (w1[CMP_STRIDE:], "b")):
            wx = jnp.einsum("sdh,pg->spdgh", half, eye2)
            lw["cmp_w" + tag + nm] = wx.reshape(CMP_STRIDE * LANES, NSA_KV_HEADS * CMP_HIDDEN).astype(BF16)
        lw["cmp_w2" + nm] = jnp.einsum("hd,pg->phgd", nsa_cmp_w2[l, c], eye2).reshape(
            NSA_KV_HEADS * CMP_HIDDEN, LANES).astype(BF16)
        pe = nsa_cmp_pos[l, c]
        for half in (pe[:CMP_STRIDE], pe[CMP_STRIDE:]):
            pos_rows.append(jnp.broadcast_to(half[:, None, :], (CMP_STRIDE, NSA_KV_HEADS, HEAD_DIM)).reshape(-1))
    lw["cmp_pos"] = jnp.stack(pos_rows)
    lw["gla_gn"] = jnp.tile(gla_norm[l], GLA_HEADS)[None]
    hq = np.arange(GLA_QK) // GLA_DK
    hv = np.arange(GLA_V) // GLA_DV
    lw["gla_bm"] = jnp.asarray(hq[:, None] == hv[None, :], dtype=BF16)
    lw["gla_bmask"] = jnp.asarray(hv[:, None] == hq[None, :], dtype=F32)
    lw["gm_ws"] = gm_ws[l]
    lw["gm_bias"] = jnp.repeat(gm_b[l].T, GM_CH, axis=1)
    return lw


def _state_to_t(s):
    b = s.shape[0]
    eye = jnp.eye(GLA_HEADS, dtype=s.dtype)
    return jnp.einsum("bhkv,hg->bhvgk", s, eye).reshape(b, GLA_V, GLA_QK)


def _state_from_t(st):
    b = st.shape[0]
    s5 = st.reshape(b, GLA_HEADS, GLA_DV, GLA_HEADS, GLA_DK)
    d = jnp.stack([s5[:, h, :, h, :] for h in range(GLA_HEADS)], axis=1)
    return jnp.swapaxes(d, 2, 3)


def kernel(x_prompt, x_sample, cache_kv, cache_win_kv, state_gla, page_table, ln_gains, ffn_w_gate_up, ffn_w_down,
           w_in, w_out, nsa_qk_norm, nsa_cmp_pos, nsa_cmp_w1, nsa_cmp_w2, gla_gate_w, gla_gate_b, gla_norm, gm_ln,
           gm_ws, gm_b):
    depth = w_in.shape[0]
    bp, tp, _ = x_prompt.shape
    bs, ts, _ = x_sample.shape
    n_pages = page_table.shape[1]
    page = cache_kv.shape[2]
    past_len = n_pages * page
    mp, ms = bp * tp, bs * ts
    nc = tp // CMP_STRIDE

    w_gu = ffn_w_gate_up.astype(BF16).reshape(depth * 2, D_MODEL, 2 * D_FF)
    w_d = ffn_w_down.astype(BF16).reshape(depth * 2, D_FF, D_MODEL)
    w_in_p = _pad_cols(w_in).astype(BF16)
    w_o = w_out.astype(BF16)
    cos_p, sin_p = _rope_tables(jnp.arange(tp))
    cos_s, sin_s = _rope_tables(past_len + jnp.arange(ms) % ts)
    cov = _cover_t(tp)
    tm_p = 512 if mp % 512 == 0 else Q_BLOCK
    tm_i = 256 if tp % 256 == 0 else Q_BLOCK
    tf = 1408

    xp = x_prompt.reshape(mp, D_MODEL)
    xs = x_sample.reshape(ms, D_MODEL)
    kv_p, win_p, gla_p, kv_s, win_s, gla_s, gmv_s = [], [], [], [], [], [], []
    eye_b = jnp.eye(bs, dtype=F32)
    for l in range(depth):
        lw = _layer_weights(l, ln_gains, w_in_p, nsa_qk_norm, nsa_cmp_pos, nsa_cmp_w1, nsa_cmp_w2, gla_gate_w,
                            gla_gate_b, gla_norm, gm_ln, gm_ws, gm_b)
        xp = _ffn(xp, lw["ln0"], w_gu, w_d, 2 * l, tm=tm_p, tf=tf)
        (newkv, newwin, _, qg, kg, la, vg, rs, u, vn, qs, qrs, gt, ksel_r, vsel_t, kwin_r, vwin_t, kcmp,
         vcmp) = _inproj(xp, lw, cos_p, sin_p, tm=tm_i, attn_layout=True, batch=bp, seq=tp)
        zk = kcmp.reshape(bp, nc, CMP_STRIDE * LANES)
        zv = vcmp.reshape(bp, nc, CMP_STRIDE * LANES)
        nxt = lambda z: jnp.concatenate([z[:, 1:], jnp.zeros_like(z[:, :1])], axis=1)
        kc, vct = _compress(zk, nxt(zk), zv, nxt(zv), lw, rows=min(128, nc))
        oa = _nsa_prompt(qs, qrs, gt, kc, vct, cov, ksel_r.reshape(bp, tp, LANES), vsel_t,
                         kwin_r.reshape(bp, tp, LANES), vwin_t, batch=bp, seq=tp)
        r3 = lambda a: a.reshape(bp, tp, a.shape[-1])
        ob, st = _gla(r3(qg), r3(kg), r3(la), r3(vg), r3(rs), jnp.zeros((bp, GLA_V, GLA_QK), F32), lw,
                      batch=bp, seq=tp, c=GLA_CHUNK)
        xp = _outproj(xp, oa, ob.reshape(mp, GLA_V), u, vn, lw["gm_ws"], lw["gm_bias"], w_o, l, tm=tm_p)
        xp = _ffn(xp, lw["ln2"], w_gu, w_d, 2 * l + 1, tm=tm_p, tf=tf)
        kv_p.append(newkv.reshape(bp, tp, 4, NSA_KV_HEADS, HEAD_DIM))
        n_keep = min(WINDOW, tp)
        win_p.append(newwin.reshape(bp, tp, 2, NSA_KV_HEADS, HEAD_DIM)[:, tp - n_keep:])
        gla_p.append(_state_from_t(st))

        xs = _ffn(xs, lw["ln0"], w_gu, w_d, 2 * l, tm=ms, tf=tf)
        (newkv, newwin, gates, qg, kg, la, vg, rs, u, vn, qn, qr) = _inproj(
            xs, lw, cos_s, sin_s, tm=ms, attn_layout=False, batch=bs, seq=ts)
        new_kv = newkv.reshape(bs, ts, 4, NSA_KV_HEADS, HEAD_DIM)
        new_win = newwin.reshape(bs, ts, 2, NSA_KV_HEADS, HEAD_DIM)
        past = cache_kv[l][page_table].reshape(bs, past_len, 4, NSA_KV_HEADS, HEAD_DIM)
        kv_all = jnp.concatenate([past, new_kv], axis=1)
        win_all = jnp.concatenate([cache_win_kv[l], new_win], axis=1)
        n_keep = cache_win_kv.shape[2]
        gates4 = gates[:, :3 * NSA_HEADS].reshape(bs, ts, NSA_HEADS, 3)
        oa = _j_nsa(qn.reshape(bs, ts, NSA_HEADS, HEAD_DIM), qr.reshape(bs, ts, NSA_HEADS, HEAD_DIM), gates4, kv_all,
                    win_all, past_len, past_len - n_keep, nsa_qk_norm[l], nsa_cmp_pos[l], nsa_cmp_w1[l],
                    nsa_cmp_w2[l]).reshape(ms, NSA_Q)
        o_g, s_t = _j_gla(qg.reshape(bs, ts, GLA_HEADS, GLA_DK), kg.reshape(bs, ts, GLA_HEADS, GLA_DK),
                          vg.reshape(bs, ts, GLA_HEADS, GLA_DV), la.reshape(bs, ts, GLA_HEADS, GLA_DK), state_gla[l])
        ob = (_j_rms(o_g, gla_norm[l]).reshape(ms, GLA_V) * rs)
        ws_s = jnp.einsum("gts,bc->gbtcs", gm_ws[l][:, :ts, :ts], eye_b).reshape(GM_GROUPS, ms, ms)
        bias_s = jnp.tile(lw["gm_bias"][:ts], (bs, 1))
        xs = _outproj(xs, oa, ob, u, vn, ws_s, bias_s, w_o, l, tm=ms)
        xs = _ffn(xs, lw["ln2"], w_gu, w_d, 2 * l + 1, tm=ms, tf=tf)
        kv_s.append(new_kv)
        win_s.append(win_all[:, win_all.shape[1] - n_keep:])
        gla_s.append(s_t)
        gmv_s.append(vn.reshape(bs, ts, GM_GROUPS, GM_CH))
    return (xp.reshape(bp, tp, D_MODEL), xs.reshape(bs, ts, D_MODEL), jnp.stack(kv_p), jnp.stack(win_p),
            jnp.stack(gla_p), jnp.stack(kv_s), jnp.stack(win_s), jnp.stack(gla_s), jnp.stack(gmv_s))
```

```python
import functools

import numpy as np
import jax
import jax.numpy as jnp
from jax import lax
from jax.experimental import pallas as pl
from jax.experimental.pallas import tpu as pltpu

F32 = jnp.float32
BF16 = jnp.bfloat16

D_MODEL = 1024
HEAD_DIM = 64
NSA_HEADS = 8
NSA_KV_HEADS = 2
NSA_REP = NSA_HEADS // NSA_KV_HEADS
CMP_STRIDE = 16
CMP_BLOCK = 2 * CMP_STRIDE
CMP_HIDDEN = 128
SEL_BLOCK = 64
N_SEL = 16
WINDOW = 512
Q_BLOCK = 128
SEL_BONUS = 1.0e4
GLA_HEADS = 4
GLA_DK = 32
GLA_DV = 64
GLA_GATE_RANK = 16
GLA_GATE_TEMP = 16.0
GLA_CHUNK = 64
GM_GROUPS = 4
GM_CH = 64
GM_CHUNK = 128
D_FF = 2816
ROPE_THETA = 10000.0
EPS = 1e-6

NSA_Q = NSA_HEADS * HEAD_DIM
NSA_KV = NSA_KV_HEADS * HEAD_DIM
GLA_QK = GLA_HEADS * GLA_DK
GLA_V = GLA_HEADS * GLA_DV
GM_W = GM_GROUPS * GM_CH
MIX_OUT = NSA_Q + GLA_V + GM_W
IN_SPLITS = (NSA_Q, 6 * NSA_KV, 3 * NSA_HEADS, GLA_QK, GLA_QK, GLA_V, GLA_GATE_RANK, GLA_V, GM_W, GM_W)
IN_PADDED = tuple(-(-s // 128) * 128 for s in IN_SPLITS)
IN_OFFS = tuple(int(v) for v in np.cumsum((0,) + IN_PADDED))
D_IN_PAD = IN_OFFS[-1]

LANES = 128
NEG_BIG = -1.0e30
VMEM_LIMIT = 56 * 1024 * 1024


def _cparams(sem):
    return pltpu.CompilerParams(dimension_semantics=sem, vmem_limit_bytes=VMEM_LIMIT)


def _gelu(x):
    c = np.float32(np.sqrt(2.0 / np.pi))
    return x * (0.5 * (1.0 + jnp.tanh(c * (x + 0.044715 * (x * x * x)))))


def _sigmoid(x):
    return 1.0 / (1.0 + jnp.exp(-x))


def _dot(a, b):
    return jnp.dot(a, b, preferred_element_type=F32)


def _dot_nt(a, b):
    return lax.dot_general(a, b, (((1,), (1,)), ((), ())), preferred_element_type=F32)


def _seg_mean_sq(x, sm):
    sq = x * x
    hi = sq.astype(BF16)
    lo = (sq - hi.astype(F32)).astype(BF16)
    outs = []
    for c in range(x.shape[1] // LANES):
        sl = slice(c * LANES, (c + 1) * LANES)
        outs.append(_dot(hi[:, sl], sm) + _dot(lo[:, sl], sm))
    return outs[0] if len(outs) == 1 else jnp.concatenate(outs, axis=1)


def _seg_rms(x, gain, sm):
    return x * lax.rsqrt(_seg_mean_sq(x, sm) + EPS) * gain


def _tile_lanes(a, w):
    n = w // a.shape[1]
    return a if n == 1 else jnp.concatenate([a] * n, axis=1)


def _rope(x, cos, sin_signed):
    w = x.shape[1]
    lane = lax.broadcasted_iota(jnp.int32, x.shape, 1)
    fwd = pltpu.roll(x, w - HEAD_DIM // 2, axis=1)
    bwd = pltpu.roll(x, HEAD_DIM // 2, axis=1)
    partner = jnp.where((lane % HEAD_DIM) < HEAD_DIM // 2, fwd, bwd)
    return x * _tile_lanes(cos, w) + partner * _tile_lanes(sin_signed, w)


def _ffn_kernel(x_ref, g_ref, wg_ref, wu_ref, wd_ref, o_ref, h_scr, acc_scr, *, nj):
    j = pl.program_id(1)

    @pl.when(j == 0)
    def _():
        x = x_ref[...]
        ms = jnp.mean(x * x, axis=-1, keepdims=True)
        h_scr[...] = (x * lax.rsqrt(ms + EPS) * g_ref[...]).astype(BF16)
        acc_scr[...] = jnp.zeros_like(acc_scr)

    h = h_scr[...]
    g = _dot(h, wg_ref[...])
    u = _dot(h, wu_ref[...])
    a = (g * _sigmoid(g)) * u
    acc_scr[...] += _dot(a.astype(BF16), wd_ref[...])

    @pl.when(j == nj - 1)
    def _():
        o_ref[...] = x_ref[...] + 0.5 * acc_scr[...]


def _ffn(x, gain, w_gu, w_d, li, *, tm, tf):
    m = x.shape[0]
    nj = D_FF // tf
    return pl.pallas_call(
        functools.partial(_ffn_kernel, nj=nj),
        out_shape=jax.ShapeDtypeStruct((m, D_MODEL), F32),
        grid=(m // tm, nj),
        in_specs=[
            pl.BlockSpec((tm, D_MODEL), lambda i, j: (i, 0)),
            pl.BlockSpec((1, D_MODEL), lambda i, j: (0, 0)),
            pl.BlockSpec((None, D_MODEL, tf), lambda i, j: (li, 0, j)),
            pl.BlockSpec((None, D_MODEL, tf), lambda i, j: (li, 0, j + nj)),
            pl.BlockSpec((None, tf, D_MODEL), lambda i, j: (li, j, 0)),
        ],
        out_specs=pl.BlockSpec((tm, D_MODEL), lambda i, j: (i, 0)),
        scratch_shapes=[pltpu.VMEM((tm, D_MODEL), BF16), pltpu.VMEM((tm, D_MODEL), F32)],
        compiler_params=_cparams(("parallel", "arbitrary")),
        name="ffn",
    )(x, gain, w_gu, w_gu, w_d)


def _group_padded(arr, h):
    c = arr[:, (h // 2) * LANES:(h // 2 + 1) * LANES]
    g = h // NSA_REP
    if (h % 2) != g:
        c = pltpu.roll(c, HEAD_DIM, axis=1)
    lane = lax.broadcasted_iota(jnp.int32, c.shape, 1)
    keep = (lane >= g * HEAD_DIM) & (lane < (g + 1) * HEAD_DIM)
    return jnp.where(keep, c, 0.0)


def _inproj_kernel(x_ref, ln_ref, w_ref, gq_ref, gks_ref, gkw_ref, cos_ref, sin_ref, gw_ref, gb_ref,
                   lng_ref, lnb_ref, sm_ref, *outs, tm, attn_layout):
    (newkv_ref, newwin_ref, gates_ref, qg_ref, kg_ref, la_ref, vg_ref, rs_ref, u_ref, vn_ref) = outs[:10]
    x = x_ref[...]
    ms = jnp.mean(x * x, axis=-1, keepdims=True)
    h = (x * lax.rsqrt(ms + EPS) * ln_ref[...]).astype(BF16)
    p = _dot(h, w_ref[...])
    sm = sm_ref[...]
    cos = cos_ref[...]
    sin = sin_ref[...]
    o = IN_OFFS

    def seg(i, a=0, b=None):
        b = IN_PADDED[i] if b is None else b
        return p[:, o[i] + a:o[i] + b]

    qn = _seg_rms(seg(0), gq_ref[...], sm)
    qr = _rope(qn, cos, sin)
    kv = [seg(1, LANES * j, LANES * (j + 1)) for j in range(6)]
    ksel = _rope(_seg_rms(kv[2], gks_ref[...], sm), cos, sin)
    kwin = _rope(_seg_rms(kv[4], gkw_ref[...], sm), cos, sin)
    vsel_t = kv[3].T
    vwin_t = kv[5].T
    if attn_layout:
        for j, a in enumerate((kv[0].T, kv[1].T, ksel.T, vsel_t)):
            newkv_ref[j * LANES:(j + 1) * LANES, :] = a
        newwin_ref[0:LANES, :] = kwin.T
        newwin_ref[LANES:2 * LANES, :] = vwin_t
    else:
        for j, a in enumerate((kv[0], kv[1], ksel, kv[3])):
            newkv_ref[:, j * LANES:(j + 1) * LANES] = a
        newwin_ref[:, 0:LANES] = kwin
        newwin_ref[:, LANES:2 * LANES] = kv[5]
    gates = _sigmoid(seg(2))
    gates_ref[...] = gates
    qg_ref[...] = seg(3) * np.float32(GLA_DK ** -0.5)
    kg_ref[...] = seg(4)
    vg_ref[...] = seg(5)
    logit = _dot(seg(6).astype(BF16), gw_ref[...]) + gb_ref[...]
    log_sig = jnp.minimum(logit, 0.0) - jnp.log1p(jnp.exp(-jnp.abs(logit)))
    la_ref[...] = log_sig * np.float32(1.0 / GLA_GATE_TEMP)
    r = seg(7)
    rs_ref[...] = r * _sigmoid(r)
    u_ref[...] = _gelu(seg(8))
    v = _gelu(seg(9))
    mu = jnp.mean(v, axis=-1, keepdims=True)
    var = jnp.mean(jnp.square(v - mu), axis=-1, keepdims=True)
    vn_ref[...] = (v - mu) * lax.rsqrt(var + EPS) * lng_ref[...] + lnb_ref[...]

    if attn_layout:
        (qs_ref, qrs_ref, gt_ref, kselr_ref, vselt_ref, kwinr_ref, vwint_ref, kcmp_ref, vcmp_ref) = outs[10:]
        scale = np.float32(HEAD_DIM ** -0.5)
        qs = qn * scale
        qrs = qr * scale
        for hh in range(NSA_HEADS):
            a = _group_padded(qs, hh).astype(BF16)
            b = _group_padded(qrs, hh).astype(BF16)
            for rb in range(tm // Q_BLOCK):
                qs_ref[rb, hh] = a[rb * Q_BLOCK:(rb + 1) * Q_BLOCK]
                qrs_ref[rb, hh] = b[rb * Q_BLOCK:(rb + 1) * Q_BLOCK]
        for rb in range(tm // Q_BLOCK):
            gt_ref[rb] = gates[rb * Q_BLOCK:(rb + 1) * Q_BLOCK].T
        kselr_ref[...] = ksel.astype(BF16)
        vselt_ref[...] = vsel_t.astype(BF16)
        kwinr_ref[...] = kwin.astype(BF16)
        vwint_ref[...] = vwin_t.astype(BF16)
        kcmp_ref[...] = kv[0]
        vcmp_ref[...] = kv[1]
    else:
        qn_ref, qr_ref = outs[10:]
        qn_ref[...] = qn
        qr_ref[...] = qr


def _inproj(x, lw, cos_t, sin_t, *, tm, attn_layout, batch, seq):
    m = x.shape[0]
    nt = m // tm
    ntab = cos_t.shape[0] // tm
    row = lambda w: pl.BlockSpec((tm, w), lambda i: (i, 0))
    full = lambda a: pl.BlockSpec(a.shape, lambda i: (0,) * a.ndim)
    ins = [x, lw["ln1"], lw["w_in"], lw["gq"], lw["gks"], lw["gkw"], cos_t, sin_t, lw["gla_gw"], lw["gla_gb"],
           lw["gm_lng"], lw["gm_lnb"], lw["sm"]]
    in_specs = [row(D_MODEL), full(lw["ln1"]), full(lw["w_in"]), full(lw["gq"]), full(lw["gks"]), full(lw["gkw"]),
                pl.BlockSpec((tm, LANES), lambda i: (i % ntab, 0)), pl.BlockSpec((tm, LANES), lambda i: (i % ntab, 0)),
                full(lw["gla_gw"]), full(lw["gla_gb"]), full(lw["gm_lng"]), full(lw["gm_lnb"]), full(lw["sm"])]
    widths = [512, 256, 128, 128, 128, 128, 256, 256, 256, 256]
    out_shape = [jax.ShapeDtypeStruct((m, w), F32) for w in widths]
    out_specs = [row(w) for w in widths]
    if attn_layout:
        nqb = m // Q_BLOCK
        rpb = tm // Q_BLOCK
        tpb = seq // tm
        for j in range(2):
            out_shape[j] = jax.ShapeDtypeStruct((batch, widths[j], seq), F32)
            out_specs[j] = pl.BlockSpec((None, widths[j], tm), lambda i: (i // tpb, 0, i % tpb))
        out_shape += [jax.ShapeDtypeStruct((nqb, NSA_HEADS, Q_BLOCK, LANES), BF16)] * 2
        out_specs += [pl.BlockSpec((rpb, NSA_HEADS, Q_BLOCK, LANES), lambda i: (i, 0, 0, 0))] * 2
        out_shape += [jax.ShapeDtypeStruct((nqb, LANES, Q_BLOCK), F32)]
        out_specs += [pl.BlockSpec((rpb, LANES, Q_BLOCK), lambda i: (i, 0, 0))]
        rowmaj = (jax.ShapeDtypeStruct((m, LANES), BF16), row(LANES))
        trans = (jax.ShapeDtypeStruct((batch, LANES, seq), BF16),
                 pl.BlockSpec((None, LANES, tm), lambda i: (i // tpb, 0, i % tpb)))
        for sh, sp in (rowmaj, trans, rowmaj, trans):
            out_shape.append(sh)
            out_specs.append(sp)
        out_shape += [jax.ShapeDtypeStruct((m, LANES), F32)] * 2
        out_specs += [row(LANES)] * 2
    else:
        out_shape += [jax.ShapeDtypeStruct((m, NSA_Q), F32)] * 2
        out_specs += [row(NSA_Q)] * 2
    return pl.pallas_call(
        functools.partial(_inproj_kernel, tm=tm, attn_layout=attn_layout),
        out_shape=out_shape,
        grid=(nt,),
        in_specs=in_specs,
        out_specs=out_specs,
        compiler_params=_cparams(("parallel",)),
        name="inproj",
    )(*ins)


def _compress_kernel(zk_ref, zkn_ref, zv_ref, zvn_ref, pos_ref, wak_ref, wbk_ref, wav_ref, wbv_ref,
                     w2k_ref, w2v_ref, gkc_ref, sm_ref, kc_ref, vct_ref):
    pos = pos_ref[...]

    def one(z_ref, zn_ref, pa, pb, wa_ref, wb_ref, w2_ref):
        a = _dot((z_ref[...] + pa).astype(BF16), wa_ref[...])
        b = _dot((zn_ref[...] + pb).astype(BF16), wb_ref[...])
        return _dot(_gelu(a + b).astype(BF16), w2_ref[...])

    ck = one(zk_ref, zkn_ref, pos[0:1], pos[1:2], wak_ref, wbk_ref, w2k_ref)
    cv = one(zv_ref, zvn_ref, pos[2:3], pos[3:4], wav_ref, wbv_ref, w2v_ref)
    kc_ref[...] = _seg_rms(ck, gkc_ref[...], sm_ref[...]).astype(BF16)
    vct_ref[...] = cv.T.astype(BF16)


def _compress(zk, zkn, zv, zvn, lw, *, rows):
    b, nc, kdim = zk.shape
    zspec = pl.BlockSpec((None, rows, kdim), lambda i, j: (i, j, 0))
    full = lambda a: pl.BlockSpec(a.shape, lambda i, j: (0,) * a.ndim)
    ws = [lw["cmp_pos"], lw["cmp_wak"], lw["cmp_wbk"], lw["cmp_wav"], lw["cmp_wbv"], lw["cmp_w2k"], lw["cmp_w2v"],
          lw["gkc"], lw["sm"]]
    return pl.pallas_call(
        _compress_kernel,
        out_shape=[jax.ShapeDtypeStruct((b, nc, LANES), BF16), jax.ShapeDtypeStruct((b, LANES, nc), BF16)],
        grid=(b, nc // rows),
        in_specs=[zspec] * 4 + [full(w) for w in ws],
        out_specs=[pl.BlockSpec((None, rows, LANES), lambda i, j: (i, j, 0)),
                   pl.BlockSpec((None, LANES, rows), lambda i, j: (i, 0, j))],
        compiler_params=_cparams(("parallel", "parallel")),
        name="nsa_compress",
    )(zk, zkn, zv, zvn, *ws)


def _nsa_kernel(qs_ref, qrs_ref, gt_ref, kc_ref, vct_ref, cov_ref, ksel_ref, vselt_ref, kwin_ref, vwint_ref,
                o_ref, score_scr, bias_scr, *, nc, ns, kb_keys):
    ib = pl.program_id(1)
    p0 = ib * Q_BLOCK
    nl = NSA_REP * Q_BLOCK
    ng = NSA_KV_HEADS
    t_row = p0 + lax.broadcasted_iota(jnp.int32, (1, nl), 1) % Q_BLOCK
    t_q = p0 + lax.broadcasted_iota(jnp.int32, (1, Q_BLOCK), 1)
    gt = gt_ref[...]
    n_top = min(N_SEL, ns)
    per_kb = kb_keys // SEL_BLOCK
    grows = [slice(g * HEAD_DIM, (g + 1) * HEAD_DIM) for g in range(ng)]

    def q_of(ref, g):
        return ref[NSA_REP * g:NSA_REP * (g + 1)].reshape(nl, LANES)

    def online(carry, s, vt):
        m, l, acc = carry
        m_new = jnp.maximum(m, jnp.max(s, axis=0, keepdims=True))
        p = jnp.exp(s - m_new)
        alpha = jnp.exp(m - m_new)
        l = alpha * l + jnp.sum(p, axis=0, keepdims=True)
        acc = alpha * acc + _dot(vt, p.astype(BF16))
        return m_new, l, acc

    init1 = (jnp.full((1, nl), NEG_BIG, F32), jnp.zeros((1, nl), F32), jnp.zeros((HEAD_DIM, nl), F32))
    init = tuple(init1 for _ in range(ng))

    sj = lax.broadcasted_iota(jnp.int32, (ns, 1), 0)
    o_cmp = []
    for g in range(ng):
        s = _dot_nt(kc_ref[...], q_of(qs_ref, g))
        ci = lax.broadcasted_iota(jnp.int32, (nc, 1), 0)
        cmask = (ci * CMP_STRIDE + (CMP_BLOCK - 1) <= t_row) & (ci < nc - 1)
        s = jnp.where(cmask, s, -jnp.inf)
        m = jnp.max(s, axis=0, keepdims=True)
        m = jnp.where(m == -jnp.inf, 0.0, m)
        e = jnp.exp(s - m)
        l = jnp.sum(e, axis=0, keepdims=True)
        p = e * (1.0 / jnp.maximum(l, 1e-30))
        o_cmp.append(_dot(vct_ref[grows[g], :], p.astype(BF16)))
        psum = p[:, 0:Q_BLOCK]
        for r in range(1, NSA_REP):
            psum = psum + p[:, r * Q_BLOCK:(r + 1) * Q_BLOCK]
        imp = _dot(cov_ref[...], psum.astype(BF16))
        bt = t_q // SEL_BLOCK
        allowed = sj * SEL_BLOCK <= t_q
        forced = (sj == 0) | (sj == bt) | (sj == bt - 1)
        score_scr[g] = jnp.where(allowed, imp + jnp.where(forced, np.float32(SEL_BONUS), 0.0), -jnp.inf)

    def rank_body(k, ranks):
        later = jnp.where(sj > k, 1, 0)
        out = []
        for g in range(ng):
            score = score_scr[g]
            row = score_scr[g, pl.ds(k, 1), :]
            out.append(ranks[g] + jnp.where(row > score, 1, 0) + jnp.where(row == score, later, 0))
        return tuple(out)

    n_live = jnp.minimum((p0 + Q_BLOCK + SEL_BLOCK - 1) // SEL_BLOCK, ns)
    ranks = lax.fori_loop(0, n_live, rank_body, tuple(jnp.zeros((ns, Q_BLOCK), jnp.int32) for _ in range(ng)))
    for g in range(ng):
        keep = (ranks[g] < n_top) & (score_scr[g] > -jnp.inf)
        bias_scr[g] = jnp.where(keep, 0.0, -jnp.inf)

    def sel_scores(kb, g):
        k0 = pl.multiple_of(kb * kb_keys, kb_keys)
        s = _dot_nt(ksel_ref[pl.ds(k0, kb_keys), :], q_of(qrs_ref, g))
        parts = []
        for i in range(per_kb):
            row = bias_scr[g, pl.ds(kb * per_kb + i, 1), :]
            parts.append(s[i * SEL_BLOCK:(i + 1) * SEL_BLOCK] + jnp.concatenate([row] * NSA_REP, axis=1))
        return jnp.concatenate(parts, axis=0), k0

    def sel_body(kb, carry):
        out = []
        for g in range(ng):
            s, k0 = sel_scores(kb, g)
            out.append(online(carry[g], s, vselt_ref[grows[g], pl.ds(k0, kb_keys)]))
        return tuple(out)

    n_kb = (p0 + Q_BLOCK + kb_keys - 1) // kb_keys
    carry = lax.fori_loop(0, n_kb - 1, sel_body, init)
    o_sel = []
    for g in range(ng):
        s, k0 = sel_scores(n_kb - 1, g)
        kpos = k0 + lax.broadcasted_iota(jnp.int32, (kb_keys, 1), 0)
        s = jnp.where(kpos <= t_row, s, -jnp.inf)
        _, l_s, acc_s = online(carry[g], s, vselt_ref[grows[g], pl.ds(k0, kb_keys)])
        o_sel.append(acc_s * (1.0 / jnp.maximum(l_s, 1e-30)))

    def win_body(wb, carry):
        k0 = pl.multiple_of(wb * Q_BLOCK, Q_BLOCK)
        kpos = k0 + lax.broadcasted_iota(jnp.int32, (Q_BLOCK, 1), 0)
        diff = t_row - kpos
        mask = (diff >= 0) & (diff < WINDOW)
        kblk = kwin_ref[pl.ds(k0, Q_BLOCK), :]
        out = []
        for g in range(ng):
            s = jnp.where(mask, _dot_nt(kblk, q_of(qrs_ref, g)), -jnp.inf)
            out.append(online(carry[g], s, vwint_ref[grows[g], pl.ds(k0, Q_BLOCK)]))
        return tuple(out)

    w_lo = jnp.maximum(ib - WINDOW // Q_BLOCK, 0)
    wcarry = lax.fori_loop(w_lo, ib + 1, win_body, init)

    for g in range(ng):
        _, l_w, acc_w = wcarry[g]
        o_w = acc_w * (1.0 / jnp.maximum(l_w, 1e-30))

        def gate_row(jb):
            return jnp.concatenate(
                [gt[(NSA_REP * g + r) * 3 + jb:(NSA_REP * g + r) * 3 + jb + 1, :] for r in range(NSA_REP)], axis=1)

        o = o_cmp[g] * gate_row(0) + o_sel[g] * gate_row(1) + o_w * gate_row(2)
        for pr in range(NSA_REP // 2):
            blk = jnp.concatenate([o[:, (2 * pr) * Q_BLOCK:(2 * pr + 1) * Q_BLOCK],
                                   o[:, (2 * pr + 1) * Q_BLOCK:(2 * pr + 2) * Q_BLOCK]], axis=0)
            c0 = g * NSA_REP * HEAD_DIM + pr * LANES
            o_ref[:, c0:c0 + LANES] = blk.T


def _nsa_prompt(qs, qrs, gt, kc, vct, cov, ksel_r, vsel_t, kwin_r, vwin_t, *, batch, seq):
    nb = seq // Q_BLOCK
    nc = kc.shape[1]
    ns = cov.shape[0]
    kb_keys = min(256, seq)
    per_b3 = lambda a: pl.BlockSpec((None,) + a.shape[1:], lambda b, i: (b, 0, 0))
    return pl.pallas_call(
        functools.partial(_nsa_kernel, nc=nc, ns=ns, kb_keys=kb_keys),
        out_shape=jax.ShapeDtypeStruct((batch * seq, NSA_Q), F32),
        grid=(batch, nb),
        in_specs=[
            pl.BlockSpec((None, NSA_HEADS, Q_BLOCK, LANES), lambda b, i: (b * nb + i, 0, 0, 0)),
            pl.BlockSpec((None, NSA_HEADS, Q_BLOCK, LANES), lambda b, i: (b * nb + i, 0, 0, 0)),
            pl.BlockSpec((None, LANES, Q_BLOCK), lambda b, i: (b * nb + i, 0, 0)),
            per_b3(kc), per_b3(vct),
            pl.BlockSpec(cov.shape, lambda b, i: (0, 0)),
            per_b3(ksel_r), per_b3(vsel_t), per_b3(kwin_r), per_b3(vwin_t),
        ],
        out_specs=pl.BlockSpec((Q_BLOCK, NSA_Q), lambda b, i: (b * nb + i, 0)),
        scratch_shapes=[pltpu.VMEM((NSA_KV_HEADS, ns, Q_BLOCK), F32), pltpu.VMEM((NSA_KV_HEADS, ns, Q_BLOCK), F32)],
        compiler_params=_cparams(("parallel", "arbitrary")),
        name="nsa_prompt",
    )(qs, qrs, gt, kc, vct, cov, ksel_r, vsel_t, kwin_r, vwin_t)


def _gla_rows(c):
    offs, n = [], 0
    for s in range(c):
        t0 = (s // 8) * 8
        offs.append((n, t0))
        n += c - t0
    return offs, n


def _gla_kernel(q_ref, k_ref, la_ref, v_ref, rs_ref, gn_ref, bm_ref, bmask_ref, sm_ref, s0_ref,
                o_ref, sout_ref, s_scr, prod_scr, res_scr, *, c, nchunks):
    ci = pl.program_id(1)

    @pl.when(ci == 0)
    def _():
        s_scr[...] = s0_ref[...]
        prod_scr[...] = jnp.zeros_like(prod_scr)

    q = q_ref[...]
    k = k_ref[...]
    v = v_ref[...]
    la = la_ref[...]
    tt = lax.broadcasted_iota(jnp.int32, (c, 1), 0)
    b = la
    sh = 1
    while sh < c:
        b = b + jnp.where(tt >= sh, pltpu.roll(b, sh, axis=0), 0.0)
        sh *= 2
    state = s_scr[...]
    inter = _dot_nt((q * jnp.exp(b)).astype(BF16), state.astype(BF16))

    offs, _ = _gla_rows(c)
    for s in range(c):
        r0, t0 = offs[s]
        d = b[t0:] - b[s:s + 1]
        e = jnp.exp(jnp.where(tt[t0:] >= s, d, -jnp.inf))
        prod_scr[r0:r0 + c - t0, :] = (q[t0:] * k[s:s + 1] * e).astype(BF16)
    res_scr[...] = _dot(prod_scr[...], bm_ref[...])
    pieces = []
    for tb in range(0, c, 8):
        hi = min(tb + 8, c)
        acc = inter[tb:hi]
        for s in range(hi):
            r0, t0 = offs[s]
            if t0 <= tb:
                acc = acc + res_scr[r0 + tb - t0:r0 + hi - t0, :] * v[s:s + 1]
        pieces.append(acc)
    o = pieces[0] if len(pieces) == 1 else jnp.concatenate(pieces, axis=0)

    bl = b[c - 1:c]
    kd = (k * jnp.exp(bl - b)).astype(BF16)
    if c >= 16:
        upd = _dot(v.T.astype(BF16), kd)
    else:
        upd = jnp.dot(v.T, kd.astype(F32), preferred_element_type=F32)
    new_state = jnp.exp(bl) * state + upd * bmask_ref[...]
    s_scr[...] = new_state

    o_ref[...] = _seg_rms(o, gn_ref[...], sm_ref[...]) * rs_ref[...]

    @pl.when(ci == nchunks - 1)
    def _():
        sout_ref[...] = new_state


def _gla(qg, kg, la, vg, rs, s0_bd, lw, *, batch, seq, c):
    nchunks = seq // c
    _, npack = _gla_rows(c)
    npad = -(-npack // 16) * 16
    blk = lambda w: pl.BlockSpec((None, c, w), lambda b, i: (b, i, 0))
    full = lambda a: pl.BlockSpec(a.shape, lambda b, i: (0,) * a.ndim)
    consts = [lw["gla_gn"], lw["gla_bm"], lw["gla_bmask"], lw["sm"]]
    st = pl.BlockSpec((None, GLA_V, GLA_QK), lambda b, i: (b, 0, 0))
    return pl.pallas_call(
        functools.partial(_gla_kernel, c=c, nchunks=nchunks),
        out_shape=[jax.ShapeDtypeStruct((batch, seq, GLA_V), F32), jax.ShapeDtypeStruct((batch, GLA_V, GLA_QK), F32)],
        grid=(batch, nchunks),
        in_specs=[blk(GLA_QK), blk(GLA_QK), blk(GLA_QK), blk(GLA_V), blk(GLA_V)] + [full(a) for a in consts] + [st],
        out_specs=[blk(GLA_V), st],
        scratch_shapes=[pltpu.VMEM((GLA_V, GLA_QK), F32), pltpu.VMEM((npad, GLA_QK), BF16),
                        pltpu.VMEM((npad, GLA_V), F32)],
        compiler_params=_cparams(("parallel", "arbitrary")),
        name="gla",
    )(qg, kg, la, vg, rs, *consts, s0_bd)


def _outproj_kernel(x_ref, oa_ref, ob_ref, u_ref, vn_ref, ws_ref, bias_ref, wo_ref, o_ref, *, tm):
    lane = lax.broadcasted_iota(jnp.int32, (GM_CHUNK, GM_W), 1)
    tri = (lax.broadcasted_iota(jnp.int32, (GM_CHUNK, GM_CHUNK), 0)
           >= lax.broadcasted_iota(jnp.int32, (GM_CHUNK, GM_CHUNK), 1))
    zs = []
    for cb in range(tm // GM_CHUNK):
        vn = vn_ref[cb * GM_CHUNK:(cb + 1) * GM_CHUNK, :]
        z = bias_ref[...]
        for g in range(GM_GROUPS):
            wm = jnp.where(tri, ws_ref[g], 0.0).astype(BF16)
            vg = jnp.where((lane >= g * GM_CH) & (lane < (g + 1) * GM_CH), vn, 0.0).astype(BF16)
            z = z + _dot(wm, vg)
        zs.append(z)
    z = zs[0] if len(zs) == 1 else jnp.concatenate(zs, axis=0)
    oc = u_ref[...] * z
    y = _dot(oa_ref[...].astype(BF16), wo_ref[0:NSA_Q, :])
    y = y + _dot(ob_ref[...].astype(BF16), wo_ref[NSA_Q:NSA_Q + GLA_V, :])
    y = y + _dot(oc.astype(BF16), wo_ref[NSA_Q + GLA_V:MIX_OUT, :])
    o_ref[...] = x_ref[...] + y


def _outproj(x, oa, ob, u, vn, ws, bias, wo, li, *, tm):
    m = x.shape[0]
    row = lambda w: pl.BlockSpec((tm, w), lambda i: (i, 0))
    return pl.pallas_call(
        functools.partial(_outproj_kernel, tm=tm),
        out_shape=jax.ShapeDtypeStruct((m, D_MODEL), F32),
        grid=(m // tm,),
        in_specs=[row(D_MODEL), row(NSA_Q), row(GLA_V), row(GM_W), row(GM_W),
                  pl.BlockSpec(ws.shape, lambda i: (0, 0, 0)), pl.BlockSpec(bias.shape, lambda i: (0, 0)),
                  pl.BlockSpec((None, MIX_OUT, D_MODEL), lambda i: (li, 0, 0))],
        out_specs=row(D_MODEL),
        compiler_params=_cparams(("parallel",)),
        name="outproj",
    )(x, oa, ob, u, vn, ws, bias, wo)


def _pcompress_kernel(pt_ref, *refs, npg):
    pages = refs[:npg]
    perm_ref, w_ref, out_ref, z_scr = refs[npg:npg + 4]
    perm = perm_ref[...]
    for pp in range(npg // 2):
        xt = jnp.concatenate([pages[2 * pp][...], pages[2 * pp + 1][...]], axis=1).astype(BF16)
        y = _dot_nt(perm, xt).astype(BF16)
        for s in range(CMP_STRIDE):
            z_scr[pp * 16:(pp + 1) * 16, s * 256:(s + 1) * 256] = y[s * 16:(s + 1) * 16, :]
    out_ref[...] = _dot(z_scr[...], w_ref[...])


def _pcompress(cache_t, page_table, li, perm, w, *, npg):
    bs, n_pages = page_table.shape
    page = cache_t.shape[-1]
    nc = n_pages * page // CMP_STRIDE
    steps = n_pages // npg
    cpp = page // CMP_STRIDE

    def page_spec(k):
        return pl.BlockSpec((None, None, 2 * LANES, page), lambda b, h, pt: (li, pt[b, h * npg + k], 0, 0))

    grid_spec = pltpu.PrefetchScalarGridSpec(
        num_scalar_prefetch=1,
        grid=(bs, steps),
        in_specs=[page_spec(k) for k in range(npg)] + [
            pl.BlockSpec(perm.shape, lambda b, h, pt: (0, 0)), pl.BlockSpec(w.shape, lambda b, h, pt: (0, 0))],
        out_specs=pl.BlockSpec((None, npg * cpp, w.shape[1]), lambda b, h, pt: (b, h, 0)),
        scratch_shapes=[pltpu.VMEM((npg * cpp, w.shape[0]), BF16)],
    )
    return pl.pallas_call(
        functools.partial(_pcompress_kernel, npg=npg),
        out_shape=jax.ShapeDtypeStruct((bs, nc, w.shape[1]), F32),
        grid_spec=grid_spec,
        compiler_params=_cparams(("parallel", "arbitrary")),
        name="nsa_page_compress",
    )(page_table, *([cache_t] * npg), perm, w)


def _ctail_kernel(a_ref, b_ref, bias_ref, w2_ref, gkc_ref, sm_ref, kct_ref, vcr_ref):
    h = a_ref[...] + b_ref[...] + bias_ref[...]
    out = _dot(_gelu(h).astype(BF16), w2_ref[...])
    kc = _seg_rms(out[:, 0:LANES], gkc_ref[...], sm_ref[...])
    kct_ref[...] = kc.T.astype(BF16)
    vcr_ref[...] = out[:, LANES:2 * LANES].astype(BF16)


def _ctail(hid_ab, hid_b_next, bias, w2, gkc, sm):
    bs, nc, w2x = hid_ab.shape
    half = w2x // 2
    full = lambda a: pl.BlockSpec(a.shape, lambda b: (0,) * a.ndim)
    return pl.pallas_call(
        _ctail_kernel,
        out_shape=[jax.ShapeDtypeStruct((bs, LANES, nc), BF16), jax.ShapeDtypeStruct((bs, nc, LANES), BF16)],
        grid=(bs,),
        in_specs=[pl.BlockSpec((None, nc, half), lambda b: (b, 0, 0)), pl.BlockSpec((None, nc, half), lambda b: (b, 0, 0)),
                  full(bias), full(w2), full(gkc), full(sm)],
        out_specs=[pl.BlockSpec((None, LANES, nc), lambda b: (b, 0, 0)), pl.BlockSpec((None, nc, LANES), lambda b: (b, 0, 0))],
        compiler_params=_cparams(("parallel",)),
        name="nsa_compress_tail",
    )(hid_ab, hid_b_next, bias, w2, gkc, sm)


def _nsa_sample_kernel(pt_ref, qp_ref, qrp_ref, gcol_ref, kct_ref, vcr_ref, cov_ref, e_ref, *refs,
                       npg, nsteps, nc, ns_tot, past_len, ts):
    pages = refs[:npg]
    (wint_ref, newk_ref, neww_ref, newwsh_ref, o_ref, winout_ref,
     bias_scr, m_scr, l_scr, acc_scr, oc_scr, ow_scr) = refs[npg:]
    step = pl.program_id(1)
    nrow = NSA_KV_HEADS * NSA_REP * 8
    q8 = lax.broadcasted_iota(jnp.int32, (nrow, 1), 0) % 8
    t_row = past_len + q8
    qrp = qrp_ref[...]
    n_keep = wint_ref.shape[-1]
    n_top = min(N_SEL, ns_tot)
    nsp = bias_scr.shape[1]

    def expand_rows(a):
        return jnp.concatenate([a[0:8]] * NSA_REP + [a[8:16]] * NSA_REP, axis=0)

    @pl.when(step == 0)
    def _():
        s = _dot(qp_ref[...], kct_ref[...])
        ci = lax.broadcasted_iota(jnp.int32, (1, nc), 1)
        cmask = (ci * CMP_STRIDE + (CMP_BLOCK - 1) <= t_row) & (ci < nc - 1)
        s = jnp.where(cmask, s, -jnp.inf)
        m = jnp.max(s, axis=1, keepdims=True)
        m = jnp.where(m == -jnp.inf, 0.0, m)
        e = jnp.exp(s - m)
        p = e * (1.0 / jnp.maximum(jnp.sum(e, axis=1, keepdims=True), 1e-30))
        oc_scr[...] = _dot(p.astype(BF16), vcr_ref[...])
        sjl = lax.broadcasted_iota(jnp.int32, (1, nsp), 1)
        tq = past_len + lax.broadcasted_iota(jnp.int32, (8, 1), 0)
        bt = tq // SEL_BLOCK
        allowed = (sjl * SEL_BLOCK <= tq) & (sjl < ns_tot)
        forced = (sjl == 0) | (sjl == bt) | (sjl == bt - 1)
        for g in range(NSA_KV_HEADS):
            base = g * NSA_REP * 8
            psum = p[base:base + 8]
            for r in range(1, NSA_REP):
                psum = psum + p[base + r * 8:base + (r + 1) * 8]
            imp = _dot(psum.astype(BF16), cov_ref[...])
            score = jnp.where(allowed, imp + jnp.where(forced, np.float32(SEL_BONUS), 0.0), -jnp.inf)
            rank = jnp.zeros((8, nsp), jnp.int32)
            for k in range(ns_tot):
                col = score[:, k:k + 1]
                later = jnp.where(sjl > k, 1, 0)
                rank = rank + jnp.where(col > score, 1, 0) + jnp.where(col == score, later, 0)
            keep = (rank < n_top) & (score > -jnp.inf)
            bias_scr[g * 8:(g + 1) * 8, :] = jnp.where(keep, 0.0, -jnp.inf)
        m_scr[...] = jnp.full(m_scr.shape, NEG_BIG, F32)
        l_scr[...] = jnp.zeros(l_scr.shape, F32)
        acc_scr[...] = jnp.zeros(acc_scr.shape, F32)

        lane_w = lax.broadcasted_iota(jnp.int32, (1, n_keep), 1)
        diff = t_row - (past_len - n_keep + lane_w)
        s_w = jnp.where((diff >= 0) & (diff < WINDOW), _dot(qrp, wint_ref[0:LANES, :].astype(BF16)), -jnp.inf)
        lane_n = lax.broadcasted_iota(jnp.int32, (1, LANES), 1)
        diff_n = t_row - (past_len + lane_n)
        s_n = jnp.where((lane_n < ts) & (diff_n >= 0) & (diff_n < WINDOW),
                        _dot(qrp, neww_ref[0:LANES, :].astype(BF16)), -jnp.inf)
        sw = jnp.concatenate([s_w, s_n], axis=1)
        mw = jnp.max(sw, axis=1, keepdims=True)
        mw = jnp.where(mw == -jnp.inf, 0.0, mw)
        ew = jnp.exp(sw - mw)
        pw = ew * (1.0 / jnp.maximum(jnp.sum(ew, axis=1, keepdims=True), 1e-30))
        ow_scr[...] = (_dot_nt(pw[:, 0:n_keep].astype(BF16), wint_ref[LANES:2 * LANES, :].astype(BF16))
                       + _dot_nt(pw[:, n_keep:].astype(BF16), neww_ref[LANES:2 * LANES, :].astype(BF16)))
        rolled = pltpu.roll(wint_ref[...], n_keep - ts, axis=1)
        lane_o = lax.broadcasted_iota(jnp.int32, (2 * LANES, LANES), 1)
        winout_ref[:, 0:n_keep - LANES] = rolled[:, 0:n_keep - LANES]
        winout_ref[:, n_keep - LANES:n_keep] = jnp.where(lane_o >= LANES - ts, newwsh_ref[...],
                                                         rolled[:, n_keep - LANES:n_keep])

    def online(s, vt):
        m_old = m_scr[:, 0:1]
        m_new = jnp.maximum(m_old, jnp.max(s, axis=1, keepdims=True))
        p = jnp.exp(s - m_new)
        alpha = jnp.exp(m_old - m_new)
        l_new = alpha * l_scr[:, 0:1] + jnp.sum(p, axis=1, keepdims=True)
        acc_scr[...] = alpha * acc_scr[...] + _dot_nt(p.astype(BF16), vt)
        m_scr[...] = jnp.broadcast_to(m_new, m_scr.shape)
        l_scr[...] = jnp.broadcast_to(l_new, l_scr.shape)

    sel01 = jnp.where(bias_scr[...] == 0.0, 1.0, 0.0).astype(BF16)
    bexp = _dot(sel01, e_ref[...])
    bias = expand_rows(jnp.where(bexp > 0.5, 0.0, -jnp.inf))
    kt = jnp.concatenate([pg[0:LANES, :] for pg in pages], axis=1).astype(BF16)
    vt = jnp.concatenate([pg[LANES:2 * LANES, :] for pg in pages], axis=1).astype(BF16)
    online(_dot(qrp, kt) + bias, vt)

    @pl.when(step == nsteps - 1)
    def _():
        lane_n = lax.broadcasted_iota(jnp.int32, (1, LANES), 1)
        bcol = expand_rows(bias_scr[:, ns_tot - 1:ns_tot])
        ok = (lane_n < ts) & (past_len + lane_n <= t_row)
        s_n = jnp.where(ok, _dot(qrp, newk_ref[2 * LANES:3 * LANES, :].astype(BF16)) + bcol, -jnp.inf)
        online(s_n, newk_ref[3 * LANES:4 * LANES, :].astype(BF16))
        o_s = acc_scr[...] * (1.0 / jnp.maximum(l_scr[:, 0:1], 1e-30))
        gc = gcol_ref[...]
        o_ref[...] = oc_scr[...] * gc[:, 0:1] + o_s * gc[:, 1:2] + ow_scr[...] * gc[:, 2:3]


def _nsa_sample(cache_t, page_table, li, qp, qrp, gcol, kct, vcr, cov_s, emat, win_t, newk_t, neww_t, newwsh_t, *,
                npg, past_len, ts):
    bs, n_pages = page_table.shape
    page = cache_t.shape[-1]
    nsteps = n_pages // npg
    nc = kct.shape[-1]
    ns_tot = -(-(past_len + ts) // SEL_BLOCK)
    nsp = cov_s.shape[1]
    n_keep = win_t.shape[-1]
    nrow = qp.shape[1]
    per_b = lambda a: pl.BlockSpec((None,) + a.shape[1:], lambda b, h, pt: (b,) + (0,) * (a.ndim - 1))

    def page_spec(k):
        return pl.BlockSpec((None, None, 2 * LANES, page), lambda b, h, pt: (li, pt[b, h * npg + k], 1, 0))

    grid_spec = pltpu.PrefetchScalarGridSpec(
        num_scalar_prefetch=1,
        grid=(bs, nsteps),
        in_specs=[per_b(qp), per_b(qrp), per_b(gcol), per_b(kct), per_b(vcr),
                  pl.BlockSpec(cov_s.shape, lambda b, h, pt: (0, 0)),
                  pl.BlockSpec((nsp, npg * page), lambda b, h, pt: (0, h))]
                 + [page_spec(k) for k in range(npg)]
                 + [pl.BlockSpec((None, None, 2 * LANES, n_keep), lambda b, h, pt: (li, b, 0, 0)),
                    per_b(newk_t), per_b(neww_t), per_b(newwsh_t)],
        out_specs=[pl.BlockSpec((None, nrow, LANES), lambda b, h, pt: (b, 0, 0)),
                   pl.BlockSpec((None, 2 * LANES, n_keep), lambda b, h, pt: (b, 0, 0))],
        scratch_shapes=[pltpu.VMEM((2 * 8, nsp), F32)] + [pltpu.VMEM((nrow, LANES), F32)] * 5,
    )
    return pl.pallas_call(
        functools.partial(_nsa_sample_kernel, npg=npg, nsteps=nsteps, nc=nc, ns_tot=ns_tot, past_len=past_len, ts=ts),
        out_shape=[jax.ShapeDtypeStruct((bs, nrow, LANES), F32), jax.ShapeDtypeStruct((bs, 2 * LANES, n_keep), F32)],
        grid_spec=grid_spec,
        compiler_params=_cparams(("parallel", "arbitrary")),
        name="nsa_sample",
    )(page_table, qp, qrp, gcol, kct, vcr, cov_s, emat, *([cache_t] * npg), win_t, newk_t, neww_t, newwsh_t)


def _pad_cols(w):
    cuts = np.cumsum((0,) + IN_SPLITS)
    parts = []
    for i, (n, p) in enumerate(zip(IN_SPLITS, IN_PADDED)):
        seg = w[..., cuts[i]:cuts[i] + n]
        if p != n:
            seg = jnp.pad(seg, [(0, 0)] * (w.ndim - 1) + [(0, p - n)])
        parts.append(seg)
    return jnp.concatenate(parts, axis=-1)


def _rope_tables(pos):
    half = HEAD_DIM // 2
    inv = 1.0 / (ROPE_THETA ** (jnp.arange(half, dtype=F32) * (2.0 / HEAD_DIM)))
    ang = pos.astype(F32)[:, None] * inv[None, :]
    cos = jnp.cos(ang)
    sin = jnp.sin(ang)
    cos_f = jnp.concatenate([cos, cos, cos, cos], axis=1)
    sin_f = jnp.concatenate([-sin, sin, -sin, sin], axis=1)
    return cos_f, sin_f


def _cover_t(seq):
    nc = seq // CMP_STRIDE
    ns = seq // SEL_BLOCK
    ci = np.arange(nc)[None, :]
    sj = np.arange(ns)[:, None]
    cov = ((ci * CMP_STRIDE <= sj * SEL_BLOCK + SEL_BLOCK - 1)
           & (ci * CMP_STRIDE + CMP_BLOCK - 1 >= sj * SEL_BLOCK) & (ci < nc - 1))
    return jnp.asarray(cov, dtype=BF16)


def _layer_weights(l, ln_gains, w_in_p, nsa_qk_norm, nsa_cmp_pos, nsa_cmp_w1, nsa_cmp_w2, gla_gate_w, gla_gate_b,
                   gla_norm, gm_ln, gm_ws, gm_b):
    eye2 = jnp.eye(NSA_KV_HEADS, dtype=F32)
    lw = {"ln0": ln_gains[l, 0][None], "ln1": ln_gains[l, 1][None], "ln2": ln_gains[l, 2][None], "w_in": w_in_p[l]}
    lw["gq"] = jnp.tile(nsa_qk_norm[l, 0], NSA_HEADS)[None]
    lw["gkc"] = jnp.tile(nsa_qk_norm[l, 1], NSA_KV_HEADS)[None]
    lw["gks"] = jnp.tile(nsa_qk_norm[l, 2], NSA_KV_HEADS)[None]
    lw["gkw"] = jnp.tile(nsa_qk_norm[l, 3], NSA_KV_HEADS)[None]
    seg = (np.arange(LANES)[:, None] // HEAD_DIM) == (np.arange(LANES)[None, :] // HEAD_DIM)
    lw["sm"] = jnp.asarray(seg * (1.0 / HEAD_DIM), dtype=BF16)
    lw["gla_gw"] = jnp.pad(gla_gate_w[l], ((0, LANES - GLA_GATE_RANK), (0, 0))).astype(BF16)
    lw["gla_gb"] = gla_gate_b[l][None]
    lw["gm_lng"] = gm_ln[l, 0][None]
    lw["gm_lnb"] = gm_ln[l, 1][None]
    pos_rows = []
    for c, nm in ((0, "k"), (1, "v")):
        w1 = nsa_cmp_w1[l, c].reshape(CMP_BLOCK, HEAD_DIM, CMP_HIDDEN)
        for half, tag in ((w1[:CMP_STRIDE], "a"), (w1[CMP_STRIDE:], "b")):
            wx = jnp.einsum("sdh,pg->spdgh", half, eye2)
            lw["cmp_w" + tag + nm] = wx.reshape(CMP_STRIDE * LANES, NSA_KV_HEADS * CMP_HIDDEN).astype(BF16)
        lw["cmp_w2" + nm] = jnp.einsum("hd,pg->phgd", nsa_cmp_w2[l, c], eye2).reshape(
            NSA_KV_HEADS * CMP_HIDDEN, LANES).astype(BF16)
        pe = nsa_cmp_pos[l, c]
        for half in (pe[:CMP_STRIDE], pe[CMP_STRIDE:]):
            pos_rows.append(jnp.broadcast_to(half[:, None, :], (CMP_STRIDE, NSA_KV_HEADS, HEAD_DIM)).reshape(-1))
    lw["cmp_pos"] = jnp.stack(pos_rows)
    w1ab = nsa_cmp_w1[l].reshape(2, 2, CMP_STRIDE, HEAD_DIM, CMP_HIDDEN)
    lw["pc_w"] = jnp.einsum("casdh,cx,gy->scgdaxyh", w1ab, eye2, eye2).reshape(
        CMP_STRIDE * 2 * LANES, 2 * 2 * NSA_KV_HEADS * CMP_HIDDEN).astype(BF16)
    pb = jnp.einsum("ck,ckh->ch", nsa_cmp_pos[l].reshape(2, -1), nsa_cmp_w1[l], precision=lax.Precision.HIGHEST)
    lw["pc_bias"] = jnp.broadcast_to(pb[:, None, :], (2, NSA_KV_HEADS, CMP_HIDDEN)).reshape(1, -1)
    lw["pc_w2"] = jnp.einsum("chd,cx,gy->cghxyd", nsa_cmp_w2[l], eye2, eye2).reshape(
        2 * NSA_KV_HEADS * CMP_HIDDEN, 2 * LANES).astype(BF16)
    lw["gla_gn"] = jnp.tile(gla_norm[l], GLA_HEADS)[None]
    hq = np.arange(GLA_QK) // GLA_DK
    hv = np.arange(GLA_V) // GLA_DV
    lw["gla_bm"] = jnp.asarray(hq[:, None] == hv[None, :], dtype=BF16)
    lw["gla_bmask"] = jnp.asarray(hv[:, None] == hq[None, :], dtype=F32)
    lw["gm_ws"] = gm_ws[l]
    lw["gm_bias"] = jnp.repeat(gm_b[l].T, GM_CH, axis=1)
    return lw


def _page_perm(page):
    cpp = page // CMP_STRIDE
    r = np.arange(2 * page)
    s_, rem = r // (2 * cpp), r % (2 * cpp)
    t = (rem // cpp) * page + CMP_STRIDE * (rem % cpp) + s_
    m = np.zeros((2 * page, 2 * page), np.float32)
    m[r, t] = 1.0
    return jnp.asarray(m, dtype=BF16)


def _cover_sample(past_len, ts, nsp):
    t_tot = past_len + ts
    n_c = (t_tot - CMP_BLOCK) // CMP_STRIDE + 1
    nc = past_len // CMP_STRIDE
    n_s = -(-t_tot // SEL_BLOCK)
    ci = np.arange(nc)[:, None]
    sj = np.arange(nsp)[None, :]
    cov = ((ci * CMP_STRIDE <= sj * SEL_BLOCK + SEL_BLOCK - 1) & (ci * CMP_STRIDE + CMP_BLOCK - 1 >= sj * SEL_BLOCK)
           & (ci < n_c) & (sj < n_s))
    emat = (np.arange(past_len)[None, :] // SEL_BLOCK) == np.arange(nsp)[:, None]
    return jnp.asarray(cov, dtype=BF16), jnp.asarray(emat, dtype=BF16)


def _rows_gr8(a, ts):
    bs = a.shape[0]
    a = a.reshape(bs, ts, NSA_KV_HEADS, NSA_REP, a.shape[-1]).transpose(0, 2, 3, 1, 4)
    return jnp.pad(a, ((0, 0), (0, 0), (0, 0), (0, 8 - ts), (0, 0)))


def _state_to_t(s):
    b = s.shape[0]
    eye = jnp.eye(GLA_HEADS, dtype=s.dtype)
    return jnp.einsum("bhkv,hg->bhvgk", s, eye).reshape(b, GLA_V, GLA_QK)


def _state_from_t(st):
    b = st.shape[0]
    s5 = st.reshape(b, GLA_HEADS, GLA_DV, GLA_HEADS, GLA_DK)
    d = jnp.stack([s5[:, h, :, h, :] for h in range(GLA_HEADS)], axis=1)
    return jnp.swapaxes(d, 2, 3)


def kernel(x_prompt, x_sample, cache_kv, cache_win_kv, state_gla, page_table, ln_gains, ffn_w_gate_up, ffn_w_down,
           w_in, w_out, nsa_qk_norm, nsa_cmp_pos, nsa_cmp_w1, nsa_cmp_w2, gla_gate_w, gla_gate_b, gla_norm, gm_ln,
           gm_ws, gm_b):
    depth = w_in.shape[0]
    bp, tp, _ = x_prompt.shape
    bs, ts, _ = x_sample.shape
    n_pages = page_table.shape[1]
    page = cache_kv.shape[2]
    past_len = n_pages * page
    mp, ms = bp * tp, bs * ts
    nc = tp // CMP_STRIDE

    w_gu = ffn_w_gate_up.astype(BF16).reshape(depth * 2, D_MODEL, 2 * D_FF)
    w_d = ffn_w_down.astype(BF16).reshape(depth * 2, D_FF, D_MODEL)
    w_in_p = _pad_cols(w_in).astype(BF16)
    w_o = w_out.astype(BF16)
    cos_p, sin_p = _rope_tables(jnp.arange(tp))
    cos_s, sin_s = _rope_tables(past_len + jnp.arange(ms) % ts)
    cov = _cover_t(tp)
    tm_p = 512 if mp % 512 == 0 else Q_BLOCK
    tm_i = 256 if tp % 256 == 0 else Q_BLOCK
    tf = 1408

    nrow = NSA_HEADS * 8
    n_keep_s = cache_win_kv.shape[2]
    nsp = -(-(past_len // SEL_BLOCK + 1) // LANES) * LANES
    cov_s, emat = _cover_sample(past_len, ts, nsp)
    perm = _page_perm(page)
    eye2 = jnp.eye(NSA_KV_HEADS, dtype=F32)
    npg_c = min(32, n_pages)
    npg_a = min(16, n_pages)
    cache_t = jnp.transpose(cache_kv, (0, 1, 3, 4, 5, 2)).reshape(depth, cache_kv.shape[1], 4 * LANES, page)
    win_t = jnp.transpose(cache_win_kv, (0, 1, 3, 4, 5, 2)).reshape(depth, bs, 2 * LANES, n_keep_s)
    nxt = lambda z: jnp.concatenate([z[:, 1:], jnp.zeros_like(z[:, :1])], axis=1)

    xp = x_prompt.reshape(mp, D_MODEL)
    xs = x_sample.reshape(ms, D_MODEL)
    kv_p, win_p, gla_p, kv_s, win_s, gla_s, gmv_s = [], [], [], [], [], [], []
    eye_b = jnp.eye(bs, dtype=F32)
    for l in range(depth):
        lw = _layer_weights(l, ln_gains, w_in_p, nsa_qk_norm, nsa_cmp_pos, nsa_cmp_w1, nsa_cmp_w2, gla_gate_w,
                            gla_gate_b, gla_norm, gm_ln, gm_ws, gm_b)
        xp = _ffn(xp, lw["ln0"], w_gu, w_d, 2 * l, tm=tm_p, tf=tf)
        (newkv, newwin, _, qg, kg, la, vg, rs, u, vn, qs, qrs, gt, ksel_r, vsel_t, kwin_r, vwin_t, kcmp,
         vcmp) = _inproj(xp, lw, cos_p, sin_p, tm=tm_i, attn_layout=True, batch=bp, seq=tp)
        zk = kcmp.reshape(bp, nc, CMP_STRIDE * LANES)
        zv = vcmp.reshape(bp, nc, CMP_STRIDE * LANES)
        kc, vct = _compress(zk, nxt(zk), zv, nxt(zv), lw, rows=min(128, nc))
        oa = _nsa_prompt(qs, qrs, gt, kc, vct, cov, ksel_r.reshape(bp, tp, LANES), vsel_t,
                         kwin_r.reshape(bp, tp, LANES), vwin_t, batch=bp, seq=tp)
        r3 = lambda a: a.reshape(bp, tp, a.shape[-1])
        ob, st = _gla(r3(qg), r3(kg), r3(la), r3(vg), r3(rs), jnp.zeros((bp, GLA_V, GLA_QK), F32), lw,
                      batch=bp, seq=tp, c=GLA_CHUNK)
        xp = _outproj(xp, oa, ob.reshape(mp, GLA_V), u, vn, lw["gm_ws"], lw["gm_bias"], w_o, l, tm=tm_p)
        xp = _ffn(xp, lw["ln2"], w_gu, w_d, 2 * l + 1, tm=tm_p, tf=tf)
        kv_p.append(newkv.reshape(bp, 4, NSA_KV_HEADS, HEAD_DIM, tp).transpose(0, 4, 1, 2, 3))
        n_keep = min(WINDOW, tp)
        win_p.append(newwin[:, :, tp - n_keep:].reshape(bp, 2, NSA_KV_HEADS, HEAD_DIM, n_keep).transpose(0, 4, 1, 2, 3))
        gla_p.append(_state_from_t(st))

        xs = _ffn(xs, lw["ln0"], w_gu, w_d, 2 * l, tm=ms, tf=tf)
        (newkv, newwin, gates, qg, kg, la, vg, rs, u, vn, qn, qr) = _inproj(
            xs, lw, cos_s, sin_s, tm=ms, attn_layout=False, batch=bs, seq=ts)
        hid = _pcompress(cache_t, page_table, l, perm, lw["pc_w"], npg=npg_c)
        hb = hid[:, :, hid.shape[2] // 2:]
        kct, vcr = _ctail(hid, nxt(hb), lw["pc_bias"], lw["pc_w2"], lw["gkc"], lw["sm"])
        scale = np.float32(HEAD_DIM ** -0.5)
        to_pad = lambda q: jnp.einsum("bgrqd,gx->bgrqxd", _rows_gr8(q.reshape(bs, ts, NSA_HEADS, HEAD_DIM) * scale, ts),
                                      eye2).reshape(bs, nrow, LANES).astype(BF16)
        gcol = _rows_gr8(gates[:, :3 * NSA_HEADS].reshape(bs, ts, NSA_HEADS, 3), ts).reshape(bs, nrow, 3)
        gcol = jnp.pad(gcol, ((0, 0), (0, 0), (0, LANES - 3)))
        newk_t = jnp.pad(newkv.reshape(bs, ts, -1).transpose(0, 2, 1), ((0, 0), (0, 0), (0, LANES - ts)))
        neww_c = newwin.reshape(bs, ts, -1).transpose(0, 2, 1)
        neww_t = jnp.pad(neww_c, ((0, 0), (0, 0), (0, LANES - ts)))
        newwsh_t = jnp.pad(neww_c, ((0, 0), (0, 0), (LANES - ts, 0)))
        o64, win_o = _nsa_sample(cache_t, page_table, l, to_pad(qn), to_pad(qr), gcol, kct, vcr, cov_s, emat, win_t,
                                 newk_t, neww_t, newwsh_t, npg=npg_a, past_len=past_len, ts=ts)
        o6 = o64.reshape(bs, NSA_KV_HEADS, NSA_REP, 8, NSA_KV_HEADS, HEAD_DIM)
        oa = jnp.stack([o6[:, g, :, :ts, g, :] for g in range(NSA_KV_HEADS)], axis=1)
        oa = oa.transpose(0, 3, 1, 2, 4).reshape(ms, NSA_Q)
        pad8 = lambda a: jnp.pad(a.reshape(bs, ts, a.shape[-1]), ((0, 0), (0, 8 - ts), (0, 0)))
        ob8, st_s = _gla(pad8(qg), pad8(kg), pad8(la), pad8(vg), pad8(rs), _state_to_t(state_gla[l]), lw,
                         batch=bs, seq=8, c=8)
        ob = ob8[:, :ts].reshape(ms, GLA_V)
        ws_s = jnp.einsum("gts,bc->gbtcs", gm_ws[l][:, :ts, :ts], eye_b).reshape(GM_GROUPS, ms, ms)
        bias_s = jnp.tile(lw["gm_bias"][:ts], (bs, 1))
        xs = _outproj(xs, oa, ob, u, vn, ws_s, bias_s, w_o, l, tm=ms)
        xs = _ffn(xs, lw["ln2"], w_gu, w_d, 2 * l + 1, tm=ms, tf=tf)
        kv_s.append(newkv.reshape(bs, ts, 4, NSA_KV_HEADS, HEAD_DIM))
        win_s.append(win_o.reshape(bs, 2, NSA_KV_HEADS, HEAD_DIM, n_keep_s).transpose(0, 4, 1, 2, 3))
        gla_s.append(_state_from_t(st_s))
        gmv_s.append(vn.reshape(bs, ts, GM_GROUPS, GM_CH))
    return (xp.reshape(bp, tp, D_MODEL), xs.reshape(bs, ts, D_MODEL), jnp.stack(kv_p), jnp.stack(win_p),
            jnp.stack(gla_p), jnp.stack(kv_s), jnp.stack(win_s), jnp.stack(gla_s), jnp.stack(gmv_s))
```

```python
import functools

import numpy as np
import jax
import jax.numpy as jnp
from jax import lax
from jax.experimental import pallas as pl
from jax.experimental.pallas import tpu as pltpu

F32 = jnp.float32
BF16 = jnp.bfloat16

D_MODEL = 1024
HEAD_DIM = 64
NSA_HEADS = 8
NSA_KV_HEADS = 2
NSA_REP = NSA_HEADS // NSA_KV_HEADS
CMP_STRIDE = 16
CMP_BLOCK = 2 * CMP_STRIDE
CMP_HIDDEN = 128
SEL_BLOCK = 64
N_SEL = 16
WINDOW = 512
Q_BLOCK = 128
SEL_BONUS = 1.0e4
GLA_HEADS = 4
GLA_DK = 32
GLA_DV = 64
GLA_GATE_RANK = 16
GLA_GATE_TEMP = 16.0
GLA_CHUNK = 64
GM_GROUPS = 4
GM_CH = 64
GM_CHUNK = 128
D_FF = 2816
ROPE_THETA = 10000.0
EPS = 1e-6

NSA_Q = NSA_HEADS * HEAD_DIM
NSA_KV = NSA_KV_HEADS * HEAD_DIM
GLA_QK = GLA_HEADS * GLA_DK
GLA_V = GLA_HEADS * GLA_DV
GM_W = GM_GROUPS * GM_CH
MIX_OUT = NSA_Q + GLA_V + GM_W
IN_SPLITS = (NSA_Q, 6 * NSA_KV, 3 * NSA_HEADS, GLA_QK, GLA_QK, GLA_V, GLA_GATE_RANK, GLA_V, GM_W, GM_W)
IN_PADDED = tuple(-(-s // 128) * 128 for s in IN_SPLITS)
IN_OFFS = tuple(int(v) for v in np.cumsum((0,) + IN_PADDED))
D_IN_PAD = IN_OFFS[-1]

LANES = 128
NEG_BIG = -1.0e30
VMEM_LIMIT = 56 * 1024 * 1024


def _cparams(sem):
    return pltpu.CompilerParams(dimension_semantics=sem, vmem_limit_bytes=VMEM_LIMIT)


def _gelu(x):
    c = np.float32(np.sqrt(2.0 / np.pi))
    return x * (0.5 * (1.0 + jnp.tanh(c * (x + 0.044715 * (x * x * x)))))


def _sigmoid(x):
    return 1.0 / (1.0 + jnp.exp(-x))


def _dot(a, b):
    return jnp.dot(a, b, preferred_element_type=F32)


def _dot_nt(a, b):
    return lax.dot_general(a, b, (((1,), (1,)), ((), ())), preferred_element_type=F32)


def _seg_mean_sq(x, sm):
    sq = x * x
    hi = sq.astype(BF16)
    lo = (sq - hi.astype(F32)).astype(BF16)
    outs = []
    for c in range(x.shape[1] // LANES):
        sl = slice(c * LANES, (c + 1) * LANES)
        outs.append(_dot(hi[:, sl], sm) + _dot(lo[:, sl], sm))
    return outs[0] if len(outs) == 1 else jnp.concatenate(outs, axis=1)


def _seg_rms(x, gain, sm):
    return x * lax.rsqrt(_seg_mean_sq(x, sm) + EPS) * gain


def _tile_lanes(a, w):
    n = w // a.shape[1]
    return a if n == 1 else jnp.concatenate([a] * n, axis=1)


def _rope(x, cos, sin_signed):
    w = x.shape[1]
    lane = lax.broadcasted_iota(jnp.int32, x.shape, 1)
    fwd = pltpu.roll(x, w - HEAD_DIM // 2, axis=1)
    bwd = pltpu.roll(x, HEAD_DIM // 2, axis=1)
    partner = jnp.where((lane % HEAD_DIM) < HEAD_DIM // 2, fwd, bwd)
    return x * _tile_lanes(cos, w) + partner * _tile_lanes(sin_signed, w)


def _ffn_kernel(x_ref, g_ref, wg_ref, wu_ref, wd_ref, o_ref, h_scr, acc_scr, *, nj):
    j = pl.program_id(1)

    @pl.when(j == 0)
    def _():
        x = x_ref[...]
        ms = jnp.mean(x * x, axis=-1, keepdims=True)
        h_scr[...] = (x * lax.rsqrt(ms + EPS) * g_ref[...]).astype(BF16)
        acc_scr[...] = jnp.zeros_like(acc_scr)

    h = h_scr[...]
    g = _dot(h, wg_ref[...])
    u = _dot(h, wu_ref[...])
    a = (g * _sigmoid(g)) * u
    acc_scr[...] += _dot(a.astype(BF16), wd_ref[...])

    @pl.when(j == nj - 1)
    def _():
        o_ref[...] = x_ref[...] + 0.5 * acc_scr[...]


def _ffn(x, gain, w_gu, w_d, li, *, tm, tf):
    m = x.shape[0]
    nj = D_FF // tf
    return pl.pallas_call(
        functools.partial(_ffn_kernel, nj=nj),
        out_shape=jax.ShapeDtypeStruct((m, D_MODEL), F32),
        grid=(m // tm, nj),
        in_specs=[
            pl.BlockSpec((tm, D_MODEL), lambda i, j: (i, 0)),
            pl.BlockSpec((1, D_MODEL), lambda i, j: (0, 0)),
            pl.BlockSpec((None, D_MODEL, tf), lambda i, j: (li, 0, j)),
            pl.BlockSpec((None, D_MODEL, tf), lambda i, j: (li, 0, j + nj)),
            pl.BlockSpec((None, tf, D_MODEL), lambda i, j: (li, j, 0)),
        ],
        out_specs=pl.BlockSpec((tm, D_MODEL), lambda i, j: (i, 0)),
        scratch_shapes=[pltpu.VMEM((tm, D_MODEL), BF16), pltpu.VMEM((tm, D_MODEL), F32)],
        compiler_params=_cparams(("parallel", "arbitrary")),
        name="ffn",
    )(x, gain, w_gu, w_gu, w_d)


def _group_padded(arr, h):
    c = arr[:, (h // 2) * LANES:(h // 2 + 1) * LANES]
    g = h // NSA_REP
    if (h % 2) != g:
        c = pltpu.roll(c, HEAD_DIM, axis=1)
    lane = lax.broadcasted_iota(jnp.int32, c.shape, 1)
    keep = (lane >= g * HEAD_DIM) & (lane < (g + 1) * HEAD_DIM)
    return jnp.where(keep, c, 0.0)


def _inproj_kernel(x_ref, ln_ref, w_ref, gq_ref, gks_ref, gkw_ref, cos_ref, sin_ref, gw_ref, gb_ref,
                   lng_ref, lnb_ref, sm_ref, *outs, tm, attn_layout):
    (newkv_ref, newwin_ref, gates_ref, qg_ref, kg_ref, la_ref, vg_ref, rs_ref, u_ref, vn_ref) = outs[:10]
    x = x_ref[...]
    ms = jnp.mean(x * x, axis=-1, keepdims=True)
    h = (x * lax.rsqrt(ms + EPS) * ln_ref[...]).astype(BF16)
    p = _dot(h, w_ref[...])
    sm = sm_ref[...]
    cos = cos_ref[...]
    sin = sin_ref[...]
    o = IN_OFFS

    def seg(i, a=0, b=None):
        b = IN_PADDED[i] if b is None else b
        return p[:, o[i] + a:o[i] + b]

    qn = _seg_rms(seg(0), gq_ref[...], sm)
    qr = _rope(qn, cos, sin)
    kv = [seg(1, LANES * j, LANES * (j + 1)) for j in range(6)]
    ksel = _rope(_seg_rms(kv[2], gks_ref[...], sm), cos, sin)
    kwin = _rope(_seg_rms(kv[4], gkw_ref[...], sm), cos, sin)
    vsel_t = kv[3].T
    vwin_t = kv[5].T
    if attn_layout:
        for j, a in enumerate((kv[0].T, kv[1].T, ksel.T, vsel_t)):
            newkv_ref[j * LANES:(j + 1) * LANES, :] = a
        newwin_ref[0:LANES, :] = kwin.T
        newwin_ref[LANES:2 * LANES, :] = vwin_t
    else:
        for j, a in enumerate((kv[0], kv[1], ksel, kv[3])):
            newkv_ref[:, j * LANES:(j + 1) * LANES] = a
        newwin_ref[:, 0:LANES] = kwin
        newwin_ref[:, LANES:2 * LANES] = kv[5]
    gates = _sigmoid(seg(2))
    gates_ref[...] = gates
    qg_ref[...] = seg(3) * np.float32(GLA_DK ** -0.5)
    kg_ref[...] = seg(4)
    vg_ref[...] = seg(5)
    logit = _dot(seg(6).astype(BF16), gw_ref[...]) + gb_ref[...]
    log_sig = jnp.minimum(logit, 0.0) - jnp.log1p(jnp.exp(-jnp.abs(logit)))
    la_ref[...] = log_sig * np.float32(1.0 / GLA_GATE_TEMP)
    r = seg(7)
    rs_ref[...] = r * _sigmoid(r)
    u_ref[...] = _gelu(seg(8))
    v = _gelu(seg(9))
    mu = jnp.mean(v, axis=-1, keepdims=True)
    var = jnp.mean(jnp.square(v - mu), axis=-1, keepdims=True)
    vn_ref[...] = (v - mu) * lax.rsqrt(var + EPS) * lng_ref[...] + lnb_ref[...]

    if attn_layout:
        (qs_ref, qrs_ref, gt_ref, kselr_ref, vselt_ref, kwinr_ref, vwint_ref, kcmp_ref, vcmp_ref) = outs[10:]
        scale = np.float32(HEAD_DIM ** -0.5 * np.log2(np.e))
        qs = qn * scale
        qrs = qr * scale
        for hh in range(NSA_HEADS):
            a = _group_padded(qs, hh).astype(BF16)
            b = _group_padded(qrs, hh).astype(BF16)
            for rb in range(tm // Q_BLOCK):
                qs_ref[rb, hh] = a[rb * Q_BLOCK:(rb + 1) * Q_BLOCK]
                qrs_ref[rb, hh] = b[rb * Q_BLOCK:(rb + 1) * Q_BLOCK]
        for rb in range(tm // Q_BLOCK):
            gt_ref[rb] = gates[rb * Q_BLOCK:(rb + 1) * Q_BLOCK].T
        kselr_ref[...] = ksel.astype(BF16)
        vselt_ref[...] = vsel_t.astype(BF16)
        kwinr_ref[...] = kwin.astype(BF16)
        vwint_ref[...] = vwin_t.astype(BF16)
        kcmp_ref[...] = kv[0]
        vcmp_ref[...] = kv[1]
    else:
        qn_ref, qr_ref = outs[10:]
        qn_ref[...] = qn
        qr_ref[...] = qr


def _inproj(x, lw, cos_t, sin_t, *, tm, attn_layout, batch, seq):
    m = x.shape[0]
    nt = m // tm
    ntab = cos_t.shape[0] // tm
    row = lambda w: pl.BlockSpec((tm, w), lambda i: (i, 0))
    full = lambda a: pl.BlockSpec(a.shape, lambda i: (0,) * a.ndim)
    ins = [x, lw["ln1"], lw["w_in"], lw["gq"], lw["gks"], lw["gkw"], cos_t, sin_t, lw["gla_gw"], lw["gla_gb"],
           lw["gm_lng"], lw["gm_lnb"], lw["sm"]]
    in_specs = [row(D_MODEL), full(lw["ln1"]), full(lw["w_in"]), full(lw["gq"]), full(lw["gks"]), full(lw["gkw"]),
                pl.BlockSpec((tm, LANES), lambda i: (i % ntab, 0)), pl.BlockSpec((tm, LANES), lambda i: (i % ntab, 0)),
                full(lw["gla_gw"]), full(lw["gla_gb"]), full(lw["gm_lng"]), full(lw["gm_lnb"]), full(lw["sm"])]
    widths = [512, 256, 128, 128, 128, 128, 256, 256, 256, 256]
    out_shape = [jax.ShapeDtypeStruct((m, w), F32) for w in widths]
    out_specs = [row(w) for w in widths]
    if attn_layout:
        nqb = m // Q_BLOCK
        rpb = tm // Q_BLOCK
        tpb = seq // tm
        for j in range(2):
            out_shape[j] = jax.ShapeDtypeStruct((batch, widths[j], seq), F32)
            out_specs[j] = pl.BlockSpec((None, widths[j], tm), lambda i: (i // tpb, 0, i % tpb))
        out_shape += [jax.ShapeDtypeStruct((nqb, NSA_HEADS, Q_BLOCK, LANES), BF16)] * 2
        out_specs += [pl.BlockSpec((rpb, NSA_HEADS, Q_BLOCK, LANES), lambda i: (i, 0, 0, 0))] * 2
        out_shape += [jax.ShapeDtypeStruct((nqb, LANES, Q_BLOCK), F32)]
        out_specs += [pl.BlockSpec((rpb, LANES, Q_BLOCK), lambda i: (i, 0, 0))]
        rowmaj = (jax.ShapeDtypeStruct((m, LANES), BF16), row(LANES))
        trans = (jax.ShapeDtypeStruct((batch, LANES, seq), BF16),
                 pl.BlockSpec((None, LANES, tm), lambda i: (i // tpb, 0, i % tpb)))
        for sh, sp in (rowmaj, trans, rowmaj, trans):
            out_shape.append(sh)
            out_specs.append(sp)
        out_shape += [jax.ShapeDtypeStruct((m, LANES), F32)] * 2
        out_specs += [row(LANES)] * 2
    else:
        out_shape += [jax.ShapeDtypeStruct((m, NSA_Q), F32)] * 2
        out_specs += [row(NSA_Q)] * 2
    return pl.pallas_call(
        functools.partial(_inproj_kernel, tm=tm, attn_layout=attn_layout),
        out_shape=out_shape,
        grid=(nt,),
        in_specs=in_specs,
        out_specs=out_specs,
        compiler_params=_cparams(("parallel",)),
        name="inproj",
    )(*ins)


def _compress_kernel(zk_ref, zkn_ref, zv_ref, zvn_ref, pos_ref, wak_ref, wbk_ref, wav_ref, wbv_ref,
                     w2k_ref, w2v_ref, gkc_ref, sm_ref, kc_ref, vct_ref):
    pos = pos_ref[...]

    def one(z_ref, zn_ref, pa, pb, wa_ref, wb_ref, w2_ref):
        a = _dot((z_ref[...] + pa).astype(BF16), wa_ref[...])
        b = _dot((zn_ref[...] + pb).astype(BF16), wb_ref[...])
        return _dot(_gelu(a + b).astype(BF16), w2_ref[...])

    ck = one(zk_ref, zkn_ref, pos[0:1], pos[1:2], wak_ref, wbk_ref, w2k_ref)
    cv = one(zv_ref, zvn_ref, pos[2:3], pos[3:4], wav_ref, wbv_ref, w2v_ref)
    kc_ref[...] = _seg_rms(ck, gkc_ref[...], sm_ref[...]).astype(BF16)
    vct_ref[...] = cv.T.astype(BF16)


def _compress(zk, zkn, zv, zvn, lw, *, rows):
    b, nc, kdim = zk.shape
    zspec = pl.BlockSpec((None, rows, kdim), lambda i, j: (i, j, 0))
    full = lambda a: pl.BlockSpec(a.shape, lambda i, j: (0,) * a.ndim)
    ws = [lw["cmp_pos"], lw["cmp_wak"], lw["cmp_wbk"], lw["cmp_wav"], lw["cmp_wbv"], lw["cmp_w2k"], lw["cmp_w2v"],
          lw["gkc"], lw["sm"]]
    return pl.pallas_call(
        _compress_kernel,
        out_shape=[jax.ShapeDtypeStruct((b, nc, LANES), BF16), jax.ShapeDtypeStruct((b, LANES, nc), BF16)],
        grid=(b, nc // rows),
        in_specs=[zspec] * 4 + [full(w) for w in ws],
        out_specs=[pl.BlockSpec((None, rows, LANES), lambda i, j: (i, j, 0)),
                   pl.BlockSpec((None, LANES, rows), lambda i, j: (i, 0, j))],
        compiler_params=_cparams(("parallel", "parallel")),
        name="nsa_compress",
    )(zk, zkn, zv, zvn, *ws)


def _nsa_kernel(qs_ref, qrs_ref, gt_ref, kc_ref, vct_ref, cov_ref, ksel_ref, vselt_ref, kwin_ref, vwint_ref,
                o_ref, score_scr, bias_scr, sa_scr, sb_scr, *, nc, ns, kb_keys):
    ib = pl.program_id(1)
    p0 = ib * Q_BLOCK
    nl = NSA_REP * Q_BLOCK
    ng = NSA_KV_HEADS
    t_row = p0 + lax.broadcasted_iota(jnp.int32, (1, nl), 1) % Q_BLOCK
    t_q = p0 + lax.broadcasted_iota(jnp.int32, (1, Q_BLOCK), 1)
    gt = gt_ref[...]
    n_top = min(N_SEL, ns)
    per_kb = kb_keys // SEL_BLOCK
    grows = [slice(g * HEAD_DIM, (g + 1) * HEAD_DIM) for g in range(ng)]

    def q_of(ref, g):
        return ref[NSA_REP * g:NSA_REP * (g + 1)].reshape(nl, LANES)

    ones_rows = 16

    def with_ones(vt):
        return jnp.concatenate([vt, jnp.ones((ones_rows, vt.shape[1]), BF16)], axis=0)

    def online(carry, s, vt):
        m, acc = carry
        m_new = jnp.maximum(m, jnp.max(s, axis=0, keepdims=True))
        p = jnp.exp2(s - m_new)
        acc = jnp.exp2(m - m_new) * acc + _dot(with_ones(vt), p.astype(BF16))
        return m_new, acc

    def normalised(acc):
        return acc[0:HEAD_DIM] * (1.0 / jnp.maximum(acc[HEAD_DIM:HEAD_DIM + 1], 1e-30))

    init1 = (jnp.full((1, nl), NEG_BIG, F32), jnp.zeros((HEAD_DIM + ones_rows, nl), F32))
    init = tuple(init1 for _ in range(ng))

    sj = lax.broadcasted_iota(jnp.int32, (ns, 1), 0)
    o_cmp = []
    for g in range(ng):
        s = _dot_nt(kc_ref[...], q_of(qs_ref, g))
        ci = lax.broadcasted_iota(jnp.int32, (nc, 1), 0)
        cmask = (ci * CMP_STRIDE + (CMP_BLOCK - 1) <= t_row) & (ci < nc - 1)
        s = jnp.where(cmask, s, -jnp.inf)
        m = jnp.max(s, axis=0, keepdims=True)
        m = jnp.where(m == -jnp.inf, 0.0, m)
        e = jnp.exp2(s - m)
        l = jnp.sum(e, axis=0, keepdims=True)
        p = e * (1.0 / jnp.maximum(l, 1e-30))
        o_cmp.append(_dot(vct_ref[grows[g], :], p.astype(BF16)))
        psum = p[:, 0:Q_BLOCK]
        for r in range(1, NSA_REP):
            psum = psum + p[:, r * Q_BLOCK:(r + 1) * Q_BLOCK]
        imp = _dot(cov_ref[...], psum.astype(BF16))
        bt = t_q // SEL_BLOCK
        allowed = sj * SEL_BLOCK <= t_q
        forced = (sj == 0) | (sj == bt) | (sj == bt - 1)
        score_scr[g] = jnp.where(allowed, imp + jnp.where(forced, np.float32(SEL_BONUS), 0.0), -jnp.inf)

    def rank_body(k, ranks):
        later = sj > k
        out = []
        for g in range(ng):
            score = score_scr[g]
            row = score_scr[g, pl.ds(k, 1), :]
            bumped = lax.bitcast_convert_type(lax.bitcast_convert_type(row, jnp.int32) + 1, F32)
            row_up = jnp.where(row == -jnp.inf, row, jnp.where(row == 0.0, np.finfo(np.float32).tiny, bumped))
            out.append(ranks[g] + jnp.where(jnp.where(later, row_up, row) > score, 1, 0))
        return tuple(out)

    n_live = jnp.minimum((p0 + Q_BLOCK + SEL_BLOCK - 1) // SEL_BLOCK, ns)
    ranks = lax.fori_loop(0, n_live, rank_body, tuple(jnp.zeros((ns, Q_BLOCK), jnp.int32) for _ in range(ng)))
    for g in range(ng):
        keep = (ranks[g] < n_top) & (score_scr[g] > -jnp.inf)
        bias_scr[g] = jnp.where(keep, 0.0, -jnp.inf)

    def sel_qk(kb, g, dst):
        k0 = pl.multiple_of(kb * kb_keys, kb_keys)
        dst[g] = _dot_nt(ksel_ref[pl.ds(k0, kb_keys), :], q_of(qrs_ref, g))

    def sel_step(carry_g, src, kb, g, causal):
        k0 = pl.multiple_of(kb * kb_keys, kb_keys)
        parts = []
        for i in range(per_kb):
            row = bias_scr[g, pl.ds(kb * per_kb + i, 1), :]
            sl = slice(i * SEL_BLOCK, (i + 1) * SEL_BLOCK)
            parts.append(src[g, sl, :] + jnp.concatenate([row] * NSA_REP, axis=1))
        s = jnp.concatenate(parts, axis=0)
        if causal:
            kpos = k0 + lax.broadcasted_iota(jnp.int32, (kb_keys, 1), 0)
            s = jnp.where(kpos <= t_row, s, -jnp.inf)
        return online(carry_g, s, vselt_ref[grows[g], pl.ds(k0, kb_keys)])

    def sel_pair(j, carry, causal):
        a = 2 * j
        for g in range(ng):
            sel_qk(a + 1, g, sb_scr)
        carry = tuple(sel_step(carry[g], sa_scr, a, g, causal) for g in range(ng))
        if not causal:
            for g in range(ng):
                sel_qk(a + 2, g, sa_scr)
        return tuple(sel_step(carry[g], sb_scr, a + 1, g, causal) for g in range(ng))

    n_kb = (p0 + Q_BLOCK + kb_keys - 1) // kb_keys
    n_pairs = (n_kb + 1) // 2
    for g in range(ng):
        sel_qk(0, g, sa_scr)
    carry = lax.fori_loop(0, n_pairs - 1, lambda j, c: sel_pair(j, c, False), init)
    carry = sel_pair(n_pairs - 1, carry, True)
    o_sel = [normalised(carry[g][1]) for g in range(ng)]

    wkeys = WINDOW + Q_BLOCK
    w0 = pl.multiple_of(jnp.maximum(p0 - WINDOW, 0), Q_BLOCK)
    diff = t_row - (w0 + lax.broadcasted_iota(jnp.int32, (wkeys, 1), 0))
    wmask = (diff >= 0) & (diff < WINDOW)
    kblk = kwin_ref[pl.ds(w0, wkeys), :]
    o_win = []
    for g in range(ng):
        s = jnp.where(wmask, _dot_nt(kblk, q_of(qrs_ref, g)), -jnp.inf)
        o_win.append(normalised(online(init1, s, vwint_ref[grows[g], pl.ds(w0, wkeys)])[1]))

    for g in range(ng):
        o_w = o_win[g]

        def gate_row(jb):
            return jnp.concatenate(
                [gt[(NSA_REP * g + r) * 3 + jb:(NSA_REP * g + r) * 3 + jb + 1, :] for r in range(NSA_REP)], axis=1)

        o = o_cmp[g] * gate_row(0) + o_sel[g] * gate_row(1) + o_w * gate_row(2)
        for pr in range(NSA_REP // 2):
            blk = jnp.concatenate([o[:, (2 * pr) * Q_BLOCK:(2 * pr + 1) * Q_BLOCK],
                                   o[:, (2 * pr + 1) * Q_BLOCK:(2 * pr + 2) * Q_BLOCK]], axis=0)
            c0 = g * NSA_REP * HEAD_DIM + pr * LANES
            o_ref[:, c0:c0 + LANES] = blk.T


def _nsa_prompt(qs, qrs, gt, kc, vct, cov, ksel_r, vsel_t, kwin_r, vwin_t, *, batch, seq):
    nb = seq // Q_BLOCK
    nc = kc.shape[1]
    ns = cov.shape[0]
    kb_keys = min(512, seq)
    per_b3 = lambda a: pl.BlockSpec((None,) + a.shape[1:], lambda b, i: (b, 0, 0))
    return pl.pallas_call(
        functools.partial(_nsa_kernel, nc=nc, ns=ns, kb_keys=kb_keys),
        out_shape=jax.ShapeDtypeStruct((batch * seq, NSA_Q), F32),
        grid=(batch, nb),
        in_specs=[
            pl.BlockSpec((None, NSA_HEADS, Q_BLOCK, LANES), lambda b, i: (b * nb + i, 0, 0, 0)),
            pl.BlockSpec((None, NSA_HEADS, Q_BLOCK, LANES), lambda b, i: (b * nb + i, 0, 0, 0)),
            pl.BlockSpec((None, LANES, Q_BLOCK), lambda b, i: (b * nb + i, 0, 0)),
            per_b3(kc), per_b3(vct),
            pl.BlockSpec(cov.shape, lambda b, i: (0, 0)),
            per_b3(ksel_r), per_b3(vsel_t), per_b3(kwin_r), per_b3(vwin_t),
        ],
        out_specs=pl.BlockSpec((Q_BLOCK, NSA_Q), lambda b, i: (b * nb + i, 0)),
        scratch_shapes=[pltpu.VMEM((NSA_KV_HEADS, ns, Q_BLOCK), F32), pltpu.VMEM((NSA_KV_HEADS, ns, Q_BLOCK), F32),
                        pltpu.VMEM((NSA_KV_HEADS, kb_keys, NSA_REP * Q_BLOCK), F32),
                        pltpu.VMEM((NSA_KV_HEADS, kb_keys, NSA_REP * Q_BLOCK), F32)],
        compiler_params=_cparams(("parallel", "arbitrary")),
        name="nsa_prompt",
    )(qs, qrs, gt, kc, vct, cov, ksel_r, vsel_t, kwin_r, vwin_t)


def _gla_rows(c):
    offs, n = [], 0
    for s in range(c):
        t0 = (s // 8) * 8
        offs.append((n, t0))
        n += c - t0
    return offs, n


def _gla_kernel(q_ref, k_ref, la_ref, v_ref, rs_ref, gn_ref, bm_ref, bmask_ref, sm_ref, s0_ref,
                o_ref, sout_ref, s_scr, prod_scr, res_scr, *, c, nchunks):
    ci = pl.program_id(1)

    @pl.when(ci == 0)
    def _():
        s_scr[...] = s0_ref[...]
        prod_scr[...] = jnp.zeros_like(prod_scr)

    q = q_ref[...]
    k = k_ref[...]
    v = v_ref[...]
    la = la_ref[...]
    tt = lax.broadcasted_iota(jnp.int32, (c, 1), 0)
    b = la
    sh = 1
    while sh < c:
        b = b + jnp.where(tt >= sh, pltpu.roll(b, sh, axis=0), 0.0)
        sh *= 2
    state = s_scr[...]
    inter = _dot_nt((q * jnp.exp(b)).astype(BF16), state.astype(BF16))

    offs, _ = _gla_rows(c)
    for s in range(c):
        r0, t0 = offs[s]
        d = b[t0:] - b[s:s + 1]
        e = jnp.exp(jnp.where(tt[t0:] >= s, d, -jnp.inf))
        prod_scr[r0:r0 + c - t0, :] = (q[t0:] * k[s:s + 1] * e).astype(BF16)
    res_scr[...] = _dot(prod_scr[...], bm_ref[...])
    pieces = []
    for tb in range(0, c, 8):
        hi = min(tb + 8, c)
        acc = inter[tb:hi]
        for s in range(hi):
            r0, t0 = offs[s]
            if t0 <= tb:
                acc = acc + res_scr[r0 + tb - t0:r0 + hi - t0, :] * v[s:s + 1]
        pieces.append(acc)
    o = pieces[0] if len(pieces) == 1 else jnp.concatenate(pieces, axis=0)

    bl = b[c - 1:c]
    kd = (k * jnp.exp(bl - b)).astype(BF16)
    if c >= 16:
        upd = _dot(v.T.astype(BF16), kd)
    else:
        upd = jnp.dot(v.T, kd.astype(F32), preferred_element_type=F32)
    new_state = jnp.exp(bl) * state + upd * bmask_ref[...]
    s_scr[...] = new_state

    o_ref[...] = _seg_rms(o, gn_ref[...], sm_ref[...]) * rs_ref[...]

    @pl.when(ci == nchunks - 1)
    def _():
        sout_ref[...] = new_state


def _gla(qg, kg, la, vg, rs, s0_bd, lw, *, batch, seq, c):
    nchunks = seq // c
    _, npack = _gla_rows(c)
    npad = -(-npack // 16) * 16
    blk = lambda w: pl.BlockSpec((None, c, w), lambda b, i: (b, i, 0))
    full = lambda a: pl.BlockSpec(a.shape, lambda b, i: (0,) * a.ndim)
    consts = [lw["gla_gn"], lw["gla_bm"], lw["gla_bmask"], lw["sm"]]
    st = pl.BlockSpec((None, GLA_V, GLA_QK), lambda b, i: (b, 0, 0))
    return pl.pallas_call(
        functools.partial(_gla_kernel, c=c, nchunks=nchunks),
        out_shape=[jax.ShapeDtypeStruct((batch, seq, GLA_V), F32), jax.ShapeDtypeStruct((batch, GLA_V, GLA_QK), F32)],
        grid=(batch, nchunks),
        in_specs=[blk(GLA_QK), blk(GLA_QK), blk(GLA_QK), blk(GLA_V), blk(GLA_V)] + [full(a) for a in consts] + [st],
        out_specs=[blk(GLA_V), st],
        scratch_shapes=[pltpu.VMEM((GLA_V, GLA_QK), F32), pltpu.VMEM((npad, GLA_QK), BF16),
                        pltpu.VMEM((npad, GLA_V), F32)],
        compiler_params=_cparams(("parallel", "arbitrary")),
        name="gla",
    )(qg, kg, la, vg, rs, *consts, s0_bd)


def _outproj_kernel(x_ref, oa_ref, ob_ref, u_ref, vn_ref, ws_ref, bias_ref, wo_ref, o_ref, *, tm):
    lane = lax.broadcasted_iota(jnp.int32, (GM_CHUNK, GM_W), 1)
    tri = (lax.broadcasted_iota(jnp.int32, (GM_CHUNK, GM_CHUNK), 0)
           >= lax.broadcasted_iota(jnp.int32, (GM_CHUNK, GM_CHUNK), 1))
    zs = []
    for cb in range(tm // GM_CHUNK):
        vn = vn_ref[cb * GM_CHUNK:(cb + 1) * GM_CHUNK, :]
        z = bias_ref[...]
        for g in range(GM_GROUPS):
            wm = jnp.where(tri, ws_ref[g], 0.0).astype(BF16)
            vg = jnp.where((lane >= g * GM_CH) & (lane < (g + 1) * GM_CH), vn, 0.0).astype(BF16)
            z = z + _dot(wm, vg)
        zs.append(z)
    z = zs[0] if len(zs) == 1 else jnp.concatenate(zs, axis=0)
    oc = u_ref[...] * z
    y = _dot(oa_ref[...].astype(BF16), wo_ref[0:NSA_Q, :])
    y = y + _dot(ob_ref[...].astype(BF16), wo_ref[NSA_Q:NSA_Q + GLA_V, :])
    y = y + _dot(oc.astype(BF16), wo_ref[NSA_Q + GLA_V:MIX_OUT, :])
    o_ref[...] = x_ref[...] + y


def _outproj(x, oa, ob, u, vn, ws, bias, wo, li, *, tm):
    m = x.shape[0]
    row = lambda w: pl.BlockSpec((tm, w), lambda i: (i, 0))
    return pl.pallas_call(
        functools.partial(_outproj_kernel, tm=tm),
        out_shape=jax.ShapeDtypeStruct((m, D_MODEL), F32),
        grid=(m // tm,),
        in_specs=[row(D_MODEL), row(NSA_Q), row(GLA_V), row(GM_W), row(GM_W),
                  pl.BlockSpec(ws.shape, lambda i: (0, 0, 0)), pl.BlockSpec(bias.shape, lambda i: (0, 0)),
                  pl.BlockSpec((None, MIX_OUT, D_MODEL), lambda i: (li, 0, 0))],
        out_specs=row(D_MODEL),
        compiler_params=_cparams(("parallel",)),
        name="outproj",
    )(x, oa, ob, u, vn, ws, bias, wo)


def _pcompress_kernel(pt_ref, *refs, npg):
    pages = refs[:npg]
    perm_ref, wk_ref, wv_ref, out_ref, zk_scr, zv_scr = refs[npg:npg + 6]
    perm = perm_ref[...]
    for pp in range(npg // 2):
        xt = jnp.concatenate([pages[2 * pp][...], pages[2 * pp + 1][...]], axis=1).astype(BF16)
        y = _dot_nt(perm, xt).astype(BF16)
        for s in range(CMP_STRIDE):
            rows = slice(pp * 16, (pp + 1) * 16)
            zk_scr[rows, s * LANES:(s + 1) * LANES] = y[s * 16:(s + 1) * 16, 0:LANES]
            zv_scr[rows, s * LANES:(s + 1) * LANES] = y[s * 16:(s + 1) * 16, LANES:2 * LANES]
    q = out_ref.shape[1] // 4
    rk = _dot(zk_scr[...], wk_ref[...])
    rv = _dot(zv_scr[...], wv_ref[...])
    out_ref[:, 0:q] = rk[:, 0:q]
    out_ref[:, q:2 * q] = rv[:, 0:q]
    out_ref[:, 2 * q:3 * q] = rk[:, q:2 * q]
    out_ref[:, 3 * q:4 * q] = rv[:, q:2 * q]


def _pcompress(cache_t, page_table, li, perm, wk, wv, *, npg):
    bs, n_pages = page_table.shape
    page = cache_t.shape[-1]
    nc = n_pages * page // CMP_STRIDE
    steps = n_pages // npg
    cpp = page // CMP_STRIDE

    def page_spec(k):
        return pl.BlockSpec((None, None, 2 * LANES, page), lambda b, h, pt: (li, pt[b, h * npg + k], 0, 0))

    grid_spec = pltpu.PrefetchScalarGridSpec(
        num_scalar_prefetch=1,
        grid=(bs, steps),
        in_specs=[page_spec(k) for k in range(npg)] + [
            pl.BlockSpec(perm.shape, lambda b, h, pt: (0, 0)), pl.BlockSpec(wk.shape, lambda b, h, pt: (0, 0)),
            pl.BlockSpec(wv.shape, lambda b, h, pt: (0, 0))],
        out_specs=pl.BlockSpec((None, npg * cpp, 2 * wk.shape[1]), lambda b, h, pt: (b, h, 0)),
        scratch_shapes=[pltpu.VMEM((npg * cpp, wk.shape[0]), BF16)] * 2,
    )
    return pl.pallas_call(
        functools.partial(_pcompress_kernel, npg=npg),
        out_shape=jax.ShapeDtypeStruct((bs, nc, 2 * wk.shape[1]), F32),
        grid_spec=grid_spec,
        compiler_params=_cparams(("parallel", "arbitrary")),
        name="nsa_page_compress",
    )(page_table, *([cache_t] * npg), perm, wk, wv)


def _ctail_kernel(a_ref, b_ref, bias_ref, w2_ref, gkc_ref, sm_ref, kct_ref, vcr_ref):
    h = a_ref[...] + b_ref[...] + bias_ref[...]
    out = _dot(_gelu(h).astype(BF16), w2_ref[...])
    kc = _seg_rms(out[:, 0:LANES], gkc_ref[...], sm_ref[...])
    kct_ref[...] = kc.T.astype(BF16)
    vcr_ref[...] = out[:, LANES:2 * LANES].astype(BF16)


def _ctail(hid_ab, hid_b_next, bias, w2, gkc, sm):
    bs, nc, w2x = hid_ab.shape
    half = w2x // 2
    full = lambda a: pl.BlockSpec(a.shape, lambda b: (0,) * a.ndim)
    return pl.pallas_call(
        _ctail_kernel,
        out_shape=[jax.ShapeDtypeStruct((bs, LANES, nc), BF16), jax.ShapeDtypeStruct((bs, nc, LANES), BF16)],
        grid=(bs,),
        in_specs=[pl.BlockSpec((None, nc, half), lambda b: (b, 0, 0)), pl.BlockSpec((None, nc, half), lambda b: (b, 0, 0)),
                  full(bias), full(w2), full(gkc), full(sm)],
        out_specs=[pl.BlockSpec((None, LANES, nc), lambda b: (b, 0, 0)), pl.BlockSpec((None, nc, LANES), lambda b: (b, 0, 0))],
        compiler_params=_cparams(("parallel",)),
        name="nsa_compress_tail",
    )(hid_ab, hid_b_next, bias, w2, gkc, sm)


def _nsa_sample_kernel(pt_ref, qp_ref, qrp_ref, gcol_ref, kct_ref, vcr_ref, cov_ref, e_ref, *refs,
                       npg, nsteps, nc, ns_tot, past_len, ts):
    pages = refs[:npg]
    (wint_ref, newk_ref, neww_ref, newwsh_ref, o_ref, winout_ref,
     bias_scr, m_scr, l_scr, acc_scr, oc_scr, ow_scr) = refs[npg:]
    step = pl.program_id(1)
    nrow = NSA_KV_HEADS * NSA_REP * 8
    q8 = lax.broadcasted_iota(jnp.int32, (nrow, 1), 0) % 8
    t_row = past_len + q8
    qrp = qrp_ref[...]
    n_keep = wint_ref.shape[-1]
    n_top = min(N_SEL, ns_tot)
    nsp = bias_scr.shape[1]

    def expand_rows(a):
        return jnp.concatenate([a[0:8]] * NSA_REP + [a[8:16]] * NSA_REP, axis=0)

    @pl.when(step == 0)
    def _():
        s = _dot(qp_ref[...], kct_ref[...])
        ci = lax.broadcasted_iota(jnp.int32, (1, nc), 1)
        cmask = (ci * CMP_STRIDE + (CMP_BLOCK - 1) <= t_row) & (ci < nc - 1)
        s = jnp.where(cmask, s, -jnp.inf)
        m = jnp.max(s, axis=1, keepdims=True)
        m = jnp.where(m == -jnp.inf, 0.0, m)
        e = jnp.exp(s - m)
        p = e * (1.0 / jnp.maximum(jnp.sum(e, axis=1, keepdims=True), 1e-30))
        oc_scr[...] = _dot(p.astype(BF16), vcr_ref[...])
        sjl = lax.broadcasted_iota(jnp.int32, (1, nsp), 1)
        tq = past_len + lax.broadcasted_iota(jnp.int32, (8, 1), 0)
        bt = tq // SEL_BLOCK
        allowed = (sjl * SEL_BLOCK <= tq) & (sjl < ns_tot)
        forced = (sjl == 0) | (sjl == bt) | (sjl == bt - 1)
        for g in range(NSA_KV_HEADS):
            base = g * NSA_REP * 8
            psum = p[base:base + 8]
            for r in range(1, NSA_REP):
                psum = psum + p[base + r * 8:base + (r + 1) * 8]
            imp = _dot(psum.astype(BF16), cov_ref[...])
            score = jnp.where(allowed, imp + jnp.where(forced, np.float32(SEL_BONUS), 0.0), -jnp.inf)
            rank = jnp.zeros((8, nsp), jnp.int32)
            for k in range(ns_tot):
                col = score[:, k:k + 1]
                later = jnp.where(sjl > k, 1, 0)
                rank = rank + jnp.where(col > score, 1, 0) + jnp.where(col == score, later, 0)
            keep = (rank < n_top) & (score > -jnp.inf)
            bias_scr[g * 8:(g + 1) * 8, :] = jnp.where(keep, 0.0, -jnp.inf)
        m_scr[...] = jnp.full(m_scr.shape, NEG_BIG, F32)
        l_scr[...] = jnp.zeros(l_scr.shape, F32)
        acc_scr[...] = jnp.zeros(acc_scr.shape, F32)

        lane_w = lax.broadcasted_iota(jnp.int32, (1, n_keep), 1)
        diff = t_row - (past_len - n_keep + lane_w)
        s_w = jnp.where((diff >= 0) & (diff < WINDOW), _dot(qrp, wint_ref[0:LANES, :].astype(BF16)), -jnp.inf)
        lane_n = lax.broadcasted_iota(jnp.int32, (1, LANES), 1)
        diff_n = t_row - (past_len + lane_n)
        s_n = jnp.where((lane_n < ts) & (diff_n >= 0) & (diff_n < WINDOW),
                        _dot(qrp, neww_ref[0:LANES, :].astype(BF16)), -jnp.inf)
        sw = jnp.concatenate([s_w, s_n], axis=1)
        mw = jnp.max(sw, axis=1, keepdims=True)
        mw = jnp.where(mw == -jnp.inf, 0.0, mw)
        ew = jnp.exp(sw - mw)
        pw = ew * (1.0 / jnp.maximum(jnp.sum(ew, axis=1, keepdims=True), 1e-30))
        ow_scr[...] = (_dot_nt(pw[:, 0:n_keep].astype(BF16), wint_ref[LANES:2 * LANES, :].astype(BF16))
                       + _dot_nt(pw[:, n_keep:].astype(BF16), neww_ref[LANES:2 * LANES, :].astype(BF16)))
        rolled = pltpu.roll(wint_ref[...], n_keep - ts, axis=1)
        lane_o = lax.broadcasted_iota(jnp.int32, (2 * LANES, LANES), 1)
        winout_ref[:, 0:n_keep - LANES] = rolled[:, 0:n_keep - LANES]
        winout_ref[:, n_keep - LANES:n_keep] = jnp.where(lane_o >= LANES - ts, newwsh_ref[...],
                                                         rolled[:, n_keep - LANES:n_keep])

    def online(s, vt):
        m_old = m_scr[:, 0:1]
        m_new = jnp.maximum(m_old, jnp.max(s, axis=1, keepdims=True))
        p = jnp.exp(s - m_new)
        alpha = jnp.exp(m_old - m_new)
        l_new = alpha * l_scr[:, 0:1] + jnp.sum(p, axis=1, keepdims=True)
        acc_scr[...] = alpha * acc_scr[...] + _dot_nt(p.astype(BF16), vt)
        m_scr[...] = jnp.broadcast_to(m_new, m_scr.shape)
        l_scr[...] = jnp.broadcast_to(l_new, l_scr.shape)

    sel01 = jnp.where(bias_scr[...] == 0.0, 1.0, 0.0).astype(BF16)
    bexp = _dot(sel01, e_ref[...])
    bias = expand_rows(jnp.where(bexp > 0.5, 0.0, -jnp.inf))
    kt = jnp.concatenate([pg[0:LANES, :] for pg in pages], axis=1).astype(BF16)
    vt = jnp.concatenate([pg[LANES:2 * LANES, :] for pg in pages], axis=1).astype(BF16)
    online(_dot(qrp, kt) + bias, vt)

    @pl.when(step == nsteps - 1)
    def _():
        lane_n = lax.broadcasted_iota(jnp.int32, (1, LANES), 1)
        bcol = expand_rows(bias_scr[:, ns_tot - 1:ns_tot])
        ok = (lane_n < ts) & (past_len + lane_n <= t_row)
        s_n = jnp.where(ok, _dot(qrp, newk_ref[2 * LANES:3 * LANES, :].astype(BF16)) + bcol, -jnp.inf)
        online(s_n, newk_ref[3 * LANES:4 * LANES, :].astype(BF16))
        o_s = acc_scr[...] * (1.0 / jnp.maximum(l_scr[:, 0:1], 1e-30))
        gc = gcol_ref[...]
        o_ref[...] = oc_scr[...] * gc[:, 0:1] + o_s * gc[:, 1:2] + ow_scr[...] * gc[:, 2:3]


def _nsa_sample(cache_t, page_table, li, qp, qrp, gcol, kct, vcr, cov_s, emat, win_t, newk_t, neww_t, newwsh_t, *,
                npg, past_len, ts):
    bs, n_pages = page_table.shape
    page = cache_t.shape[-1]
    nsteps = n_pages // npg
    nc = kct.shape[-1]
    ns_tot = -(-(past_len + ts) // SEL_BLOCK)
    nsp = cov_s.shape[1]
    n_keep = win_t.shape[-1]
    nrow = qp.shape[1]
    per_b = lambda a: pl.BlockSpec((None,) + a.shape[1:], lambda b, h, pt: (b,) + (0,) * (a.ndim - 1))

    def page_spec(k):
        return pl.BlockSpec((None, None, 2 * LANES, page), lambda b, h, pt: (li, pt[b, h * npg + k], 1, 0))

    grid_spec = pltpu.PrefetchScalarGridSpec(
        num_scalar_prefetch=1,
        grid=(bs, nsteps),
        in_specs=[per_b(qp), per_b(qrp), per_b(gcol), per_b(kct), per_b(vcr),
                  pl.BlockSpec(cov_s.shape, lambda b, h, pt: (0, 0)),
                  pl.BlockSpec((nsp, npg * page), lambda b, h, pt: (0, h))]
                 + [page_spec(k) for k in range(npg)]
                 + [pl.BlockSpec((None, None, 2 * LANES, n_keep), lambda b, h, pt: (li, b, 0, 0)),
                    per_b(newk_t), per_b(neww_t), per_b(newwsh_t)],
        out_specs=[pl.BlockSpec((None, nrow, LANES), lambda b, h, pt: (b, 0, 0)),
                   pl.BlockSpec((None, 2 * LANES, n_keep), lambda b, h, pt: (b, 0, 0))],
        scratch_shapes=[pltpu.VMEM((2 * 8, nsp), F32)] + [pltpu.VMEM((nrow, LANES), F32)] * 5,
    )
    return pl.pallas_call(
        functools.partial(_nsa_sample_kernel, npg=npg, nsteps=nsteps, nc=nc, ns_tot=ns_tot, past_len=past_len, ts=ts),
        out_shape=[jax.ShapeDtypeStruct((bs, nrow, LANES), F32), jax.ShapeDtypeStruct((bs, 2 * LANES, n_keep), F32)],
        grid_spec=grid_spec,
        compiler_params=_cparams(("parallel", "arbitrary")),
        name="nsa_sample",
    )(page_table, qp, qrp, gcol, kct, vcr, cov_s, emat, *([cache_t] * npg), win_t, newk_t, neww_t, newwsh_t)


def _pad_cols(w):
    cuts = np.cumsum((0,) + IN_SPLITS)
    parts = []
    for i, (n, p) in enumerate(zip(IN_SPLITS, IN_PADDED)):
        seg = w[..., cuts[i]:cuts[i] + n]
        if p != n:
            seg = jnp.pad(seg, [(0, 0)] * (w.ndim - 1) + [(0, p - n)])
        parts.append(seg)
    return jnp.concatenate(parts, axis=-1)


def _rope_tables(pos):
    half = HEAD_DIM // 2
    inv = 1.0 / (ROPE_THETA ** (jnp.arange(half, dtype=F32) * (2.0 / HEAD_DIM)))
    ang = pos.astype(F32)[:, None] * inv[None, :]
    cos = jnp.cos(ang)
    sin = jnp.sin(ang)
    cos_f = jnp.concatenate([cos, cos, cos, cos], axis=1)
    sin_f = jnp.concatenate([-sin, sin, -sin, sin], axis=1)
    return cos_f, sin_f


def _cover_t(seq):
    nc = seq // CMP_STRIDE
    ns = seq // SEL_BLOCK
    ci = np.arange(nc)[None, :]
    sj = np.arange(ns)[:, None]
    cov = ((ci * CMP_STRIDE <= sj * SEL_BLOCK + SEL_BLOCK - 1)
           & (ci * CMP_STRIDE + CMP_BLOCK - 1 >= sj * SEL_BLOCK) & (ci < nc - 1))
    return jnp.asarray(cov, dtype=BF16)


def _layer_weights(l, ln_gains, w_in_p, nsa_qk_norm, nsa_cmp_pos, nsa_cmp_w1, nsa_cmp_w2, gla_gate_w, gla_gate_b,
                   gla_norm, gm_ln, gm_ws, gm_b):
    eye2 = jnp.eye(NSA_KV_HEADS, dtype=F32)
    lw = {"ln0": ln_gains[l, 0][None], "ln1": ln_gains[l, 1][None], "ln2": ln_gains[l, 2][None], "w_in": w_in_p[l]}
    lw["gq"] = jnp.tile(nsa_qk_norm[l, 0], NSA_HEADS)[None]
    lw["gkc"] = jnp.tile(nsa_qk_norm[l, 1], NSA_KV_HEADS)[None]
    lw["gks"] = jnp.tile(nsa_qk_norm[l, 2], NSA_KV_HEADS)[None]
    lw["gkw"] = jnp.tile(nsa_qk_norm[l, 3], NSA_KV_HEADS)[None]
    seg = (np.arange(LANES)[:, None] // HEAD_DIM) == (np.arange(LANES)[None, :] // HEAD_DIM)
    lw["sm"] = jnp.asarray(seg * (1.0 / HEAD_DIM), dtype=BF16)
    lw["gla_gw"] = jnp.pad(gla_gate_w[l], ((0, LANES - GLA_GATE_RANK), (0, 0))).astype(BF16)
    lw["gla_gb"] = gla_gate_b[l][None]
    lw["gm_lng"] = gm_ln[l, 0][None]
    lw["gm_lnb"] = gm_ln[l, 1][None]
    pos_rows = []
    for c, nm in ((0, "k"), (1, "v")):
        w1 = nsa_cmp_w1[l, c].reshape(CMP_BLOCK, HEAD_DIM, CMP_HIDDEN)
        for half, tag in ((w1[:CMP_STRIDE], "a"), (w1[CMP_STRIDE:], "b")):
            wx = jnp.einsum("sdh,pg->spdgh", half, eye2)
            lw["cmp_w" + tag + nm] = wx.reshape(CMP_STRIDE * LANES, NSA_KV_HEADS * CMP_HIDDEN).astype(BF16)
        lw["cmp_w2" + nm] = jnp.einsum("hd,pg->phgd", nsa_cmp_w2[l, c], eye2).reshape(
            NSA_KV_HEADS * CMP_HIDDEN, LANES).astype(BF16)
        pe = nsa_cmp_pos[l, c]
        for half in (pe[:CMP_STRIDE], pe[CMP_STRIDE:]):
            pos_rows.append(jnp.broadcast_to(half[:, None, :], (CMP_STRIDE, NSA_KV_HEADS, HEAD_DIM)).reshape(-1))
    lw["cmp_pos"] = jnp.stack(pos_rows)
    w1ab = nsa_cmp_w1[l].reshape(2, 2, CMP_STRIDE, HEAD_DIM, CMP_HIDDEN)
    for c, nm in ((0, "k"), (1, "v")):
        lw["pc_w" + nm] = jnp.einsum("asdh,gy->sgdayh", w1ab[c], eye2).reshape(
            CMP_STRIDE * LANES, 2 * NSA_KV_HEADS * CMP_HIDDEN).astype(BF16)
    pb = jnp.einsum("ck,ckh->ch", nsa_cmp_pos[l].reshape(2, -1), nsa_cmp_w1[l], precision=lax.Precision.HIGHEST)
    lw["pc_bias"] = jnp.broadcast_to(pb[:, None, :], (2, NSA_KV_HEADS, CMP_HIDDEN)).reshape(1, -1)
    lw["pc_w2"] = jnp.einsum("chd,cx,gy->cghxyd", nsa_cmp_w2[l], eye2, eye2).reshape(
        2 * NSA_KV_HEADS * CMP_HIDDEN, 2 * LANES).astype(BF16)
    lw["gla_gn"] = jnp.tile(gla_norm[l], GLA_HEADS)[None]
    hq = np.arange(GLA_QK) // GLA_DK
    hv = np.arange(GLA_V) // GLA_DV
    lw["gla_bm"] = jnp.asarray(hq[:, None] == hv[None, :], dtype=BF16)
    lw["gla_bmask"] = jnp.asarray(hv[:, None] == hq[None, :], dtype=F32)
    lw["gm_ws"] = gm_ws[l]
    lw["gm_bias"] = jnp.repeat(gm_b[l].T, GM_CH, axis=1)
    return lw


def _page_perm(page):
    cpp = page // CMP_STRIDE
    r = np.arange(2 * page)
    s_, rem = r // (2 * cpp), r % (2 * cpp)
    t = (rem // cpp) * page + CMP_STRIDE * (rem % cpp) + s_
    m = np.zeros((2 * page, 2 * page), np.float32)
    m[r, t] = 1.0
    return jnp.asarray(m, dtype=BF16)


def _cover_sample(past_len, ts, nsp):
    t_tot = past_len + ts
    n_c = (t_tot - CMP_BLOCK) // CMP_STRIDE + 1
    nc = past_len // CMP_STRIDE
    n_s = -(-t_tot // SEL_BLOCK)
    ci = np.arange(nc)[:, None]
    sj = np.arange(nsp)[None, :]
    cov = ((ci * CMP_STRIDE <= sj * SEL_BLOCK + SEL_BLOCK - 1) & (ci * CMP_STRIDE + CMP_BLOCK - 1 >= sj * SEL_BLOCK)
           & (ci < n_c) & (sj < n_s))
    emat = (np.arange(past_len)[None, :] // SEL_BLOCK) == np.arange(nsp)[:, None]
    return jnp.asarray(cov, dtype=BF16), jnp.asarray(emat, dtype=BF16)


def _rows_gr8(a, ts):
    bs = a.shape[0]
    a = a.reshape(bs, ts, NSA_KV_HEADS, NSA_REP, a.shape[-1]).transpose(0, 2, 3, 1, 4)
    return jnp.pad(a, ((0, 0), (0, 0), (0, 0), (0, 8 - ts), (0, 0)))


def _state_to_t(s):
    b = s.shape[0]
    eye = jnp.eye(GLA_HEADS, dtype=s.dtype)
    return jnp.einsum("bhkv,hg->bhvgk", s, eye).reshape(b, GLA_V, GLA_QK)


def _state_from_t(st):
    b = st.shape[0]
    s5 = st.reshape(b, GLA_HEADS, GLA_DV, GLA_HEADS, GLA_DK)
    d = jnp.stack([s5[:, h, :, h, :] for h in range(GLA_HEADS)], axis=1)
    return jnp.swapaxes(d, 2, 3)


def kernel(x_prompt, x_sample, cache_kv, cache_win_kv, state_gla, page_table, ln_gains, ffn_w_gate_up, ffn_w_down,
           w_in, w_out, nsa_qk_norm, nsa_cmp_pos, nsa_cmp_w1, nsa_cmp_w2, gla_gate_w, gla_gate_b, gla_norm, gm_ln,
           gm_ws, gm_b):
    depth = w_in.shape[0]
    bp, tp, _ = x_prompt.shape
    bs, ts, _ = x_sample.shape
    n_pages = page_table.shape[1]
    page = cache_kv.shape[2]
    past_len = n_pages * page
    mp, ms = bp * tp, bs * ts
    nc = tp // CMP_STRIDE

    w_gu = ffn_w_gate_up.astype(BF16).reshape(depth * 2, D_MODEL, 2 * D_FF)
    w_d = ffn_w_down.astype(BF16).reshape(depth * 2, D_FF, D_MODEL)
    w_in_p = _pad_cols(w_in).astype(BF16)
    w_o = w_out.astype(BF16)
    cos_p, sin_p = _rope_tables(jnp.arange(tp))
    cos_s, sin_s = _rope_tables(past_len + jnp.arange(ms) % ts)
    cov = _cover_t(tp)
    tm_p = 512 if mp % 512 == 0 else Q_BLOCK
    tm_i = 256 if tp % 256 == 0 else Q_BLOCK
    tf = 1408

    nrow = NSA_HEADS * 8
    n_keep_s = cache_win_kv.shape[2]
    nsp = -(-(past_len // SEL_BLOCK + 1) // LANES) * LANES
    cov_s, emat = _cover_sample(past_len, ts, nsp)
    perm = _page_perm(page)
    eye2 = jnp.eye(NSA_KV_HEADS, dtype=F32)
    npg_c = min(32, n_pages)
    npg_a = min(16, n_pages)
    cache_t = jnp.transpose(cache_kv, (0, 1, 3, 4, 5, 2)).reshape(depth, cache_kv.shape[1], 4 * LANES, page)
    win_t = jnp.transpose(cache_win_kv, (0, 1, 3, 4, 5, 2)).reshape(depth, bs, 2 * LANES, n_keep_s)
    nxt = lambda z: jnp.concatenate([z[:, 1:], jnp.zeros_like(z[:, :1])], axis=1)

    xp = x_prompt.reshape(mp, D_MODEL)
    xs = x_sample.reshape(ms, D_MODEL)
    kv_p, win_p, gla_p, kv_s, win_s, gla_s, gmv_s = [], [], [], [], [], [], []
    eye_b = jnp.eye(bs, dtype=F32)
    for l in range(depth):
        lw = _layer_weights(l, ln_gains, w_in_p, nsa_qk_norm, nsa_cmp_pos, nsa_cmp_w1, nsa_cmp_w2, gla_gate_w,
                            gla_gate_b, gla_norm, gm_ln, gm_ws, gm_b)
        xp = _ffn(xp, lw["ln0"], w_gu, w_d, 2 * l, tm=tm_p, tf=tf)
        (newkv, newwin, _, qg, kg, la, vg, rs, u, vn, qs, qrs, gt, ksel_r, vsel_t, kwin_r, vwin_t, kcmp,
         vcmp) = _inproj(xp, lw, cos_p, sin_p, tm=tm_i, attn_layout=True, batch=bp, seq=tp)
        zk = kcmp.reshape(bp, nc, CMP_STRIDE * LANES)
        zv = vcmp.reshape(bp, nc, CMP_STRIDE * LANES)
        kc, vct = _compress(zk, nxt(zk), zv, nxt(zv), lw, rows=min(128, nc))
        oa = _nsa_prompt(qs, qrs, gt, kc, vct, cov, ksel_r.reshape(bp, tp, LANES), vsel_t,
                         kwin_r.reshape(bp, tp, LANES), vwin_t, batch=bp, seq=tp)
        r3 = lambda a: a.reshape(bp, tp, a.shape[-1])
        ob, st = _gla(r3(qg), r3(kg), r3(la), r3(vg), r3(rs), jnp.zeros((bp, GLA_V, GLA_QK), F32), lw,
                      batch=bp, seq=tp, c=GLA_CHUNK)
        xp = _outproj(xp, oa, ob.reshape(mp, GLA_V), u, vn, lw["gm_ws"], lw["gm_bias"], w_o, l, tm=tm_p)
        xp = _ffn(xp, lw["ln2"], w_gu, w_d, 2 * l + 1, tm=tm_p, tf=tf)
        kv_p.append(newkv.reshape(bp, 4, NSA_KV_HEADS, HEAD_DIM, tp).transpose(0, 4, 1, 2, 3))
        n_keep = min(WINDOW, tp)
        win_p.append(newwin[:, :, tp - n_keep:].reshape(bp, 2, NSA_KV_HEADS, HEAD_DIM, n_keep).transpose(0, 4, 1, 2, 3))
        gla_p.append(_state_from_t(st))

        xs = _ffn(xs, lw["ln0"], w_gu, w_d, 2 * l, tm=ms, tf=tf)
        (newkv, newwin, gates, qg, kg, la, vg, rs, u, vn, qn, qr) = _inproj(
            xs, lw, cos_s, sin_s, tm=ms, attn_layout=False, batch=bs, seq=ts)
        hid = _pcompress(cache_t, page_table, l, perm, lw["pc_wk"], lw["pc_wv"], npg=npg_c)
        hb = hid[:, :, hid.shape[2] // 2:]
        kct, vcr = _ctail(hid, nxt(hb), lw["pc_bias"], lw["pc_w2"], lw["gkc"], lw["sm"])
        scale = np.float32(HEAD_DIM ** -0.5)
        to_pad = lambda q: jnp.einsum("bgrqd,gx->bgrqxd", _rows_gr8(q.reshape(bs, ts, NSA_HEADS, HEAD_DIM) * scale, ts),
                                      eye2).reshape(bs, nrow, LANES).astype(BF16)
        gcol = _rows_gr8(gates[:, :3 * NSA_HEADS].reshape(bs, ts, NSA_HEADS, 3), ts).reshape(bs, nrow, 3)
        gcol = jnp.pad(gcol, ((0, 0), (0, 0), (0, LANES - 3)))
        newk_t = jnp.pad(newkv.reshape(bs, ts, -1).transpose(0, 2, 1), ((0, 0), (0, 0), (0, LANES - ts)))
        neww_c = newwin.reshape(bs, ts, -1).transpose(0, 2, 1)
        neww_t = jnp.pad(neww_c, ((0, 0), (0, 0), (0, LANES - ts)))
        newwsh_t = jnp.pad(neww_c, ((0, 0), (0, 0), (LANES - ts, 0)))
        o64, win_o = _nsa_sample(cache_t, page_table, l, to_pad(qn), to_pad(qr), gcol, kct, vcr, cov_s, emat, win_t,
                                 newk_t, neww_t, newwsh_t, npg=npg_a, past_len=past_len, ts=ts)
        o6 = o64.reshape(bs, NSA_KV_HEADS, NSA_REP, 8, NSA_KV_HEADS, HEAD_DIM)
        oa = jnp.stack([o6[:, g, :, :ts, g, :] for g in range(NSA_KV_HEADS)], axis=1)
        oa = oa.transpose(0, 3, 1, 2, 4).reshape(ms, NSA_Q)
        pad8 = lambda a: jnp.pad(a.reshape(bs, ts, a.shape[-1]), ((0, 0), (0, 8 - ts), (0, 0)))
        ob8, st_s = _gla(pad8(qg), pad8(kg), pad8(la), pad8(vg), pad8(rs), _state_to_t(state_gla[l]), lw,
                         batch=bs, seq=8, c=8)
        ob = ob8[:, :ts].reshape(ms, GLA_V)
        ws_s = jnp.einsum("gts,bc->gbtcs", gm_ws[l][:, :ts, :ts], eye_b).reshape(GM_GROUPS, ms, ms)
        bias_s = jnp.tile(lw["gm_bias"][:ts], (bs, 1))
        xs = _outproj(xs, oa, ob, u, vn, ws_s, bias_s, w_o, l, tm=ms)
        xs = _ffn(xs, lw["ln2"], w_gu, w_d, 2 * l + 1, tm=ms, tf=tf)
        kv_s.append(newkv.reshape(bs, ts, 4, NSA_KV_HEADS, HEAD_DIM))
        win_s.append(win_o.reshape(bs, 2, NSA_KV_HEADS, HEAD_DIM, n_keep_s).transpose(0, 4, 1, 2, 3))
        gla_s.append(_state_from_t(st_s))
        gmv_s.append(vn.reshape(bs, ts, GM_GROUPS, GM_CH))
    return (xp.reshape(bp, tp, D_MODEL), xs.reshape(bs, ts, D_MODEL), jnp.stack(kv_p), jnp.stack(win_p),
            jnp.stack(gla_p), jnp.stack(kv_s), jnp.stack(win_s), jnp.stack(gla_s), jnp.stack(gmv_s))
```

```python
import functools

import numpy as np
import jax
import jax.numpy as jnp
from jax import lax
from jax.experimental import pallas as pl
from jax.experimental.pallas import tpu as pltpu

F32 = jnp.float32
BF16 = jnp.bfloat16

D_MODEL = 1024
HEAD_DIM = 64
NSA_HEADS = 8
NSA_KV_HEADS = 2
NSA_REP = NSA_HEADS // NSA_KV_HEADS
CMP_STRIDE = 16
CMP_BLOCK = 2 * CMP_STRIDE
CMP_HIDDEN = 128
SEL_BLOCK = 64
N_SEL = 16
WINDOW = 512
Q_BLOCK = 128
SEL_BONUS = 1.0e4
GLA_HEADS = 4
GLA_DK = 32
GLA_DV = 64
GLA_GATE_RANK = 16
GLA_GATE_TEMP = 16.0
GLA_CHUNK = 64
GM_GROUPS = 4
GM_CH = 64
GM_CHUNK = 128
D_FF = 2816
ROPE_THETA = 10000.0
EPS = 1e-6

NSA_Q = NSA_HEADS * HEAD_DIM
NSA_KV = NSA_KV_HEADS * HEAD_DIM
GLA_QK = GLA_HEADS * GLA_DK
GLA_V = GLA_HEADS * GLA_DV
GM_W = GM_GROUPS * GM_CH
MIX_OUT = NSA_Q + GLA_V + GM_W
IN_SPLITS = (NSA_Q, 6 * NSA_KV, 3 * NSA_HEADS, GLA_QK, GLA_QK, GLA_V, GLA_GATE_RANK, GLA_V, GM_W, GM_W)
IN_PADDED = tuple(-(-s // 128) * 128 for s in IN_SPLITS)
IN_OFFS = tuple(int(v) for v in np.cumsum((0,) + IN_PADDED))
D_IN_PAD = IN_OFFS[-1]

LANES = 128
NEG_BIG = -1.0e30
VMEM_LIMIT = 56 * 1024 * 1024


def _cparams(sem):
    return pltpu.CompilerParams(dimension_semantics=sem, vmem_limit_bytes=VMEM_LIMIT)


def _gelu(x):
    c = np.float32(np.sqrt(2.0 / np.pi))
    return x * (0.5 * (1.0 + jnp.tanh(c * (x + 0.044715 * (x * x * x)))))


def _sigmoid(x):
    return 1.0 / (1.0 + jnp.exp(-x))


def _dot(a, b):
    return jnp.dot(a, b, preferred_element_type=F32)


def _dot_nt(a, b):
    return lax.dot_general(a, b, (((1,), (1,)), ((), ())), preferred_element_type=F32)


def _seg_mean_sq(x, sm):
    sq = x * x
    hi = sq.astype(BF16)
    lo = (sq - hi.astype(F32)).astype(BF16)
    outs = []
    for c in range(x.shape[1] // LANES):
        sl = slice(c * LANES, (c + 1) * LANES)
        outs.append(_dot(hi[:, sl], sm) + _dot(lo[:, sl], sm))
    return outs[0] if len(outs) == 1 else jnp.concatenate(outs, axis=1)


def _seg_rms(x, gain, sm):
    return x * lax.rsqrt(_seg_mean_sq(x, sm) + EPS) * gain


def _tile_lanes(a, w):
    n = w // a.shape[1]
    return a if n == 1 else jnp.concatenate([a] * n, axis=1)


def _rope(x, cos, sin_signed):
    w = x.shape[1]
    lane = lax.broadcasted_iota(jnp.int32, x.shape, 1)
    fwd = pltpu.roll(x, w - HEAD_DIM // 2, axis=1)
    bwd = pltpu.roll(x, HEAD_DIM // 2, axis=1)
    partner = jnp.where((lane % HEAD_DIM) < HEAD_DIM // 2, fwd, bwd)
    return x * _tile_lanes(cos, w) + partner * _tile_lanes(sin_signed, w)


def _ffn_kernel(x_ref, g_ref, wg_ref, wu_ref, wd_ref, o_ref, h_scr, acc_scr, *, nj):
    j = pl.program_id(1)

    @pl.when(j == 0)
    def _():
        x = x_ref[...]
        ms = jnp.mean(x * x, axis=-1, keepdims=True)
        h_scr[...] = (x * lax.rsqrt(ms + EPS) * g_ref[...]).astype(BF16)
        acc_scr[...] = jnp.zeros_like(acc_scr)

    h = h_scr[...]
    g = _dot(h, wg_ref[...])
    u = _dot(h, wu_ref[...])
    a = (g * _sigmoid(g)) * u
    acc_scr[...] += _dot(a.astype(BF16), wd_ref[...])

    @pl.when(j == nj - 1)
    def _():
        o_ref[...] = x_ref[...] + 0.5 * acc_scr[...]


def _ffn(x, gain, w_gu, w_d, li, *, tm, tf):
    m = x.shape[0]
    nj = D_FF // tf
    return pl.pallas_call(
        functools.partial(_ffn_kernel, nj=nj),
        out_shape=jax.ShapeDtypeStruct((m, D_MODEL), F32),
        grid=(m // tm, nj),
        in_specs=[
            pl.BlockSpec((tm, D_MODEL), lambda i, j: (i, 0)),
            pl.BlockSpec((1, D_MODEL), lambda i, j: (0, 0)),
            pl.BlockSpec((None, D_MODEL, tf), lambda i, j: (li, 0, j)),
            pl.BlockSpec((None, D_MODEL, tf), lambda i, j: (li, 0, j + nj)),
            pl.BlockSpec((None, tf, D_MODEL), lambda i, j: (li, j, 0)),
        ],
        out_specs=pl.BlockSpec((tm, D_MODEL), lambda i, j: (i, 0)),
        scratch_shapes=[pltpu.VMEM((tm, D_MODEL), BF16), pltpu.VMEM((tm, D_MODEL), F32)],
        compiler_params=_cparams(("parallel", "arbitrary")),
        name="ffn",
    )(x, gain, w_gu, w_gu, w_d)


def _group_padded(arr, h):
    c = arr[:, (h // 2) * LANES:(h // 2 + 1) * LANES]
    g = h // NSA_REP
    if (h % 2) != g:
        c = pltpu.roll(c, HEAD_DIM, axis=1)
    lane = lax.broadcasted_iota(jnp.int32, c.shape, 1)
    keep = (lane >= g * HEAD_DIM) & (lane < (g + 1) * HEAD_DIM)
    return jnp.where(keep, c, 0.0)


def _inproj_kernel(x_ref, ln_ref, w_ref, gq_ref, gks_ref, gkw_ref, cos_ref, sin_ref, gw_ref, gb_ref,
                   lng_ref, lnb_ref, sm_ref, *outs, tm, attn_layout):
    (newkv_ref, newwin_ref, gates_ref, qg_ref, kg_ref, la_ref, vg_ref, rs_ref, u_ref, vn_ref) = outs[:10]
    x = x_ref[...]
    ms = jnp.mean(x * x, axis=-1, keepdims=True)
    h = (x * lax.rsqrt(ms + EPS) * ln_ref[...]).astype(BF16)
    p = _dot(h, w_ref[...])
    sm = sm_ref[...]
    cos = cos_ref[...]
    sin = sin_ref[...]
    o = IN_OFFS

    def seg(i, a=0, b=None):
        b = IN_PADDED[i] if b is None else b
        return p[:, o[i] + a:o[i] + b]

    qn = _seg_rms(seg(0), gq_ref[...], sm)
    qr = _rope(qn, cos, sin)
    kv = [seg(1, LANES * j, LANES * (j + 1)) for j in range(6)]
    ksel = _rope(_seg_rms(kv[2], gks_ref[...], sm), cos, sin)
    kwin = _rope(_seg_rms(kv[4], gkw_ref[...], sm), cos, sin)
    vsel_t = kv[3].T
    vwin_t = kv[5].T
    if attn_layout:
        for j, a in enumerate((kv[0].T, kv[1].T, ksel.T, vsel_t)):
            newkv_ref[j * LANES:(j + 1) * LANES, :] = a
        newwin_ref[0:LANES, :] = kwin.T
        newwin_ref[LANES:2 * LANES, :] = vwin_t
    else:
        for j, a in enumerate((kv[0], kv[1], ksel, kv[3])):
            newkv_ref[:, j * LANES:(j + 1) * LANES] = a
        newwin_ref[:, 0:LANES] = kwin
        newwin_ref[:, LANES:2 * LANES] = kv[5]
    gates = _sigmoid(seg(2))
    gates_ref[...] = gates
    qg_ref[...] = seg(3) * np.float32(GLA_DK ** -0.5)
    kg_ref[...] = seg(4)
    vg_ref[...] = seg(5)
    logit = _dot(seg(6).astype(BF16), gw_ref[...]) + gb_ref[...]
    log_sig = jnp.minimum(logit, 0.0) - jnp.log1p(jnp.exp(-jnp.abs(logit)))
    la_ref[...] = log_sig * np.float32(1.0 / GLA_GATE_TEMP)
    r = seg(7)
    rs_ref[...] = r * _sigmoid(r)
    u_ref[...] = _gelu(seg(8))
    v = _gelu(seg(9))
    mu = jnp.mean(v, axis=-1, keepdims=True)
    var = jnp.mean(jnp.square(v - mu), axis=-1, keepdims=True)
    vn_ref[...] = (v - mu) * lax.rsqrt(var + EPS) * lng_ref[...] + lnb_ref[...]

    if attn_layout:
        (qs_ref, qrs_ref, gt_ref, kselr_ref, vselt_ref, kwinr_ref, vwint_ref, kcmp_ref, vcmp_ref) = outs[10:]
        scale = np.float32(HEAD_DIM ** -0.5 * np.log2(np.e))
        qs = qn * scale
        qrs = qr * scale
        for hh in range(NSA_HEADS):
            a = _group_padded(qs, hh).astype(BF16)
            b = _group_padded(qrs, hh).astype(BF16)
            for rb in range(tm // Q_BLOCK):
                qs_ref[rb, hh] = a[rb * Q_BLOCK:(rb + 1) * Q_BLOCK]
                qrs_ref[rb, hh] = b[rb * Q_BLOCK:(rb + 1) * Q_BLOCK]
        for rb in range(tm // Q_BLOCK):
            gt_ref[rb] = gates[rb * Q_BLOCK:(rb + 1) * Q_BLOCK].T
        kselr_ref[...] = ksel.astype(BF16)
        vselt_ref[...] = vsel_t.astype(BF16)
        kwinr_ref[...] = kwin.astype(BF16)
        vwint_ref[...] = vwin_t.astype(BF16)
        kcmp_ref[...] = kv[0]
        vcmp_ref[...] = kv[1]
    else:
        qn_ref, qr_ref = outs[10:]
        qn_ref[...] = qn
        qr_ref[...] = qr


def _inproj(x, lw, cos_t, sin_t, *, tm, attn_layout, batch, seq):
    m = x.shape[0]
    nt = m // tm
    ntab = cos_t.shape[0] // tm
    row = lambda w: pl.BlockSpec((tm, w), lambda i: (i, 0))
    full = lambda a: pl.BlockSpec(a.shape, lambda i: (0,) * a.ndim)
    ins = [x, lw["ln1"], lw["w_in"], lw["gq"], lw["gks"], lw["gkw"], cos_t, sin_t, lw["gla_gw"], lw["gla_gb"],
           lw["gm_lng"], lw["gm_lnb"], lw["sm"]]
    in_specs = [row(D_MODEL), full(lw["ln1"]), full(lw["w_in"]), full(lw["gq"]), full(lw["gks"]), full(lw["gkw"]),
                pl.BlockSpec((tm, LANES), lambda i: (i % ntab, 0)), pl.BlockSpec((tm, LANES), lambda i: (i % ntab, 0)),
                full(lw["gla_gw"]), full(lw["gla_gb"]), full(lw["gm_lng"]), full(lw["gm_lnb"]), full(lw["sm"])]
    widths = [512, 256, 128, 128, 128, 128, 256, 256, 256, 256]
    out_shape = [jax.ShapeDtypeStruct((m, w), F32) for w in widths]
    out_specs = [row(w) for w in widths]
    if attn_layout:
        nqb = m // Q_BLOCK
        rpb = tm // Q_BLOCK
        tpb = seq // tm
        for j in range(2):
            out_shape[j] = jax.ShapeDtypeStruct((batch, widths[j], seq), F32)
            out_specs[j] = pl.BlockSpec((None, widths[j], tm), lambda i: (i // tpb, 0, i % tpb))
        out_shape += [jax.ShapeDtypeStruct((nqb, NSA_HEADS, Q_BLOCK, LANES), BF16)] * 2
        out_specs += [pl.BlockSpec((rpb, NSA_HEADS, Q_BLOCK, LANES), lambda i: (i, 0, 0, 0))] * 2
        out_shape += [jax.ShapeDtypeStruct((nqb, LANES, Q_BLOCK), F32)]
        out_specs += [pl.BlockSpec((rpb, LANES, Q_BLOCK), lambda i: (i, 0, 0))]
        rowmaj = (jax.ShapeDtypeStruct((m, LANES), BF16), row(LANES))
        trans = (jax.ShapeDtypeStruct((batch, LANES, seq), BF16),
                 pl.BlockSpec((None, LANES, tm), lambda i: (i // tpb, 0, i % tpb)))
        for sh, sp in (rowmaj, trans, rowmaj, trans):
            out_shape.append(sh)
            out_specs.append(sp)
        out_shape += [jax.ShapeDtypeStruct((m, LANES), F32)] * 2
        out_specs += [row(LANES)] * 2
    else:
        out_shape += [jax.ShapeDtypeStruct((m, NSA_Q), F32)] * 2
        out_specs += [row(NSA_Q)] * 2
    return pl.pallas_call(
        functools.partial(_inproj_kernel, tm=tm, attn_layout=attn_layout),
        out_shape=out_shape,
        grid=(nt,),
        in_specs=in_specs,
        out_specs=out_specs,
        compiler_params=_cparams(("parallel",)),
        name="inproj",
    )(*ins)


def _compress_kernel(zk_ref, zkn_ref, zv_ref, zvn_ref, pos_ref, wak_ref, wbk_ref, wav_ref, wbv_ref,
                     w2k_ref, w2v_ref, gkc_ref, sm_ref, kc_ref, vct_ref):
    pos = pos_ref[...]

    def one(z_ref, zn_ref, pa, pb, wa_ref, wb_ref, w2_ref):
        a = _dot((z_ref[...] + pa).astype(BF16), wa_ref[...])
        b = _dot((zn_ref[...] + pb).astype(BF16), wb_ref[...])
        return _dot(_gelu(a + b).astype(BF16), w2_ref[...])

    ck = one(zk_ref, zkn_ref, pos[0:1], pos[1:2], wak_ref, wbk_ref, w2k_ref)
    cv = one(zv_ref, zvn_ref, pos[2:3], pos[3:4], wav_ref, wbv_ref, w2v_ref)
    kc_ref[...] = _seg_rms(ck, gkc_ref[...], sm_ref[...]).astype(BF16)
    vct_ref[...] = cv.T.astype(BF16)


def _compress(zk, zkn, zv, zvn, lw, *, rows):
    b, nc, kdim = zk.shape
    zspec = pl.BlockSpec((None, rows, kdim), lambda i, j: (i, j, 0))
    full = lambda a: pl.BlockSpec(a.shape, lambda i, j: (0,) * a.ndim)
    ws = [lw["cmp_pos"], lw["cmp_wak"], lw["cmp_wbk"], lw["cmp_wav"], lw["cmp_wbv"], lw["cmp_w2k"], lw["cmp_w2v"],
          lw["gkc"], lw["sm"]]
    return pl.pallas_call(
        _compress_kernel,
        out_shape=[jax.ShapeDtypeStruct((b, nc, LANES), BF16), jax.ShapeDtypeStruct((b, LANES, nc), BF16)],
        grid=(b, nc // rows),
        in_specs=[zspec] * 4 + [full(w) for w in ws],
        out_specs=[pl.BlockSpec((None, rows, LANES), lambda i, j: (i, j, 0)),
                   pl.BlockSpec((None, LANES, rows), lambda i, j: (i, 0, j))],
        compiler_params=_cparams(("parallel", "parallel")),
        name="nsa_compress",
    )(zk, zkn, zv, zvn, *ws)


def _nsa_kernel(qs_ref, qrs_ref, gt_ref, kc_ref, vct_ref, cov_ref, ksel_ref, vselt_ref, kwin_ref, vwint_ref,
                o_ref, score_scr, bias_scr, sa_scr, sb_scr, *, nc, ns, kb_keys):
    ib = pl.program_id(1)
    p0 = ib * Q_BLOCK
    nl = NSA_REP * Q_BLOCK
    ng = NSA_KV_HEADS
    t_row = p0 + lax.broadcasted_iota(jnp.int32, (1, nl), 1) % Q_BLOCK
    t_q = p0 + lax.broadcasted_iota(jnp.int32, (1, Q_BLOCK), 1)
    gt = gt_ref[...]
    n_top = min(N_SEL, ns)
    per_kb = kb_keys // SEL_BLOCK
    grows = [slice(g * HEAD_DIM, (g + 1) * HEAD_DIM) for g in range(ng)]

    def q_of(ref, g):
        return ref[NSA_REP * g:NSA_REP * (g + 1)].reshape(nl, LANES)

    ones_rows = 16

    def with_ones(vt):
        return jnp.concatenate([vt, jnp.ones((ones_rows, vt.shape[1]), BF16)], axis=0)

    def online(carry, s, vt):
        m, acc = carry
        m_new = jnp.maximum(m, jnp.max(s, axis=0, keepdims=True))
        p = jnp.exp2(s - m_new)
        acc = jnp.exp2(m - m_new) * acc + _dot(with_ones(vt), p.astype(BF16))
        return m_new, acc

    def normalised(acc):
        return acc[0:HEAD_DIM] * (1.0 / jnp.maximum(acc[HEAD_DIM:HEAD_DIM + 1], 1e-30))

    init1 = (jnp.full((1, nl), NEG_BIG, F32), jnp.zeros((HEAD_DIM + ones_rows, nl), F32))
    init = tuple(init1 for _ in range(ng))

    sj = lax.broadcasted_iota(jnp.int32, (ns, 1), 0)
    o_cmp = []
    for g in range(ng):
        s = _dot_nt(kc_ref[...], q_of(qs_ref, g))
        ci = lax.broadcasted_iota(jnp.int32, (nc, 1), 0)
        cmask = (ci * CMP_STRIDE + (CMP_BLOCK - 1) <= t_row) & (ci < nc - 1)
        s = jnp.where(cmask, s, -jnp.inf)
        m = jnp.max(s, axis=0, keepdims=True)
        m = jnp.where(m == -jnp.inf, 0.0, m)
        e = jnp.exp2(s - m)
        l = jnp.sum(e, axis=0, keepdims=True)
        p = e * (1.0 / jnp.maximum(l, 1e-30))
        o_cmp.append(_dot(vct_ref[grows[g], :], p.astype(BF16)))
        psum = p[:, 0:Q_BLOCK]
        for r in range(1, NSA_REP):
            psum = psum + p[:, r * Q_BLOCK:(r + 1) * Q_BLOCK]
        imp = _dot(cov_ref[...], psum.astype(BF16))
        bt = t_q // SEL_BLOCK
        allowed = sj * SEL_BLOCK <= t_q
        forced = (sj == 0) | (sj == bt) | (sj == bt - 1)
        score_scr[g] = jnp.where(allowed, imp + jnp.where(forced, np.float32(SEL_BONUS), 0.0), -jnp.inf)

    sjf = sj.astype(F32)
    left = [score_scr[g] for g in range(ng)]
    bias = [jnp.full((ns, Q_BLOCK), -jnp.inf, F32) for _ in range(ng)]
    for _ in range(n_top):
        for g in range(ng):
            mx = jnp.max(left[g], axis=0, keepdims=True)
            idx = jnp.min(jnp.where(left[g] == mx, sjf, np.float32(ns)), axis=0, keepdims=True)
            hit = sjf == jnp.where(mx > -jnp.inf, idx, -1.0)
            left[g] = jnp.where(hit, -jnp.inf, left[g])
            bias[g] = jnp.where(hit, 0.0, bias[g])
    for g in range(ng):
        bias_scr[g] = bias[g]

    def sel_qk(kb, g, dst):
        k0 = pl.multiple_of(kb * kb_keys, kb_keys)
        dst[g] = _dot_nt(ksel_ref[pl.ds(k0, kb_keys), :], q_of(qrs_ref, g))

    def sel_step(carry_g, src, kb, g, causal):
        k0 = pl.multiple_of(kb * kb_keys, kb_keys)
        parts = []
        for i in range(per_kb):
            row = bias_scr[g, pl.ds(kb * per_kb + i, 1), :]
            sl = slice(i * SEL_BLOCK, (i + 1) * SEL_BLOCK)
            parts.append(src[g, sl, :] + jnp.concatenate([row] * NSA_REP, axis=1))
        s = jnp.concatenate(parts, axis=0)
        if causal:
            kpos = k0 + lax.broadcasted_iota(jnp.int32, (kb_keys, 1), 0)
            s = jnp.where(kpos <= t_row, s, -jnp.inf)
        return online(carry_g, s, vselt_ref[grows[g], pl.ds(k0, kb_keys)])

    def sel_pair(j, carry, causal):
        a = 2 * j
        for g in range(ng):
            sel_qk(a + 1, g, sb_scr)
        carry = tuple(sel_step(carry[g], sa_scr, a, g, causal) for g in range(ng))
        if not causal:
            for g in range(ng):
                sel_qk(a + 2, g, sa_scr)
        return tuple(sel_step(carry[g], sb_scr, a + 1, g, causal) for g in range(ng))

    n_kb = (p0 + Q_BLOCK + kb_keys - 1) // kb_keys
    n_pairs = (n_kb + 1) // 2
    for g in range(ng):
        sel_qk(0, g, sa_scr)
    carry = lax.fori_loop(0, n_pairs - 1, lambda j, c: sel_pair(j, c, False), init)
    carry = sel_pair(n_pairs - 1, carry, True)
    o_sel = [normalised(carry[g][1]) for g in range(ng)]

    wkeys = WINDOW + Q_BLOCK
    w0 = pl.multiple_of(jnp.maximum(p0 - WINDOW, 0), Q_BLOCK)
    diff = t_row - (w0 + lax.broadcasted_iota(jnp.int32, (wkeys, 1), 0))
    wmask = (diff >= 0) & (diff < WINDOW)
    kblk = kwin_ref[pl.ds(w0, wkeys), :]
    o_win = []
    for g in range(ng):
        s = jnp.where(wmask, _dot_nt(kblk, q_of(qrs_ref, g)), -jnp.inf)
        o_win.append(normalised(online(init1, s, vwint_ref[grows[g], pl.ds(w0, wkeys)])[1]))

    for g in range(ng):
        o_w = o_win[g]

        def gate_row(jb):
            return jnp.concatenate(
                [gt[(NSA_REP * g + r) * 3 + jb:(NSA_REP * g + r) * 3 + jb + 1, :] for r in range(NSA_REP)], axis=1)

        o = o_cmp[g] * gate_row(0) + o_sel[g] * gate_row(1) + o_w * gate_row(2)
        for pr in range(NSA_REP // 2):
            blk = jnp.concatenate([o[:, (2 * pr) * Q_BLOCK:(2 * pr + 1) * Q_BLOCK],
                                   o[:, (2 * pr + 1) * Q_BLOCK:(2 * pr + 2) * Q_BLOCK]], axis=0)
            c0 = g * NSA_REP * HEAD_DIM + pr * LANES
            o_ref[:, c0:c0 + LANES] = blk.T


def _nsa_prompt(qs, qrs, gt, kc, vct, cov, ksel_r, vsel_t, kwin_r, vwin_t, *, batch, seq):
    nb = seq // Q_BLOCK
    nc = kc.shape[1]
    ns = cov.shape[0]
    kb_keys = min(512, seq)
    per_b3 = lambda a: pl.BlockSpec((None,) + a.shape[1:], lambda b, i: (b, 0, 0))
    return pl.pallas_call(
        functools.partial(_nsa_kernel, nc=nc, ns=ns, kb_keys=kb_keys),
        out_shape=jax.ShapeDtypeStruct((batch * seq, NSA_Q), F32),
        grid=(batch, nb),
        in_specs=[
            pl.BlockSpec((None, NSA_HEADS, Q_BLOCK, LANES), lambda b, i: (b * nb + i, 0, 0, 0)),
            pl.BlockSpec((None, NSA_HEADS, Q_BLOCK, LANES), lambda b, i: (b * nb + i, 0, 0, 0)),
            pl.BlockSpec((None, LANES, Q_BLOCK), lambda b, i: (b * nb + i, 0, 0)),
            per_b3(kc), per_b3(vct),
            pl.BlockSpec(cov.shape, lambda b, i: (0, 0)),
            per_b3(ksel_r), per_b3(vsel_t), per_b3(kwin_r), per_b3(vwin_t),
        ],
        out_specs=pl.BlockSpec((Q_BLOCK, NSA_Q), lambda b, i: (b * nb + i, 0)),
        scratch_shapes=[pltpu.VMEM((NSA_KV_HEADS, ns, Q_BLOCK), F32), pltpu.VMEM((NSA_KV_HEADS, ns, Q_BLOCK), F32),
                        pltpu.VMEM((NSA_KV_HEADS, kb_keys, NSA_REP * Q_BLOCK), F32),
                        pltpu.VMEM((NSA_KV_HEADS, kb_keys, NSA_REP * Q_BLOCK), F32)],
        compiler_params=_cparams(("parallel", "arbitrary")),
        name="nsa_prompt",
    )(qs, qrs, gt, kc, vct, cov, ksel_r, vsel_t, kwin_r, vwin_t)


def _gla_rows(c):
    offs, n = [], 0
    for s in range(c):
        t0 = (s // 8) * 8
        offs.append((n, t0))
        n += c - t0
    return offs, n


def _gla_kernel(q_ref, k_ref, la_ref, v_ref, rs_ref, gn_ref, bm_ref, bmask_ref, sm_ref, s0_ref,
                o_ref, sout_ref, s_scr, prod_scr, res_scr, *, c, nchunks, bpb):
    ci = pl.program_id(1)

    @pl.when(ci == 0)
    def _():
        s_scr[...] = s0_ref[...]
        prod_scr[...] = jnp.zeros_like(prod_scr)

    for bb in range(bpb):
        _gla_one(q_ref.at[bb], k_ref.at[bb], la_ref.at[bb], v_ref.at[bb], rs_ref.at[bb], gn_ref, bm_ref, bmask_ref,
                 sm_ref, o_ref.at[bb], sout_ref.at[bb], s_scr.at[bb], prod_scr.at[bb], res_scr.at[bb],
                 c=c, last=ci == nchunks - 1)


def _gla_one(q_ref, k_ref, la_ref, v_ref, rs_ref, gn_ref, bm_ref, bmask_ref, sm_ref, o_ref, sout_ref,
             s_scr, prod_scr, res_scr, *, c, last):
    q = q_ref[...]
    k = k_ref[...]
    v = v_ref[...]
    la = la_ref[...]
    tt = lax.broadcasted_iota(jnp.int32, (c, 1), 0)
    b = la
    sh = 1
    while sh < c:
        b = b + jnp.where(tt >= sh, pltpu.roll(b, sh, axis=0), 0.0)
        sh *= 2
    state = s_scr[...]
    inter = _dot_nt((q * jnp.exp(b)).astype(BF16), state.astype(BF16))

    offs, _ = _gla_rows(c)
    for s in range(c):
        r0, t0 = offs[s]
        d = b[t0:] - b[s:s + 1]
        e = jnp.exp(jnp.where(tt[t0:] >= s, d, -jnp.inf))
        prod_scr[r0:r0 + c - t0, :] = (q[t0:] * k[s:s + 1] * e).astype(BF16)
    res_scr[...] = _dot(prod_scr[...], bm_ref[...])
    pieces = []
    for tb in range(0, c, 8):
        hi = min(tb + 8, c)
        acc = inter[tb:hi]
        for s in range(hi):
            r0, t0 = offs[s]
            if t0 <= tb:
                acc = acc + res_scr[r0 + tb - t0:r0 + hi - t0, :] * v[s:s + 1]
        pieces.append(acc)
    o = pieces[0] if len(pieces) == 1 else jnp.concatenate(pieces, axis=0)

    bl = b[c - 1:c]
    kd = (k * jnp.exp(bl - b)).astype(BF16)
    if c >= 16:
        upd = _dot(v.T.astype(BF16), kd)
    else:
        upd = jnp.dot(v.T, kd.astype(F32), preferred_element_type=F32)
    new_state = jnp.exp(bl) * state + upd * bmask_ref[...]
    s_scr[...] = new_state

    o_ref[...] = _seg_rms(o, gn_ref[...], sm_ref[...]) * rs_ref[...]

    @pl.when(last)
    def _():
        sout_ref[...] = new_state


def _gla(qg, kg, la, vg, rs, s0_bd, lw, *, batch, seq, c, bpb):
    nchunks = seq // c
    _, npack = _gla_rows(c)
    npad = -(-npack // 16) * 16
    blk = lambda w: pl.BlockSpec((bpb, c, w), lambda b, i: (b, i, 0))
    full = lambda a: pl.BlockSpec(a.shape, lambda b, i: (0,) * a.ndim)
    consts = [lw["gla_gn"], lw["gla_bm"], lw["gla_bmask"], lw["sm"]]
    st = pl.BlockSpec((bpb, GLA_V, GLA_QK), lambda b, i: (b, 0, 0))
    return pl.pallas_call(
        functools.partial(_gla_kernel, c=c, nchunks=nchunks, bpb=bpb),
        out_shape=[jax.ShapeDtypeStruct((batch, seq, GLA_V), F32), jax.ShapeDtypeStruct((batch, GLA_V, GLA_QK), F32)],
        grid=(batch // bpb, nchunks),
        in_specs=[blk(GLA_QK), blk(GLA_QK), blk(GLA_QK), blk(GLA_V), blk(GLA_V)] + [full(a) for a in consts] + [st],
        out_specs=[blk(GLA_V), st],
        scratch_shapes=[pltpu.VMEM((bpb, GLA_V, GLA_QK), F32), pltpu.VMEM((bpb, npad, GLA_QK), BF16),
                        pltpu.VMEM((bpb, npad, GLA_V), F32)],
        compiler_params=_cparams(("parallel", "arbitrary")),
        name="gla",
    )(qg, kg, la, vg, rs, *consts, s0_bd)


def _outproj_kernel(x_ref, oa_ref, ob_ref, u_ref, vn_ref, ws_ref, bias_ref, wo_ref, o_ref, *, tm):
    lane = lax.broadcasted_iota(jnp.int32, (GM_CHUNK, GM_W), 1)
    tri = (lax.broadcasted_iota(jnp.int32, (GM_CHUNK, GM_CHUNK), 0)
           >= lax.broadcasted_iota(jnp.int32, (GM_CHUNK, GM_CHUNK), 1))
    zs = []
    for cb in range(tm // GM_CHUNK):
        vn = vn_ref[cb * GM_CHUNK:(cb + 1) * GM_CHUNK, :]
        z = bias_ref[...]
        for g in range(GM_GROUPS):
            wm = jnp.where(tri, ws_ref[g], 0.0).astype(BF16)
            vg = jnp.where((lane >= g * GM_CH) & (lane < (g + 1) * GM_CH), vn, 0.0).astype(BF16)
            z = z + _dot(wm, vg)
        zs.append(z)
    z = zs[0] if len(zs) == 1 else jnp.concatenate(zs, axis=0)
    oc = u_ref[...] * z
    y = _dot(oa_ref[...].astype(BF16), wo_ref[0:NSA_Q, :])
    y = y + _dot(ob_ref[...].astype(BF16), wo_ref[NSA_Q:NSA_Q + GLA_V, :])
    y = y + _dot(oc.astype(BF16), wo_ref[NSA_Q + GLA_V:MIX_OUT, :])
    o_ref[...] = x_ref[...] + y


def _outproj(x, oa, ob, u, vn, ws, bias, wo, li, *, tm):
    m = x.shape[0]
    row = lambda w: pl.BlockSpec((tm, w), lambda i: (i, 0))
    return pl.pallas_call(
        functools.partial(_outproj_kernel, tm=tm),
        out_shape=jax.ShapeDtypeStruct((m, D_MODEL), F32),
        grid=(m // tm,),
        in_specs=[row(D_MODEL), row(NSA_Q), row(GLA_V), row(GM_W), row(GM_W),
                  pl.BlockSpec(ws.shape, lambda i: (0, 0, 0)), pl.BlockSpec(bias.shape, lambda i: (0, 0)),
                  pl.BlockSpec((None, MIX_OUT, D_MODEL), lambda i: (li, 0, 0))],
        out_specs=row(D_MODEL),
        compiler_params=_cparams(("parallel",)),
        name="outproj",
    )(x, oa, ob, u, vn, ws, bias, wo)


def _pcompress_kernel(pt_ref, *refs, npg):
    pages = refs[:npg]
    perm_ref, wk_ref, wv_ref, out_ref, zk_scr, zv_scr = refs[npg:npg + 6]
    perm = perm_ref[...]
    for pp in range(npg // 2):
        xt = jnp.concatenate([pages[2 * pp][...], pages[2 * pp + 1][...]], axis=1).astype(BF16)
        y = _dot_nt(perm, xt).astype(BF16)
        for s in range(CMP_STRIDE):
            rows = slice(pp * 16, (pp + 1) * 16)
            zk_scr[rows, s * LANES:(s + 1) * LANES] = y[s * 16:(s + 1) * 16, 0:LANES]
            zv_scr[rows, s * LANES:(s + 1) * LANES] = y[s * 16:(s + 1) * 16, LANES:2 * LANES]
    q = out_ref.shape[1] // 4
    rk = _dot(zk_scr[...], wk_ref[...])
    rv = _dot(zv_scr[...], wv_ref[...])
    out_ref[:, 0:q] = rk[:, 0:q]
    out_ref[:, q:2 * q] = rv[:, 0:q]
    out_ref[:, 2 * q:3 * q] = rk[:, q:2 * q]
    out_ref[:, 3 * q:4 * q] = rv[:, q:2 * q]


def _pcompress(cache_t, page_table, li, perm, wk, wv, *, npg):
    bs, n_pages = page_table.shape
    page = cache_t.shape[-1]
    nc = n_pages * page // CMP_STRIDE
    steps = n_pages // npg
    cpp = page // CMP_STRIDE

    def page_spec(k):
        return pl.BlockSpec((None, None, 2 * LANES, page), lambda b, h, pt: (li, pt[b, h * npg + k], 0, 0))

    grid_spec = pltpu.PrefetchScalarGridSpec(
        num_scalar_prefetch=1,
        grid=(bs, steps),
        in_specs=[page_spec(k) for k in range(npg)] + [
            pl.BlockSpec(perm.shape, lambda b, h, pt: (0, 0)), pl.BlockSpec(wk.shape, lambda b, h, pt: (0, 0)),
            pl.BlockSpec(wv.shape, lambda b, h, pt: (0, 0))],
        out_specs=pl.BlockSpec((None, npg * cpp, 2 * wk.shape[1]), lambda b, h, pt: (b, h, 0)),
        scratch_shapes=[pltpu.VMEM((npg * cpp, wk.shape[0]), BF16)] * 2,
    )
    return pl.pallas_call(
        functools.partial(_pcompress_kernel, npg=npg),
        out_shape=jax.ShapeDtypeStruct((bs, nc, 2 * wk.shape[1]), F32),
        grid_spec=grid_spec,
        compiler_params=_cparams(("parallel", "arbitrary")),
        name="nsa_page_compress",
    )(page_table, *([cache_t] * npg), perm, wk, wv)


def _ctail_kernel(a_ref, b_ref, bias_ref, w2_ref, gkc_ref, sm_ref, kct_ref, vcr_ref):
    h = a_ref[...] + b_ref[...] + bias_ref[...]
    out = _dot(_gelu(h).astype(BF16), w2_ref[...])
    kc = _seg_rms(out[:, 0:LANES], gkc_ref[...], sm_ref[...])
    kct_ref[...] = kc.T.astype(BF16)
    vcr_ref[...] = out[:, LANES:2 * LANES].astype(BF16)


def _ctail(hid_ab, hid_b_next, bias, w2, gkc, sm):
    bs, nc, w2x = hid_ab.shape
    half = w2x // 2
    full = lambda a: pl.BlockSpec(a.shape, lambda b: (0,) * a.ndim)
    return pl.pallas_call(
        _ctail_kernel,
        out_shape=[jax.ShapeDtypeStruct((bs, LANES, nc), BF16), jax.ShapeDtypeStruct((bs, nc, LANES), BF16)],
        grid=(bs,),
        in_specs=[pl.BlockSpec((None, nc, half), lambda b: (b, 0, 0)), pl.BlockSpec((None, nc, half), lambda b: (b, 0, 0)),
                  full(bias), full(w2), full(gkc), full(sm)],
        out_specs=[pl.BlockSpec((None, LANES, nc), lambda b: (b, 0, 0)), pl.BlockSpec((None, nc, LANES), lambda b: (b, 0, 0))],
        compiler_params=_cparams(("parallel",)),
        name="nsa_compress_tail",
    )(hid_ab, hid_b_next, bias, w2, gkc, sm)


def _nsa_sample_kernel(pt_ref, qp_ref, qrp_ref, gcol_ref, kct_ref, vcr_ref, cov_ref, e_ref, *refs,
                       npg, nsteps, nc, ns_tot, past_len, ts):
    pages = refs[:npg]
    (wint_ref, newk_ref, neww_ref, newwsh_ref, o_ref, winout_ref,
     bias_scr, m_scr, l_scr, acc_scr, oc_scr, ow_scr) = refs[npg:]
    step = pl.program_id(1)
    nrow = NSA_KV_HEADS * NSA_REP * 8
    q8 = lax.broadcasted_iota(jnp.int32, (nrow, 1), 0) % 8
    t_row = past_len + q8
    qrp = qrp_ref[...]
    n_keep = wint_ref.shape[-1]
    n_top = min(N_SEL, ns_tot)
    nsp = bias_scr.shape[1]

    def expand_rows(a):
        return jnp.concatenate([a[0:8]] * NSA_REP + [a[8:16]] * NSA_REP, axis=0)

    @pl.when(step == 0)
    def _():
        s = _dot(qp_ref[...], kct_ref[...])
        ci = lax.broadcasted_iota(jnp.int32, (1, nc), 1)
        cmask = (ci * CMP_STRIDE + (CMP_BLOCK - 1) <= t_row) & (ci < nc - 1)
        s = jnp.where(cmask, s, -jnp.inf)
        m = jnp.max(s, axis=1, keepdims=True)
        m = jnp.where(m == -jnp.inf, 0.0, m)
        e = jnp.exp(s - m)
        p = e * (1.0 / jnp.maximum(jnp.sum(e, axis=1, keepdims=True), 1e-30))
        oc_scr[...] = _dot(p.astype(BF16), vcr_ref[...])
        sjl = lax.broadcasted_iota(jnp.int32, (1, nsp), 1)
        tq = past_len + lax.broadcasted_iota(jnp.int32, (8, 1), 0)
        bt = tq // SEL_BLOCK
        allowed = (sjl * SEL_BLOCK <= tq) & (sjl < ns_tot)
        forced = (sjl == 0) | (sjl == bt) | (sjl == bt - 1)
        for g in range(NSA_KV_HEADS):
            base = g * NSA_REP * 8
            psum = p[base:base + 8]
            for r in range(1, NSA_REP):
                psum = psum + p[base + r * 8:base + (r + 1) * 8]
            imp = _dot(psum.astype(BF16), cov_ref[...])
            score = jnp.where(allowed, imp + jnp.where(forced, np.float32(SEL_BONUS), 0.0), -jnp.inf)
            rank = jnp.zeros((8, nsp), jnp.int32)
            for k in range(ns_tot):
                col = score[:, k:k + 1]
                later = jnp.where(sjl > k, 1, 0)
                rank = rank + jnp.where(col > score, 1, 0) + jnp.where(col == score, later, 0)
            keep = (rank < n_top) & (score > -jnp.inf)
            bias_scr[g * 8:(g + 1) * 8, :] = jnp.where(keep, 0.0, -jnp.inf)
        m_scr[...] = jnp.full(m_scr.shape, NEG_BIG, F32)
        l_scr[...] = jnp.zeros(l_scr.shape, F32)
        acc_scr[...] = jnp.zeros(acc_scr.shape, F32)

        lane_w = lax.broadcasted_iota(jnp.int32, (1, n_keep), 1)
        diff = t_row - (past_len - n_keep + lane_w)
        s_w = jnp.where((diff >= 0) & (diff < WINDOW), _dot(qrp, wint_ref[0:LANES, :].astype(BF16)), -jnp.inf)
        lane_n = lax.broadcasted_iota(jnp.int32, (1, LANES), 1)
        diff_n = t_row - (past_len + lane_n)
        s_n = jnp.where((lane_n < ts) & (diff_n >= 0) & (diff_n < WINDOW),
                        _dot(qrp, neww_ref[0:LANES, :].astype(BF16)), -jnp.inf)
        sw = jnp.concatenate([s_w, s_n], axis=1)
        mw = jnp.max(sw, axis=1, keepdims=True)
        mw = jnp.where(mw == -jnp.inf, 0.0, mw)
        ew = jnp.exp(sw - mw)
        pw = ew * (1.0 / jnp.maximum(jnp.sum(ew, axis=1, keepdims=True), 1e-30))
        ow_scr[...] = (_dot_nt(pw[:, 0:n_keep].astype(BF16), wint_ref[LANES:2 * LANES, :].astype(BF16))
                       + _dot_nt(pw[:, n_keep:].astype(BF16), neww_ref[LANES:2 * LANES, :].astype(BF16)))
        rolled = pltpu.roll(wint_ref[...], n_keep - ts, axis=1)
        lane_o = lax.broadcasted_iota(jnp.int32, (2 * LANES, LANES), 1)
        winout_ref[:, 0:n_keep - LANES] = rolled[:, 0:n_keep - LANES]
        winout_ref[:, n_keep - LANES:n_keep] = jnp.where(lane_o >= LANES - ts, newwsh_ref[...],
                                                         rolled[:, n_keep - LANES:n_keep])

    def online(s, vt):
        m_old = m_scr[:, 0:1]
        m_new = jnp.maximum(m_old, jnp.max(s, axis=1, keepdims=True))
        p = jnp.exp(s - m_new)
        alpha = jnp.exp(m_old - m_new)
        l_new = alpha * l_scr[:, 0:1] + jnp.sum(p, axis=1, keepdims=True)
        acc_scr[...] = alpha * acc_scr[...] + _dot_nt(p.astype(BF16), vt)
        m_scr[...] = jnp.broadcast_to(m_new, m_scr.shape)
        l_scr[...] = jnp.broadcast_to(l_new, l_scr.shape)

    sel01 = jnp.where(bias_scr[...] == 0.0, 1.0, 0.0).astype(BF16)
    bexp = _dot(sel01, e_ref[...])
    bias = expand_rows(jnp.where(bexp > 0.5, 0.0, -jnp.inf))
    kt = jnp.concatenate([pg[0:LANES, :] for pg in pages], axis=1).astype(BF16)
    vt = jnp.concatenate([pg[LANES:2 * LANES, :] for pg in pages], axis=1).astype(BF16)
    online(_dot(qrp, kt) + bias, vt)

    @pl.when(step == nsteps - 1)
    def _():
        lane_n = lax.broadcasted_iota(jnp.int32, (1, LANES), 1)
        bcol = expand_rows(bias_scr[:, ns_tot - 1:ns_tot])
        ok = (lane_n < ts) & (past_len + lane_n <= t_row)
        s_n = jnp.where(ok, _dot(qrp, newk_ref[2 * LANES:3 * LANES, :].astype(BF16)) + bcol, -jnp.inf)
        online(s_n, newk_ref[3 * LANES:4 * LANES, :].astype(BF16))
        o_s = acc_scr[...] * (1.0 / jnp.maximum(l_scr[:, 0:1], 1e-30))
        gc = gcol_ref[...]
        o_ref[...] = oc_scr[...] * gc[:, 0:1] + o_s * gc[:, 1:2] + ow_scr[...] * gc[:, 2:3]


def _nsa_sample(cache_t, page_table, li, qp, qrp, gcol, kct, vcr, cov_s, emat, win_t, newk_t, neww_t, newwsh_t, *,
                npg, past_len, ts):
    bs, n_pages = page_table.shape
    page = cache_t.shape[-1]
    nsteps = n_pages // npg
    nc = kct.shape[-1]
    ns_tot = -(-(past_len + ts) // SEL_BLOCK)
    nsp = cov_s.shape[1]
    n_keep = win_t.shape[-1]
    nrow = qp.shape[1]
    per_b = lambda a: pl.BlockSpec((None,) + a.shape[1:], lambda b, h, pt: (b,) + (0,) * (a.ndim - 1))

    def page_spec(k):
        return pl.BlockSpec((None, None, 2 * LANES, page), lambda b, h, pt: (li, pt[b, h * npg + k], 1, 0))

    grid_spec = pltpu.PrefetchScalarGridSpec(
        num_scalar_prefetch=1,
        grid=(bs, nsteps),
        in_specs=[per_b(qp), per_b(qrp), per_b(gcol), per_b(kct), per_b(vcr),
                  pl.BlockSpec(cov_s.shape, lambda b, h, pt: (0, 0)),
                  pl.BlockSpec((nsp, npg * page), lambda b, h, pt: (0, h))]
                 + [page_spec(k) for k in range(npg)]
                 + [pl.BlockSpec((None, None, 2 * LANES, n_keep), lambda b, h, pt: (li, b, 0, 0)),
                    per_b(newk_t), per_b(neww_t), per_b(newwsh_t)],
        out_specs=[pl.BlockSpec((None, nrow, LANES), lambda b, h, pt: (b, 0, 0)),
                   pl.BlockSpec((None, 2 * LANES, n_keep), lambda b, h, pt: (b, 0, 0))],
        scratch_shapes=[pltpu.VMEM((2 * 8, nsp), F32)] + [pltpu.VMEM((nrow, LANES), F32)] * 5,
    )
    return pl.pallas_call(
        functools.partial(_nsa_sample_kernel, npg=npg, nsteps=nsteps, nc=nc, ns_tot=ns_tot, past_len=past_len, ts=ts),
        out_shape=[jax.ShapeDtypeStruct((bs, nrow, LANES), F32), jax.ShapeDtypeStruct((bs, 2 * LANES, n_keep), F32)],
        grid_spec=grid_spec,
        compiler_params=_cparams(("parallel", "arbitrary")),
        name="nsa_sample",
    )(page_table, qp, qrp, gcol, kct, vcr, cov_s, emat, *([cache_t] * npg), win_t, newk_t, neww_t, newwsh_t)


def _pad_cols(w):
    cuts = np.cumsum((0,) + IN_SPLITS)
    parts = []
    for i, (n, p) in enumerate(zip(IN_SPLITS, IN_PADDED)):
        seg = w[..., cuts[i]:cuts[i] + n]
        if p != n:
            seg = jnp.pad(seg, [(0, 0)] * (w.ndim - 1) + [(0, p - n)])
        parts.append(seg)
    return jnp.concatenate(parts, axis=-1)


def _rope_tables(pos):
    half = HEAD_DIM // 2
    inv = 1.0 / (ROPE_THETA ** (jnp.arange(half, dtype=F32) * (2.0 / HEAD_DIM)))
    ang = pos.astype(F32)[:, None] * inv[None, :]
    cos = jnp.cos(ang)
    sin = jnp.sin(ang)
    cos_f = jnp.concatenate([cos, cos, cos, cos], axis=1)
    sin_f = jnp.concatenate([-sin, sin, -sin, sin], axis=1)
    return cos_f, sin_f


def _cover_t(seq):
    nc = seq // CMP_STRIDE
    ns = seq // SEL_BLOCK
    ci = np.arange(nc)[None, :]
    sj = np.arange(ns)[:, None]
    cov = ((ci * CMP_STRIDE <= sj * SEL_BLOCK + SEL_BLOCK - 1)
           & (ci * CMP_STRIDE + CMP_BLOCK - 1 >= sj * SEL_BLOCK) & (ci < nc - 1))
    return jnp.asarray(cov, dtype=BF16)


def _layer_weights(l, ln_gains, w_in_p, nsa_qk_norm, nsa_cmp_pos, nsa_cmp_w1, nsa_cmp_w2, gla_gate_w, gla_gate_b,
                   gla_norm, gm_ln, gm_ws, gm_b):
    eye2 = jnp.eye(NSA_KV_HEADS, dtype=F32)
    lw = {"ln0": ln_gains[l, 0][None], "ln1": ln_gains[l, 1][None], "ln2": ln_gains[l, 2][None], "w_in": w_in_p[l]}
    lw["gq"] = jnp.tile(nsa_qk_norm[l, 0], NSA_HEADS)[None]
    lw["gkc"] = jnp.tile(nsa_qk_norm[l, 1], NSA_KV_HEADS)[None]
    lw["gks"] = jnp.tile(nsa_qk_norm[l, 2], NSA_KV_HEADS)[None]
    lw["gkw"] = jnp.tile(nsa_qk_norm[l, 3], NSA_KV_HEADS)[None]
    seg = (np.arange(LANES)[:, None] // HEAD_DIM) == (np.arange(LANES)[None, :] // HEAD_DIM)
    lw["sm"] = jnp.asarray(seg * (1.0 / HEAD_DIM), dtype=BF16)
    lw["gla_gw"] = jnp.pad(gla_gate_w[l], ((0, LANES - GLA_GATE_RANK), (0, 0))).astype(BF16)
    lw["gla_gb"] = gla_gate_b[l][None]
    lw["gm_lng"] = gm_ln[l, 0][None]
    lw["gm_lnb"] = gm_ln[l, 1][None]
    pos_rows = []
    for c, nm in ((0, "k"), (1, "v")):
        w1 = nsa_cmp_w1[l, c].reshape(CMP_BLOCK, HEAD_DIM, CMP_HIDDEN)
        for half, tag in ((w1[:CMP_STRIDE], "a"), (w1[CMP_STRIDE:], "b")):
            wx = jnp.einsum("sdh,pg->spdgh", half, eye2)
            lw["cmp_w" + tag + nm] = wx.reshape(CMP_STRIDE * LANES, NSA_KV_HEADS * CMP_HIDDEN).astype(BF16)
        lw["cmp_w2" + nm] = jnp.einsum("hd,pg->phgd", nsa_cmp_w2[l, c], eye2).reshape(
            NSA_KV_HEADS * CMP_HIDDEN, LANES).astype(BF16)
        pe = nsa_cmp_pos[l, c]
        for half in (pe[:CMP_STRIDE], pe[CMP_STRIDE:]):
            pos_rows.append(jnp.broadcast_to(half[:, None, :], (CMP_STRIDE, NSA_KV_HEADS, HEAD_DIM)).reshape(-1))
    lw["cmp_pos"] = jnp.stack(pos_rows)
    w1ab = nsa_cmp_w1[l].reshape(2, 2, CMP_STRIDE, HEAD_DIM, CMP_HIDDEN)
    for c, nm in ((0, "k"), (1, "v")):
        lw["pc_w" + nm] = jnp.einsum("asdh,gy->sgdayh", w1ab[c], eye2).reshape(
            CMP_STRIDE * LANES, 2 * NSA_KV_HEADS * CMP_HIDDEN).astype(BF16)
    pb = jnp.einsum("ck,ckh->ch", nsa_cmp_pos[l].reshape(2, -1), nsa_cmp_w1[l], precision=lax.Precision.HIGHEST)
    lw["pc_bias"] = jnp.broadcast_to(pb[:, None, :], (2, NSA_KV_HEADS, CMP_HIDDEN)).reshape(1, -1)
    lw["pc_w2"] = jnp.einsum("chd,cx,gy->cghxyd", nsa_cmp_w2[l], eye2, eye2).reshape(
        2 * NSA_KV_HEADS * CMP_HIDDEN, 2 * LANES).astype(BF16)
    lw["gla_gn"] = jnp.tile(gla_norm[l], GLA_HEADS)[None]
    hq = np.arange(GLA_QK) // GLA_DK
    hv = np.arange(GLA_V) // GLA_DV
    lw["gla_bm"] = jnp.asarray(hq[:, None] == hv[None, :], dtype=BF16)
    lw["gla_bmask"] = jnp.asarray(hv[:, None] == hq[None, :], dtype=F32)
    lw["gm_ws"] = gm_ws[l]
    lw["gm_bias"] = jnp.repeat(gm_b[l].T, GM_CH, axis=1)
    return lw


def _page_perm(page):
    cpp = page // CMP_STRIDE
    r = np.arange(2 * page)
    s_, rem = r // (2 * cpp), r % (2 * cpp)
    t = (rem // cpp) * page + CMP_STRIDE * (rem % cpp) + s_
    m = np.zeros((2 * page, 2 * page), np.float32)
    m[r, t] = 1.0
    return jnp.asarray(m, dtype=BF16)


def _cover_sample(past_len, ts, nsp):
    t_tot = past_len + ts
    n_c = (t_tot - CMP_BLOCK) // CMP_STRIDE + 1
    nc = past_len // CMP_STRIDE
    n_s = -(-t_tot // SEL_BLOCK)
    ci = np.arange(nc)[:, None]
    sj = np.arange(nsp)[None, :]
    cov = ((ci * CMP_STRIDE <= sj * SEL_BLOCK + SEL_BLOCK - 1) & (ci * CMP_STRIDE + CMP_BLOCK - 1 >= sj * SEL_BLOCK)
           & (ci < n_c) & (sj < n_s))
    emat = (np.arange(past_len)[None, :] // SEL_BLOCK) == np.arange(nsp)[:, None]
    return jnp.asarray(cov, dtype=BF16), jnp.asarray(emat, dtype=BF16)


def _rows_gr8(a, ts):
    bs = a.shape[0]
    a = a.reshape(bs, ts, NSA_KV_HEADS, NSA_REP, a.shape[-1]).transpose(0, 2, 3, 1, 4)
    return jnp.pad(a, ((0, 0), (0, 0), (0, 0), (0, 8 - ts), (0, 0)))


def _state_to_t(s):
    b = s.shape[0]
    eye = jnp.eye(GLA_HEADS, dtype=s.dtype)
    return jnp.einsum("bhkv,hg->bhvgk", s, eye).reshape(b, GLA_V, GLA_QK)


def _state_from_t(st):
    b = st.shape[0]
    s5 = st.reshape(b, GLA_HEADS, GLA_DV, GLA_HEADS, GLA_DK)
    d = jnp.stack([s5[:, h, :, h, :] for h in range(GLA_HEADS)], axis=1)
    return jnp.swapaxes(d, 2, 3)


def kernel(x_prompt, x_sample, cache_kv, cache_win_kv, state_gla, page_table, ln_gains, ffn_w_gate_up, ffn_w_down,
           w_in, w_out, nsa_qk_norm, nsa_cmp_pos, nsa_cmp_w1, nsa_cmp_w2, gla_gate_w, gla_gate_b, gla_norm, gm_ln,
           gm_ws, gm_b):
    depth = w_in.shape[0]
    bp, tp, _ = x_prompt.shape
    bs, ts, _ = x_sample.shape
    n_pages = page_table.shape[1]
    page = cache_kv.shape[2]
    past_len = n_pages * page
    mp, ms = bp * tp, bs * ts
    nc = tp // CMP_STRIDE

    w_gu = ffn_w_gate_up.astype(BF16).reshape(depth * 2, D_MODEL, 2 * D_FF)
    w_d = ffn_w_down.astype(BF16).reshape(depth * 2, D_FF, D_MODEL)
    w_in_p = _pad_cols(w_in).astype(BF16)
    w_o = w_out.astype(BF16)
    cos_p, sin_p = _rope_tables(jnp.arange(tp))
    cos_s, sin_s = _rope_tables(past_len + jnp.arange(ms) % ts)
    cov = _cover_t(tp)
    tm_p = 512 if mp % 512 == 0 else Q_BLOCK
    tm_i = 512 if tp % 512 == 0 else Q_BLOCK
    tf = 1408

    nrow = NSA_HEADS * 8
    n_keep_s = cache_win_kv.shape[2]
    nsp = -(-(past_len // SEL_BLOCK + 1) // LANES) * LANES
    cov_s, emat = _cover_sample(past_len, ts, nsp)
    perm = _page_perm(page)
    eye2 = jnp.eye(NSA_KV_HEADS, dtype=F32)
    npg_c = min(32, n_pages)
    npg_a = min(32, n_pages)
    cache_t = jnp.transpose(cache_kv, (0, 1, 3, 4, 5, 2)).reshape(depth, cache_kv.shape[1], 4 * LANES, page)
    win_t = jnp.transpose(cache_win_kv, (0, 1, 3, 4, 5, 2)).reshape(depth, bs, 2 * LANES, n_keep_s)
    nxt = lambda z: jnp.concatenate([z[:, 1:], jnp.zeros_like(z[:, :1])], axis=1)

    xp = x_prompt.reshape(mp, D_MODEL)
    xs = x_sample.reshape(ms, D_MODEL)
    kv_p, win_p, gla_p, kv_s, win_s, gla_s, gmv_s = [], [], [], [], [], [], []
    eye_b = jnp.eye(bs, dtype=F32)
    for l in range(depth):
        lw = _layer_weights(l, ln_gains, w_in_p, nsa_qk_norm, nsa_cmp_pos, nsa_cmp_w1, nsa_cmp_w2, gla_gate_w,
                            gla_gate_b, gla_norm, gm_ln, gm_ws, gm_b)
        xp = _ffn(xp, lw["ln0"], w_gu, w_d, 2 * l, tm=tm_p, tf=tf)
        (newkv, newwin, _, qg, kg, la, vg, rs, u, vn, qs, qrs, gt, ksel_r, vsel_t, kwin_r, vwin_t, kcmp,
         vcmp) = _inproj(xp, lw, cos_p, sin_p, tm=tm_i, attn_layout=True, batch=bp, seq=tp)
        zk = kcmp.reshape(bp, nc, CMP_STRIDE * LANES)
        zv = vcmp.reshape(bp, nc, CMP_STRIDE * LANES)
        kc, vct = _compress(zk, nxt(zk), zv, nxt(zv), lw, rows=min(128, nc))
        oa = _nsa_prompt(qs, qrs, gt, kc, vct, cov, ksel_r.reshape(bp, tp, LANES), vsel_t,
                         kwin_r.reshape(bp, tp, LANES), vwin_t, batch=bp, seq=tp)
        r3 = lambda a: a.reshape(bp, tp, a.shape[-1])
        ob, st = _gla(r3(qg), r3(kg), r3(la), r3(vg), r3(rs), jnp.zeros((bp, GLA_V, GLA_QK), F32), lw,
                      batch=bp, seq=tp, c=GLA_CHUNK, bpb=1)
        xp = _outproj(xp, oa, ob.reshape(mp, GLA_V), u, vn, lw["gm_ws"], lw["gm_bias"], w_o, l, tm=tm_p)
        xp = _ffn(xp, lw["ln2"], w_gu, w_d, 2 * l + 1, tm=tm_p, tf=tf)
        kv_p.append(newkv.reshape(bp, 4, NSA_KV_HEADS, HEAD_DIM, tp).transpose(0, 4, 1, 2, 3))
        n_keep = min(WINDOW, tp)
        win_p.append(newwin[:, :, tp - n_keep:].reshape(bp, 2, NSA_KV_HEADS, HEAD_DIM, n_keep).transpose(0, 4, 1, 2, 3))
        gla_p.append(_state_from_t(st))

        xs = _ffn(xs, lw["ln0"], w_gu, w_d, 2 * l, tm=ms, tf=tf)
        (newkv, newwin, gates, qg, kg, la, vg, rs, u, vn, qn, qr) = _inproj(
            xs, lw, cos_s, sin_s, tm=ms, attn_layout=False, batch=bs, seq=ts)
        hid = _pcompress(cache_t, page_table, l, perm, lw["pc_wk"], lw["pc_wv"], npg=npg_c)
        hb = hid[:, :, hid.shape[2] // 2:]
        kct, vcr = _ctail(hid, nxt(hb), lw["pc_bias"], lw["pc_w2"], lw["gkc"], lw["sm"])
        scale = np.float32(HEAD_DIM ** -0.5)
        to_pad = lambda q: jnp.einsum("bgrqd,gx->bgrqxd", _rows_gr8(q.reshape(bs, ts, NSA_HEADS, HEAD_DIM) * scale, ts),
                                      eye2).reshape(bs, nrow, LANES).astype(BF16)
        gcol = _rows_gr8(gates[:, :3 * NSA_HEADS].reshape(bs, ts, NSA_HEADS, 3), ts).reshape(bs, nrow, 3)
        gcol = jnp.pad(gcol, ((0, 0), (0, 0), (0, LANES - 3)))
        newk_t = jnp.pad(newkv.reshape(bs, ts, -1).transpose(0, 2, 1), ((0, 0), (0, 0), (0, LANES - ts)))
        neww_c = newwin.reshape(bs, ts, -1).transpose(0, 2, 1)
        neww_t = jnp.pad(neww_c, ((0, 0), (0, 0), (0, LANES - ts)))
        newwsh_t = jnp.pad(neww_c, ((0, 0), (0, 0), (LANES - ts, 0)))
        o64, win_o = _nsa_sample(cache_t, page_table, l, to_pad(qn), to_pad(qr), gcol, kct, vcr, cov_s, emat, win_t,
                                 newk_t, neww_t, newwsh_t, npg=npg_a, past_len=past_len, ts=ts)
        o6 = o64.reshape(bs, NSA_KV_HEADS, NSA_REP, 8, NSA_KV_HEADS, HEAD_DIM)
        oa = jnp.stack([o6[:, g, :, :ts, g, :] for g in range(NSA_KV_HEADS)], axis=1)
        oa = oa.transpose(0, 3, 1, 2, 4).reshape(ms, NSA_Q)
        pad8 = lambda a: jnp.pad(a.reshape(bs, ts, a.shape[-1]), ((0, 0), (0, 8 - ts), (0, 0)))
        ob8, st_s = _gla(pad8(qg), pad8(kg), pad8(la), pad8(vg), pad8(rs), _state_to_t(state_gla[l]), lw,
                         batch=bs, seq=8, c=8, bpb=4 if bs % 4 == 0 else 1)
        ob = ob8[:, :ts].reshape(ms, GLA_V)
        ws_s = jnp.einsum("gts,bc->gbtcs", gm_ws[l][:, :ts, :ts], eye_b).reshape(GM_GROUPS, ms, ms)
        bias_s = jnp.tile(lw["gm_bias"][:ts], (bs, 1))
        xs = _outproj(xs, oa, ob, u, vn, ws_s, bias_s, w_o, l, tm=ms)
        xs = _ffn(xs, lw["ln2"], w_gu, w_d, 2 * l + 1, tm=ms, tf=tf)
        kv_s.append(newkv.reshape(bs, ts, 4, NSA_KV_HEADS, HEAD_DIM))
        win_s.append(win_o.reshape(bs, 2, NSA_KV_HEADS, HEAD_DIM, n_keep_s).transpose(0, 4, 1, 2, 3))
        gla_s.append(_state_from_t(st_s))
        gmv_s.append(vn.reshape(bs, ts, GM_GROUPS, GM_CH))
    return (xp.reshape(bp, tp, D_MODEL), xs.reshape(bs, ts, D_MODEL), jnp.stack(kv_p), jnp.stack(win_p),
            jnp.stack(gla_p), jnp.stack(kv_s), jnp.stack(win_s), jnp.stack(gla_s), jnp.stack(gmv_s))
```

```python
import functools

import numpy as np
import jax
import jax.numpy as jnp
from jax import lax
from jax.experimental import pallas as pl
from jax.experimental.pallas import tpu as pltpu

F32 = jnp.float32
BF16 = jnp.bfloat16

D_MODEL = 1024
HEAD_DIM = 64
NSA_HEADS = 8
NSA_KV_HEADS = 2
NSA_REP = NSA_HEADS // NSA_KV_HEADS
CMP_STRIDE = 16
CMP_BLOCK = 2 * CMP_STRIDE
CMP_HIDDEN = 128
SEL_BLOCK = 64
N_SEL = 16
WINDOW = 512
Q_BLOCK = 128
SEL_BONUS = 1.0e4
GLA_HEADS = 4
GLA_DK = 32
GLA_DV = 64
GLA_GATE_RANK = 16
GLA_GATE_TEMP = 16.0
GLA_CHUNK = 64
GM_GROUPS = 4
GM_CH = 64
GM_CHUNK = 128
D_FF = 2816
ROPE_THETA = 10000.0
EPS = 1e-6

NSA_Q = NSA_HEADS * HEAD_DIM
NSA_KV = NSA_KV_HEADS * HEAD_DIM
GLA_QK = GLA_HEADS * GLA_DK
GLA_V = GLA_HEADS * GLA_DV
GM_W = GM_GROUPS * GM_CH
MIX_OUT = NSA_Q + GLA_V + GM_W
IN_SPLITS = (NSA_Q, 6 * NSA_KV, 3 * NSA_HEADS, GLA_QK, GLA_QK, GLA_V, GLA_GATE_RANK, GLA_V, GM_W, GM_W)
IN_PADDED = tuple(-(-s // 128) * 128 for s in IN_SPLITS)
IN_OFFS = tuple(int(v) for v in np.cumsum((0,) + IN_PADDED))
D_IN_PAD = IN_OFFS[-1]

LANES = 128
NEG_BIG = -1.0e30
VMEM_LIMIT = 56 * 1024 * 1024


def _cparams(sem):
    return pltpu.CompilerParams(dimension_semantics=sem, vmem_limit_bytes=VMEM_LIMIT)


def _gelu(x):
    c = np.float32(np.sqrt(2.0 / np.pi))
    return x * (0.5 * (1.0 + jnp.tanh(c * (x + 0.044715 * (x * x * x)))))


def _sigmoid(x):
    return 1.0 / (1.0 + jnp.exp(-x))


def _dot(a, b):
    return jnp.dot(a, b, preferred_element_type=F32)


def _dot_nt(a, b):
    return lax.dot_general(a, b, (((1,), (1,)), ((), ())), preferred_element_type=F32)


def _seg_mean_sq(x, sm):
    sq = x * x
    hi = sq.astype(BF16)
    lo = (sq - hi.astype(F32)).astype(BF16)
    outs = []
    for c in range(x.shape[1] // LANES):
        sl = slice(c * LANES, (c + 1) * LANES)
        outs.append(_dot(hi[:, sl], sm) + _dot(lo[:, sl], sm))
    return outs[0] if len(outs) == 1 else jnp.concatenate(outs, axis=1)


def _seg_rms(x, gain, sm):
    return x * lax.rsqrt(_seg_mean_sq(x, sm) + EPS) * gain


def _tile_lanes(a, w):
    n = w // a.shape[1]
    return a if n == 1 else jnp.concatenate([a] * n, axis=1)


def _rope(x, cos, sin_signed):
    w = x.shape[1]
    lane = lax.broadcasted_iota(jnp.int32, x.shape, 1)
    fwd = pltpu.roll(x, w - HEAD_DIM // 2, axis=1)
    bwd = pltpu.roll(x, HEAD_DIM // 2, axis=1)
    partner = jnp.where((lane % HEAD_DIM) < HEAD_DIM // 2, fwd, bwd)
    return x * _tile_lanes(cos, w) + partner * _tile_lanes(sin_signed, w)


def _ffn_kernel(x_ref, g_ref, wg_ref, wu_ref, wd_ref, o_ref, h_scr, acc_scr, *, nj):
    j = pl.program_id(1)

    @pl.when(j == 0)
    def _():
        x = x_ref[...]
        ms = jnp.mean(x * x, axis=-1, keepdims=True)
        h_scr[...] = (x * lax.rsqrt(ms + EPS) * g_ref[...]).astype(BF16)
        acc_scr[...] = jnp.zeros_like(acc_scr)

    h = h_scr[...]
    g = _dot(h, wg_ref[...])
    u = _dot(h, wu_ref[...])
    a = (g * _sigmoid(g)) * u
    acc_scr[...] += _dot(a.astype(BF16), wd_ref[...])

    @pl.when(j == nj - 1)
    def _():
        o_ref[...] = x_ref[...] + 0.5 * acc_scr[...]


def _ffn(x, gain, w_gu, w_d, li, *, tm, tf):
    m = x.shape[0]
    nj = D_FF // tf
    return pl.pallas_call(
        functools.partial(_ffn_kernel, nj=nj),
        out_shape=jax.ShapeDtypeStruct((m, D_MODEL), F32),
        grid=(m // tm, nj),
        in_specs=[
            pl.BlockSpec((tm, D_MODEL), lambda i, j: (i, 0)),
            pl.BlockSpec((1, D_MODEL), lambda i, j: (0, 0)),
            pl.BlockSpec((None, D_MODEL, tf), lambda i, j: (li, 0, j)),
            pl.BlockSpec((None, D_MODEL, tf), lambda i, j: (li, 0, j + nj)),
            pl.BlockSpec((None, tf, D_MODEL), lambda i, j: (li, j, 0)),
        ],
        out_specs=pl.BlockSpec((tm, D_MODEL), lambda i, j: (i, 0)),
        scratch_shapes=[pltpu.VMEM((tm, D_MODEL), BF16), pltpu.VMEM((tm, D_MODEL), F32)],
        compiler_params=_cparams(("parallel", "arbitrary")),
        name="ffn",
    )(x, gain, w_gu, w_gu, w_d)


def _group_padded(arr, h):
    c = arr[:, (h // 2) * LANES:(h // 2 + 1) * LANES]
    g = h // NSA_REP
    if (h % 2) != g:
        c = pltpu.roll(c, HEAD_DIM, axis=1)
    lane = lax.broadcasted_iota(jnp.int32, c.shape, 1)
    keep = (lane >= g * HEAD_DIM) & (lane < (g + 1) * HEAD_DIM)
    return jnp.where(keep, c, 0.0)


def _inproj_kernel(x_ref, ln_ref, w_ref, gq_ref, gks_ref, gkw_ref, cos_ref, sin_ref, gw_ref, gb_ref,
                   lng_ref, lnb_ref, sm_ref, *outs, tm, attn_layout):
    (newkv_ref, newwin_ref, gates_ref, qg_ref, kg_ref, la_ref, vg_ref, rs_ref, u_ref, vn_ref) = outs[:10]
    x = x_ref[...]
    ms = jnp.mean(x * x, axis=-1, keepdims=True)
    h = (x * lax.rsqrt(ms + EPS) * ln_ref[...]).astype(BF16)
    p = _dot(h, w_ref[...])
    sm = sm_ref[...]
    cos = cos_ref[...]
    sin = sin_ref[...]
    o = IN_OFFS

    def seg(i, a=0, b=None):
        b = IN_PADDED[i] if b is None else b
        return p[:, o[i] + a:o[i] + b]

    qn = _seg_rms(seg(0), gq_ref[...], sm)
    qr = _rope(qn, cos, sin)
    kv = [seg(1, LANES * j, LANES * (j + 1)) for j in range(6)]
    ksel = _rope(_seg_rms(kv[2], gks_ref[...], sm), cos, sin)
    kwin = _rope(_seg_rms(kv[4], gkw_ref[...], sm), cos, sin)
    vsel_t = kv[3].T
    vwin_t = kv[5].T
    if attn_layout:
        for j, a in enumerate((kv[0].T, kv[1].T, ksel.T, vsel_t)):
            newkv_ref[j * LANES:(j + 1) * LANES, :] = a
        newwin_ref[0:LANES, :] = kwin.T
        newwin_ref[LANES:2 * LANES, :] = vwin_t
    else:
        for j, a in enumerate((kv[0], kv[1], ksel, kv[3])):
            newkv_ref[:, j * LANES:(j + 1) * LANES] = a
        newwin_ref[:, 0:LANES] = kwin
        newwin_ref[:, LANES:2 * LANES] = kv[5]
    gates = _sigmoid(seg(2))
    gates_ref[...] = gates
    qg_ref[...] = seg(3) * np.float32(GLA_DK ** -0.5)
    kg_ref[...] = seg(4)
    vg_ref[...] = seg(5)
    logit = _dot(seg(6).astype(BF16), gw_ref[...]) + gb_ref[...]
    log_sig = jnp.minimum(logit, 0.0) - jnp.log1p(jnp.exp(-jnp.abs(logit)))
    la_ref[...] = log_sig * np.float32(1.0 / GLA_GATE_TEMP)
    r = seg(7)
    rs_ref[...] = r * _sigmoid(r)
    u_ref[...] = _gelu(seg(8))
    v = _gelu(seg(9))
    mu = jnp.mean(v, axis=-1, keepdims=True)
    var = jnp.mean(jnp.square(v - mu), axis=-1, keepdims=True)
    vn_ref[...] = (v - mu) * lax.rsqrt(var + EPS) * lng_ref[...] + lnb_ref[...]

    if attn_layout:
        (qs_ref, qrs_ref, gt_ref, kselr_ref, vselt_ref, kwinr_ref, vwint_ref, kcmp_ref, vcmp_ref) = outs[10:]
        scale = np.float32(HEAD_DIM ** -0.5 * np.log2(np.e))
        qs = qn * scale
        qrs = qr * scale
        for hh in range(NSA_HEADS):
            a = _group_padded(qs, hh).astype(BF16)
            b = _group_padded(qrs, hh).astype(BF16)
            for rb in range(tm // Q_BLOCK):
                qs_ref[rb, hh] = a[rb * Q_BLOCK:(rb + 1) * Q_BLOCK]
                qrs_ref[rb, hh] = b[rb * Q_BLOCK:(rb + 1) * Q_BLOCK]
        for rb in range(tm // Q_BLOCK):
            gt_ref[rb] = gates[rb * Q_BLOCK:(rb + 1) * Q_BLOCK].T
        kselr_ref[...] = ksel.astype(BF16)
        vselt_ref[...] = vsel_t.astype(BF16)
        kwinr_ref[...] = kwin.astype(BF16)
        vwint_ref[...] = vwin_t.astype(BF16)
        kcmp_ref[...] = kv[0]
        vcmp_ref[...] = kv[1]
    else:
        qn_ref, qr_ref = outs[10:]
        qn_ref[...] = qn
        qr_ref[...] = qr


def _inproj(x, lw, cos_t, sin_t, *, tm, attn_layout, batch, seq):
    m = x.shape[0]
    nt = m // tm
    ntab = cos_t.shape[0] // tm
    row = lambda w: pl.BlockSpec((tm, w), lambda i: (i, 0))
    full = lambda a: pl.BlockSpec(a.shape, lambda i: (0,) * a.ndim)
    ins = [x, lw["ln1"], lw["w_in"], lw["gq"], lw["gks"], lw["gkw"], cos_t, sin_t, lw["gla_gw"], lw["gla_gb"],
           lw["gm_lng"], lw["gm_lnb"], lw["sm"]]
    in_specs = [row(D_MODEL), full(lw["ln1"]), full(lw["w_in"]), full(lw["gq"]), full(lw["gks"]), full(lw["gkw"]),
                pl.BlockSpec((tm, LANES), lambda i: (i % ntab, 0)), pl.BlockSpec((tm, LANES), lambda i: (i % ntab, 0)),
                full(lw["gla_gw"]), full(lw["gla_gb"]), full(lw["gm_lng"]), full(lw["gm_lnb"]), full(lw["sm"])]
    widths = [512, 256, 128, 128, 128, 128, 256, 256, 256, 256]
    out_shape = [jax.ShapeDtypeStruct((m, w), F32) for w in widths]
    out_specs = [row(w) for w in widths]
    if attn_layout:
        nqb = m // Q_BLOCK
        rpb = tm // Q_BLOCK
        tpb = seq // tm
        for j in range(2):
            out_shape[j] = jax.ShapeDtypeStruct((batch, widths[j], seq), F32)
            out_specs[j] = pl.BlockSpec((None, widths[j], tm), lambda i: (i // tpb, 0, i % tpb))
        out_shape += [jax.ShapeDtypeStruct((nqb, NSA_HEADS, Q_BLOCK, LANES), BF16)] * 2
        out_specs += [pl.BlockSpec((rpb, NSA_HEADS, Q_BLOCK, LANES), lambda i: (i, 0, 0, 0))] * 2
        out_shape += [jax.ShapeDtypeStruct((nqb, LANES, Q_BLOCK), F32)]
        out_specs += [pl.BlockSpec((rpb, LANES, Q_BLOCK), lambda i: (i, 0, 0))]
        rowmaj = (jax.ShapeDtypeStruct((m, LANES), BF16), row(LANES))
        trans = (jax.ShapeDtypeStruct((batch, LANES, seq), BF16),
                 pl.BlockSpec((None, LANES, tm), lambda i: (i // tpb, 0, i % tpb)))
        for sh, sp in (rowmaj, trans, rowmaj, trans):
            out_shape.append(sh)
            out_specs.append(sp)
        out_shape += [jax.ShapeDtypeStruct((m, LANES), F32)] * 2
        out_specs += [row(LANES)] * 2
    else:
        out_shape += [jax.ShapeDtypeStruct((m, NSA_Q), F32)] * 2
        out_specs += [row(NSA_Q)] * 2
    return pl.pallas_call(
        functools.partial(_inproj_kernel, tm=tm, attn_layout=attn_layout),
        out_shape=out_shape,
        grid=(nt,),
        in_specs=in_specs,
        out_specs=out_specs,
        compiler_params=_cparams(("parallel",)),
        name="inproj",
    )(*ins)


def _next_row(b):
    return pltpu.roll(b, b.shape[0] - 1, axis=0)


def _compress_kernel(zk_ref, zv_ref, pos_ref, wak_ref, wbk_ref, wav_ref, wbv_ref,
                     w2k_ref, w2v_ref, gkc_ref, sm_ref, kc_ref, vct_ref):
    pos = pos_ref[...]

    def one(z_ref, pa, pb, wa_ref, wb_ref, w2_ref):
        z = z_ref[...]
        a = _dot((z + pa).astype(BF16), wa_ref[...])
        b = _dot((z + pb).astype(BF16), wb_ref[...])
        return _dot(_gelu(a + _next_row(b)).astype(BF16), w2_ref[...])

    ck = one(zk_ref, pos[0:1], pos[1:2], wak_ref, wbk_ref, w2k_ref)
    cv = one(zv_ref, pos[2:3], pos[3:4], wav_ref, wbv_ref, w2v_ref)
    kc_ref[...] = _seg_rms(ck, gkc_ref[...], sm_ref[...]).astype(BF16)
    vct_ref[...] = cv.T.astype(BF16)


def _compress(zk, zv, lw):
    b, nc, kdim = zk.shape
    zspec = pl.BlockSpec((None, nc, kdim), lambda i: (i, 0, 0))
    full = lambda a: pl.BlockSpec(a.shape, lambda i: (0,) * a.ndim)
    ws = [lw["cmp_pos"], lw["cmp_wak"], lw["cmp_wbk"], lw["cmp_wav"], lw["cmp_wbv"], lw["cmp_w2k"], lw["cmp_w2v"],
          lw["gkc"], lw["sm"]]
    return pl.pallas_call(
        _compress_kernel,
        out_shape=[jax.ShapeDtypeStruct((b, nc, LANES), BF16), jax.ShapeDtypeStruct((b, LANES, nc), BF16)],
        grid=(b,),
        in_specs=[zspec] * 2 + [full(w) for w in ws],
        out_specs=[pl.BlockSpec((None, nc, LANES), lambda i: (i, 0, 0)),
                   pl.BlockSpec((None, LANES, nc), lambda i: (i, 0, 0))],
        compiler_params=_cparams(("parallel",)),
        name="nsa_compress",
    )(zk, zv, *ws)


def _nsa_kernel(qs_ref, qrs_ref, gt_ref, kc_ref, vct_ref, cov_ref, ksel_ref, vselt_ref, kwin_ref, vwint_ref,
                o_ref, score_scr, bias_scr, sa_scr, sb_scr, *, nc, ns, kb_keys):
    ib = pl.program_id(1)
    p0 = ib * Q_BLOCK
    nl = NSA_REP * Q_BLOCK
    ng = NSA_KV_HEADS
    t_row = p0 + lax.broadcasted_iota(jnp.int32, (1, nl), 1) % Q_BLOCK
    t_q = p0 + lax.broadcasted_iota(jnp.int32, (1, Q_BLOCK), 1)
    gt = gt_ref[...]
    n_top = min(N_SEL, ns)
    per_kb = kb_keys // SEL_BLOCK
    grows = [slice(g * HEAD_DIM, (g + 1) * HEAD_DIM) for g in range(ng)]

    def q_of(ref, g):
        return ref[NSA_REP * g:NSA_REP * (g + 1)].reshape(nl, LANES)

    ones_rows = 16

    def with_ones(vt):
        return jnp.concatenate([vt, jnp.ones((ones_rows, vt.shape[1]), BF16)], axis=0)

    def online(carry, s, vt):
        m, acc = carry
        m_new = jnp.maximum(m, jnp.max(s, axis=0, keepdims=True))
        p = jnp.exp2(s - m_new)
        acc = jnp.exp2(m - m_new) * acc + _dot(with_ones(vt), p.astype(BF16))
        return m_new, acc

    def normalised(acc):
        return acc[0:HEAD_DIM] * (1.0 / jnp.maximum(acc[HEAD_DIM:HEAD_DIM + 1], 1e-30))

    init1 = (jnp.full((1, nl), NEG_BIG, F32), jnp.zeros((HEAD_DIM + ones_rows, nl), F32))
    init = tuple(init1 for _ in range(ng))

    sj = lax.broadcasted_iota(jnp.int32, (ns, 1), 0)
    o_cmp = []
    for g in range(ng):
        s = _dot_nt(kc_ref[...], q_of(qs_ref, g))
        ci = lax.broadcasted_iota(jnp.int32, (nc, 1), 0)
        cmask = (ci * CMP_STRIDE + (CMP_BLOCK - 1) <= t_row) & (ci < nc - 1)
        s = jnp.where(cmask, s, -jnp.inf)
        m = jnp.max(s, axis=0, keepdims=True)
        m = jnp.where(m == -jnp.inf, 0.0, m)
        e = jnp.exp2(s - m)
        l = jnp.sum(e, axis=0, keepdims=True)
        p = e * (1.0 / jnp.maximum(l, 1e-30))
        o_cmp.append(_dot(vct_ref[grows[g], :], p.astype(BF16)))
        psum = p[:, 0:Q_BLOCK]
        for r in range(1, NSA_REP):
            psum = psum + p[:, r * Q_BLOCK:(r + 1) * Q_BLOCK]
        imp = _dot(cov_ref[...], psum.astype(BF16))
        bt = t_q // SEL_BLOCK
        allowed = sj * SEL_BLOCK <= t_q
        forced = (sj == 0) | (sj == bt) | (sj == bt - 1)
        score_scr[g] = jnp.where(allowed, imp + jnp.where(forced, np.float32(SEL_BONUS), 0.0), -jnp.inf)

    sjf = sj.astype(F32)
    left = [score_scr[g] for g in range(ng)]
    bias = [jnp.full((ns, Q_BLOCK), -jnp.inf, F32) for _ in range(ng)]
    for _ in range(n_top):
        for g in range(ng):
            mx = jnp.max(left[g], axis=0, keepdims=True)
            idx = jnp.min(jnp.where(left[g] == mx, sjf, np.float32(ns)), axis=0, keepdims=True)
            hit = sjf == jnp.where(mx > -jnp.inf, idx, -1.0)
            left[g] = jnp.where(hit, -jnp.inf, left[g])
            bias[g] = jnp.where(hit, 0.0, bias[g])
    for g in range(ng):
        bias_scr[g] = bias[g]

    def sel_qk(kb, g, dst):
        k0 = pl.multiple_of(kb * kb_keys, kb_keys)
        dst[g] = _dot_nt(ksel_ref[pl.ds(k0, kb_keys), :], q_of(qrs_ref, g))

    def sel_step(carry_g, src, kb, g, causal):
        k0 = pl.multiple_of(kb * kb_keys, kb_keys)
        parts = []
        for i in range(per_kb):
            row = bias_scr[g, pl.ds(kb * per_kb + i, 1), :]
            sl = slice(i * SEL_BLOCK, (i + 1) * SEL_BLOCK)
            parts.append(src[g, sl, :] + jnp.concatenate([row] * NSA_REP, axis=1))
        s = jnp.concatenate(parts, axis=0)
        if causal:
            kpos = k0 + lax.broadcasted_iota(jnp.int32, (kb_keys, 1), 0)
            s = jnp.where(kpos <= t_row, s, -jnp.inf)
        return online(carry_g, s, vselt_ref[grows[g], pl.ds(k0, kb_keys)])

    def sel_pair(j, carry, causal):
        a = 2 * j
        for g in range(ng):
            sel_qk(a + 1, g, sb_scr)
        carry = tuple(sel_step(carry[g], sa_scr, a, g, causal) for g in range(ng))
        if not causal:
            for g in range(ng):
                sel_qk(a + 2, g, sa_scr)
        return tuple(sel_step(carry[g], sb_scr, a + 1, g, causal) for g in range(ng))

    n_kb = (p0 + Q_BLOCK + kb_keys - 1) // kb_keys
    n_pairs = (n_kb + 1) // 2
    for g in range(ng):
        sel_qk(0, g, sa_scr)
    carry = lax.fori_loop(0, n_pairs - 1, lambda j, c: sel_pair(j, c, False), init)
    carry = sel_pair(n_pairs - 1, carry, True)
    o_sel = [normalised(carry[g][1]) for g in range(ng)]

    wkeys = WINDOW + Q_BLOCK
    w0 = pl.multiple_of(jnp.maximum(p0 - WINDOW, 0), Q_BLOCK)
    diff = t_row - (w0 + lax.broadcasted_iota(jnp.int32, (wkeys, 1), 0))
    wmask = (diff >= 0) & (diff < WINDOW)
    kblk = kwin_ref[pl.ds(w0, wkeys), :]
    o_win = []
    for g in range(ng):
        s = jnp.where(wmask, _dot_nt(kblk, q_of(qrs_ref, g)), -jnp.inf)
        o_win.append(normalised(online(init1, s, vwint_ref[grows[g], pl.ds(w0, wkeys)])[1]))

    for g in range(ng):
        o_w = o_win[g]

        def gate_row(jb):
            return jnp.concatenate(
                [gt[(NSA_REP * g + r) * 3 + jb:(NSA_REP * g + r) * 3 + jb + 1, :] for r in range(NSA_REP)], axis=1)

        o = o_cmp[g] * gate_row(0) + o_sel[g] * gate_row(1) + o_w * gate_row(2)
        for pr in range(NSA_REP // 2):
            blk = jnp.concatenate([o[:, (2 * pr) * Q_BLOCK:(2 * pr + 1) * Q_BLOCK],
                                   o[:, (2 * pr + 1) * Q_BLOCK:(2 * pr + 2) * Q_BLOCK]], axis=0)
            c0 = g * NSA_REP * HEAD_DIM + pr * LANES
            o_ref[:, c0:c0 + LANES] = blk.T


def _nsa_prompt(qs, qrs, gt, kc, vct, cov, ksel_r, vsel_t, kwin_r, vwin_t, *, batch, seq):
    nb = seq // Q_BLOCK
    nc = kc.shape[1]
    ns = cov.shape[0]
    kb_keys = min(512, seq)
    per_b3 = lambda a: pl.BlockSpec((None,) + a.shape[1:], lambda b, i: (b, 0, 0))
    return pl.pallas_call(
        functools.partial(_nsa_kernel, nc=nc, ns=ns, kb_keys=kb_keys),
        out_shape=jax.ShapeDtypeStruct((batch * seq, NSA_Q), F32),
        grid=(batch, nb),
        in_specs=[
            pl.BlockSpec((None, NSA_HEADS, Q_BLOCK, LANES), lambda b, i: (b * nb + i, 0, 0, 0)),
            pl.BlockSpec((None, NSA_HEADS, Q_BLOCK, LANES), lambda b, i: (b * nb + i, 0, 0, 0)),
            pl.BlockSpec((None, LANES, Q_BLOCK), lambda b, i: (b * nb + i, 0, 0)),
            per_b3(kc), per_b3(vct),
            pl.BlockSpec(cov.shape, lambda b, i: (0, 0)),
            per_b3(ksel_r), per_b3(vsel_t), per_b3(kwin_r), per_b3(vwin_t),
        ],
        out_specs=pl.BlockSpec((Q_BLOCK, NSA_Q), lambda b, i: (b * nb + i, 0)),
        scratch_shapes=[pltpu.VMEM((NSA_KV_HEADS, ns, Q_BLOCK), F32), pltpu.VMEM((NSA_KV_HEADS, ns, Q_BLOCK), F32),
                        pltpu.VMEM((NSA_KV_HEADS, kb_keys, NSA_REP * Q_BLOCK), F32),
                        pltpu.VMEM((NSA_KV_HEADS, kb_keys, NSA_REP * Q_BLOCK), F32)],
        compiler_params=_cparams(("parallel", "arbitrary")),
        name="nsa_prompt",
    )(qs, qrs, gt, kc, vct, cov, ksel_r, vsel_t, kwin_r, vwin_t)


def _gla_rows(c):
    offs, n = [], 0
    for s in range(c):
        t0 = (s // 8) * 8
        offs.append((n, t0))
        n += c - t0
    return offs, n


def _gla_kernel(q_ref, k_ref, la_ref, v_ref, rs_ref, gn_ref, bm_ref, bmask_ref, sm_ref, s0_ref,
                o_ref, sout_ref, s_scr, prod_scr, res_scr, *, c, nchunks, bpb):
    ci = pl.program_id(1)

    @pl.when(ci == 0)
    def _():
        s_scr[...] = s0_ref[...]
        prod_scr[...] = jnp.zeros_like(prod_scr)

    for bb in range(bpb):
        _gla_one(q_ref.at[bb], k_ref.at[bb], la_ref.at[bb], v_ref.at[bb], rs_ref.at[bb], gn_ref, bm_ref, bmask_ref,
                 sm_ref, o_ref.at[bb], sout_ref.at[bb], s_scr.at[bb], prod_scr.at[bb], res_scr.at[bb],
                 c=c, last=ci == nchunks - 1)


def _gla_one(q_ref, k_ref, la_ref, v_ref, rs_ref, gn_ref, bm_ref, bmask_ref, sm_ref, o_ref, sout_ref,
             s_scr, prod_scr, res_scr, *, c, last):
    q = q_ref[...]
    k = k_ref[...]
    v = v_ref[...]
    la = la_ref[...]
    tt = lax.broadcasted_iota(jnp.int32, (c, 1), 0)
    b = la
    sh = 1
    while sh < c:
        b = b + jnp.where(tt >= sh, pltpu.roll(b, sh, axis=0), 0.0)
        sh *= 2
    state = s_scr[...]
    inter = _dot_nt((q * jnp.exp(b)).astype(BF16), state.astype(BF16))

    offs, _ = _gla_rows(c)
    for s in range(c):
        r0, t0 = offs[s]
        d = b[t0:] - b[s:s + 1]
        e = jnp.exp(jnp.where(tt[t0:] >= s, d, -jnp.inf))
        prod_scr[r0:r0 + c - t0, :] = (q[t0:] * k[s:s + 1] * e).astype(BF16)
    res_scr[...] = _dot(prod_scr[...], bm_ref[...])
    pieces = []
    for tb in range(0, c, 8):
        hi = min(tb + 8, c)
        acc = inter[tb:hi]
        for s in range(hi):
            r0, t0 = offs[s]
            if t0 <= tb:
                acc = acc + res_scr[r0 + tb - t0:r0 + hi - t0, :] * v[s:s + 1]
        pieces.append(acc)
    o = pieces[0] if len(pieces) == 1 else jnp.concatenate(pieces, axis=0)

    bl = b[c - 1:c]
    kd = (k * jnp.exp(bl - b)).astype(BF16)
    if c >= 16:
        upd = _dot(v.T.astype(BF16), kd)
    else:
        upd = jnp.dot(v.T, kd.astype(F32), preferred_element_type=F32)
    new_state = jnp.exp(bl) * state + upd * bmask_ref[...]
    s_scr[...] = new_state

    o_ref[...] = _seg_rms(o, gn_ref[...], sm_ref[...]) * rs_ref[...]

    @pl.when(last)
    def _():
        sout_ref[...] = new_state


def _gla(qg, kg, la, vg, rs, s0_bd, lw, *, batch, seq, c, bpb):
    nchunks = seq // c
    _, npack = _gla_rows(c)
    npad = -(-npack // 16) * 16
    blk = lambda w: pl.BlockSpec((bpb, c, w), lambda b, i: (b, i, 0))
    full = lambda a: pl.BlockSpec(a.shape, lambda b, i: (0,) * a.ndim)
    consts = [lw["gla_gn"], lw["gla_bm"], lw["gla_bmask"], lw["sm"]]
    st = pl.BlockSpec((bpb, GLA_V, GLA_QK), lambda b, i: (b, 0, 0))
    return pl.pallas_call(
        functools.partial(_gla_kernel, c=c, nchunks=nchunks, bpb=bpb),
        out_shape=[jax.ShapeDtypeStruct((batch, seq, GLA_V), F32), jax.ShapeDtypeStruct((batch, GLA_V, GLA_QK), F32)],
        grid=(batch // bpb, nchunks),
        in_specs=[blk(GLA_QK), blk(GLA_QK), blk(GLA_QK), blk(GLA_V), blk(GLA_V)] + [full(a) for a in consts] + [st],
        out_specs=[blk(GLA_V), st],
        scratch_shapes=[pltpu.VMEM((bpb, GLA_V, GLA_QK), F32), pltpu.VMEM((bpb, npad, GLA_QK), BF16),
                        pltpu.VMEM((bpb, npad, GLA_V), F32)],
        compiler_params=_cparams(("parallel", "arbitrary")),
        name="gla",
    )(qg, kg, la, vg, rs, *consts, s0_bd)


def _outproj_kernel(x_ref, oa_ref, ob_ref, u_ref, vn_ref, ws_ref, bias_ref, wo_ref, o_ref, *, tm):
    lane = lax.broadcasted_iota(jnp.int32, (GM_CHUNK, GM_W), 1)
    tri = (lax.broadcasted_iota(jnp.int32, (GM_CHUNK, GM_CHUNK), 0)
           >= lax.broadcasted_iota(jnp.int32, (GM_CHUNK, GM_CHUNK), 1))
    zs = []
    for cb in range(tm // GM_CHUNK):
        vn = vn_ref[cb * GM_CHUNK:(cb + 1) * GM_CHUNK, :]
        z = bias_ref[...]
        for g in range(GM_GROUPS):
            wm = jnp.where(tri, ws_ref[g], 0.0).astype(BF16)
            vg = jnp.where((lane >= g * GM_CH) & (lane < (g + 1) * GM_CH), vn, 0.0).astype(BF16)
            z = z + _dot(wm, vg)
        zs.append(z)
    z = zs[0] if len(zs) == 1 else jnp.concatenate(zs, axis=0)
    oc = u_ref[...] * z
    y = _dot(oa_ref[...].astype(BF16), wo_ref[0:NSA_Q, :])
    y = y + _dot(ob_ref[...].astype(BF16), wo_ref[NSA_Q:NSA_Q + GLA_V, :])
    y = y + _dot(oc.astype(BF16), wo_ref[NSA_Q + GLA_V:MIX_OUT, :])
    o_ref[...] = x_ref[...] + y


def _outproj(x, oa, ob, u, vn, ws, bias, wo, li, *, tm):
    m = x.shape[0]
    row = lambda w: pl.BlockSpec((tm, w), lambda i: (i, 0))
    return pl.pallas_call(
        functools.partial(_outproj_kernel, tm=tm),
        out_shape=jax.ShapeDtypeStruct((m, D_MODEL), F32),
        grid=(m // tm,),
        in_specs=[row(D_MODEL), row(NSA_Q), row(GLA_V), row(GM_W), row(GM_W),
                  pl.BlockSpec(ws.shape, lambda i: (0, 0, 0)), pl.BlockSpec(bias.shape, lambda i: (0, 0)),
                  pl.BlockSpec((None, MIX_OUT, D_MODEL), lambda i: (li, 0, 0))],
        out_specs=row(D_MODEL),
        compiler_params=_cparams(("parallel",)),
        name="outproj",
    )(x, oa, ob, u, vn, ws, bias, wo)


def _pcompress_kernel(pt_ref, *refs, npg):
    pages = refs[:npg]
    perm_ref, wk_ref, wv_ref, out_ref, zk_scr, zv_scr = refs[npg:npg + 6]
    perm = perm_ref[...]
    for pp in range(npg // 2):
        xt = jnp.concatenate([pages[2 * pp][...], pages[2 * pp + 1][...]], axis=1).astype(BF16)
        y = _dot_nt(perm, xt).astype(BF16)
        for s in range(CMP_STRIDE):
            rows = slice(pp * 16, (pp + 1) * 16)
            zk_scr[rows, s * LANES:(s + 1) * LANES] = y[s * 16:(s + 1) * 16, 0:LANES]
            zv_scr[rows, s * LANES:(s + 1) * LANES] = y[s * 16:(s + 1) * 16, LANES:2 * LANES]
    q = out_ref.shape[1] // 4
    rk = _dot(zk_scr[...], wk_ref[...])
    rv = _dot(zv_scr[...], wv_ref[...])
    out_ref[:, 0:q] = rk[:, 0:q]
    out_ref[:, q:2 * q] = rv[:, 0:q]
    out_ref[:, 2 * q:3 * q] = rk[:, q:2 * q]
    out_ref[:, 3 * q:4 * q] = rv[:, q:2 * q]


def _pcompress(cache_t, page_table, li, perm, wk, wv, *, npg):
    bs, n_pages = page_table.shape
    page = cache_t.shape[-1]
    nc = n_pages * page // CMP_STRIDE
    steps = n_pages // npg
    cpp = page // CMP_STRIDE

    def page_spec(k):
        return pl.BlockSpec((None, None, 2 * LANES, page), lambda b, h, pt: (li, pt[b, h * npg + k], 0, 0))

    grid_spec = pltpu.PrefetchScalarGridSpec(
        num_scalar_prefetch=1,
        grid=(bs, steps),
        in_specs=[page_spec(k) for k in range(npg)] + [
            pl.BlockSpec(perm.shape, lambda b, h, pt: (0, 0)), pl.BlockSpec(wk.shape, lambda b, h, pt: (0, 0)),
            pl.BlockSpec(wv.shape, lambda b, h, pt: (0, 0))],
        out_specs=pl.BlockSpec((None, npg * cpp, 2 * wk.shape[1]), lambda b, h, pt: (b, h, 0)),
        scratch_shapes=[pltpu.VMEM((npg * cpp, wk.shape[0]), BF16)] * 2,
    )
    return pl.pallas_call(
        functools.partial(_pcompress_kernel, npg=npg),
        out_shape=jax.ShapeDtypeStruct((bs, nc, 2 * wk.shape[1]), F32),
        grid_spec=grid_spec,
        compiler_params=_cparams(("parallel", "arbitrary")),
        name="nsa_page_compress",
    )(page_table, *([cache_t] * npg), perm, wk, wv)


def _ctail_kernel(a_ref, b_ref, bias_ref, w2_ref, gkc_ref, sm_ref, kct_ref, vcr_ref):
    h = a_ref[...] + _next_row(b_ref[...]) + bias_ref[...]
    out = _dot(_gelu(h).astype(BF16), w2_ref[...])
    kc = _seg_rms(out[:, 0:LANES], gkc_ref[...], sm_ref[...])
    kct_ref[...] = kc.T.astype(BF16)
    vcr_ref[...] = out[:, LANES:2 * LANES].astype(BF16)


def _ctail(hid_ab, bias, w2, gkc, sm):
    bs, nc, w2x = hid_ab.shape
    half = w2x // 2
    full = lambda a: pl.BlockSpec(a.shape, lambda b: (0,) * a.ndim)
    return pl.pallas_call(
        _ctail_kernel,
        out_shape=[jax.ShapeDtypeStruct((bs, LANES, nc), BF16), jax.ShapeDtypeStruct((bs, nc, LANES), BF16)],
        grid=(bs,),
        in_specs=[pl.BlockSpec((None, nc, half), lambda b: (b, 0, 0)), pl.BlockSpec((None, nc, half), lambda b: (b, 0, 1)),
                  full(bias), full(w2), full(gkc), full(sm)],
        out_specs=[pl.BlockSpec((None, LANES, nc), lambda b: (b, 0, 0)), pl.BlockSpec((None, nc, LANES), lambda b: (b, 0, 0))],
        compiler_params=_cparams(("parallel",)),
        name="nsa_compress_tail",
    )(hid_ab, hid_ab, bias, w2, gkc, sm)


def _nsa_sample_kernel(pt_ref, qp_ref, qrp_ref, gcol_ref, kct_ref, vcr_ref, cov_ref, e_ref, *refs,
                       npg, nsteps, nc, ns_tot, past_len, ts):
    pages = refs[:npg]
    (wint_ref, newk_ref, neww_ref, newwsh_ref, o_ref, winout_ref,
     bias_scr, m_scr, l_scr, acc_scr, oc_scr, ow_scr) = refs[npg:]
    step = pl.program_id(1)
    nrow = NSA_KV_HEADS * NSA_REP * 8
    q8 = lax.broadcasted_iota(jnp.int32, (nrow, 1), 0) % 8
    t_row = past_len + q8
    qrp = qrp_ref[...]
    n_keep = wint_ref.shape[-1]
    n_top = min(N_SEL, ns_tot)
    nsp = bias_scr.shape[1]

    def expand_rows(a):
        return jnp.concatenate([a[0:8]] * NSA_REP + [a[8:16]] * NSA_REP, axis=0)

    @pl.when(step == 0)
    def _():
        s = _dot(qp_ref[...], kct_ref[...])
        ci = lax.broadcasted_iota(jnp.int32, (1, nc), 1)
        cmask = (ci * CMP_STRIDE + (CMP_BLOCK - 1) <= t_row) & (ci < nc - 1)
        s = jnp.where(cmask, s, -jnp.inf)
        m = jnp.max(s, axis=1, keepdims=True)
        m = jnp.where(m == -jnp.inf, 0.0, m)
        e = jnp.exp(s - m)
        p = e * (1.0 / jnp.maximum(jnp.sum(e, axis=1, keepdims=True), 1e-30))
        oc_scr[...] = _dot(p.astype(BF16), vcr_ref[...])
        sjl = lax.broadcasted_iota(jnp.int32, (1, nsp), 1)
        tq = past_len + lax.broadcasted_iota(jnp.int32, (8, 1), 0)
        bt = tq // SEL_BLOCK
        allowed = (sjl * SEL_BLOCK <= tq) & (sjl < ns_tot)
        forced = (sjl == 0) | (sjl == bt) | (sjl == bt - 1)
        for g in range(NSA_KV_HEADS):
            base = g * NSA_REP * 8
            psum = p[base:base + 8]
            for r in range(1, NSA_REP):
                psum = psum + p[base + r * 8:base + (r + 1) * 8]
            imp = _dot(psum.astype(BF16), cov_ref[...])
            score = jnp.where(allowed, imp + jnp.where(forced, np.float32(SEL_BONUS), 0.0), -jnp.inf)
            rank = jnp.zeros((8, nsp), jnp.int32)
            for k in range(ns_tot):
                col = score[:, k:k + 1]
                later = jnp.where(sjl > k, 1, 0)
                rank = rank + jnp.where(col > score, 1, 0) + jnp.where(col == score, later, 0)
            keep = (rank < n_top) & (score > -jnp.inf)
            bias_scr[g * 8:(g + 1) * 8, :] = jnp.where(keep, 0.0, -jnp.inf)
        m_scr[...] = jnp.full(m_scr.shape, NEG_BIG, F32)
        l_scr[...] = jnp.zeros(l_scr.shape, F32)
        acc_scr[...] = jnp.zeros(acc_scr.shape, F32)

        lane_w = lax.broadcasted_iota(jnp.int32, (1, n_keep), 1)
        diff = t_row - (past_len - n_keep + lane_w)
        s_w = jnp.where((diff >= 0) & (diff < WINDOW), _dot(qrp, wint_ref[0:LANES, :].astype(BF16)), -jnp.inf)
        lane_n = lax.broadcasted_iota(jnp.int32, (1, LANES), 1)
        diff_n = t_row - (past_len + lane_n)
        s_n = jnp.where((lane_n < ts) & (diff_n >= 0) & (diff_n < WINDOW),
                        _dot(qrp, neww_ref[0:LANES, :].astype(BF16)), -jnp.inf)
        sw = jnp.concatenate([s_w, s_n], axis=1)
        mw = jnp.max(sw, axis=1, keepdims=True)
        mw = jnp.where(mw == -jnp.inf, 0.0, mw)
        ew = jnp.exp(sw - mw)
        pw = ew * (1.0 / jnp.maximum(jnp.sum(ew, axis=1, keepdims=True), 1e-30))
        ow_scr[...] = (_dot_nt(pw[:, 0:n_keep].astype(BF16), wint_ref[LANES:2 * LANES, :].astype(BF16))
                       + _dot_nt(pw[:, n_keep:].astype(BF16), neww_ref[LANES:2 * LANES, :].astype(BF16)))
        rolled = pltpu.roll(wint_ref[...], n_keep - ts, axis=1)
        lane_o = lax.broadcasted_iota(jnp.int32, (2 * LANES, LANES), 1)
        winout_ref[:, 0:n_keep - LANES] = rolled[:, 0:n_keep - LANES]
        winout_ref[:, n_keep - LANES:n_keep] = jnp.where(lane_o >= LANES - ts, newwsh_ref[...],
                                                         rolled[:, n_keep - LANES:n_keep])

    def online(s, vt):
        m_old = m_scr[:, 0:1]
        m_new = jnp.maximum(m_old, jnp.max(s, axis=1, keepdims=True))
        p = jnp.exp(s - m_new)
        alpha = jnp.exp(m_old - m_new)
        l_new = alpha * l_scr[:, 0:1] + jnp.sum(p, axis=1, keepdims=True)
        acc_scr[...] = alpha * acc_scr[...] + _dot_nt(p.astype(BF16), vt)
        m_scr[...] = jnp.broadcast_to(m_new, m_scr.shape)
        l_scr[...] = jnp.broadcast_to(l_new, l_scr.shape)

    sel01 = jnp.where(bias_scr[...] == 0.0, 1.0, 0.0).astype(BF16)
    bexp = _dot(sel01, e_ref[...])
    bias = expand_rows(jnp.where(bexp > 0.5, 0.0, -jnp.inf))
    kt = jnp.concatenate([pg[0:LANES, :] for pg in pages], axis=1).astype(BF16)
    vt = jnp.concatenate([pg[LANES:2 * LANES, :] for pg in pages], axis=1).astype(BF16)
    online(_dot(qrp, kt) + bias, vt)

    @pl.when(step == nsteps - 1)
    def _():
        lane_n = lax.broadcasted_iota(jnp.int32, (1, LANES), 1)
        bcol = expand_rows(bias_scr[:, ns_tot - 1:ns_tot])
        ok = (lane_n < ts) & (past_len + lane_n <= t_row)
        s_n = jnp.where(ok, _dot(qrp, newk_ref[2 * LANES:3 * LANES, :].astype(BF16)) + bcol, -jnp.inf)
        online(s_n, newk_ref[3 * LANES:4 * LANES, :].astype(BF16))
        o_s = acc_scr[...] * (1.0 / jnp.maximum(l_scr[:, 0:1], 1e-30))
        gc = gcol_ref[...]
        o_ref[...] = oc_scr[...] * gc[:, 0:1] + o_s * gc[:, 1:2] + ow_scr[...] * gc[:, 2:3]


def _nsa_sample(cache_t, page_table, li, qp, qrp, gcol, kct, vcr, cov_s, emat, win_t, newk_t, neww_t, newwsh_t, *,
                npg, past_len, ts):
    bs, n_pages = page_table.shape
    page = cache_t.shape[-1]
    nsteps = n_pages // npg
    nc = kct.shape[-1]
    ns_tot = -(-(past_len + ts) // SEL_BLOCK)
    nsp = cov_s.shape[1]
    n_keep = win_t.shape[-1]
    nrow = qp.shape[1]
    per_b = lambda a: pl.BlockSpec((None,) + a.shape[1:], lambda b, h, pt: (b,) + (0,) * (a.ndim - 1))

    def page_spec(k):
        return pl.BlockSpec((None, None, 2 * LANES, page), lambda b, h, pt: (li, pt[b, h * npg + k], 1, 0))

    grid_spec = pltpu.PrefetchScalarGridSpec(
        num_scalar_prefetch=1,
        grid=(bs, nsteps),
        in_specs=[per_b(qp), per_b(qrp), per_b(gcol), per_b(kct), per_b(vcr),
                  pl.BlockSpec(cov_s.shape, lambda b, h, pt: (0, 0)),
                  pl.BlockSpec((nsp, npg * page), lambda b, h, pt: (0, h))]
                 + [page_spec(k) for k in range(npg)]
                 + [pl.BlockSpec((None, None, 2 * LANES, n_keep), lambda b, h, pt: (li, b, 0, 0)),
                    per_b(newk_t), per_b(neww_t), per_b(newwsh_t)],
        out_specs=[pl.BlockSpec((None, nrow, LANES), lambda b, h, pt: (b, 0, 0)),
                   pl.BlockSpec((None, 2 * LANES, n_keep), lambda b, h, pt: (b, 0, 0))],
        scratch_shapes=[pltpu.VMEM((2 * 8, nsp), F32)] + [pltpu.VMEM((nrow, LANES), F32)] * 5,
    )
    return pl.pallas_call(
        functools.partial(_nsa_sample_kernel, npg=npg, nsteps=nsteps, nc=nc, ns_tot=ns_tot, past_len=past_len, ts=ts),
        out_shape=[jax.ShapeDtypeStruct((bs, nrow, LANES), F32), jax.ShapeDtypeStruct((bs, 2 * LANES, n_keep), F32)],
        grid_spec=grid_spec,
        compiler_params=_cparams(("parallel", "arbitrary")),
        name="nsa_sample",
    )(page_table, qp, qrp, gcol, kct, vcr, cov_s, emat, *([cache_t] * npg), win_t, newk_t, neww_t, newwsh_t)


def _pad_cols(w):
    cuts = np.cumsum((0,) + IN_SPLITS)
    parts = []
    for i, (n, p) in enumerate(zip(IN_SPLITS, IN_PADDED)):
        seg = w[..., cuts[i]:cuts[i] + n]
        if p != n:
            seg = jnp.pad(seg, [(0, 0)] * (w.ndim - 1) + [(0, p - n)])
        parts.append(seg)
    return jnp.concatenate(parts, axis=-1)


def _rope_tables(pos):
    half = HEAD_DIM // 2
    inv = 1.0 / (ROPE_THETA ** (jnp.arange(half, dtype=F32) * (2.0 / HEAD_DIM)))
    ang = pos.astype(F32)[:, None] * inv[None, :]
    cos = jnp.cos(ang)
    sin = jnp.sin(ang)
    cos_f = jnp.concatenate([cos, cos, cos, cos], axis=1)
    sin_f = jnp.concatenate([-sin, sin, -sin, sin], axis=1)
    return cos_f, sin_f


def _cover_t(seq):
    nc = seq // CMP_STRIDE
    ns = seq // SEL_BLOCK
    ci = np.arange(nc)[None, :]
    sj = np.arange(ns)[:, None]
    cov = ((ci * CMP_STRIDE <= sj * SEL_BLOCK + SEL_BLOCK - 1)
           & (ci * CMP_STRIDE + CMP_BLOCK - 1 >= sj * SEL_BLOCK) & (ci < nc - 1))
    return jnp.asarray(cov, dtype=BF16)


def _layer_weights(l, ln_gains, w_in_p, nsa_qk_norm, nsa_cmp_pos, nsa_cmp_w1, nsa_cmp_w2, gla_gate_w, gla_gate_b,
                   gla_norm, gm_ln, gm_ws, gm_b):
    eye2 = jnp.eye(NSA_KV_HEADS, dtype=F32)
    lw = {"ln0": ln_gains[l, 0][None], "ln1": ln_gains[l, 1][None], "ln2": ln_gains[l, 2][None], "w_in": w_in_p[l]}
    lw["gq"] = jnp.tile(nsa_qk_norm[l, 0], NSA_HEADS)[None]
    lw["gkc"] = jnp.tile(nsa_qk_norm[l, 1], NSA_KV_HEADS)[None]
    lw["gks"] = jnp.tile(nsa_qk_norm[l, 2], NSA_KV_HEADS)[None]
    lw["gkw"] = jnp.tile(nsa_qk_norm[l, 3], NSA_KV_HEADS)[None]
    seg = (np.arange(LANES)[:, None] // HEAD_DIM) == (np.arange(LANES)[None, :] // HEAD_DIM)
    lw["sm"] = jnp.asarray(seg * (1.0 / HEAD_DIM), dtype=BF16)
    lw["gla_gw"] = jnp.pad(gla_gate_w[l], ((0, LANES - GLA_GATE_RANK), (0, 0))).astype(BF16)
    lw["gla_gb"] = gla_gate_b[l][None]
    lw["gm_lng"] = gm_ln[l, 0][None]
    lw["gm_lnb"] = gm_ln[l, 1][None]
    pos_rows = []
    for c, nm in ((0, "k"), (1, "v")):
        w1 = nsa_cmp_w1[l, c].reshape(CMP_BLOCK, HEAD_DIM, CMP_HIDDEN)
        for half, tag in ((w1[:CMP_STRIDE], "a"), (w1[CMP_STRIDE:], "b")):
            wx = jnp.einsum("sdh,pg->spdgh", half, eye2)
            lw["cmp_w" + tag + nm] = wx.reshape(CMP_STRIDE * LANES, NSA_KV_HEADS * CMP_HIDDEN).astype(BF16)
        lw["cmp_w2" + nm] = jnp.einsum("hd,pg->phgd", nsa_cmp_w2[l, c], eye2).reshape(
            NSA_KV_HEADS * CMP_HIDDEN, LANES).astype(BF16)
        pe = nsa_cmp_pos[l, c]
        for half in (pe[:CMP_STRIDE], pe[CMP_STRIDE:]):
            pos_rows.append(jnp.broadcast_to(half[:, None, :], (CMP_STRIDE, NSA_KV_HEADS, HEAD_DIM)).reshape(-1))
    lw["cmp_pos"] = jnp.stack(pos_rows)
    w1ab = nsa_cmp_w1[l].reshape(2, 2, CMP_STRIDE, HEAD_DIM, CMP_HIDDEN)
    for c, nm in ((0, "k"), (1, "v")):
        lw["pc_w" + nm] = jnp.einsum("asdh,gy->sgdayh", w1ab[c], eye2).reshape(
            CMP_STRIDE * LANES, 2 * NSA_KV_HEADS * CMP_HIDDEN).astype(BF16)
    pb = jnp.einsum("ck,ckh->ch", nsa_cmp_pos[l].reshape(2, -1), nsa_cmp_w1[l], precision=lax.Precision.HIGHEST)
    lw["pc_bias"] = jnp.broadcast_to(pb[:, None, :], (2, NSA_KV_HEADS, CMP_HIDDEN)).reshape(1, -1)
    lw["pc_w2"] = jnp.einsum("chd,cx,gy->cghxyd", nsa_cmp_w2[l], eye2, eye2).reshape(
        2 * NSA_KV_HEADS * CMP_HIDDEN, 2 * LANES).astype(BF16)
    lw["gla_gn"] = jnp.tile(gla_norm[l], GLA_HEADS)[None]
    hq = np.arange(GLA_QK) // GLA_DK
    hv = np.arange(GLA_V) // GLA_DV
    lw["gla_bm"] = jnp.asarray(hq[:, None] == hv[None, :], dtype=BF16)
    lw["gla_bmask"] = jnp.asarray(hv[:, None] == hq[None, :], dtype=F32)
    lw["gm_ws"] = gm_ws[l]
    lw["gm_bias"] = jnp.repeat(gm_b[l].T, GM_CH, axis=1)
    return lw


def _page_perm(page):
    cpp = page // CMP_STRIDE
    r = np.arange(2 * page)
    s_, rem = r // (2 * cpp), r % (2 * cpp)
    t = (rem // cpp) * page + CMP_STRIDE * (rem % cpp) + s_
    m = np.zeros((2 * page, 2 * page), np.float32)
    m[r, t] = 1.0
    return jnp.asarray(m, dtype=BF16)


def _cover_sample(past_len, ts, nsp):
    t_tot = past_len + ts
    n_c = (t_tot - CMP_BLOCK) // CMP_STRIDE + 1
    nc = past_len // CMP_STRIDE
    n_s = -(-t_tot // SEL_BLOCK)
    ci = np.arange(nc)[:, None]
    sj = np.arange(nsp)[None, :]
    cov = ((ci * CMP_STRIDE <= sj * SEL_BLOCK + SEL_BLOCK - 1) & (ci * CMP_STRIDE + CMP_BLOCK - 1 >= sj * SEL_BLOCK)
           & (ci < n_c) & (sj < n_s))
    emat = (np.arange(past_len)[None, :] // SEL_BLOCK) == np.arange(nsp)[:, None]
    return jnp.asarray(cov, dtype=BF16), jnp.asarray(emat, dtype=BF16)


def _rows_gr8(a, ts):
    bs = a.shape[0]
    a = a.reshape(bs, ts, NSA_KV_HEADS, NSA_REP, a.shape[-1]).transpose(0, 2, 3, 1, 4)
    return jnp.pad(a, ((0, 0), (0, 0), (0, 0), (0, 8 - ts), (0, 0)))


def _state_to_t(s):
    b = s.shape[0]
    eye = jnp.eye(GLA_HEADS, dtype=s.dtype)
    return jnp.einsum("bhkv,hg->bhvgk", s, eye).reshape(b, GLA_V, GLA_QK)


def _state_from_t(st):
    b = st.shape[0]
    s5 = st.reshape(b, GLA_HEADS, GLA_DV, GLA_HEADS, GLA_DK)
    d = jnp.stack([s5[:, h, :, h, :] for h in range(GLA_HEADS)], axis=1)
    return jnp.swapaxes(d, 2, 3)


def kernel(x_prompt, x_sample, cache_kv, cache_win_kv, state_gla, page_table, ln_gains, ffn_w_gate_up, ffn_w_down,
           w_in, w_out, nsa_qk_norm, nsa_cmp_pos, nsa_cmp_w1, nsa_cmp_w2, gla_gate_w, gla_gate_b, gla_norm, gm_ln,
           gm_ws, gm_b):
    depth = w_in.shape[0]
    bp, tp, _ = x_prompt.shape
    bs, ts, _ = x_sample.shape
    n_pages = page_table.shape[1]
    page = cache_kv.shape[2]
    past_len = n_pages * page
    mp, ms = bp * tp, bs * ts
    nc = tp // CMP_STRIDE

    w_gu = ffn_w_gate_up.astype(BF16).reshape(depth * 2, D_MODEL, 2 * D_FF)
    w_d = ffn_w_down.astype(BF16).reshape(depth * 2, D_FF, D_MODEL)
    w_in_p = _pad_cols(w_in).astype(BF16)
    w_o = w_out.astype(BF16)
    cos_p, sin_p = _rope_tables(jnp.arange(tp))
    cos_s, sin_s = _rope_tables(past_len + jnp.arange(ms) % ts)
    cov = _cover_t(tp)
    tm_p = 512 if mp % 512 == 0 else Q_BLOCK
    tm_f = 1024 if mp % 1024 == 0 else tm_p
    tm_i = 512 if tp % 512 == 0 else Q_BLOCK
    tf = 1408

    nrow = NSA_HEADS * 8
    n_keep_s = cache_win_kv.shape[2]
    nsp = -(-(past_len // SEL_BLOCK + 1) // LANES) * LANES
    cov_s, emat = _cover_sample(past_len, ts, nsp)
    perm = _page_perm(page)
    eye2 = jnp.eye(NSA_KV_HEADS, dtype=F32)
    npg_c = min(32, n_pages)
    npg_a = min(32, n_pages)
    cache_t = jnp.transpose(cache_kv, (0, 1, 3, 4, 5, 2)).reshape(depth, cache_kv.shape[1], 4 * LANES, page)
    win_t = jnp.transpose(cache_win_kv, (0, 1, 3, 4, 5, 2)).reshape(depth, bs, 2 * LANES, n_keep_s)

    xp = x_prompt.reshape(mp, D_MODEL)
    xs = x_sample.reshape(ms, D_MODEL)
    kv_p, win_p, gla_p, kv_s, win_s, gla_s, gmv_s = [], [], [], [], [], [], []
    eye_b = jnp.eye(bs, dtype=F32)
    for l in range(depth):
        lw = _layer_weights(l, ln_gains, w_in_p, nsa_qk_norm, nsa_cmp_pos, nsa_cmp_w1, nsa_cmp_w2, gla_gate_w,
                            gla_gate_b, gla_norm, gm_ln, gm_ws, gm_b)
        xp = _ffn(xp, lw["ln0"], w_gu, w_d, 2 * l, tm=tm_f, tf=tf)
        (newkv, newwin, _, qg, kg, la, vg, rs, u, vn, qs, qrs, gt, ksel_r, vsel_t, kwin_r, vwin_t, kcmp,
         vcmp) = _inproj(xp, lw, cos_p, sin_p, tm=tm_i, attn_layout=True, batch=bp, seq=tp)
        zk = kcmp.reshape(bp, nc, CMP_STRIDE * LANES)
        zv = vcmp.reshape(bp, nc, CMP_STRIDE * LANES)
        kc, vct = _compress(zk, zv, lw)
        oa = _nsa_prompt(qs, qrs, gt, kc, vct, cov, ksel_r.reshape(bp, tp, LANES), vsel_t,
                         kwin_r.reshape(bp, tp, LANES), vwin_t, batch=bp, seq=tp)
        r3 = lambda a: a.reshape(bp, tp, a.shape[-1])
        ob, st = _gla(r3(qg), r3(kg), r3(la), r3(vg), r3(rs), jnp.zeros((bp, GLA_V, GLA_QK), F32), lw,
                      batch=bp, seq=tp, c=GLA_CHUNK, bpb=1)
        xp = _outproj(xp, oa, ob.reshape(mp, GLA_V), u, vn, lw["gm_ws"], lw["gm_bias"], w_o, l, tm=tm_p)
        xp = _ffn(xp, lw["ln2"], w_gu, w_d, 2 * l + 1, tm=tm_f, tf=tf)
        kv_p.append(newkv.reshape(bp, 4, NSA_KV_HEADS, HEAD_DIM, tp).transpose(0, 4, 1, 2, 3))
        n_keep = min(WINDOW, tp)
        win_p.append(newwin[:, :, tp - n_keep:].reshape(bp, 2, NSA_KV_HEADS, HEAD_DIM, n_keep).transpose(0, 4, 1, 2, 3))
        gla_p.append(_state_from_t(st))

        xs = _ffn(xs, lw["ln0"], w_gu, w_d, 2 * l, tm=ms, tf=tf)
        (newkv, newwin, gates, qg, kg, la, vg, rs, u, vn, qn, qr) = _inproj(
            xs, lw, cos_s, sin_s, tm=ms, attn_layout=False, batch=bs, seq=ts)
        hid = _pcompress(cache_t, page_table, l, perm, lw["pc_wk"], lw["pc_wv"], npg=npg_c)
        kct, vcr = _ctail(hid, lw["pc_bias"], lw["pc_w2"], lw["gkc"], lw["sm"])
        scale = np.float32(HEAD_DIM ** -0.5)
        to_pad = lambda q: jnp.einsum("bgrqd,gx->bgrqxd", _rows_gr8(q.reshape(bs, ts, NSA_HEADS, HEAD_DIM) * scale, ts),
                                      eye2).reshape(bs, nrow, LANES).astype(BF16)
        gcol = _rows_gr8(gates[:, :3 * NSA_HEADS].reshape(bs, ts, NSA_HEADS, 3), ts).reshape(bs, nrow, 3)
        gcol = jnp.pad(gcol, ((0, 0), (0, 0), (0, LANES - 3)))
        newk_t = jnp.pad(newkv.reshape(bs, ts, -1).transpose(0, 2, 1), ((0, 0), (0, 0), (0, LANES - ts)))
        neww_c = newwin.reshape(bs, ts, -1).transpose(0, 2, 1)
        neww_t = jnp.pad(neww_c, ((0, 0), (0, 0), (0, LANES - ts)))
        newwsh_t = jnp.pad(neww_c, ((0, 0), (0, 0), (LANES - ts, 0)))
        o64, win_o = _nsa_sample(cache_t, page_table, l, to_pad(qn), to_pad(qr), gcol, kct, vcr, cov_s, emat, win_t,
                                 newk_t, neww_t, newwsh_t, npg=npg_a, past_len=past_len, ts=ts)
        o6 = o64.reshape(bs, NSA_KV_HEADS, NSA_REP, 8, NSA_KV_HEADS, HEAD_DIM)
        oa = jnp.stack([o6[:, g, :, :ts, g, :] for g in range(NSA_KV_HEADS)], axis=1)
        oa = oa.transpose(0, 3, 1, 2, 4).reshape(ms, NSA_Q)
        pad8 = lambda a: jnp.pad(a.reshape(bs, ts, a.shape[-1]), ((0, 0), (0, 8 - ts), (0, 0)))
        ob8, st_s = _gla(pad8(qg), pad8(kg), pad8(la), pad8(vg), pad8(rs), _state_to_t(state_gla[l]), lw,
                         batch=bs, seq=8, c=8, bpb=4 if bs % 4 == 0 else 1)
        ob = ob8[:, :ts].reshape(ms, GLA_V)
        ws_s = jnp.einsum("gts,bc->gbtcs", gm_ws[l][:, :ts, :ts], eye_b).reshape(GM_GROUPS, ms, ms)
        bias_s = jnp.tile(lw["gm_bias"][:ts], (bs, 1))
        xs = _outproj(xs, oa, ob, u, vn, ws_s, bias_s, w_o, l, tm=ms)
        xs = _ffn(xs, lw["ln2"], w_gu, w_d, 2 * l + 1, tm=ms, tf=tf)
        kv_s.append(newkv.reshape(bs, ts, 4, NSA_KV_HEADS, HEAD_DIM))
        win_s.append(win_o.reshape(bs, 2, NSA_KV_HEADS, HEAD_DIM, n_keep_s).transpose(0, 4, 1, 2, 3))
        gla_s.append(_state_from_t(st_s))
        gmv_s.append(vn.reshape(bs, ts, GM_GROUPS, GM_CH))
    return (xp.reshape(bp, tp, D_MODEL), xs.reshape(bs, ts, D_MODEL), jnp.stack(kv_p), jnp.stack(win_p),
            jnp.stack(gla_p), jnp.stack(kv_s), jnp.stack(win_s), jnp.stack(gla_s), jnp.stack(gmv_s))
```

```python
import functools

import numpy as np
import jax
import jax.numpy as jnp
from jax import lax
from jax.experimental import pallas as pl
from jax.experimental.pallas import tpu as pltpu

F32 = jnp.float32
BF16 = jnp.bfloat16

D_MODEL = 1024
HEAD_DIM = 64
NSA_HEADS = 8
NSA_KV_HEADS = 2
NSA_REP = NSA_HEADS // NSA_KV_HEADS
CMP_STRIDE = 16
CMP_BLOCK = 2 * CMP_STRIDE
CMP_HIDDEN = 128
SEL_BLOCK = 64
N_SEL = 16
WINDOW = 512
Q_BLOCK = 128
SEL_BONUS = 1.0e4
GLA_HEADS = 4
GLA_DK = 32
GLA_DV = 64
GLA_GATE_RANK = 16
GLA_GATE_TEMP = 16.0
GLA_CHUNK = 64
GM_GROUPS = 4
GM_CH = 64
GM_CHUNK = 128
D_FF = 2816
ROPE_THETA = 10000.0
EPS = 1e-6

NSA_Q = NSA_HEADS * HEAD_DIM
NSA_KV = NSA_KV_HEADS * HEAD_DIM
GLA_QK = GLA_HEADS * GLA_DK
GLA_V = GLA_HEADS * GLA_DV
GM_W = GM_GROUPS * GM_CH
MIX_OUT = NSA_Q + GLA_V + GM_W
IN_SPLITS = (NSA_Q, 6 * NSA_KV, 3 * NSA_HEADS, GLA_QK, GLA_QK, GLA_V, GLA_GATE_RANK, GLA_V, GM_W, GM_W)
IN_PADDED = tuple(-(-s // 128) * 128 for s in IN_SPLITS)
IN_OFFS = tuple(int(v) for v in np.cumsum((0,) + IN_PADDED))
D_IN_PAD = IN_OFFS[-1]

LANES = 128
NEG_BIG = -1.0e30
VMEM_LIMIT = 56 * 1024 * 1024


def _cparams(sem):
    return pltpu.CompilerParams(dimension_semantics=sem, vmem_limit_bytes=VMEM_LIMIT)


def _gelu(x):
    c = np.float32(np.sqrt(2.0 / np.pi))
    return x * (0.5 * (1.0 + jnp.tanh(c * (x + 0.044715 * (x * x * x)))))


def _sigmoid(x):
    return 1.0 / (1.0 + jnp.exp(-x))


def _dot(a, b):
    return jnp.dot(a, b, preferred_element_type=F32)


def _dot_nt(a, b):
    return lax.dot_general(a, b, (((1,), (1,)), ((), ())), preferred_element_type=F32)


def _seg_mean_sq(x, sm):
    sq = x * x
    hi = sq.astype(BF16)
    lo = (sq - hi.astype(F32)).astype(BF16)
    outs = []
    for c in range(x.shape[1] // LANES):
        sl = slice(c * LANES, (c + 1) * LANES)
        outs.append(_dot(hi[:, sl], sm) + _dot(lo[:, sl], sm))
    return outs[0] if len(outs) == 1 else jnp.concatenate(outs, axis=1)


def _seg_rms(x, gain, sm):
    return x * lax.rsqrt(_seg_mean_sq(x, sm) + EPS) * gain


def _tile_lanes(a, w):
    n = w // a.shape[1]
    return a if n == 1 else jnp.concatenate([a] * n, axis=1)


def _rope(x, cos, sin_signed):
    w = x.shape[1]
    lane = lax.broadcasted_iota(jnp.int32, x.shape, 1)
    fwd = pltpu.roll(x, w - HEAD_DIM // 2, axis=1)
    bwd = pltpu.roll(x, HEAD_DIM // 2, axis=1)
    partner = jnp.where((lane % HEAD_DIM) < HEAD_DIM // 2, fwd, bwd)
    return x * _tile_lanes(cos, w) + partner * _tile_lanes(sin_signed, w)


def _ffn_kernel(x_ref, g_ref, wg_ref, wu_ref, wd_ref, o_ref, h_scr, acc_scr, *, nj):
    j = pl.program_id(1)

    @pl.when(j == 0)
    def _():
        x = x_ref[...]
        ms = jnp.mean(x * x, axis=-1, keepdims=True)
        h_scr[...] = (x * lax.rsqrt(ms + EPS) * g_ref[...]).astype(BF16)
        acc_scr[...] = jnp.zeros_like(acc_scr)

    h = h_scr[...]
    g = _dot(h, wg_ref[...])
    u = _dot(h, wu_ref[...])
    a = (g * _sigmoid(g)) * u
    acc_scr[...] += _dot(a.astype(BF16), wd_ref[...])

    @pl.when(j == nj - 1)
    def _():
        o_ref[...] = x_ref[...] + 0.5 * acc_scr[...]


def _ffn(x, gain, w_gu, w_d, li, *, tm, tf):
    m = x.shape[0]
    nj = D_FF // tf
    return pl.pallas_call(
        functools.partial(_ffn_kernel, nj=nj),
        out_shape=jax.ShapeDtypeStruct((m, D_MODEL), F32),
        grid=(m // tm, nj),
        in_specs=[
            pl.BlockSpec((tm, D_MODEL), lambda i, j: (i, 0)),
            pl.BlockSpec((1, D_MODEL), lambda i, j: (0, 0)),
            pl.BlockSpec((None, D_MODEL, tf), lambda i, j: (li, 0, j)),
            pl.BlockSpec((None, D_MODEL, tf), lambda i, j: (li, 0, j + nj)),
            pl.BlockSpec((None, tf, D_MODEL), lambda i, j: (li, j, 0)),
        ],
        out_specs=pl.BlockSpec((tm, D_MODEL), lambda i, j: (i, 0)),
        scratch_shapes=[pltpu.VMEM((tm, D_MODEL), BF16), pltpu.VMEM((tm, D_MODEL), F32)],
        compiler_params=_cparams(("parallel", "arbitrary")),
        name="ffn",
    )(x, gain, w_gu, w_gu, w_d)


def _group_padded(arr, h):
    c = arr[:, (h // 2) * LANES:(h // 2 + 1) * LANES]
    g = h // NSA_REP
    if (h % 2) != g:
        c = pltpu.roll(c, HEAD_DIM, axis=1)
    lane = lax.broadcasted_iota(jnp.int32, c.shape, 1)
    keep = (lane >= g * HEAD_DIM) & (lane < (g + 1) * HEAD_DIM)
    return jnp.where(keep, c, 0.0)


def _inproj_kernel(x_ref, ln_ref, w_ref, gq_ref, gks_ref, gkw_ref, cos_ref, sin_ref, gw_ref, gb_ref,
                   lng_ref, lnb_ref, sm_ref, *outs, tm, attn_layout):
    (newkv_ref, newwin_ref, gates_ref, qg_ref, kg_ref, la_ref, vg_ref, rs_ref, u_ref, vn_ref) = outs[:10]
    x = x_ref[...]
    ms = jnp.mean(x * x, axis=-1, keepdims=True)
    h = (x * lax.rsqrt(ms + EPS) * ln_ref[...]).astype(BF16)
    p = _dot(h, w_ref[...])
    sm = sm_ref[...]
    cos = cos_ref[...]
    sin = sin_ref[...]
    o = IN_OFFS

    def seg(i, a=0, b=None):
        b = IN_PADDED[i] if b is None else b
        return p[:, o[i] + a:o[i] + b]

    qn = _seg_rms(seg(0), gq_ref[...], sm)
    qr = _rope(qn, cos, sin)
    kv = [seg(1, LANES * j, LANES * (j + 1)) for j in range(6)]
    ksel = _rope(_seg_rms(kv[2], gks_ref[...], sm), cos, sin)
    kwin = _rope(_seg_rms(kv[4], gkw_ref[...], sm), cos, sin)
    vsel_t = kv[3].T
    vwin_t = kv[5].T
    if attn_layout:
        for j, a in enumerate((kv[0].T, kv[1].T, ksel.T, vsel_t)):
            newkv_ref[j * LANES:(j + 1) * LANES, :] = a
        newwin_ref[0:LANES, :] = kwin.T
        newwin_ref[LANES:2 * LANES, :] = vwin_t
    else:
        for j, a in enumerate((kv[0], kv[1], ksel, kv[3])):
            newkv_ref[:, j * LANES:(j + 1) * LANES] = a
        newwin_ref[:, 0:LANES] = kwin
        newwin_ref[:, LANES:2 * LANES] = kv[5]
    gates = _sigmoid(seg(2))
    gates_ref[...] = gates
    qg_ref[...] = seg(3) * np.float32(GLA_DK ** -0.5)
    kg_ref[...] = seg(4)
    vg_ref[...] = seg(5)
    logit = _dot(seg(6).astype(BF16), gw_ref[...]) + gb_ref[...]
    log_sig = jnp.minimum(logit, 0.0) - jnp.log1p(jnp.exp(-jnp.abs(logit)))
    la_ref[...] = log_sig * np.float32(1.0 / GLA_GATE_TEMP)
    r = seg(7)
    rs_ref[...] = r * _sigmoid(r)
    u_ref[...] = _gelu(seg(8))
    v = _gelu(seg(9))
    mu = jnp.mean(v, axis=-1, keepdims=True)
    var = jnp.mean(jnp.square(v - mu), axis=-1, keepdims=True)
    vn_ref[...] = (v - mu) * lax.rsqrt(var + EPS) * lng_ref[...] + lnb_ref[...]

    if attn_layout:
        (qs_ref, qrs_ref, gt_ref, kselr_ref, vselt_ref, kwinr_ref, vwint_ref, kcmp_ref, vcmp_ref) = outs[10:]
        scale = np.float32(HEAD_DIM ** -0.5 * np.log2(np.e))
        qs = qn * scale
        qrs = qr * scale
        for hh in range(NSA_HEADS):
            a = _group_padded(qs, hh).astype(BF16)
            b = _group_padded(qrs, hh).astype(BF16)
            for rb in range(tm // Q_BLOCK):
                qs_ref[rb, hh] = a[rb * Q_BLOCK:(rb + 1) * Q_BLOCK]
                qrs_ref[rb, hh] = b[rb * Q_BLOCK:(rb + 1) * Q_BLOCK]
        for rb in range(tm // Q_BLOCK):
            gt_ref[rb] = gates[rb * Q_BLOCK:(rb + 1) * Q_BLOCK].T
        kselr_ref[...] = ksel.astype(BF16)
        vselt_ref[...] = vsel_t.astype(BF16)
        kwinr_ref[...] = kwin.astype(BF16)
        vwint_ref[...] = vwin_t.astype(BF16)
        kcmp_ref[...] = kv[0]
        vcmp_ref[...] = kv[1]
    else:
        qn_ref, qr_ref = outs[10:]
        qn_ref[...] = qn
        qr_ref[...] = qr


def _inproj(x, lw, cos_t, sin_t, *, tm, attn_layout, batch, seq):
    m = x.shape[0]
    nt = m // tm
    ntab = cos_t.shape[0] // tm
    row = lambda w: pl.BlockSpec((tm, w), lambda i: (i, 0))
    full = lambda a: pl.BlockSpec(a.shape, lambda i: (0,) * a.ndim)
    ins = [x, lw["ln1"], lw["w_in"], lw["gq"], lw["gks"], lw["gkw"], cos_t, sin_t, lw["gla_gw"], lw["gla_gb"],
           lw["gm_lng"], lw["gm_lnb"], lw["sm"]]
    in_specs = [row(D_MODEL), full(lw["ln1"]), full(lw["w_in"]), full(lw["gq"]), full(lw["gks"]), full(lw["gkw"]),
                pl.BlockSpec((tm, LANES), lambda i: (i % ntab, 0)), pl.BlockSpec((tm, LANES), lambda i: (i % ntab, 0)),
                full(lw["gla_gw"]), full(lw["gla_gb"]), full(lw["gm_lng"]), full(lw["gm_lnb"]), full(lw["sm"])]
    widths = [512, 256, 128, 128, 128, 128, 256, 256, 256, 256]
    out_shape = [jax.ShapeDtypeStruct((m, w), F32) for w in widths]
    out_specs = [row(w) for w in widths]
    if attn_layout:
        nqb = m // Q_BLOCK
        rpb = tm // Q_BLOCK
        tpb = seq // tm
        for j in range(2):
            out_shape[j] = jax.ShapeDtypeStruct((batch, widths[j], seq), F32)
            out_specs[j] = pl.BlockSpec((None, widths[j], tm), lambda i: (i // tpb, 0, i % tpb))
        out_shape += [jax.ShapeDtypeStruct((nqb, NSA_HEADS, Q_BLOCK, LANES), BF16)] * 2
        out_specs += [pl.BlockSpec((rpb, NSA_HEADS, Q_BLOCK, LANES), lambda i: (i, 0, 0, 0))] * 2
        out_shape += [jax.ShapeDtypeStruct((nqb, LANES, Q_BLOCK), F32)]
        out_specs += [pl.BlockSpec((rpb, LANES, Q_BLOCK), lambda i: (i, 0, 0))]
        rowmaj = (jax.ShapeDtypeStruct((m, LANES), BF16), row(LANES))
        trans = (jax.ShapeDtypeStruct((batch, LANES, seq), BF16),
                 pl.BlockSpec((None, LANES, tm), lambda i: (i // tpb, 0, i % tpb)))
        for sh, sp in (rowmaj, trans, rowmaj, trans):
            out_shape.append(sh)
            out_specs.append(sp)
        out_shape += [jax.ShapeDtypeStruct((m, LANES), F32)] * 2
        out_specs += [row(LANES)] * 2
    else:
        out_shape += [jax.ShapeDtypeStruct((m, NSA_Q), F32)] * 2
        out_specs += [row(NSA_Q)] * 2
    return pl.pallas_call(
        functools.partial(_inproj_kernel, tm=tm, attn_layout=attn_layout),
        out_shape=out_shape,
        grid=(nt,),
        in_specs=in_specs,
        out_specs=out_specs,
        compiler_params=_cparams(("parallel",)),
        name="inproj",
    )(*ins)


def _next_row(b):
    return pltpu.roll(b, b.shape[0] - 1, axis=0)


def _compress_kernel(zk_ref, zv_ref, pos_ref, wak_ref, wbk_ref, wav_ref, wbv_ref,
                     w2k_ref, w2v_ref, gkc_ref, sm_ref, kc_ref, vct_ref):
    pos = pos_ref[...]

    def one(z_ref, pa, pb, wa_ref, wb_ref, w2_ref):
        z = z_ref[...]
        a = _dot((z + pa).astype(BF16), wa_ref[...])
        b = _dot((z + pb).astype(BF16), wb_ref[...])
        return _dot(_gelu(a + _next_row(b)).astype(BF16), w2_ref[...])

    ck = one(zk_ref, pos[0:1], pos[1:2], wak_ref, wbk_ref, w2k_ref)
    cv = one(zv_ref, pos[2:3], pos[3:4], wav_ref, wbv_ref, w2v_ref)
    kc_ref[...] = _seg_rms(ck, gkc_ref[...], sm_ref[...]).astype(BF16)
    vct_ref[...] = cv.T.astype(BF16)


def _compress(zk, zv, lw):
    b, nc, kdim = zk.shape
    zspec = pl.BlockSpec((None, nc, kdim), lambda i: (i, 0, 0))
    full = lambda a: pl.BlockSpec(a.shape, lambda i: (0,) * a.ndim)
    ws = [lw["cmp_pos"], lw["cmp_wak"], lw["cmp_wbk"], lw["cmp_wav"], lw["cmp_wbv"], lw["cmp_w2k"], lw["cmp_w2v"],
          lw["gkc"], lw["sm"]]
    return pl.pallas_call(
        _compress_kernel,
        out_shape=[jax.ShapeDtypeStruct((b, nc, LANES), BF16), jax.ShapeDtypeStruct((b, LANES, nc), BF16)],
        grid=(b,),
        in_specs=[zspec] * 2 + [full(w) for w in ws],
        out_specs=[pl.BlockSpec((None, nc, LANES), lambda i: (i, 0, 0)),
                   pl.BlockSpec((None, LANES, nc), lambda i: (i, 0, 0))],
        compiler_params=_cparams(("parallel",)),
        name="nsa_compress",
    )(zk, zv, *ws)


def _nsa_kernel(qs_ref, qrs_ref, gt_ref, kc_ref, vct_ref, cov_ref, ksel_ref, vselt_ref, kwin_ref, vwint_ref,
                o_ref, score_scr, bias_scr, sa_scr, sb_scr, *, nc, ns, kb_keys):
    ib = pl.program_id(1)
    p0 = ib * Q_BLOCK
    nl = NSA_REP * Q_BLOCK
    ng = NSA_KV_HEADS
    t_row = p0 + lax.broadcasted_iota(jnp.int32, (1, nl), 1) % Q_BLOCK
    t_q = p0 + lax.broadcasted_iota(jnp.int32, (1, Q_BLOCK), 1)
    gt = gt_ref[...]
    n_top = min(N_SEL, ns)
    per_kb = kb_keys // SEL_BLOCK
    grows = [slice(g * HEAD_DIM, (g + 1) * HEAD_DIM) for g in range(ng)]

    def q_of(ref, g):
        return ref[NSA_REP * g:NSA_REP * (g + 1)].reshape(nl, LANES)

    ones_rows = 16

    def with_ones(vt):
        return jnp.concatenate([vt, jnp.ones((ones_rows, vt.shape[1]), BF16)], axis=0)

    def online(carry, s, vt):
        m, acc = carry
        m_new = jnp.maximum(m, jnp.max(s, axis=0, keepdims=True))
        p = jnp.exp2(s - m_new)
        acc = jnp.exp2(m - m_new) * acc + _dot(with_ones(vt), p.astype(BF16))
        return m_new, acc

    def normalised(acc):
        return acc[0:HEAD_DIM] * (1.0 / jnp.maximum(acc[HEAD_DIM:HEAD_DIM + 1], 1e-30))

    init1 = (jnp.full((1, nl), NEG_BIG, F32), jnp.zeros((HEAD_DIM + ones_rows, nl), F32))
    init = tuple(init1 for _ in range(ng))

    sj = lax.broadcasted_iota(jnp.int32, (ns, 1), 0)
    o_cmp = []
    for g in range(ng):
        s = _dot_nt(kc_ref[...], q_of(qs_ref, g))
        ci = lax.broadcasted_iota(jnp.int32, (nc, 1), 0)
        cmask = (ci * CMP_STRIDE + (CMP_BLOCK - 1) <= t_row) & (ci < nc - 1)
        s = jnp.where(cmask, s, -jnp.inf)
        m = jnp.max(s, axis=0, keepdims=True)
        m = jnp.where(m == -jnp.inf, 0.0, m)
        e = jnp.exp2(s - m)
        l = jnp.sum(e, axis=0, keepdims=True)
        p = e * (1.0 / jnp.maximum(l, 1e-30))
        o_cmp.append(_dot(vct_ref[grows[g], :], p.astype(BF16)))
        psum = p[:, 0:Q_BLOCK]
        for r in range(1, NSA_REP):
            psum = psum + p[:, r * Q_BLOCK:(r + 1) * Q_BLOCK]
        imp = _dot(cov_ref[...], psum.astype(BF16))
        bt = t_q // SEL_BLOCK
        allowed = sj * SEL_BLOCK <= t_q
        forced = (sj == 0) | (sj == bt) | (sj == bt - 1)
        score_scr[g] = jnp.where(allowed, imp + jnp.where(forced, np.float32(SEL_BONUS), 0.0), -jnp.inf)

    sjf = sj.astype(F32)
    left = [score_scr[g] for g in range(ng)]
    bias = [jnp.full((ns, Q_BLOCK), -jnp.inf, F32) for _ in range(ng)]
    for _ in range(n_top):
        for g in range(ng):
            mx = jnp.max(left[g], axis=0, keepdims=True)
            idx = jnp.min(jnp.where(left[g] == mx, sjf, np.float32(ns)), axis=0, keepdims=True)
            hit = sjf == jnp.where(mx > -jnp.inf, idx, -1.0)
            left[g] = jnp.where(hit, -jnp.inf, left[g])
            bias[g] = jnp.where(hit, 0.0, bias[g])
    for g in range(ng):
        bias_scr[g] = bias[g]

    def sel_qk(kb, g, dst):
        k0 = pl.multiple_of(kb * kb_keys, kb_keys)
        dst[g] = _dot_nt(ksel_ref[pl.ds(k0, kb_keys), :], q_of(qrs_ref, g))

    def sel_step(carry_g, src, kb, g, causal):
        k0 = pl.multiple_of(kb * kb_keys, kb_keys)
        parts = []
        for i in range(per_kb):
            row = bias_scr[g, pl.ds(kb * per_kb + i, 1), :]
            sl = slice(i * SEL_BLOCK, (i + 1) * SEL_BLOCK)
            parts.append(src[g, sl, :] + jnp.concatenate([row] * NSA_REP, axis=1))
        s = jnp.concatenate(parts, axis=0)
        if causal:
            kpos = k0 + lax.broadcasted_iota(jnp.int32, (kb_keys, 1), 0)
            s = jnp.where(kpos <= t_row, s, -jnp.inf)
        return online(carry_g, s, vselt_ref[grows[g], pl.ds(k0, kb_keys)])

    def sel_pair(j, carry, causal):
        a = 2 * j
        for g in range(ng):
            sel_qk(a + 1, g, sb_scr)
        carry = tuple(sel_step(carry[g], sa_scr, a, g, causal) for g in range(ng))
        if not causal:
            for g in range(ng):
                sel_qk(a + 2, g, sa_scr)
        return tuple(sel_step(carry[g], sb_scr, a + 1, g, causal) for g in range(ng))

    n_kb = (p0 + Q_BLOCK + kb_keys - 1) // kb_keys
    n_pairs = (n_kb + 1) // 2
    for g in range(ng):
        sel_qk(0, g, sa_scr)
    carry = lax.fori_loop(0, n_pairs - 1, lambda j, c: sel_pair(j, c, False), init)
    carry = sel_pair(n_pairs - 1, carry, True)
    o_sel = [normalised(carry[g][1]) for g in range(ng)]

    wkeys = WINDOW + Q_BLOCK
    w0 = pl.multiple_of(jnp.maximum(p0 - WINDOW, 0), Q_BLOCK)
    diff = t_row - (w0 + lax.broadcasted_iota(jnp.int32, (wkeys, 1), 0))
    wmask = (diff >= 0) & (diff < WINDOW)
    kblk = kwin_ref[pl.ds(w0, wkeys), :]
    o_win = []
    for g in range(ng):
        s = jnp.where(wmask, _dot_nt(kblk, q_of(qrs_ref, g)), -jnp.inf)
        o_win.append(normalised(online(init1, s, vwint_ref[grows[g], pl.ds(w0, wkeys)])[1]))

    for g in range(ng):
        o_w = o_win[g]

        def gate_row(jb):
            return jnp.concatenate(
                [gt[(NSA_REP * g + r) * 3 + jb:(NSA_REP * g + r) * 3 + jb + 1, :] for r in range(NSA_REP)], axis=1)

        o = o_cmp[g] * gate_row(0) + o_sel[g] * gate_row(1) + o_w * gate_row(2)
        for pr in range(NSA_REP // 2):
            blk = jnp.concatenate([o[:, (2 * pr) * Q_BLOCK:(2 * pr + 1) * Q_BLOCK],
                                   o[:, (2 * pr + 1) * Q_BLOCK:(2 * pr + 2) * Q_BLOCK]], axis=0)
            c0 = g * NSA_REP * HEAD_DIM + pr * LANES
            o_ref[:, c0:c0 + LANES] = blk.T


def _nsa_prompt(qs, qrs, gt, kc, vct, cov, ksel_r, vsel_t, kwin_r, vwin_t, *, batch, seq):
    nb = seq // Q_BLOCK
    nc = kc.shape[1]
    ns = cov.shape[0]
    kb_keys = min(512, seq)
    per_b3 = lambda a: pl.BlockSpec((None,) + a.shape[1:], lambda b, i: (b, 0, 0))
    return pl.pallas_call(
        functools.partial(_nsa_kernel, nc=nc, ns=ns, kb_keys=kb_keys),
        out_shape=jax.ShapeDtypeStruct((batch * seq, NSA_Q), F32),
        grid=(batch, nb),
        in_specs=[
            pl.BlockSpec((None, NSA_HEADS, Q_BLOCK, LANES), lambda b, i: (b * nb + i, 0, 0, 0)),
            pl.BlockSpec((None, NSA_HEADS, Q_BLOCK, LANES), lambda b, i: (b * nb + i, 0, 0, 0)),
            pl.BlockSpec((None, LANES, Q_BLOCK), lambda b, i: (b * nb + i, 0, 0)),
            per_b3(kc), per_b3(vct),
            pl.BlockSpec(cov.shape, lambda b, i: (0, 0)),
            per_b3(ksel_r), per_b3(vsel_t), per_b3(kwin_r), per_b3(vwin_t),
        ],
        out_specs=pl.BlockSpec((Q_BLOCK, NSA_Q), lambda b, i: (b * nb + i, 0)),
        scratch_shapes=[pltpu.VMEM((NSA_KV_HEADS, ns, Q_BLOCK), F32), pltpu.VMEM((NSA_KV_HEADS, ns, Q_BLOCK), F32),
                        pltpu.VMEM((NSA_KV_HEADS, kb_keys, NSA_REP * Q_BLOCK), F32),
                        pltpu.VMEM((NSA_KV_HEADS, kb_keys, NSA_REP * Q_BLOCK), F32)],
        compiler_params=_cparams(("parallel", "arbitrary")),
        name="nsa_prompt",
    )(qs, qrs, gt, kc, vct, cov, ksel_r, vsel_t, kwin_r, vwin_t)


def _gla_rows(c):
    offs, n = [], 0
    for s in range(c):
        t0 = (s // 8) * 8
        offs.append((n, t0))
        n += c - t0
    return offs, n


def _gla_kernel(q_ref, k_ref, la_ref, v_ref, rs_ref, gn_ref, bm_ref, bmask_ref, sm_ref, s0_ref,
                o_ref, sout_ref, s_scr, prod_scr, *, c, nchunks, bpb):
    ci = pl.program_id(1)

    @pl.when(ci == 0)
    def _():
        s_scr[...] = s0_ref[...]
        prod_scr[...] = jnp.zeros_like(prod_scr)

    for bb in range(bpb):
        _gla_one(q_ref.at[bb], k_ref.at[bb], la_ref.at[bb], v_ref.at[bb], rs_ref.at[bb], gn_ref, bm_ref, bmask_ref,
                 sm_ref, o_ref.at[bb], sout_ref.at[bb], s_scr.at[bb], prod_scr.at[bb],
                 c=c, last=ci == nchunks - 1)


def _gla_one(q_ref, k_ref, la_ref, v_ref, rs_ref, gn_ref, bm_ref, bmask_ref, sm_ref, o_ref, sout_ref,
             s_scr, prod_scr, *, c, last):
    q = q_ref[...]
    k = k_ref[...]
    v = v_ref[...]
    la = la_ref[...]
    tt = lax.broadcasted_iota(jnp.int32, (c, 1), 0)
    b = la
    sh = 1
    while sh < c:
        b = b + jnp.where(tt >= sh, pltpu.roll(b, sh, axis=0), 0.0)
        sh *= 2
    state = s_scr[...]
    inter = _dot_nt((q * jnp.exp(b)).astype(BF16), state.astype(BF16))

    offs, npack = _gla_rows(c)
    acc = [inter[tb:min(tb + 8, c)] for tb in range(0, c, 8)]
    gsz = min(16, c)
    for g0 in range(0, c, gsz):
        for s in range(g0, g0 + gsz):
            r0, t0 = offs[s]
            d = b[t0:] - b[s:s + 1]
            e = jnp.exp(jnp.where(tt[t0:] >= s, d, -jnp.inf))
            prod_scr[r0:r0 + c - t0, :] = (q[t0:] * k[s:s + 1] * e).astype(BF16)
        lo = offs[g0][0]
        hi = offs[g0 + gsz][0] if g0 + gsz < c else npack
        res = _dot(prod_scr[lo:hi, :], bm_ref[...])
        for s in range(g0, g0 + gsz):
            r0, t0 = offs[s]
            for tb in range(t0, c, 8):
                rr = r0 - lo + tb - t0
                acc[tb // 8] = acc[tb // 8] + res[rr:rr + min(8, c - tb)] * v[s:s + 1]
    o = acc[0] if len(acc) == 1 else jnp.concatenate(acc, axis=0)

    bl = b[c - 1:c]
    kd = (k * jnp.exp(bl - b)).astype(BF16)
    if c >= 16:
        upd = _dot(v.T.astype(BF16), kd)
    else:
        upd = jnp.dot(v.T, kd.astype(F32), preferred_element_type=F32)
    new_state = jnp.exp(bl) * state + upd * bmask_ref[...]
    s_scr[...] = new_state

    o_ref[...] = _seg_rms(o, gn_ref[...], sm_ref[...]) * rs_ref[...]

    @pl.when(last)
    def _():
        sout_ref[...] = new_state


def _gla(qg, kg, la, vg, rs, s0_bd, lw, *, batch, seq, c, bpb):
    nchunks = seq // c
    _, npack = _gla_rows(c)
    npad = -(-npack // 16) * 16
    blk = lambda w: pl.BlockSpec((bpb, c, w), lambda b, i: (b, i, 0))
    full = lambda a: pl.BlockSpec(a.shape, lambda b, i: (0,) * a.ndim)
    consts = [lw["gla_gn"], lw["gla_bm"], lw["gla_bmask"], lw["sm"]]
    st = pl.BlockSpec((bpb, GLA_V, GLA_QK), lambda b, i: (b, 0, 0))
    return pl.pallas_call(
        functools.partial(_gla_kernel, c=c, nchunks=nchunks, bpb=bpb),
        out_shape=[jax.ShapeDtypeStruct((batch, seq, GLA_V), F32), jax.ShapeDtypeStruct((batch, GLA_V, GLA_QK), F32)],
        grid=(batch // bpb, nchunks),
        in_specs=[blk(GLA_QK), blk(GLA_QK), blk(GLA_QK), blk(GLA_V), blk(GLA_V)] + [full(a) for a in consts] + [st],
        out_specs=[blk(GLA_V), st],
        scratch_shapes=[pltpu.VMEM((bpb, GLA_V, GLA_QK), F32), pltpu.VMEM((bpb, npad, GLA_QK), BF16)],
        compiler_params=_cparams(("parallel", "arbitrary")),
        name="gla",
    )(qg, kg, la, vg, rs, *consts, s0_bd)


def _outproj_kernel(x_ref, oa_ref, ob_ref, u_ref, vn_ref, ws_ref, bias_ref, wo_ref, o_ref, *, tm):
    lane = lax.broadcasted_iota(jnp.int32, (GM_CHUNK, GM_W), 1)
    tri = (lax.broadcasted_iota(jnp.int32, (GM_CHUNK, GM_CHUNK), 0)
           >= lax.broadcasted_iota(jnp.int32, (GM_CHUNK, GM_CHUNK), 1))
    zs = []
    for cb in range(tm // GM_CHUNK):
        vn = vn_ref[cb * GM_CHUNK:(cb + 1) * GM_CHUNK, :]
        z = bias_ref[...]
        for g in range(GM_GROUPS):
            wm = jnp.where(tri, ws_ref[g], 0.0).astype(BF16)
            vg = jnp.where((lane >= g * GM_CH) & (lane < (g + 1) * GM_CH), vn, 0.0).astype(BF16)
            z = z + _dot(wm, vg)
        zs.append(z)
    z = zs[0] if len(zs) == 1 else jnp.concatenate(zs, axis=0)
    oc = u_ref[...] * z
    y = _dot(oa_ref[...].astype(BF16), wo_ref[0:NSA_Q, :])
    y = y + _dot(ob_ref[...].astype(BF16), wo_ref[NSA_Q:NSA_Q + GLA_V, :])
    y = y + _dot(oc.astype(BF16), wo_ref[NSA_Q + GLA_V:MIX_OUT, :])
    o_ref[...] = x_ref[...] + y


def _outproj(x, oa, ob, u, vn, ws, bias, wo, li, *, tm):
    m = x.shape[0]
    row = lambda w: pl.BlockSpec((tm, w), lambda i: (i, 0))
    return pl.pallas_call(
        functools.partial(_outproj_kernel, tm=tm),
        out_shape=jax.ShapeDtypeStruct((m, D_MODEL), F32),
        grid=(m // tm,),
        in_specs=[row(D_MODEL), row(NSA_Q), row(GLA_V), row(GM_W), row(GM_W),
                  pl.BlockSpec(ws.shape, lambda i: (0, 0, 0)), pl.BlockSpec(bias.shape, lambda i: (0, 0)),
                  pl.BlockSpec((None, MIX_OUT, D_MODEL), lambda i: (li, 0, 0))],
        out_specs=row(D_MODEL),
        compiler_params=_cparams(("parallel",)),
        name="outproj",
    )(x, oa, ob, u, vn, ws, bias, wo)


def _pcompress_kernel(pt_ref, *refs, npg):
    pages = refs[:npg]
    perm_ref, wk_ref, wv_ref, out_ref, zk_scr, zv_scr = refs[npg:npg + 6]
    perm = perm_ref[...]
    for pp in range(npg // 2):
        xt = jnp.concatenate([pages[2 * pp][...], pages[2 * pp + 1][...]], axis=1).astype(BF16)
        y = _dot_nt(perm, xt).astype(BF16)
        for s in range(CMP_STRIDE):
            rows = slice(pp * 16, (pp + 1) * 16)
            zk_scr[rows, s * LANES:(s + 1) * LANES] = y[s * 16:(s + 1) * 16, 0:LANES]
            zv_scr[rows, s * LANES:(s + 1) * LANES] = y[s * 16:(s + 1) * 16, LANES:2 * LANES]
    q = out_ref.shape[1] // 4
    rk = _dot(zk_scr[...], wk_ref[...])
    rv = _dot(zv_scr[...], wv_ref[...])
    out_ref[:, 0:q] = rk[:, 0:q]
    out_ref[:, q:2 * q] = rv[:, 0:q]
    out_ref[:, 2 * q:3 * q] = rk[:, q:2 * q]
    out_ref[:, 3 * q:4 * q] = rv[:, q:2 * q]


def _pcompress(cache_t, page_table, li, perm, wk, wv, *, npg):
    bs, n_pages = page_table.shape
    page = cache_t.shape[-1]
    nc = n_pages * page // CMP_STRIDE
    steps = n_pages // npg
    cpp = page // CMP_STRIDE

    def page_spec(k):
        return pl.BlockSpec((None, None, 2 * LANES, page), lambda b, h, pt: (li, pt[b, h * npg + k], 0, 0))

    grid_spec = pltpu.PrefetchScalarGridSpec(
        num_scalar_prefetch=1,
        grid=(bs, steps),
        in_specs=[page_spec(k) for k in range(npg)] + [
            pl.BlockSpec(perm.shape, lambda b, h, pt: (0, 0)), pl.BlockSpec(wk.shape, lambda b, h, pt: (0, 0)),
            pl.BlockSpec(wv.shape, lambda b, h, pt: (0, 0))],
        out_specs=pl.BlockSpec((None, npg * cpp, 2 * wk.shape[1]), lambda b, h, pt: (b, h, 0)),
        scratch_shapes=[pltpu.VMEM((npg * cpp, wk.shape[0]), BF16)] * 2,
    )
    return pl.pallas_call(
        functools.partial(_pcompress_kernel, npg=npg),
        out_shape=jax.ShapeDtypeStruct((bs, nc, 2 * wk.shape[1]), F32),
        grid_spec=grid_spec,
        compiler_params=_cparams(("parallel", "arbitrary")),
        name="nsa_page_compress",
    )(page_table, *([cache_t] * npg), perm, wk, wv)


def _ctail_kernel(a_ref, b_ref, bias_ref, w2_ref, gkc_ref, sm_ref, kct_ref, vcr_ref):
    h = a_ref[...] + _next_row(b_ref[...]) + bias_ref[...]
    out = _dot(_gelu(h).astype(BF16), w2_ref[...])
    kc = _seg_rms(out[:, 0:LANES], gkc_ref[...], sm_ref[...])
    kct_ref[...] = kc.T.astype(BF16)
    vcr_ref[...] = out[:, LANES:2 * LANES].astype(BF16)


def _ctail(hid_ab, bias, w2, gkc, sm):
    bs, nc, w2x = hid_ab.shape
    half = w2x // 2
    full = lambda a: pl.BlockSpec(a.shape, lambda b: (0,) * a.ndim)
    return pl.pallas_call(
        _ctail_kernel,
        out_shape=[jax.ShapeDtypeStruct((bs, LANES, nc), BF16), jax.ShapeDtypeStruct((bs, nc, LANES), BF16)],
        grid=(bs,),
        in_specs=[pl.BlockSpec((None, nc, half), lambda b: (b, 0, 0)), pl.BlockSpec((None, nc, half), lambda b: (b, 0, 1)),
                  full(bias), full(w2), full(gkc), full(sm)],
        out_specs=[pl.BlockSpec((None, LANES, nc), lambda b: (b, 0, 0)), pl.BlockSpec((None, nc, LANES), lambda b: (b, 0, 0))],
        compiler_params=_cparams(("parallel",)),
        name="nsa_compress_tail",
    )(hid_ab, hid_ab, bias, w2, gkc, sm)


def _nsa_sample_kernel(pt_ref, qp_ref, qrp_ref, gcol_ref, kct_ref, vcr_ref, cov_ref, e_ref, *refs,
                       npg, nsteps, nc, ns_tot, past_len, ts):
    pages = refs[:npg]
    (wint_ref, newk_ref, neww_ref, newwsh_ref, o_ref, winout_ref,
     bias_scr, m_scr, l_scr, acc_scr, oc_scr, ow_scr) = refs[npg:]
    step = pl.program_id(1)
    nrow = NSA_KV_HEADS * NSA_REP * 8
    q8 = lax.broadcasted_iota(jnp.int32, (nrow, 1), 0) % 8
    t_row = past_len + q8
    qrp = qrp_ref[...]
    n_keep = wint_ref.shape[-1]
    n_top = min(N_SEL, ns_tot)
    nsp = bias_scr.shape[1]

    def expand_rows(a):
        return jnp.concatenate([a[0:8]] * NSA_REP + [a[8:16]] * NSA_REP, axis=0)

    @pl.when(step == 0)
    def _():
        s = _dot(qp_ref[...], kct_ref[...])
        ci = lax.broadcasted_iota(jnp.int32, (1, nc), 1)
        cmask = (ci * CMP_STRIDE + (CMP_BLOCK - 1) <= t_row) & (ci < nc - 1)
        s = jnp.where(cmask, s, -jnp.inf)
        m = jnp.max(s, axis=1, keepdims=True)
        m = jnp.where(m == -jnp.inf, 0.0, m)
        e = jnp.exp(s - m)
        p = e * (1.0 / jnp.maximum(jnp.sum(e, axis=1, keepdims=True), 1e-30))
        oc_scr[...] = _dot(p.astype(BF16), vcr_ref[...])
        sjl = lax.broadcasted_iota(jnp.int32, (1, nsp), 1)
        tq = past_len + lax.broadcasted_iota(jnp.int32, (8, 1), 0)
        bt = tq // SEL_BLOCK
        allowed = (sjl * SEL_BLOCK <= tq) & (sjl < ns_tot)
        forced = (sjl == 0) | (sjl == bt) | (sjl == bt - 1)
        for g in range(NSA_KV_HEADS):
            base = g * NSA_REP * 8
            psum = p[base:base + 8]
            for r in range(1, NSA_REP):
                psum = psum + p[base + r * 8:base + (r + 1) * 8]
            imp = _dot(psum.astype(BF16), cov_ref[...])
            score = jnp.where(allowed, imp + jnp.where(forced, np.float32(SEL_BONUS), 0.0), -jnp.inf)
            rank = jnp.zeros((8, nsp), jnp.int32)
            for k in range(ns_tot):
                col = score[:, k:k + 1]
                later = jnp.where(sjl > k, 1, 0)
                rank = rank + jnp.where(col > score, 1, 0) + jnp.where(col == score, later, 0)
            keep = (rank < n_top) & (score > -jnp.inf)
            bias_scr[g * 8:(g + 1) * 8, :] = jnp.where(keep, 0.0, -jnp.inf)
        m_scr[...] = jnp.full(m_scr.shape, NEG_BIG, F32)
        l_scr[...] = jnp.zeros(l_scr.shape, F32)
        acc_scr[...] = jnp.zeros(acc_scr.shape, F32)

        lane_w = lax.broadcasted_iota(jnp.int32, (1, n_keep), 1)
        diff = t_row - (past_len - n_keep + lane_w)
        s_w = jnp.where((diff >= 0) & (diff < WINDOW), _dot(qrp, wint_ref[0:LANES, :].astype(BF16)), -jnp.inf)
        lane_n = lax.broadcasted_iota(jnp.int32, (1, LANES), 1)
        diff_n = t_row - (past_len + lane_n)
        s_n = jnp.where((lane_n < ts) & (diff_n >= 0) & (diff_n < WINDOW),
                        _dot(qrp, neww_ref[0:LANES, :].astype(BF16)), -jnp.inf)
        sw = jnp.concatenate([s_w, s_n], axis=1)
        mw = jnp.max(sw, axis=1, keepdims=True)
        mw = jnp.where(mw == -jnp.inf, 0.0, mw)
        ew = jnp.exp(sw - mw)
        pw = ew * (1.0 / jnp.maximum(jnp.sum(ew, axis=1, keepdims=True), 1e-30))
        ow_scr[...] = (_dot_nt(pw[:, 0:n_keep].astype(BF16), wint_ref[LANES:2 * LANES, :].astype(BF16))
                       + _dot_nt(pw[:, n_keep:].astype(BF16), neww_ref[LANES:2 * LANES, :].astype(BF16)))
        rolled = pltpu.roll(wint_ref[...], n_keep - ts, axis=1)
        lane_o = lax.broadcasted_iota(jnp.int32, (2 * LANES, LANES), 1)
        winout_ref[:, 0:n_keep - LANES] = rolled[:, 0:n_keep - LANES]
        winout_ref[:, n_keep - LANES:n_keep] = jnp.where(lane_o >= LANES - ts, newwsh_ref[...],
                                                         rolled[:, n_keep - LANES:n_keep])

    def online(s, vt):
        m_old = m_scr[:, 0:1]
        m_new = jnp.maximum(m_old, jnp.max(s, axis=1, keepdims=True))
        p = jnp.exp(s - m_new)
        alpha = jnp.exp(m_old - m_new)
        l_new = alpha * l_scr[:, 0:1] + jnp.sum(p, axis=1, keepdims=True)
        acc_scr[...] = alpha * acc_scr[...] + _dot_nt(p.astype(BF16), vt)
        m_scr[...] = jnp.broadcast_to(m_new, m_scr.shape)
        l_scr[...] = jnp.broadcast_to(l_new, l_scr.shape)

    sel01 = jnp.where(bias_scr[...] == 0.0, 1.0, 0.0).astype(BF16)
    bexp = _dot(sel01, e_ref[...])
    bias = expand_rows(jnp.where(bexp > 0.5, 0.0, -jnp.inf))
    kt = jnp.concatenate([pg[0:LANES, :] for pg in pages], axis=1).astype(BF16)
    vt = jnp.concatenate([pg[LANES:2 * LANES, :] for pg in pages], axis=1).astype(BF16)
    online(_dot(qrp, kt) + bias, vt)

    @pl.when(step == nsteps - 1)
    def _():
        lane_n = lax.broadcasted_iota(jnp.int32, (1, LANES), 1)
        bcol = expand_rows(bias_scr[:, ns_tot - 1:ns_tot])
        ok = (lane_n < ts) & (past_len + lane_n <= t_row)
        s_n = jnp.where(ok, _dot(qrp, newk_ref[2 * LANES:3 * LANES, :].astype(BF16)) + bcol, -jnp.inf)
        online(s_n, newk_ref[3 * LANES:4 * LANES, :].astype(BF16))
        o_s = acc_scr[...] * (1.0 / jnp.maximum(l_scr[:, 0:1], 1e-30))
        gc = gcol_ref[...]
        o_ref[...] = oc_scr[...] * gc[:, 0:1] + o_s * gc[:, 1:2] + ow_scr[...] * gc[:, 2:3]


def _nsa_sample(cache_t, page_table, li, qp, qrp, gcol, kct, vcr, cov_s, emat, win_t, newk_t, neww_t, newwsh_t, *,
                npg, past_len, ts):
    bs, n_pages = page_table.shape
    page = cache_t.shape[-1]
    nsteps = n_pages // npg
    nc = kct.shape[-1]
    ns_tot = -(-(past_len + ts) // SEL_BLOCK)
    nsp = cov_s.shape[1]
    n_keep = win_t.shape[-1]
    nrow = qp.shape[1]
    per_b = lambda a: pl.BlockSpec((None,) + a.shape[1:], lambda b, h, pt: (b,) + (0,) * (a.ndim - 1))

    def page_spec(k):
        return pl.BlockSpec((None, None, 2 * LANES, page), lambda b, h, pt: (li, pt[b, h * npg + k], 1, 0))

    grid_spec = pltpu.PrefetchScalarGridSpec(
        num_scalar_prefetch=1,
        grid=(bs, nsteps),
        in_specs=[per_b(qp), per_b(qrp), per_b(gcol), per_b(kct), per_b(vcr),
                  pl.BlockSpec(cov_s.shape, lambda b, h, pt: (0, 0)),
                  pl.BlockSpec((nsp, npg * page), lambda b, h, pt: (0, h))]
                 + [page_spec(k) for k in range(npg)]
                 + [pl.BlockSpec((None, None, 2 * LANES, n_keep), lambda b, h, pt: (li, b, 0, 0)),
                    per_b(newk_t), per_b(neww_t), per_b(newwsh_t)],
        out_specs=[pl.BlockSpec((None, nrow, LANES), lambda b, h, pt: (b, 0, 0)),
                   pl.BlockSpec((None, 2 * LANES, n_keep), lambda b, h, pt: (b, 0, 0))],
        scratch_shapes=[pltpu.VMEM((2 * 8, nsp), F32)] + [pltpu.VMEM((nrow, LANES), F32)] * 5,
    )
    return pl.pallas_call(
        functools.partial(_nsa_sample_kernel, npg=npg, nsteps=nsteps, nc=nc, ns_tot=ns_tot, past_len=past_len, ts=ts),
        out_shape=[jax.ShapeDtypeStruct((bs, nrow, LANES), F32), jax.ShapeDtypeStruct((bs, 2 * LANES, n_keep), F32)],
        grid_spec=grid_spec,
        compiler_params=_cparams(("parallel", "arbitrary")),
        name="nsa_sample",
    )(page_table, qp, qrp, gcol, kct, vcr, cov_s, emat, *([cache_t] * npg), win_t, newk_t, neww_t, newwsh_t)


def _pad_cols(w):
    cuts = np.cumsum((0,) + IN_SPLITS)
    parts = []
    for i, (n, p) in enumerate(zip(IN_SPLITS, IN_PADDED)):
        seg = w[..., cuts[i]:cuts[i] + n]
        if p != n:
            seg = jnp.pad(seg, [(0, 0)] * (w.ndim - 1) + [(0, p - n)])
        parts.append(seg)
    return jnp.concatenate(parts, axis=-1)


def _rope_tables(pos):
    half = HEAD_DIM // 2
    inv = 1.0 / (ROPE_THETA ** (jnp.arange(half, dtype=F32) * (2.0 / HEAD_DIM)))
    ang = pos.astype(F32)[:, None] * inv[None, :]
    cos = jnp.cos(ang)
    sin = jnp.sin(ang)
    cos_f = jnp.concatenate([cos, cos, cos, cos], axis=1)
    sin_f = jnp.concatenate([-sin, sin, -sin, sin], axis=1)
    return cos_f, sin_f


def _cover_t(seq):
    nc = seq // CMP_STRIDE
    ns = seq // SEL_BLOCK
    ci = np.arange(nc)[None, :]
    sj = np.arange(ns)[:, None]
    cov = ((ci * CMP_STRIDE <= sj * SEL_BLOCK + SEL_BLOCK - 1)
           & (ci * CMP_STRIDE + CMP_BLOCK - 1 >= sj * SEL_BLOCK) & (ci < nc - 1))
    return jnp.asarray(cov, dtype=BF16)


def _layer_weights(l, ln_gains, w_in_p, nsa_qk_norm, nsa_cmp_pos, nsa_cmp_w1, nsa_cmp_w2, gla_gate_w, gla_gate_b,
                   gla_norm, gm_ln, gm_ws, gm_b):
    eye2 = jnp.eye(NSA_KV_HEADS, dtype=F32)
    lw = {"ln0": ln_gains[l, 0][None], "ln1": ln_gains[l, 1][None], "ln2": ln_gains[l, 2][None], "w_in": w_in_p[l]}
    lw["gq"] = jnp.tile(nsa_qk_norm[l, 0], NSA_HEADS)[None]
    lw["gkc"] = jnp.tile(nsa_qk_norm[l, 1], NSA_KV_HEADS)[None]
    lw["gks"] = jnp.tile(nsa_qk_norm[l, 2], NSA_KV_HEADS)[None]
    lw["gkw"] = jnp.tile(nsa_qk_norm[l, 3], NSA_KV_HEADS)[None]
    seg = (np.arange(LANES)[:, None] // HEAD_DIM) == (np.arange(LANES)[None, :] // HEAD_DIM)
    lw["sm"] = jnp.asarray(seg * (1.0 / HEAD_DIM), dtype=BF16)
    lw["gla_gw"] = jnp.pad(gla_gate_w[l], ((0, LANES - GLA_GATE_RANK), (0, 0))).astype(BF16)
    lw["gla_gb"] = gla_gate_b[l][None]
    lw["gm_lng"] = gm_ln[l, 0][None]
    lw["gm_lnb"] = gm_ln[l, 1][None]
    pos_rows = []
    for c, nm in ((0, "k"), (1, "v")):
        w1 = nsa_cmp_w1[l, c].reshape(CMP_BLOCK, HEAD_DIM, CMP_HIDDEN)
        for half, tag in ((w1[:CMP_STRIDE], "a"), (w1[CMP_STRIDE:], "b")):
            wx = jnp.einsum("sdh,pg->spdgh", half, eye2)
            lw["cmp_w" + tag + nm] = wx.reshape(CMP_STRIDE * LANES, NSA_KV_HEADS * CMP_HIDDEN).astype(BF16)
        lw["cmp_w2" + nm] = jnp.einsum("hd,pg->phgd", nsa_cmp_w2[l, c], eye2).reshape(
            NSA_KV_HEADS * CMP_HIDDEN, LANES).astype(BF16)
        pe = nsa_cmp_pos[l, c]
        for half in (pe[:CMP_STRIDE], pe[CMP_STRIDE:]):
            pos_rows.append(jnp.broadcast_to(half[:, None, :], (CMP_STRIDE, NSA_KV_HEADS, HEAD_DIM)).reshape(-1))
    lw["cmp_pos"] = jnp.stack(pos_rows)
    w1ab = nsa_cmp_w1[l].reshape(2, 2, CMP_STRIDE, HEAD_DIM, CMP_HIDDEN)
    for c, nm in ((0, "k"), (1, "v")):
        lw["pc_w" + nm] = jnp.einsum("asdh,gy->sgdayh", w1ab[c], eye2).reshape(
            CMP_STRIDE * LANES, 2 * NSA_KV_HEADS * CMP_HIDDEN).astype(BF16)
    pb = jnp.einsum("ck,ckh->ch", nsa_cmp_pos[l].reshape(2, -1), nsa_cmp_w1[l], precision=lax.Precision.HIGHEST)
    lw["pc_bias"] = jnp.broadcast_to(pb[:, None, :], (2, NSA_KV_HEADS, CMP_HIDDEN)).reshape(1, -1)
    lw["pc_w2"] = jnp.einsum("chd,cx,gy->cghxyd", nsa_cmp_w2[l], eye2, eye2).reshape(
        2 * NSA_KV_HEADS * CMP_HIDDEN, 2 * LANES).astype(BF16)
    lw["gla_gn"] = jnp.tile(gla_norm[l], GLA_HEADS)[None]
    hq = np.arange(GLA_QK) // GLA_DK
    hv = np.arange(GLA_V) // GLA_DV
    lw["gla_bm"] = jnp.asarray(hq[:, None] == hv[None, :], dtype=BF16)
    lw["gla_bmask"] = jnp.asarray(hv[:, None] == hq[None, :], dtype=F32)
    lw["gm_ws"] = gm_ws[l]
    lw["gm_bias"] = jnp.repeat(gm_b[l].T, GM_CH, axis=1)
    return lw


def _page_perm(page):
    cpp = page // CMP_STRIDE
    r = np.arange(2 * page)
    s_, rem = r // (2 * cpp), r % (2 * cpp)
    t = (rem // cpp) * page + CMP_STRIDE * (rem % cpp) + s_
    m = np.zeros((2 * page, 2 * page), np.float32)
    m[r, t] = 1.0
    return jnp.asarray(m, dtype=BF16)


def _cover_sample(past_len, ts, nsp):
    t_tot = past_len + ts
    n_c = (t_tot - CMP_BLOCK) // CMP_STRIDE + 1
    nc = past_len // CMP_STRIDE
    n_s = -(-t_tot // SEL_BLOCK)
    ci = np.arange(nc)[:, None]
    sj = np.arange(nsp)[None, :]
    cov = ((ci * CMP_STRIDE <= sj * SEL_BLOCK + SEL_BLOCK - 1) & (ci * CMP_STRIDE + CMP_BLOCK - 1 >= sj * SEL_BLOCK)
           & (ci < n_c) & (sj < n_s))
    emat = (np.arange(past_len)[None, :] // SEL_BLOCK) == np.arange(nsp)[:, None]
    return jnp.asarray(cov, dtype=BF16), jnp.asarray(emat, dtype=BF16)


def _rows_gr8(a, ts):
    bs = a.shape[0]
    a = a.reshape(bs, ts, NSA_KV_HEADS, NSA_REP, a.shape[-1]).transpose(0, 2, 3, 1, 4)
    return jnp.pad(a, ((0, 0), (0, 0), (0, 0), (0, 8 - ts), (0, 0)))


def _state_to_t(s):
    b = s.shape[0]
    eye = jnp.eye(GLA_HEADS, dtype=s.dtype)
    return jnp.einsum("bhkv,hg->bhvgk", s, eye).reshape(b, GLA_V, GLA_QK)


def _state_from_t(st):
    b = st.shape[0]
    s5 = st.reshape(b, GLA_HEADS, GLA_DV, GLA_HEADS, GLA_DK)
    d = jnp.stack([s5[:, h, :, h, :] for h in range(GLA_HEADS)], axis=1)
    return jnp.swapaxes(d, 2, 3)


def kernel(x_prompt, x_sample, cache_kv, cache_win_kv, state_gla, page_table, ln_gains, ffn_w_gate_up, ffn_w_down,
           w_in, w_out, nsa_qk_norm, nsa_cmp_pos, nsa_cmp_w1, nsa_cmp_w2, gla_gate_w, gla_gate_b, gla_norm, gm_ln,
           gm_ws, gm_b):
    depth = w_in.shape[0]
    bp, tp, _ = x_prompt.shape
    bs, ts, _ = x_sample.shape
    n_pages = page_table.shape[1]
    page = cache_kv.shape[2]
    past_len = n_pages * page
    tq = 8
    mp, ms = bp * tp, bs * tq
    nc = tp // CMP_STRIDE

    w_gu = ffn_w_gate_up.astype(BF16).reshape(depth * 2, D_MODEL, 2 * D_FF)
    w_d = ffn_w_down.astype(BF16).reshape(depth * 2, D_FF, D_MODEL)
    w_in_p = _pad_cols(w_in).astype(BF16)
    w_o = w_out.astype(BF16)
    cos_p, sin_p = _rope_tables(jnp.arange(tp))
    cos_s, sin_s = _rope_tables(past_len + jnp.arange(ms) % tq)
    cov = _cover_t(tp)
    tm_p = 512 if mp % 512 == 0 else Q_BLOCK
    tm_f = 1024 if mp % 1024 == 0 else tm_p
    tm_i = 512 if tp % 512 == 0 else Q_BLOCK
    tf = 1408

    nrow = NSA_HEADS * 8
    n_keep_s = cache_win_kv.shape[2]
    nsp = -(-(past_len // SEL_BLOCK + 1) // LANES) * LANES
    cov_s, emat = _cover_sample(past_len, ts, nsp)
    perm = _page_perm(page)
    eye2 = jnp.eye(NSA_KV_HEADS, dtype=F32)
    npg_c = min(32, n_pages)
    npg_a = min(32, n_pages)
    cache_t = jnp.transpose(cache_kv, (0, 1, 3, 4, 5, 2)).reshape(depth, cache_kv.shape[1], 4 * LANES, page)
    win_t = jnp.transpose(cache_win_kv, (0, 1, 3, 4, 5, 2)).reshape(depth, bs, 2 * LANES, n_keep_s)

    xp = x_prompt.reshape(mp, D_MODEL)
    xs = jnp.pad(x_sample, ((0, 0), (0, tq - ts), (0, 0))).reshape(ms, D_MODEL)
    real_row = (jnp.arange(tq) < ts).astype(F32)[None, :, None]
    kv_p, win_p, gla_p, kv_s, win_s, gla_s, gmv_s = [], [], [], [], [], [], []
    eye_b = jnp.eye(GM_CHUNK // tq, dtype=F32)
    for l in range(depth):
        lw = _layer_weights(l, ln_gains, w_in_p, nsa_qk_norm, nsa_cmp_pos, nsa_cmp_w1, nsa_cmp_w2, gla_gate_w,
                            gla_gate_b, gla_norm, gm_ln, gm_ws, gm_b)
        xp = _ffn(xp, lw["ln0"], w_gu, w_d, 2 * l, tm=tm_f, tf=tf)
        (newkv, newwin, _, qg, kg, la, vg, rs, u, vn, qs, qrs, gt, ksel_r, vsel_t, kwin_r, vwin_t, kcmp,
         vcmp) = _inproj(xp, lw, cos_p, sin_p, tm=tm_i, attn_layout=True, batch=bp, seq=tp)
        zk = kcmp.reshape(bp, nc, CMP_STRIDE * LANES)
        zv = vcmp.reshape(bp, nc, CMP_STRIDE * LANES)
        kc, vct = _compress(zk, zv, lw)
        oa = _nsa_prompt(qs, qrs, gt, kc, vct, cov, ksel_r.reshape(bp, tp, LANES), vsel_t,
                         kwin_r.reshape(bp, tp, LANES), vwin_t, batch=bp, seq=tp)
        r3 = lambda a: a.reshape(bp, tp, a.shape[-1])
        ob, st = _gla(r3(qg), r3(kg), r3(la), r3(vg), r3(rs), jnp.zeros((bp, GLA_V, GLA_QK), F32), lw,
                      batch=bp, seq=tp, c=GLA_CHUNK, bpb=1)
        xp = _outproj(xp, oa, ob.reshape(mp, GLA_V), u, vn, lw["gm_ws"], lw["gm_bias"], w_o, l, tm=tm_p)
        xp = _ffn(xp, lw["ln2"], w_gu, w_d, 2 * l + 1, tm=tm_f, tf=tf)
        kv_p.append(newkv.reshape(bp, 4, NSA_KV_HEADS, HEAD_DIM, tp).transpose(0, 4, 1, 2, 3))
        n_keep = min(WINDOW, tp)
        win_p.append(newwin[:, :, tp - n_keep:].reshape(bp, 2, NSA_KV_HEADS, HEAD_DIM, n_keep).transpose(0, 4, 1, 2, 3))
        gla_p.append(_state_from_t(st))

        xs = _ffn(xs, lw["ln0"], w_gu, w_d, 2 * l, tm=ms, tf=tf)
        (newkv, newwin, gates, qg, kg, la, vg, rs, u, vn, qn, qr) = _inproj(
            xs, lw, cos_s, sin_s, tm=ms, attn_layout=False, batch=bs, seq=ts)
        hid = _pcompress(cache_t, page_table, l, perm, lw["pc_wk"], lw["pc_wv"], npg=npg_c)
        kct, vcr = _ctail(hid, lw["pc_bias"], lw["pc_w2"], lw["gkc"], lw["sm"])
        scale = np.float32(HEAD_DIM ** -0.5)
        to_pad = lambda q: jnp.einsum("bgrqd,gx->bgrqxd", _rows_gr8(q.reshape(bs, tq, NSA_HEADS, HEAD_DIM) * scale, tq),
                                      eye2).reshape(bs, nrow, LANES).astype(BF16)
        gcol = _rows_gr8(gates[:, :3 * NSA_HEADS].reshape(bs, tq, NSA_HEADS, 3), tq).reshape(bs, nrow, 3)
        gcol = jnp.pad(gcol, ((0, 0), (0, 0), (0, LANES - 3)))
        newk_t = jnp.pad(newkv.reshape(bs, tq, -1).transpose(0, 2, 1), ((0, 0), (0, 0), (0, LANES - tq)))
        neww_c = newwin.reshape(bs, tq, -1).transpose(0, 2, 1)[:, :, :ts]
        neww_t = jnp.pad(neww_c, ((0, 0), (0, 0), (0, LANES - ts)))
        newwsh_t = jnp.pad(neww_c, ((0, 0), (0, 0), (LANES - ts, 0)))
        o64, win_o = _nsa_sample(cache_t, page_table, l, to_pad(qn), to_pad(qr), gcol, kct, vcr, cov_s, emat, win_t,
                                 newk_t, neww_t, newwsh_t, npg=npg_a, past_len=past_len, ts=ts)
        o6 = o64.reshape(bs, NSA_KV_HEADS, NSA_REP, tq, NSA_KV_HEADS, HEAD_DIM)
        oa = jnp.stack([o6[:, g, :, :, g, :] for g in range(NSA_KV_HEADS)], axis=1)
        oa = (oa.transpose(0, 3, 1, 2, 4).reshape(bs, tq, NSA_Q) * real_row).reshape(ms, NSA_Q)
        r8 = lambda a: a.reshape(bs, tq, a.shape[-1])
        ob8, st_s = _gla(r8(qg), r8(kg), r8(la) * real_row, r8(vg), r8(rs), _state_to_t(state_gla[l]), lw,
                         batch=bs, seq=tq, c=tq, bpb=4 if bs % 4 == 0 else 1)
        ob = ob8.reshape(ms, GLA_V)
        ws_s = jnp.einsum("gts,bc->gbtcs", gm_ws[l][:, :tq, :tq], eye_b).reshape(GM_GROUPS, GM_CHUNK, GM_CHUNK)
        bias_s = jnp.tile(lw["gm_bias"][:tq], (GM_CHUNK // tq, 1))
        xs = _outproj(xs, oa, ob, u, vn, ws_s, bias_s, w_o, l, tm=GM_CHUNK)
        xs = _ffn(xs, lw["ln2"], w_gu, w_d, 2 * l + 1, tm=ms, tf=tf)
        kv_s.append(newkv.reshape(bs, tq, 4, NSA_KV_HEADS, HEAD_DIM)[:, :ts])
        win_s.append(win_o.reshape(bs, 2, NSA_KV_HEADS, HEAD_DIM, n_keep_s).transpose(0, 4, 1, 2, 3))
        gla_s.append(_state_from_t(st_s))
        gmv_s.append(vn.reshape(bs, tq, GM_GROUPS, GM_CH)[:, :ts])
    return (xp.reshape(bp, tp, D_MODEL), xs.reshape(bs, tq, D_MODEL)[:, :ts], jnp.stack(kv_p), jnp.stack(win_p),
            jnp.stack(gla_p), jnp.stack(kv_s), jnp.stack(win_s), jnp.stack(gla_s), jnp.stack(gmv_s))
```

```python
import functools

import numpy as np
import jax
import jax.numpy as jnp
from jax import lax
from jax.experimental import pallas as pl
from jax.experimental.pallas import tpu as pltpu

F32 = jnp.float32
BF16 = jnp.bfloat16

D_MODEL = 1024
HEAD_DIM = 64
NSA_HEADS = 8
NSA_KV_HEADS = 2
NSA_REP = NSA_HEADS // NSA_KV_HEADS
CMP_STRIDE = 16
CMP_BLOCK = 2 * CMP_STRIDE
CMP_HIDDEN = 128
SEL_BLOCK = 64
N_SEL = 16
WINDOW = 512
Q_BLOCK = 128
SEL_BONUS = 1.0e4
GLA_HEADS = 4
GLA_DK = 32
GLA_DV = 64
GLA_GATE_RANK = 16
GLA_GATE_TEMP = 16.0
GLA_CHUNK = 64
GM_GROUPS = 4
GM_CH = 64
GM_CHUNK = 128
D_FF = 2816
ROPE_THETA = 10000.0
EPS = 1e-6

NSA_Q = NSA_HEADS * HEAD_DIM
NSA_KV = NSA_KV_HEADS * HEAD_DIM
GLA_QK = GLA_HEADS * GLA_DK
GLA_V = GLA_HEADS * GLA_DV
GM_W = GM_GROUPS * GM_CH
MIX_OUT = NSA_Q + GLA_V + GM_W
IN_SPLITS = (NSA_Q, 6 * NSA_KV, 3 * NSA_HEADS, GLA_QK, GLA_QK, GLA_V, GLA_GATE_RANK, GLA_V, GM_W, GM_W)
IN_PADDED = tuple(-(-s // 128) * 128 for s in IN_SPLITS)
IN_OFFS = tuple(int(v) for v in np.cumsum((0,) + IN_PADDED))
D_IN_PAD = IN_OFFS[-1]

LANES = 128
NEG_BIG = -1.0e30
VMEM_LIMIT = 56 * 1024 * 1024


def _cparams(sem):
    return pltpu.CompilerParams(dimension_semantics=sem, vmem_limit_bytes=VMEM_LIMIT)


def _gelu(x):
    c = np.float32(np.sqrt(2.0 / np.pi))
    return x * (0.5 * (1.0 + jnp.tanh(c * (x + 0.044715 * (x * x * x)))))


def _sigmoid(x):
    return 1.0 / (1.0 + jnp.exp(-x))


def _dot(a, b):
    return jnp.dot(a, b, preferred_element_type=F32)


def _dot_nt(a, b):
    return lax.dot_general(a, b, (((1,), (1,)), ((), ())), preferred_element_type=F32)


def _seg_mean_sq(x, sm):
    sq = x * x
    hi = sq.astype(BF16)
    lo = (sq - hi.astype(F32)).astype(BF16)
    outs = []
    for c in range(x.shape[1] // LANES):
        sl = slice(c * LANES, (c + 1) * LANES)
        outs.append(_dot(hi[:, sl], sm) + _dot(lo[:, sl], sm))
    return outs[0] if len(outs) == 1 else jnp.concatenate(outs, axis=1)


def _seg_rms(x, gain, sm):
    return x * lax.rsqrt(_seg_mean_sq(x, sm) + EPS) * gain


def _tile_lanes(a, w):
    n = w // a.shape[1]
    return a if n == 1 else jnp.concatenate([a] * n, axis=1)


def _rope(x, cos, sin_signed):
    w = x.shape[1]
    lane = lax.broadcasted_iota(jnp.int32, x.shape, 1)
    fwd = pltpu.roll(x, w - HEAD_DIM // 2, axis=1)
    bwd = pltpu.roll(x, HEAD_DIM // 2, axis=1)
    partner = jnp.where((lane % HEAD_DIM) < HEAD_DIM // 2, fwd, bwd)
    return x * _tile_lanes(cos, w) + partner * _tile_lanes(sin_signed, w)


def _ffn_kernel(x_ref, g_ref, wg_ref, wu_ref, wd_ref, o_ref, h_scr, acc_scr, *, nj):
    j = pl.program_id(1)

    @pl.when(j == 0)
    def _():
        x = x_ref[...]
        ms = jnp.mean(x * x, axis=-1, keepdims=True)
        h_scr[...] = (x * lax.rsqrt(ms + EPS) * g_ref[...]).astype(BF16)
        acc_scr[...] = jnp.zeros_like(acc_scr)

    h = h_scr[...]
    g = _dot(h, wg_ref[...])
    u = _dot(h, wu_ref[...])
    a = (g * _sigmoid(g)) * u
    acc_scr[...] += _dot(a.astype(BF16), wd_ref[...])

    @pl.when(j == nj - 1)
    def _():
        o_ref[...] = x_ref[...] + 0.5 * acc_scr[...]


def _ffn(x, gain, w_gu, w_d, li, *, tm, tf):
    m = x.shape[0]
    nj = D_FF // tf
    return pl.pallas_call(
        functools.partial(_ffn_kernel, nj=nj),
        out_shape=jax.ShapeDtypeStruct((m, D_MODEL), F32),
        grid=(m // tm, nj),
        in_specs=[
            pl.BlockSpec((tm, D_MODEL), lambda i, j: (i, 0)),
            pl.BlockSpec((1, D_MODEL), lambda i, j: (0, 0)),
            pl.BlockSpec((None, D_MODEL, tf), lambda i, j: (li, 0, j)),
            pl.BlockSpec((None, D_MODEL, tf), lambda i, j: (li, 0, j + nj)),
            pl.BlockSpec((None, tf, D_MODEL), lambda i, j: (li, j, 0)),
        ],
        out_specs=pl.BlockSpec((tm, D_MODEL), lambda i, j: (i, 0)),
        scratch_shapes=[pltpu.VMEM((tm, D_MODEL), BF16), pltpu.VMEM((tm, D_MODEL), F32)],
        compiler_params=_cparams(("parallel", "arbitrary")),
        name="ffn",
    )(x, gain, w_gu, w_gu, w_d)


def _group_padded(arr, h):
    c = arr[:, (h // 2) * LANES:(h // 2 + 1) * LANES]
    g = h // NSA_REP
    if (h % 2) != g:
        c = pltpu.roll(c, HEAD_DIM, axis=1)
    lane = lax.broadcasted_iota(jnp.int32, c.shape, 1)
    keep = (lane >= g * HEAD_DIM) & (lane < (g + 1) * HEAD_DIM)
    return jnp.where(keep, c, 0.0)


def _inproj_kernel(x_ref, ln_ref, w_ref, gq_ref, gks_ref, gkw_ref, cos_ref, sin_ref, gw_ref, gb_ref,
                   lng_ref, lnb_ref, sm_ref, *outs, tm, attn_layout):
    (newkv_ref, newwin_ref, gates_ref, qg_ref, kg_ref, la_ref, vg_ref, rs_ref, u_ref, vn_ref) = outs[:10]
    x = x_ref[...]
    ms = jnp.mean(x * x, axis=-1, keepdims=True)
    h = (x * lax.rsqrt(ms + EPS) * ln_ref[...]).astype(BF16)
    p = _dot(h, w_ref[...])
    sm = sm_ref[...]
    cos = cos_ref[...]
    sin = sin_ref[...]
    o = IN_OFFS

    def seg(i, a=0, b=None):
        b = IN_PADDED[i] if b is None else b
        return p[:, o[i] + a:o[i] + b]

    qn = _seg_rms(seg(0), gq_ref[...], sm)
    qr = _rope(qn, cos, sin)
    kv = [seg(1, LANES * j, LANES * (j + 1)) for j in range(6)]
    ksel = _rope(_seg_rms(kv[2], gks_ref[...], sm), cos, sin)
    kwin = _rope(_seg_rms(kv[4], gkw_ref[...], sm), cos, sin)
    vsel_t = kv[3].T
    vwin_t = kv[5].T
    if attn_layout:
        for j, a in enumerate((kv[0].T, kv[1].T, ksel.T, vsel_t)):
            newkv_ref[j * LANES:(j + 1) * LANES, :] = a
        newwin_ref[0:LANES, :] = kwin.T
        newwin_ref[LANES:2 * LANES, :] = vwin_t
    else:
        for j, a in enumerate((kv[0], kv[1], ksel, kv[3])):
            newkv_ref[:, j * LANES:(j + 1) * LANES] = a
        newwin_ref[:, 0:LANES] = kwin
        newwin_ref[:, LANES:2 * LANES] = kv[5]
    gates = _sigmoid(seg(2))
    gates_ref[...] = gates
    qg_ref[...] = seg(3) * np.float32(GLA_DK ** -0.5)
    kg_ref[...] = seg(4)
    vg_ref[...] = seg(5)
    logit = _dot(seg(6).astype(BF16), gw_ref[...]) + gb_ref[...]
    log_sig = jnp.minimum(logit, 0.0) - jnp.log1p(jnp.exp(-jnp.abs(logit)))
    la_ref[...] = log_sig * np.float32(1.0 / GLA_GATE_TEMP)
    r = seg(7)
    rs_ref[...] = r * _sigmoid(r)
    u_ref[...] = _gelu(seg(8))
    v = _gelu(seg(9))
    mu = jnp.mean(v, axis=-1, keepdims=True)
    var = jnp.mean(jnp.square(v - mu), axis=-1, keepdims=True)
    vn_ref[...] = (v - mu) * lax.rsqrt(var + EPS) * lng_ref[...] + lnb_ref[...]

    if attn_layout:
        (qs_ref, qrs_ref, gt_ref, kselr_ref, vselt_ref, kwinr_ref, vwint_ref, kcmp_ref, vcmp_ref) = outs[10:]
        scale = np.float32(HEAD_DIM ** -0.5 * np.log2(np.e))
        qs = qn * scale
        qrs = qr * scale
        for hh in range(NSA_HEADS):
            a = _group_padded(qs, hh).astype(BF16)
            b = _group_padded(qrs, hh).astype(BF16)
            for rb in range(tm // Q_BLOCK):
                qs_ref[rb, hh] = a[rb * Q_BLOCK:(rb + 1) * Q_BLOCK]
                qrs_ref[rb, hh] = b[rb * Q_BLOCK:(rb + 1) * Q_BLOCK]
        for rb in range(tm // Q_BLOCK):
            gt_ref[rb] = gates[rb * Q_BLOCK:(rb + 1) * Q_BLOCK].T
        kselr_ref[...] = ksel.astype(BF16)
        vselt_ref[...] = vsel_t.astype(BF16)
        kwinr_ref[...] = kwin.astype(BF16)
        vwint_ref[...] = vwin_t.astype(BF16)
        kcmp_ref[...] = kv[0]
        vcmp_ref[...] = kv[1]
    else:
        qn_ref, qr_ref = outs[10:]
        qn_ref[...] = qn
        qr_ref[...] = qr


def _inproj(x, lw, cos_t, sin_t, *, tm, attn_layout, batch, seq):
    m = x.shape[0]
    nt = m // tm
    ntab = cos_t.shape[0] // tm
    row = lambda w: pl.BlockSpec((tm, w), lambda i: (i, 0))
    full = lambda a: pl.BlockSpec(a.shape, lambda i: (0,) * a.ndim)
    ins = [x, lw["ln1"], lw["w_in"], lw["gq"], lw["gks"], lw["gkw"], cos_t, sin_t, lw["gla_gw"], lw["gla_gb"],
           lw["gm_lng"], lw["gm_lnb"], lw["sm"]]
    in_specs = [row(D_MODEL), full(lw["ln1"]), full(lw["w_in"]), full(lw["gq"]), full(lw["gks"]), full(lw["gkw"]),
                pl.BlockSpec((tm, LANES), lambda i: (i % ntab, 0)), pl.BlockSpec((tm, LANES), lambda i: (i % ntab, 0)),
                full(lw["gla_gw"]), full(lw["gla_gb"]), full(lw["gm_lng"]), full(lw["gm_lnb"]), full(lw["sm"])]
    widths = [512, 256, 128, 128, 128, 128, 256, 256, 256, 256]
    out_shape = [jax.ShapeDtypeStruct((m, w), F32) for w in widths]
    out_specs = [row(w) for w in widths]
    if attn_layout:
        nqb = m // Q_BLOCK
        rpb = tm // Q_BLOCK
        tpb = seq // tm
        for j in range(2):
            out_shape[j] = jax.ShapeDtypeStruct((batch, widths[j], seq), F32)
            out_specs[j] = pl.BlockSpec((None, widths[j], tm), lambda i: (i // tpb, 0, i % tpb))
        out_shape += [jax.ShapeDtypeStruct((nqb, NSA_HEADS, Q_BLOCK, LANES), BF16)] * 2
        out_specs += [pl.BlockSpec((rpb, NSA_HEADS, Q_BLOCK, LANES), lambda i: (i, 0, 0, 0))] * 2
        out_shape += [jax.ShapeDtypeStruct((nqb, LANES, Q_BLOCK), F32)]
        out_specs += [pl.BlockSpec((rpb, LANES, Q_BLOCK), lambda i: (i, 0, 0))]
        rowmaj = (jax.ShapeDtypeStruct((m, LANES), BF16), row(LANES))
        trans = (jax.ShapeDtypeStruct((batch, LANES, seq), BF16),
                 pl.BlockSpec((None, LANES, tm), lambda i: (i // tpb, 0, i % tpb)))
        for sh, sp in (rowmaj, trans, rowmaj, trans):
            out_shape.append(sh)
            out_specs.append(sp)
        out_shape += [jax.ShapeDtypeStruct((m, LANES), F32)] * 2
        out_specs += [row(LANES)] * 2
    else:
        out_shape += [jax.ShapeDtypeStruct((m, NSA_Q), F32)] * 2
        out_specs += [row(NSA_Q)] * 2
    return pl.pallas_call(
        functools.partial(_inproj_kernel, tm=tm, attn_layout=attn_layout),
        out_shape=out_shape,
        grid=(nt,),
        in_specs=in_specs,
        out_specs=out_specs,
        compiler_params=_cparams(("parallel",)),
        name="inproj",
    )(*ins)


def _next_row(b):
    return pltpu.roll(b, b.shape[0] - 1, axis=0)


def _compress_kernel(zk_ref, zv_ref, pos_ref, wak_ref, wbk_ref, wav_ref, wbv_ref,
                     w2k_ref, w2v_ref, gkc_ref, sm_ref, kc_ref, vct_ref):
    pos = pos_ref[...]

    def one(z_ref, pa, pb, wa_ref, wb_ref, w2_ref):
        z = z_ref[...]
        a = _dot((z + pa).astype(BF16), wa_ref[...])
        b = _dot((z + pb).astype(BF16), wb_ref[...])
        return _dot(_gelu(a + _next_row(b)).astype(BF16), w2_ref[...])

    ck = one(zk_ref, pos[0:1], pos[1:2], wak_ref, wbk_ref, w2k_ref)
    cv = one(zv_ref, pos[2:3], pos[3:4], wav_ref, wbv_ref, w2v_ref)
    kc_ref[...] = _seg_rms(ck, gkc_ref[...], sm_ref[...]).astype(BF16)
    vct_ref[...] = cv.T.astype(BF16)


def _compress(zk, zv, lw):
    b, nc, kdim = zk.shape
    zspec = pl.BlockSpec((None, nc, kdim), lambda i: (i, 0, 0))
    full = lambda a: pl.BlockSpec(a.shape, lambda i: (0,) * a.ndim)
    ws = [lw["cmp_pos"], lw["cmp_wak"], lw["cmp_wbk"], lw["cmp_wav"], lw["cmp_wbv"], lw["cmp_w2k"], lw["cmp_w2v"],
          lw["gkc"], lw["sm"]]
    return pl.pallas_call(
        _compress_kernel,
        out_shape=[jax.ShapeDtypeStruct((b, nc, LANES), BF16), jax.ShapeDtypeStruct((b, LANES, nc), BF16)],
        grid=(b,),
        in_specs=[zspec] * 2 + [full(w) for w in ws],
        out_specs=[pl.BlockSpec((None, nc, LANES), lambda i: (i, 0, 0)),
                   pl.BlockSpec((None, LANES, nc), lambda i: (i, 0, 0))],
        compiler_params=_cparams(("parallel",)),
        name="nsa_compress",
    )(zk, zv, *ws)


def _nsa_kernel(qs_ref, qrs_ref, gt_ref, kc_ref, vct_ref, cov_ref, ksel_ref, vselt_ref, kwin_ref, vwint_ref,
                o_ref, score_scr, bias_scr, sa_scr, sb_scr, *, nc, ns, kb_keys):
    ib = pl.program_id(1)
    p0 = ib * Q_BLOCK
    nl = NSA_REP * Q_BLOCK
    ng = NSA_KV_HEADS
    t_row = p0 + lax.broadcasted_iota(jnp.int32, (1, nl), 1) % Q_BLOCK
    t_q = p0 + lax.broadcasted_iota(jnp.int32, (1, Q_BLOCK), 1)
    gt = gt_ref[...]
    n_top = min(N_SEL, ns)
    per_kb = kb_keys // SEL_BLOCK
    grows = [slice(g * HEAD_DIM, (g + 1) * HEAD_DIM) for g in range(ng)]

    def q_of(ref, g):
        return ref[NSA_REP * g:NSA_REP * (g + 1)].reshape(nl, LANES)

    ones_rows = 16

    def with_ones(vt):
        return jnp.concatenate([vt, jnp.ones((ones_rows, vt.shape[1]), BF16)], axis=0)

    def online(carry, s, vt):
        m, acc = carry
        m_new = jnp.maximum(m, jnp.max(s, axis=0, keepdims=True))
        p = jnp.exp2(s - m_new)
        acc = jnp.exp2(m - m_new) * acc + _dot(with_ones(vt), p.astype(BF16))
        return m_new, acc

    def normalised(acc):
        return acc[0:HEAD_DIM] * (1.0 / jnp.maximum(acc[HEAD_DIM:HEAD_DIM + 1], 1e-30))

    init1 = (jnp.full((1, nl), NEG_BIG, F32), jnp.zeros((HEAD_DIM + ones_rows, nl), F32))
    init = tuple(init1 for _ in range(ng))

    sj = lax.broadcasted_iota(jnp.int32, (ns, 1), 0)

    def cmp_branch(rows):
        outs = []
        for g in range(ng):
            s = _dot_nt(kc_ref[0:rows, :], q_of(qs_ref, g))
            ci = lax.broadcasted_iota(jnp.int32, (rows, 1), 0)
            cmask = (ci * CMP_STRIDE + (CMP_BLOCK - 1) <= t_row) & (ci < nc - 1)
            s = jnp.where(cmask, s, -jnp.inf)
            m = jnp.max(s, axis=0, keepdims=True)
            m = jnp.where(m == -jnp.inf, 0.0, m)
            e = jnp.exp2(s - m)
            l = jnp.sum(e, axis=0, keepdims=True)
            p = e * (1.0 / jnp.maximum(l, 1e-30))
            outs.append(_dot(vct_ref[grows[g], 0:rows], p.astype(BF16)))
            psum = p[:, 0:Q_BLOCK]
            for r in range(1, NSA_REP):
                psum = psum + p[:, r * Q_BLOCK:(r + 1) * Q_BLOCK]
            imp = _dot(cov_ref[:, 0:rows], psum.astype(BF16))
            bt = t_q // SEL_BLOCK
            allowed = sj * SEL_BLOCK <= t_q
            forced = (sj == 0) | (sj == bt) | (sj == bt - 1)
            score_scr[g] = jnp.where(allowed, imp + jnp.where(forced, np.float32(SEL_BONUS), 0.0), -jnp.inf)
        return tuple(outs)

    half = nc // 2
    if half % LANES == 0:
        o_cmp = lax.cond((p0 + Q_BLOCK - CMP_BLOCK) // CMP_STRIDE < half,
                         lambda: cmp_branch(half), lambda: cmp_branch(nc))
    else:
        o_cmp = cmp_branch(nc)

    sjf = sj.astype(F32)
    bt_q = t_q // SEL_BLOCK
    pre = ((sj == 0) | (sj == bt_q) | (sj == bt_q - 1)) & (sj * SEL_BLOCK <= t_q)
    left = [jnp.where(pre, -jnp.inf, score_scr[g]) for g in range(ng)]
    bias = [jnp.where(pre, 0.0, -jnp.inf) for _ in range(ng)]
    for _ in range(max(n_top - 3, 0)):
        for g in range(ng):
            mx = jnp.max(left[g], axis=0, keepdims=True)
            idx = jnp.min(jnp.where(left[g] == mx, sjf, np.float32(ns)), axis=0, keepdims=True)
            hit = sjf == jnp.where(mx > -jnp.inf, idx, -1.0)
            left[g] = jnp.where(hit, -jnp.inf, left[g])
            bias[g] = jnp.where(hit, 0.0, bias[g])
    for g in range(ng):
        bias_scr[g] = bias[g]

    def sel_qk(kb, g, dst):
        k0 = pl.multiple_of(kb * kb_keys, kb_keys)
        dst[g] = _dot_nt(ksel_ref[pl.ds(k0, kb_keys), :], q_of(qrs_ref, g))

    def sel_step(carry_g, src, kb, g, causal):
        k0 = pl.multiple_of(kb * kb_keys, kb_keys)
        parts = []
        for i in range(per_kb):
            row = bias_scr[g, pl.ds(kb * per_kb + i, 1), :]
            sl = slice(i * SEL_BLOCK, (i + 1) * SEL_BLOCK)
            parts.append(src[g, sl, :] + jnp.concatenate([row] * NSA_REP, axis=1))
        s = jnp.concatenate(parts, axis=0)
        if causal:
            kpos = k0 + lax.broadcasted_iota(jnp.int32, (kb_keys, 1), 0)
            s = jnp.where(kpos <= t_row, s, -jnp.inf)
        return online(carry_g, s, vselt_ref[grows[g], pl.ds(k0, kb_keys)])

    def sel_pair(j, carry, causal):
        a = 2 * j
        for g in range(ng):
            sel_qk(a + 1, g, sb_scr)
        carry = tuple(sel_step(carry[g], sa_scr, a, g, causal) for g in range(ng))
        if not causal:
            for g in range(ng):
                sel_qk(a + 2, g, sa_scr)
        return tuple(sel_step(carry[g], sb_scr, a + 1, g, causal) for g in range(ng))

    n_kb = (p0 + Q_BLOCK + kb_keys - 1) // kb_keys
    n_pairs = (n_kb + 1) // 2
    for g in range(ng):
        sel_qk(0, g, sa_scr)
    carry = lax.fori_loop(0, n_pairs - 1, lambda j, c: sel_pair(j, c, False), init)
    carry = sel_pair(n_pairs - 1, carry, True)
    o_sel = [normalised(carry[g][1]) for g in range(ng)]

    wkeys = WINDOW + Q_BLOCK
    w0 = pl.multiple_of(jnp.maximum(p0 - WINDOW, 0), Q_BLOCK)
    diff = t_row - (w0 + lax.broadcasted_iota(jnp.int32, (wkeys, 1), 0))
    wmask = (diff >= 0) & (diff < WINDOW)
    kblk = kwin_ref[pl.ds(w0, wkeys), :]
    o_win = []
    for g in range(ng):
        s = jnp.where(wmask, _dot_nt(kblk, q_of(qrs_ref, g)), -jnp.inf)
        o_win.append(normalised(online(init1, s, vwint_ref[grows[g], pl.ds(w0, wkeys)])[1]))

    for g in range(ng):
        o_w = o_win[g]

        def gate_row(jb):
            return jnp.concatenate(
                [gt[(NSA_REP * g + r) * 3 + jb:(NSA_REP * g + r) * 3 + jb + 1, :] for r in range(NSA_REP)], axis=1)

        o = o_cmp[g] * gate_row(0) + o_sel[g] * gate_row(1) + o_w * gate_row(2)
        for pr in range(NSA_REP // 2):
            blk = jnp.concatenate([o[:, (2 * pr) * Q_BLOCK:(2 * pr + 1) * Q_BLOCK],
                                   o[:, (2 * pr + 1) * Q_BLOCK:(2 * pr + 2) * Q_BLOCK]], axis=0)
            c0 = g * NSA_REP * HEAD_DIM + pr * LANES
            o_ref[:, c0:c0 + LANES] = blk.T


def _nsa_prompt(qs, qrs, gt, kc, vct, cov, ksel_r, vsel_t, kwin_r, vwin_t, *, batch, seq):
    nb = seq // Q_BLOCK
    nc = kc.shape[1]
    ns = cov.shape[0]
    kb_keys = min(512, seq)
    per_b3 = lambda a: pl.BlockSpec((None,) + a.shape[1:], lambda b, i: (b, 0, 0))
    return pl.pallas_call(
        functools.partial(_nsa_kernel, nc=nc, ns=ns, kb_keys=kb_keys),
        out_shape=jax.ShapeDtypeStruct((batch * seq, NSA_Q), F32),
        grid=(batch, nb),
        in_specs=[
            pl.BlockSpec((None, NSA_HEADS, Q_BLOCK, LANES), lambda b, i: (b * nb + i, 0, 0, 0)),
            pl.BlockSpec((None, NSA_HEADS, Q_BLOCK, LANES), lambda b, i: (b * nb + i, 0, 0, 0)),
            pl.BlockSpec((None, LANES, Q_BLOCK), lambda b, i: (b * nb + i, 0, 0)),
            per_b3(kc), per_b3(vct),
            pl.BlockSpec(cov.shape, lambda b, i: (0, 0)),
            per_b3(ksel_r), per_b3(vsel_t), per_b3(kwin_r), per_b3(vwin_t),
        ],
        out_specs=pl.BlockSpec((Q_BLOCK, NSA_Q), lambda b, i: (b * nb + i, 0)),
        scratch_shapes=[pltpu.VMEM((NSA_KV_HEADS, ns, Q_BLOCK), F32), pltpu.VMEM((NSA_KV_HEADS, ns, Q_BLOCK), F32),
                        pltpu.VMEM((NSA_KV_HEADS, kb_keys, NSA_REP * Q_BLOCK), F32),
                        pltpu.VMEM((NSA_KV_HEADS, kb_keys, NSA_REP * Q_BLOCK), F32)],
        compiler_params=_cparams(("parallel", "arbitrary")),
        name="nsa_prompt",
    )(qs, qrs, gt, kc, vct, cov, ksel_r, vsel_t, kwin_r, vwin_t)


def _gla_rows(c):
    offs, n = [], 0
    for s in range(c):
        t0 = (s // 8) * 8
        offs.append((n, t0))
        n += c - t0
    return offs, n


def _gla_kernel(q_ref, k_ref, la_ref, v_ref, rs_ref, gn_ref, bm_ref, bmask_ref, sm_ref, s0_ref,
                o_ref, sout_ref, s_scr, prod_scr, *, c, nchunks, bpb):
    ci = pl.program_id(1)

    @pl.when(ci == 0)
    def _():
        s_scr[...] = s0_ref[...]
        prod_scr[...] = jnp.zeros_like(prod_scr)

    for bb in range(bpb):
        _gla_one(q_ref.at[bb], k_ref.at[bb], la_ref.at[bb], v_ref.at[bb], rs_ref.at[bb], gn_ref, bm_ref, bmask_ref,
                 sm_ref, o_ref.at[bb], sout_ref.at[bb], s_scr.at[bb], prod_scr.at[bb],
                 c=c, last=ci == nchunks - 1)


def _gla_one(q_ref, k_ref, la_ref, v_ref, rs_ref, gn_ref, bm_ref, bmask_ref, sm_ref, o_ref, sout_ref,
             s_scr, prod_scr, *, c, last):
    q = q_ref[...]
    k = k_ref[...]
    v = v_ref[...]
    la = la_ref[...]
    tt = lax.broadcasted_iota(jnp.int32, (c, 1), 0)
    b = la
    sh = 1
    while sh < c:
        b = b + jnp.where(tt >= sh, pltpu.roll(b, sh, axis=0), 0.0)
        sh *= 2
    state = s_scr[...]
    inter = _dot_nt((q * jnp.exp(b)).astype(BF16), state.astype(BF16))

    offs, npack = _gla_rows(c)
    acc = [inter[tb:min(tb + 8, c)] for tb in range(0, c, 8)]
    gsz = min(16, c)
    for g0 in range(0, c, gsz):
        for s in range(g0, g0 + gsz):
            r0, t0 = offs[s]
            d = b[t0:] - b[s:s + 1]
            e = jnp.exp(jnp.where(tt[t0:] >= s, d, -jnp.inf))
            prod_scr[r0:r0 + c - t0, :] = (q[t0:] * k[s:s + 1] * e).astype(BF16)
        lo = offs[g0][0]
        hi = offs[g0 + gsz][0] if g0 + gsz < c else npack
        res = _dot(prod_scr[lo:hi, :], bm_ref[...])
        for s in range(g0, g0 + gsz):
            r0, t0 = offs[s]
            for tb in range(t0, c, 8):
                rr = r0 - lo + tb - t0
                acc[tb // 8] = acc[tb // 8] + res[rr:rr + min(8, c - tb)] * v[s:s + 1]
    o = acc[0] if len(acc) == 1 else jnp.concatenate(acc, axis=0)

    bl = b[c - 1:c]
    kd = (k * jnp.exp(bl - b)).astype(BF16)
    if c >= 16:
        upd = _dot(v.T.astype(BF16), kd)
    else:
        upd = jnp.dot(v.T, kd.astype(F32), preferred_element_type=F32)
    new_state = jnp.exp(bl) * state + upd * bmask_ref[...]
    s_scr[...] = new_state

    o_ref[...] = _seg_rms(o, gn_ref[...], sm_ref[...]) * rs_ref[...]

    @pl.when(last)
    def _():
        sout_ref[...] = new_state


def _gla(qg, kg, la, vg, rs, s0_bd, lw, *, batch, seq, c, bpb):
    nchunks = seq // c
    _, npack = _gla_rows(c)
    npad = -(-npack // 16) * 16
    blk = lambda w: pl.BlockSpec((bpb, c, w), lambda b, i: (b, i, 0))
    full = lambda a: pl.BlockSpec(a.shape, lambda b, i: (0,) * a.ndim)
    consts = [lw["gla_gn"], lw["gla_bm"], lw["gla_bmask"], lw["sm"]]
    st = pl.BlockSpec((bpb, GLA_V, GLA_QK), lambda b, i: (b, 0, 0))
    return pl.pallas_call(
        functools.partial(_gla_kernel, c=c, nchunks=nchunks, bpb=bpb),
        out_shape=[jax.ShapeDtypeStruct((batch, seq, GLA_V), F32), jax.ShapeDtypeStruct((batch, GLA_V, GLA_QK), F32)],
        grid=(batch // bpb, nchunks),
        in_specs=[blk(GLA_QK), blk(GLA_QK), blk(GLA_QK), blk(GLA_V), blk(GLA_V)] + [full(a) for a in consts] + [st],
        out_specs=[blk(GLA_V), st],
        scratch_shapes=[pltpu.VMEM((bpb, GLA_V, GLA_QK), F32), pltpu.VMEM((bpb, npad, GLA_QK), BF16)],
        compiler_params=_cparams(("parallel", "arbitrary")),
        name="gla",
    )(qg, kg, la, vg, rs, *consts, s0_bd)


def _outproj_kernel(x_ref, oa_ref, ob_ref, u_ref, vn_ref, ws_ref, bias_ref, wo_ref, o_ref, *, tm):
    lane = lax.broadcasted_iota(jnp.int32, (GM_CHUNK, GM_W), 1)
    tri = (lax.broadcasted_iota(jnp.int32, (GM_CHUNK, GM_CHUNK), 0)
           >= lax.broadcasted_iota(jnp.int32, (GM_CHUNK, GM_CHUNK), 1))
    zs = []
    for cb in range(tm // GM_CHUNK):
        vn = vn_ref[cb * GM_CHUNK:(cb + 1) * GM_CHUNK, :]
        z = bias_ref[...]
        for g in range(GM_GROUPS):
            wm = jnp.where(tri, ws_ref[g], 0.0).astype(BF16)
            vg = jnp.where((lane >= g * GM_CH) & (lane < (g + 1) * GM_CH), vn, 0.0).astype(BF16)
            z = z + _dot(wm, vg)
        zs.append(z)
    z = zs[0] if len(zs) == 1 else jnp.concatenate(zs, axis=0)
    oc = u_ref[...] * z
    y = _dot(oa_ref[...].astype(BF16), wo_ref[0:NSA_Q, :])
    y = y + _dot(ob_ref[...].astype(BF16), wo_ref[NSA_Q:NSA_Q + GLA_V, :])
    y = y + _dot(oc.astype(BF16), wo_ref[NSA_Q + GLA_V:MIX_OUT, :])
    o_ref[...] = x_ref[...] + y


def _outproj(x, oa, ob, u, vn, ws, bias, wo, li, *, tm):
    m = x.shape[0]
    row = lambda w: pl.BlockSpec((tm, w), lambda i: (i, 0))
    return pl.pallas_call(
        functools.partial(_outproj_kernel, tm=tm),
        out_shape=jax.ShapeDtypeStruct((m, D_MODEL), F32),
        grid=(m // tm,),
        in_specs=[row(D_MODEL), row(NSA_Q), row(GLA_V), row(GM_W), row(GM_W),
                  pl.BlockSpec(ws.shape, lambda i: (0, 0, 0)), pl.BlockSpec(bias.shape, lambda i: (0, 0)),
                  pl.BlockSpec((None, MIX_OUT, D_MODEL), lambda i: (li, 0, 0))],
        out_specs=row(D_MODEL),
        compiler_params=_cparams(("parallel",)),
        name="outproj",
    )(x, oa, ob, u, vn, ws, bias, wo)


def _pcompress_kernel(pt_ref, *refs, npg):
    pages = refs[:npg]
    perm_ref, wk_ref, wv_ref, out_ref, zk_scr, zv_scr = refs[npg:npg + 6]
    perm = perm_ref[...]
    for pp in range(npg // 2):
        xt = jnp.concatenate([pages[2 * pp][...], pages[2 * pp + 1][...]], axis=1).astype(BF16)
        y = _dot_nt(perm, xt).astype(BF16)
        for s in range(CMP_STRIDE):
            rows = slice(pp * 16, (pp + 1) * 16)
            zk_scr[rows, s * LANES:(s + 1) * LANES] = y[s * 16:(s + 1) * 16, 0:LANES]
            zv_scr[rows, s * LANES:(s + 1) * LANES] = y[s * 16:(s + 1) * 16, LANES:2 * LANES]
    q = out_ref.shape[1] // 4
    rk = _dot(zk_scr[...], wk_ref[...])
    rv = _dot(zv_scr[...], wv_ref[...])
    out_ref[:, 0:q] = rk[:, 0:q]
    out_ref[:, q:2 * q] = rv[:, 0:q]
    out_ref[:, 2 * q:3 * q] = rk[:, q:2 * q]
    out_ref[:, 3 * q:4 * q] = rv[:, q:2 * q]


def _pcompress(cache_t, page_table, li, perm, wk, wv, *, npg):
    bs, n_pages = page_table.shape
    page = cache_t.shape[-1]
    nc = n_pages * page // CMP_STRIDE
    steps = n_pages // npg
    cpp = page // CMP_STRIDE

    def page_spec(k):
        return pl.BlockSpec((None, None, 2 * LANES, page), lambda b, h, pt: (li, pt[b, h * npg + k], 0, 0))

    grid_spec = pltpu.PrefetchScalarGridSpec(
        num_scalar_prefetch=1,
        grid=(bs, steps),
        in_specs=[page_spec(k) for k in range(npg)] + [
            pl.BlockSpec(perm.shape, lambda b, h, pt: (0, 0)), pl.BlockSpec(wk.shape, lambda b, h, pt: (0, 0)),
            pl.BlockSpec(wv.shape, lambda b, h, pt: (0, 0))],
        out_specs=pl.BlockSpec((None, npg * cpp, 2 * wk.shape[1]), lambda b, h, pt: (b, h, 0)),
        scratch_shapes=[pltpu.VMEM((npg * cpp, wk.shape[0]), BF16)] * 2,
    )
    return pl.pallas_call(
        functools.partial(_pcompress_kernel, npg=npg),
        out_shape=jax.ShapeDtypeStruct((bs, nc, 2 * wk.shape[1]), F32),
        grid_spec=grid_spec,
        compiler_params=_cparams(("parallel", "arbitrary")),
        name="nsa_page_compress",
    )(page_table, *([cache_t] * npg), perm, wk, wv)


def _ctail_kernel(a_ref, b_ref, bias_ref, w2_ref, gkc_ref, sm_ref, kct_ref, vcr_ref):
    h = a_ref[...] + _next_row(b_ref[...]) + bias_ref[...]
    out = _dot(_gelu(h).astype(BF16), w2_ref[...])
    kc = _seg_rms(out[:, 0:LANES], gkc_ref[...], sm_ref[...])
    kct_ref[...] = kc.T.astype(BF16)
    vcr_ref[...] = out[:, LANES:2 * LANES].astype(BF16)


def _ctail(hid_ab, bias, w2, gkc, sm):
    bs, nc, w2x = hid_ab.shape
    half = w2x // 2
    full = lambda a: pl.BlockSpec(a.shape, lambda b: (0,) * a.ndim)
    return pl.pallas_call(
        _ctail_kernel,
        out_shape=[jax.ShapeDtypeStruct((bs, LANES, nc), BF16), jax.ShapeDtypeStruct((bs, nc, LANES), BF16)],
        grid=(bs,),
        in_specs=[pl.BlockSpec((None, nc, half), lambda b: (b, 0, 0)), pl.BlockSpec((None, nc, half), lambda b: (b, 0, 1)),
                  full(bias), full(w2), full(gkc), full(sm)],
        out_specs=[pl.BlockSpec((None, LANES, nc), lambda b: (b, 0, 0)), pl.BlockSpec((None, nc, LANES), lambda b: (b, 0, 0))],
        compiler_params=_cparams(("parallel",)),
        name="nsa_compress_tail",
    )(hid_ab, hid_ab, bias, w2, gkc, sm)


def _nsa_sample_kernel(pt_ref, qp_ref, qrp_ref, gcol_ref, kct_ref, vcr_ref, cov_ref, e_ref, *refs,
                       npg, nsteps, nc, ns_tot, past_len, ts):
    pages = refs[:npg]
    (wint_ref, newk_ref, neww_ref, newwsh_ref, o_ref, winout_ref,
     bias_scr, m_scr, l_scr, acc_scr, oc_scr, ow_scr) = refs[npg:]
    step = pl.program_id(1)
    nrow = NSA_KV_HEADS * NSA_REP * 8
    q8 = lax.broadcasted_iota(jnp.int32, (nrow, 1), 0) % 8
    t_row = past_len + q8
    qrp = qrp_ref[...]
    n_keep = wint_ref.shape[-1]
    n_top = min(N_SEL, ns_tot)
    nsp = bias_scr.shape[1]

    def expand_rows(a):
        return jnp.concatenate([a[0:8]] * NSA_REP + [a[8:16]] * NSA_REP, axis=0)

    @pl.when(step == 0)
    def _():
        s = _dot(qp_ref[...], kct_ref[...])
        ci = lax.broadcasted_iota(jnp.int32, (1, nc), 1)
        cmask = (ci * CMP_STRIDE + (CMP_BLOCK - 1) <= t_row) & (ci < nc - 1)
        s = jnp.where(cmask, s, -jnp.inf)
        m = jnp.max(s, axis=1, keepdims=True)
        m = jnp.where(m == -jnp.inf, 0.0, m)
        e = jnp.exp(s - m)
        p = e * (1.0 / jnp.maximum(jnp.sum(e, axis=1, keepdims=True), 1e-30))
        oc_scr[...] = _dot(p.astype(BF16), vcr_ref[...])
        sjl = lax.broadcasted_iota(jnp.int32, (1, nsp), 1)
        tq = past_len + lax.broadcasted_iota(jnp.int32, (8, 1), 0)
        bt = tq // SEL_BLOCK
        allowed = (sjl * SEL_BLOCK <= tq) & (sjl < ns_tot)
        forced = (sjl == 0) | (sjl == bt) | (sjl == bt - 1)
        for g in range(NSA_KV_HEADS):
            base = g * NSA_REP * 8
            psum = p[base:base + 8]
            for r in range(1, NSA_REP):
                psum = psum + p[base + r * 8:base + (r + 1) * 8]
            imp = _dot(psum.astype(BF16), cov_ref[...])
            score = jnp.where(allowed, imp + jnp.where(forced, np.float32(SEL_BONUS), 0.0), -jnp.inf)
            rank = jnp.zeros((8, nsp), jnp.int32)
            for k in range(ns_tot):
                col = score[:, k:k + 1]
                later = jnp.where(sjl > k, 1, 0)
                rank = rank + jnp.where(col > score, 1, 0) + jnp.where(col == score, later, 0)
            keep = (rank < n_top) & (score > -jnp.inf)
            bias_scr[g * 8:(g + 1) * 8, :] = jnp.where(keep, 0.0, -jnp.inf)
        m_scr[...] = jnp.full(m_scr.shape, NEG_BIG, F32)
        l_scr[...] = jnp.zeros(l_scr.shape, F32)
        acc_scr[...] = jnp.zeros(acc_scr.shape, F32)

        lane_w = lax.broadcasted_iota(jnp.int32, (1, n_keep), 1)
        diff = t_row - (past_len - n_keep + lane_w)
        s_w = jnp.where((diff >= 0) & (diff < WINDOW), _dot(qrp, wint_ref[0:LANES, :].astype(BF16)), -jnp.inf)
        lane_n = lax.broadcasted_iota(jnp.int32, (1, LANES), 1)
        diff_n = t_row - (past_len + lane_n)
        s_n = jnp.where((lane_n < ts) & (diff_n >= 0) & (diff_n < WINDOW),
                        _dot(qrp, neww_ref[0:LANES, :].astype(BF16)), -jnp.inf)
        sw = jnp.concatenate([s_w, s_n], axis=1)
        mw = jnp.max(sw, axis=1, keepdims=True)
        mw = jnp.where(mw == -jnp.inf, 0.0, mw)
        ew = jnp.exp(sw - mw)
        pw = ew * (1.0 / jnp.maximum(jnp.sum(ew, axis=1, keepdims=True), 1e-30))
        ow_scr[...] = (_dot_nt(pw[:, 0:n_keep].astype(BF16), wint_ref[LANES:2 * LANES, :].astype(BF16))
                       + _dot_nt(pw[:, n_keep:].astype(BF16), neww_ref[LANES:2 * LANES, :].astype(BF16)))
        rolled = pltpu.roll(wint_ref[...], n_keep - ts, axis=1)
        lane_o = lax.broadcasted_iota(jnp.int32, (2 * LANES, LANES), 1)
        winout_ref[:, 0:n_keep - LANES] = rolled[:, 0:n_keep - LANES]
        winout_ref[:, n_keep - LANES:n_keep] = jnp.where(lane_o >= LANES - ts, newwsh_ref[...],
                                                         rolled[:, n_keep - LANES:n_keep])

    def online(s, vt):
        m_old = m_scr[:, 0:1]
        m_new = jnp.maximum(m_old, jnp.max(s, axis=1, keepdims=True))
        p = jnp.exp(s - m_new)
        alpha = jnp.exp(m_old - m_new)
        l_new = alpha * l_scr[:, 0:1] + jnp.sum(p, axis=1, keepdims=True)
        acc_scr[...] = alpha * acc_scr[...] + _dot_nt(p.astype(BF16), vt)
        m_scr[...] = jnp.broadcast_to(m_new, m_scr.shape)
        l_scr[...] = jnp.broadcast_to(l_new, l_scr.shape)

    sel01 = jnp.where(bias_scr[...] == 0.0, 1.0, 0.0).astype(BF16)
    bexp = _dot(sel01, e_ref[...])
    bias = expand_rows(jnp.where(bexp > 0.5, 0.0, -jnp.inf))
    kt = jnp.concatenate([pg[0:LANES, :] for pg in pages], axis=1).astype(BF16)
    vt = jnp.concatenate([pg[LANES:2 * LANES, :] for pg in pages], axis=1).astype(BF16)
    online(_dot(qrp, kt) + bias, vt)

    @pl.when(step == nsteps - 1)
    def _():
        lane_n = lax.broadcasted_iota(jnp.int32, (1, LANES), 1)
        bcol = expand_rows(bias_scr[:, ns_tot - 1:ns_tot])
        ok = (lane_n < ts) & (past_len + lane_n <= t_row)
        s_n = jnp.where(ok, _dot(qrp, newk_ref[2 * LANES:3 * LANES, :].astype(BF16)) + bcol, -jnp.inf)
        online(s_n, newk_ref[3 * LANES:4 * LANES, :].astype(BF16))
        o_s = acc_scr[...] * (1.0 / jnp.maximum(l_scr[:, 0:1], 1e-30))
        gc = gcol_ref[...]
        o_ref[...] = oc_scr[...] * gc[:, 0:1] + o_s * gc[:, 1:2] + ow_scr[...] * gc[:, 2:3]


def _nsa_sample(cache_t, page_table, li, qp, qrp, gcol, kct, vcr, cov_s, emat, win_t, newk_t, neww_t, newwsh_t, *,
                npg, past_len, ts):
    bs, n_pages = page_table.shape
    page = cache_t.shape[-1]
    nsteps = n_pages // npg
    nc = kct.shape[-1]
    ns_tot = -(-(past_len + ts) // SEL_BLOCK)
    nsp = cov_s.shape[1]
    n_keep = win_t.shape[-1]
    nrow = qp.shape[1]
    per_b = lambda a: pl.BlockSpec((None,) + a.shape[1:], lambda b, h, pt: (b,) + (0,) * (a.ndim - 1))

    def page_spec(k):
        return pl.BlockSpec((None, None, 2 * LANES, page), lambda b, h, pt: (li, pt[b, h * npg + k], 1, 0))

    grid_spec = pltpu.PrefetchScalarGridSpec(
        num_scalar_prefetch=1,
        grid=(bs, nsteps),
        in_specs=[per_b(qp), per_b(qrp), per_b(gcol), per_b(kct), per_b(vcr),
                  pl.BlockSpec(cov_s.shape, lambda b, h, pt: (0, 0)),
                  pl.BlockSpec((nsp, npg * page), lambda b, h, pt: (0, h))]
                 + [page_spec(k) for k in range(npg)]
                 + [pl.BlockSpec((None, None, 2 * LANES, n_keep), lambda b, h, pt: (li, b, 0, 0)),
                    per_b(newk_t), per_b(neww_t), per_b(newwsh_t)],
        out_specs=[pl.BlockSpec((None, nrow, LANES), lambda b, h, pt: (b, 0, 0)),
                   pl.BlockSpec((None, 2 * LANES, n_keep), lambda b, h, pt: (b, 0, 0))],
        scratch_shapes=[pltpu.VMEM((2 * 8, nsp), F32)] + [pltpu.VMEM((nrow, LANES), F32)] * 5,
    )
    return pl.pallas_call(
        functools.partial(_nsa_sample_kernel, npg=npg, nsteps=nsteps, nc=nc, ns_tot=ns_tot, past_len=past_len, ts=ts),
        out_shape=[jax.ShapeDtypeStruct((bs, nrow, LANES), F32), jax.ShapeDtypeStruct((bs, 2 * LANES, n_keep), F32)],
        grid_spec=grid_spec,
        compiler_params=_cparams(("parallel", "arbitrary")),
        name="nsa_sample",
    )(page_table, qp, qrp, gcol, kct, vcr, cov_s, emat, *([cache_t] * npg), win_t, newk_t, neww_t, newwsh_t)


def _pad_cols(w):
    cuts = np.cumsum((0,) + IN_SPLITS)
    parts = []
    for i, (n, p) in enumerate(zip(IN_SPLITS, IN_PADDED)):
        seg = w[..., cuts[i]:cuts[i] + n]
        if p != n:
            seg = jnp.pad(seg, [(0, 0)] * (w.ndim - 1) + [(0, p - n)])
        parts.append(seg)
    return jnp.concatenate(parts, axis=-1)


def _rope_tables(pos):
    half = HEAD_DIM // 2
    inv = 1.0 / (ROPE_THETA ** (jnp.arange(half, dtype=F32) * (2.0 / HEAD_DIM)))
    ang = pos.astype(F32)[:, None] * inv[None, :]
    cos = jnp.cos(ang)
    sin = jnp.sin(ang)
    cos_f = jnp.concatenate([cos, cos, cos, cos], axis=1)
    sin_f = jnp.concatenate([-sin, sin, -sin, sin], axis=1)
    return cos_f, sin_f


def _cover_t(seq):
    nc = seq // CMP_STRIDE
    ns = seq // SEL_BLOCK
    ci = np.arange(nc)[None, :]
    sj = np.arange(ns)[:, None]
    cov = ((ci * CMP_STRIDE <= sj * SEL_BLOCK + SEL_BLOCK - 1)
           & (ci * CMP_STRIDE + CMP_BLOCK - 1 >= sj * SEL_BLOCK) & (ci < nc - 1))
    return jnp.asarray(cov, dtype=BF16)


def _layer_weights(l, ln_gains, w_in_p, nsa_qk_norm, nsa_cmp_pos, nsa_cmp_w1, nsa_cmp_w2, gla_gate_w, gla_gate_b,
                   gla_norm, gm_ln, gm_ws, gm_b):
    eye2 = jnp.eye(NSA_KV_HEADS, dtype=F32)
    lw = {"ln0": ln_gains[l, 0][None], "ln1": ln_gains[l, 1][None], "ln2": ln_gains[l, 2][None], "w_in": w_in_p[l]}
    lw["gq"] = jnp.tile(nsa_qk_norm[l, 0], NSA_HEADS)[None]
    lw["gkc"] = jnp.tile(nsa_qk_norm[l, 1], NSA_KV_HEADS)[None]
    lw["gks"] = jnp.tile(nsa_qk_norm[l, 2], NSA_KV_HEADS)[None]
    lw["gkw"] = jnp.tile(nsa_qk_norm[l, 3], NSA_KV_HEADS)[None]
    seg = (np.arange(LANES)[:, None] // HEAD_DIM) == (np.arange(LANES)[None, :] // HEAD_DIM)
    lw["sm"] = jnp.asarray(seg * (1.0 / HEAD_DIM), dtype=BF16)
    lw["gla_gw"] = jnp.pad(gla_gate_w[l], ((0, LANES - GLA_GATE_RANK), (0, 0))).astype(BF16)
    lw["gla_gb"] = gla_gate_b[l][None]
    lw["gm_lng"] = gm_ln[l, 0][None]
    lw["gm_lnb"] = gm_ln[l, 1][None]
    pos_rows = []
    for c, nm in ((0, "k"), (1, "v")):
        w1 = nsa_cmp_w1[l, c].reshape(CMP_BLOCK, HEAD_DIM, CMP_HIDDEN)
        for half, tag in ((w1[:CMP_STRIDE], "a"), (w1[CMP_STRIDE:], "b")):
            wx = jnp.einsum("sdh,pg->spdgh", half, eye2)
            lw["cmp_w" + tag + nm] = wx.reshape(CMP_STRIDE * LANES, NSA_KV_HEADS * CMP_HIDDEN).astype(BF16)
        lw["cmp_w2" + nm] = jnp.einsum("hd,pg->phgd", nsa_cmp_w2[l, c], eye2).reshape(
            NSA_KV_HEADS * CMP_HIDDEN, LANES).astype(BF16)
        pe = nsa_cmp_pos[l, c]
        for half in (pe[:CMP_STRIDE], pe[CMP_STRIDE:]):
            pos_rows.append(jnp.broadcast_to(half[:, None, :], (CMP_STRIDE, NSA_KV_HEADS, HEAD_DIM)).reshape(-1))
    lw["cmp_pos"] = jnp.stack(pos_rows)
    w1ab = nsa_cmp_w1[l].reshape(2, 2, CMP_STRIDE, HEAD_DIM, CMP_HIDDEN)
    for c, nm in ((0, "k"), (1, "v")):
        lw["pc_w" + nm] = jnp.einsum("asdh,gy->sgdayh", w1ab[c], eye2).reshape(
            CMP_STRIDE * LANES, 2 * NSA_KV_HEADS * CMP_HIDDEN).astype(BF16)
    pb = jnp.einsum("ck,ckh->ch", nsa_cmp_pos[l].reshape(2, -1), nsa_cmp_w1[l], precision=lax.Precision.HIGHEST)
    lw["pc_bias"] = jnp.broadcast_to(pb[:, None, :], (2, NSA_KV_HEADS, CMP_HIDDEN)).reshape(1, -1)
    lw["pc_w2"] = jnp.einsum("chd,cx,gy->cghxyd", nsa_cmp_w2[l], eye2, eye2).reshape(
        2 * NSA_KV_HEADS * CMP_HIDDEN, 2 * LANES).astype(BF16)
    lw["gla_gn"] = jnp.tile(gla_norm[l], GLA_HEADS)[None]
    hq = np.arange(GLA_QK) // GLA_DK
    hv = np.arange(GLA_V) // GLA_DV
    lw["gla_bm"] = jnp.asarray(hq[:, None] == hv[None, :], dtype=BF16)
    lw["gla_bmask"] = jnp.asarray(hv[:, None] == hq[None, :], dtype=F32)
    lw["gm_ws"] = gm_ws[l]
    lw["gm_bias"] = jnp.repeat(gm_b[l].T, GM_CH, axis=1)
    return lw


def _page_perm(page):
    cpp = page // CMP_STRIDE
    r = np.arange(2 * page)
    s_, rem = r // (2 * cpp), r % (2 * cpp)
    t = (rem // cpp) * page + CMP_STRIDE * (rem % cpp) + s_
    m = np.zeros((2 * page, 2 * page), np.float32)
    m[r, t] = 1.0
    return jnp.asarray(m, dtype=BF16)


def _cover_sample(past_len, ts, nsp):
    t_tot = past_len + ts
    n_c = (t_tot - CMP_BLOCK) // CMP_STRIDE + 1
    nc = past_len // CMP_STRIDE
    n_s = -(-t_tot // SEL_BLOCK)
    ci = np.arange(nc)[:, None]
    sj = np.arange(nsp)[None, :]
    cov = ((ci * CMP_STRIDE <= sj * SEL_BLOCK + SEL_BLOCK - 1) & (ci * CMP_STRIDE + CMP_BLOCK - 1 >= sj * SEL_BLOCK)
           & (ci < n_c) & (sj < n_s))
    emat = (np.arange(past_len)[None, :] // SEL_BLOCK) == np.arange(nsp)[:, None]
    return jnp.asarray(cov, dtype=BF16), jnp.asarray(emat, dtype=BF16)


def _rows_gr8(a, ts):
    bs = a.shape[0]
    a = a.reshape(bs, ts, NSA_KV_HEADS, NSA_REP, a.shape[-1]).transpose(0, 2, 3, 1, 4)
    return jnp.pad(a, ((0, 0), (0, 0), (0, 0), (0, 8 - ts), (0, 0)))


def _state_to_t(s):
    b = s.shape[0]
    eye = jnp.eye(GLA_HEADS, dtype=s.dtype)
    return jnp.einsum("bhkv,hg->bhvgk", s, eye).reshape(b, GLA_V, GLA_QK)


def _state_from_t(st):
    b = st.shape[0]
    s5 = st.reshape(b, GLA_HEADS, GLA_DV, GLA_HEADS, GLA_DK)
    d = jnp.stack([s5[:, h, :, h, :] for h in range(GLA_HEADS)], axis=1)
    return jnp.swapaxes(d, 2, 3)


def kernel(x_prompt, x_sample, cache_kv, cache_win_kv, state_gla, page_table, ln_gains, ffn_w_gate_up, ffn_w_down,
           w_in, w_out, nsa_qk_norm, nsa_cmp_pos, nsa_cmp_w1, nsa_cmp_w2, gla_gate_w, gla_gate_b, gla_norm, gm_ln,
           gm_ws, gm_b):
    depth = w_in.shape[0]
    bp, tp, _ = x_prompt.shape
    bs, ts, _ = x_sample.shape
    n_pages = page_table.shape[1]
    page = cache_kv.shape[2]
    past_len = n_pages * page
    tq = 8
    mp, ms = bp * tp, bs * tq
    nc = tp // CMP_STRIDE

    w_gu = ffn_w_gate_up.astype(BF16).reshape(depth * 2, D_MODEL, 2 * D_FF)
    w_d = ffn_w_down.astype(BF16).reshape(depth * 2, D_FF, D_MODEL)
    w_in_p = _pad_cols(w_in).astype(BF16)
    w_o = w_out.astype(BF16)
    cos_p, sin_p = _rope_tables(jnp.arange(tp))
    cos_s, sin_s = _rope_tables(past_len + jnp.arange(ms) % tq)
    cov = _cover_t(tp)
    tm_p = 512 if mp % 512 == 0 else Q_BLOCK
    tm_f = 1024 if mp % 1024 == 0 else tm_p
    tm_i = 512 if tp % 512 == 0 else Q_BLOCK
    tf = 1408

    nrow = NSA_HEADS * 8
    n_keep_s = cache_win_kv.shape[2]
    nsp = -(-(past_len // SEL_BLOCK + 1) // LANES) * LANES
    cov_s, emat = _cover_sample(past_len, ts, nsp)
    perm = _page_perm(page)
    eye2 = jnp.eye(NSA_KV_HEADS, dtype=F32)
    npg_c = min(32, n_pages)
    npg_a = min(32, n_pages)
    cache_t = jnp.transpose(cache_kv, (0, 1, 3, 4, 5, 2)).reshape(depth, cache_kv.shape[1], 4 * LANES, page)
    win_t = jnp.transpose(cache_win_kv, (0, 1, 3, 4, 5, 2)).reshape(depth, bs, 2 * LANES, n_keep_s)

    xp = x_prompt.reshape(mp, D_MODEL)
    xs = jnp.pad(x_sample, ((0, 0), (0, tq - ts), (0, 0))).reshape(ms, D_MODEL)
    real_row = (jnp.arange(tq) < ts).astype(F32)[None, :, None]
    kv_p, win_p, gla_p, kv_s, win_s, gla_s, gmv_s = [], [], [], [], [], [], []
    eye_b = jnp.eye(GM_CHUNK // tq, dtype=F32)
    for l in range(depth):
        lw = _layer_weights(l, ln_gains, w_in_p, nsa_qk_norm, nsa_cmp_pos, nsa_cmp_w1, nsa_cmp_w2, gla_gate_w,
                            gla_gate_b, gla_norm, gm_ln, gm_ws, gm_b)
        xp = _ffn(xp, lw["ln0"], w_gu, w_d, 2 * l, tm=tm_f, tf=tf)
        (newkv, newwin, _, qg, kg, la, vg, rs, u, vn, qs, qrs, gt, ksel_r, vsel_t, kwin_r, vwin_t, kcmp,
         vcmp) = _inproj(xp, lw, cos_p, sin_p, tm=tm_i, attn_layout=True, batch=bp, seq=tp)
        zk = kcmp.reshape(bp, nc, CMP_STRIDE * LANES)
        zv = vcmp.reshape(bp, nc, CMP_STRIDE * LANES)
        kc, vct = _compress(zk, zv, lw)
        oa = _nsa_prompt(qs, qrs, gt, kc, vct, cov, ksel_r.reshape(bp, tp, LANES), vsel_t,
                         kwin_r.reshape(bp, tp, LANES), vwin_t, batch=bp, seq=tp)
        r3 = lambda a: a.reshape(bp, tp, a.shape[-1])
        ob, st = _gla(r3(qg), r3(kg), r3(la), r3(vg), r3(rs), jnp.zeros((bp, GLA_V, GLA_QK), F32), lw,
                      batch=bp, seq=tp, c=GLA_CHUNK, bpb=1)
        xp = _outproj(xp, oa, ob.reshape(mp, GLA_V), u, vn, lw["gm_ws"], lw["gm_bias"], w_o, l, tm=tm_p)
        xp = _ffn(xp, lw["ln2"], w_gu, w_d, 2 * l + 1, tm=tm_f, tf=tf)
        kv_p.append(newkv.reshape(bp, 4, NSA_KV_HEADS, HEAD_DIM, tp).transpose(0, 4, 1, 2, 3))
        n_keep = min(WINDOW, tp)
        win_p.append(newwin[:, :, tp - n_keep:].reshape(bp, 2, NSA_KV_HEADS, HEAD_DIM, n_keep).transpose(0, 4, 1, 2, 3))
        gla_p.append(_state_from_t(st))

        xs = _ffn(xs, lw["ln0"], w_gu, w_d, 2 * l, tm=ms, tf=tf)
        (newkv, newwin, gates, qg, kg, la, vg, rs, u, vn, qn, qr) = _inproj(
            xs, lw, cos_s, sin_s, tm=ms, attn_layout=False, batch=bs, seq=ts)
        hid = _pcompress(cache_t, page_table, l, perm, lw["pc_wk"], lw["pc_wv"], npg=npg_c)
        kct, vcr = _ctail(hid, lw["pc_bias"], lw["pc_w2"], lw["gkc"], lw["sm"])
        scale = np.float32(HEAD_DIM ** -0.5)
        to_pad = lambda q: jnp.einsum("bgrqd,gx->bgrqxd", _rows_gr8(q.reshape(bs, tq, NSA_HEADS, HEAD_DIM) * scale, tq),
                                      eye2).reshape(bs, nrow, LANES).astype(BF16)
        gcol = _rows_gr8(gates[:, :3 * NSA_HEADS].reshape(bs, tq, NSA_HEADS, 3), tq).reshape(bs, nrow, 3)
        gcol = jnp.pad(gcol, ((0, 0), (0, 0), (0, LANES - 3)))
        newk_t = jnp.pad(newkv.reshape(bs, tq, -1).transpose(0, 2, 1), ((0, 0), (0, 0), (0, LANES - tq)))
        neww_c = newwin.reshape(bs, tq, -1).transpose(0, 2, 1)[:, :, :ts]
        neww_t = jnp.pad(neww_c, ((0, 0), (0, 0), (0, LANES - ts)))
        newwsh_t = jnp.pad(neww_c, ((0, 0), (0, 0), (LANES - ts, 0)))
        o64, win_o = _nsa_sample(cache_t, page_table, l, to_pad(qn), to_pad(qr), gcol, kct, vcr, cov_s, emat, win_t,
                                 newk_t, neww_t, newwsh_t, npg=npg_a, past_len=past_len, ts=ts)
        o6 = o64.reshape(bs, NSA_KV_HEADS, NSA_REP, tq, NSA_KV_HEADS, HEAD_DIM)
        oa = jnp.stack([o6[:, g, :, :, g, :] for g in range(NSA_KV_HEADS)], axis=1)
        oa = (oa.transpose(0, 3, 1, 2, 4).reshape(bs, tq, NSA_Q) * real_row).reshape(ms, NSA_Q)
        r8 = lambda a: a.reshape(bs, tq, a.shape[-1])
        ob8, st_s = _gla(r8(qg), r8(kg), r8(la) * real_row, r8(vg), r8(rs), _state_to_t(state_gla[l]), lw,
                         batch=bs, seq=tq, c=tq, bpb=4 if bs % 4 == 0 else 1)
        ob = ob8.reshape(ms, GLA_V)
        ws_s = jnp.einsum("gts,bc->gbtcs", gm_ws[l][:, :tq, :tq], eye_b).reshape(GM_GROUPS, GM_CHUNK, GM_CHUNK)
        bias_s = jnp.tile(lw["gm_bias"][:tq], (GM_CHUNK // tq, 1))
        xs = _outproj(xs, oa, ob, u, vn, ws_s, bias_s, w_o, l, tm=GM_CHUNK)
        xs = _ffn(xs, lw["ln2"], w_gu, w_d, 2 * l + 1, tm=ms, tf=tf)
        kv_s.append(newkv.reshape(bs, tq, 4, NSA_KV_HEADS, HEAD_DIM)[:, :ts])
        win_s.append(win_o.reshape(bs, 2, NSA_KV_HEADS, HEAD_DIM, n_keep_s).transpose(0, 4, 1, 2, 3))
        gla_s.append(_state_from_t(st_s))
        gmv_s.append(vn.reshape(bs, tq, GM_GROUPS, GM_CH)[:, :ts])
    return (xp.reshape(bp, tp, D_MODEL), xs.reshape(bs, tq, D_MODEL)[:, :ts], jnp.stack(kv_p), jnp.stack(win_p),
            jnp.stack(gla_p), jnp.stack(kv_s), jnp.stack(win_s), jnp.stack(gla_s), jnp.stack(gmv_s))
```

```python
import functools

import numpy as np
import jax
import jax.numpy as jnp
from jax import lax
from jax.experimental import pallas as pl
from jax.experimental.pallas import tpu as pltpu

F32 = jnp.float32
BF16 = jnp.bfloat16

D_MODEL = 1024
HEAD_DIM = 64
NSA_HEADS = 8
NSA_KV_HEADS = 2
NSA_REP = NSA_HEADS // NSA_KV_HEADS
CMP_STRIDE = 16
CMP_BLOCK = 2 * CMP_STRIDE
CMP_HIDDEN = 128
SEL_BLOCK = 64
N_SEL = 16
WINDOW = 512
Q_BLOCK = 128
SEL_BONUS = 1.0e4
GLA_HEADS = 4
GLA_DK = 32
GLA_DV = 64
GLA_GATE_RANK = 16
GLA_GATE_TEMP = 16.0
GLA_CHUNK = 64
GM_GROUPS = 4
GM_CH = 64
GM_CHUNK = 128
D_FF = 2816
ROPE_THETA = 10000.0
EPS = 1e-6

NSA_Q = NSA_HEADS * HEAD_DIM
NSA_KV = NSA_KV_HEADS * HEAD_DIM
GLA_QK = GLA_HEADS * GLA_DK
GLA_V = GLA_HEADS * GLA_DV
GM_W = GM_GROUPS * GM_CH
MIX_OUT = NSA_Q + GLA_V + GM_W
IN_SPLITS = (NSA_Q, 6 * NSA_KV, 3 * NSA_HEADS, GLA_QK, GLA_QK, GLA_V, GLA_GATE_RANK, GLA_V, GM_W, GM_W)
IN_PADDED = tuple(-(-s // 128) * 128 for s in IN_SPLITS)
IN_OFFS = tuple(int(v) for v in np.cumsum((0,) + IN_PADDED))
D_IN_PAD = IN_OFFS[-1]

LANES = 128
NEG_BIG = -1.0e30
VMEM_LIMIT = 56 * 1024 * 1024


def _cparams(sem):
    return pltpu.CompilerParams(dimension_semantics=sem, vmem_limit_bytes=VMEM_LIMIT)


def _gelu(x):
    c = np.float32(np.sqrt(2.0 / np.pi))
    return x * (0.5 * (1.0 + jnp.tanh(c * (x + 0.044715 * (x * x * x)))))


def _sigmoid(x):
    return 1.0 / (1.0 + jnp.exp(-x))


def _dot(a, b):
    return jnp.dot(a, b, preferred_element_type=F32)


def _dot_nt(a, b):
    return lax.dot_general(a, b, (((1,), (1,)), ((), ())), preferred_element_type=F32)


def _seg_mean_sq(x, sm):
    sq = x * x
    hi = sq.astype(BF16)
    lo = (sq - hi.astype(F32)).astype(BF16)
    outs = []
    for c in range(x.shape[1] // LANES):
        sl = slice(c * LANES, (c + 1) * LANES)
        outs.append(_dot(hi[:, sl], sm) + _dot(lo[:, sl], sm))
    return outs[0] if len(outs) == 1 else jnp.concatenate(outs, axis=1)


def _seg_rms(x, gain, sm):
    return x * lax.rsqrt(_seg_mean_sq(x, sm) + EPS) * gain


def _tile_lanes(a, w):
    n = w // a.shape[1]
    return a if n == 1 else jnp.concatenate([a] * n, axis=1)


def _rope(x, cos, sin_signed):
    w = x.shape[1]
    lane = lax.broadcasted_iota(jnp.int32, x.shape, 1)
    fwd = pltpu.roll(x, w - HEAD_DIM // 2, axis=1)
    bwd = pltpu.roll(x, HEAD_DIM // 2, axis=1)
    partner = jnp.where((lane % HEAD_DIM) < HEAD_DIM // 2, fwd, bwd)
    return x * _tile_lanes(cos, w) + partner * _tile_lanes(sin_signed, w)


def _ffn_kernel(x_ref, g_ref, wg_ref, wu_ref, wd_ref, o_ref, h_scr, acc_scr, *, nj):
    j = pl.program_id(1)

    @pl.when(j == 0)
    def _():
        x = x_ref[...]
        ms = jnp.mean(x * x, axis=-1, keepdims=True)
        h_scr[...] = (x * lax.rsqrt(ms + EPS) * g_ref[...]).astype(BF16)
        acc_scr[...] = jnp.zeros_like(acc_scr)

    h = h_scr[...]
    g = _dot(h, wg_ref[...])
    u = _dot(h, wu_ref[...])
    a = (g * _sigmoid(g)) * u
    acc_scr[...] += _dot(a.astype(BF16), wd_ref[...])

    @pl.when(j == nj - 1)
    def _():
        o_ref[...] = x_ref[...] + 0.5 * acc_scr[...]


def _ffn(x, gain, w_gu, w_d, li, *, tm, tf):
    m = x.shape[0]
    nj = D_FF // tf
    return pl.pallas_call(
        functools.partial(_ffn_kernel, nj=nj),
        out_shape=jax.ShapeDtypeStruct((m, D_MODEL), F32),
        grid=(m // tm, nj),
        in_specs=[
            pl.BlockSpec((tm, D_MODEL), lambda i, j: (i, 0)),
            pl.BlockSpec((1, D_MODEL), lambda i, j: (0, 0)),
            pl.BlockSpec((None, D_MODEL, tf), lambda i, j: (li, 0, j)),
            pl.BlockSpec((None, D_MODEL, tf), lambda i, j: (li, 0, j + nj)),
            pl.BlockSpec((None, tf, D_MODEL), lambda i, j: (li, j, 0)),
        ],
        out_specs=pl.BlockSpec((tm, D_MODEL), lambda i, j: (i, 0)),
        scratch_shapes=[pltpu.VMEM((tm, D_MODEL), BF16), pltpu.VMEM((tm, D_MODEL), F32)],
        compiler_params=_cparams(("parallel", "arbitrary")),
        name="ffn",
    )(x, gain, w_gu, w_gu, w_d)


def _group_padded(arr, h):
    c = arr[:, (h // 2) * LANES:(h // 2 + 1) * LANES]
    g = h // NSA_REP
    if (h % 2) != g:
        c = pltpu.roll(c, HEAD_DIM, axis=1)
    lane = lax.broadcasted_iota(jnp.int32, c.shape, 1)
    keep = (lane >= g * HEAD_DIM) & (lane < (g + 1) * HEAD_DIM)
    return jnp.where(keep, c, 0.0)


def _inproj_kernel(x_ref, ln_ref, w_ref, gq_ref, gks_ref, gkw_ref, cos_ref, sin_ref, gw_ref, gb_ref,
                   lng_ref, lnb_ref, sm_ref, *outs, tm, attn_layout):
    (newkv_ref, newwin_ref, gates_ref, qg_ref, kg_ref, la_ref, vg_ref, rs_ref, u_ref, vn_ref) = outs[:10]
    x = x_ref[...]
    ms = jnp.mean(x * x, axis=-1, keepdims=True)
    h = (x * lax.rsqrt(ms + EPS) * ln_ref[...]).astype(BF16)
    p = _dot(h, w_ref[...])
    sm = sm_ref[...]
    cos = cos_ref[...]
    sin = sin_ref[...]
    o = IN_OFFS

    def seg(i, a=0, b=None):
        b = IN_PADDED[i] if b is None else b
        return p[:, o[i] + a:o[i] + b]

    qn = _seg_rms(seg(0), gq_ref[...], sm)
    qr = _rope(qn, cos, sin)
    kv = [seg(1, LANES * j, LANES * (j + 1)) for j in range(6)]
    ksel = _rope(_seg_rms(kv[2], gks_ref[...], sm), cos, sin)
    kwin = _rope(_seg_rms(kv[4], gkw_ref[...], sm), cos, sin)
    vsel_t = kv[3].T
    vwin_t = kv[5].T
    if attn_layout:
        for j, a in enumerate((kv[0].T, kv[1].T, ksel.T, vsel_t)):
            newkv_ref[j * LANES:(j + 1) * LANES, :] = a
        newwin_ref[0:LANES, :] = kwin.T
        newwin_ref[LANES:2 * LANES, :] = vwin_t
    else:
        for j, a in enumerate((kv[0], kv[1], ksel, kv[3])):
            newkv_ref[:, j * LANES:(j + 1) * LANES] = a
        newwin_ref[:, 0:LANES] = kwin
        newwin_ref[:, LANES:2 * LANES] = kv[5]
    gates = _sigmoid(seg(2))
    gates_ref[...] = gates
    qg_ref[...] = seg(3) * np.float32(GLA_DK ** -0.5)
    kg_ref[...] = seg(4)
    vg_ref[...] = seg(5)
    logit = _dot(seg(6).astype(BF16), gw_ref[...]) + gb_ref[...]
    log_sig = jnp.minimum(logit, 0.0) - jnp.log1p(jnp.exp(-jnp.abs(logit)))
    la_ref[...] = log_sig * np.float32(np.log2(np.e) / GLA_GATE_TEMP)
    r = seg(7)
    rs_ref[...] = r * _sigmoid(r)
    u_ref[...] = _gelu(seg(8))
    v = _gelu(seg(9))
    mu = jnp.mean(v, axis=-1, keepdims=True)
    var = jnp.mean(jnp.square(v - mu), axis=-1, keepdims=True)
    vn_ref[...] = (v - mu) * lax.rsqrt(var + EPS) * lng_ref[...] + lnb_ref[...]

    if attn_layout:
        (qs_ref, qrs_ref, gt_ref, kselr_ref, vselt_ref, kwinr_ref, vwint_ref, kcmp_ref, vcmp_ref) = outs[10:]
        scale = np.float32(HEAD_DIM ** -0.5 * np.log2(np.e))
        qs = qn * scale
        qrs = qr * scale
        for hh in range(NSA_HEADS):
            a = _group_padded(qs, hh).astype(BF16)
            b = _group_padded(qrs, hh).astype(BF16)
            for rb in range(tm // Q_BLOCK):
                qs_ref[rb, hh] = a[rb * Q_BLOCK:(rb + 1) * Q_BLOCK]
                qrs_ref[rb, hh] = b[rb * Q_BLOCK:(rb + 1) * Q_BLOCK]
        for rb in range(tm // Q_BLOCK):
            gt_ref[rb] = gates[rb * Q_BLOCK:(rb + 1) * Q_BLOCK].T
        kselr_ref[...] = ksel.astype(BF16)
        vselt_ref[...] = vsel_t.astype(BF16)
        kwinr_ref[...] = kwin.astype(BF16)
        vwint_ref[...] = vwin_t.astype(BF16)
        kcmp_ref[...] = kv[0]
        vcmp_ref[...] = kv[1]
    else:
        qn_ref, qr_ref = outs[10:]
        qn_ref[...] = qn
        qr_ref[...] = qr


def _inproj(x, lw, cos_t, sin_t, *, tm, attn_layout, batch, seq):
    m = x.shape[0]
    nt = m // tm
    ntab = cos_t.shape[0] // tm
    row = lambda w: pl.BlockSpec((tm, w), lambda i: (i, 0))
    full = lambda a: pl.BlockSpec(a.shape, lambda i: (0,) * a.ndim)
    ins = [x, lw["ln1"], lw["w_in"], lw["gq"], lw["gks"], lw["gkw"], cos_t, sin_t, lw["gla_gw"], lw["gla_gb"],
           lw["gm_lng"], lw["gm_lnb"], lw["sm"]]
    in_specs = [row(D_MODEL), full(lw["ln1"]), full(lw["w_in"]), full(lw["gq"]), full(lw["gks"]), full(lw["gkw"]),
                pl.BlockSpec((tm, LANES), lambda i: (i % ntab, 0)), pl.BlockSpec((tm, LANES), lambda i: (i % ntab, 0)),
                full(lw["gla_gw"]), full(lw["gla_gb"]), full(lw["gm_lng"]), full(lw["gm_lnb"]), full(lw["sm"])]
    widths = [512, 256, 128, 128, 128, 128, 256, 256, 256, 256]
    out_shape = [jax.ShapeDtypeStruct((m, w), F32) for w in widths]
    out_specs = [row(w) for w in widths]
    if attn_layout:
        nqb = m // Q_BLOCK
        rpb = tm // Q_BLOCK
        tpb = seq // tm
        for j in range(2):
            out_shape[j] = jax.ShapeDtypeStruct((batch, widths[j], seq), F32)
            out_specs[j] = pl.BlockSpec((None, widths[j], tm), lambda i: (i // tpb, 0, i % tpb))
        out_shape += [jax.ShapeDtypeStruct((nqb, NSA_HEADS, Q_BLOCK, LANES), BF16)] * 2
        out_specs += [pl.BlockSpec((rpb, NSA_HEADS, Q_BLOCK, LANES), lambda i: (i, 0, 0, 0))] * 2
        out_shape += [jax.ShapeDtypeStruct((nqb, LANES, Q_BLOCK), F32)]
        out_specs += [pl.BlockSpec((rpb, LANES, Q_BLOCK), lambda i: (i, 0, 0))]
        rowmaj = (jax.ShapeDtypeStruct((m, LANES), BF16), row(LANES))
        trans = (jax.ShapeDtypeStruct((batch, LANES, seq), BF16),
                 pl.BlockSpec((None, LANES, tm), lambda i: (i // tpb, 0, i % tpb)))
        for sh, sp in (rowmaj, trans, rowmaj, trans):
            out_shape.append(sh)
            out_specs.append(sp)
        out_shape += [jax.ShapeDtypeStruct((m, LANES), F32)] * 2
        out_specs += [row(LANES)] * 2
    else:
        out_shape += [jax.ShapeDtypeStruct((m, NSA_Q), F32)] * 2
        out_specs += [row(NSA_Q)] * 2
    return pl.pallas_call(
        functools.partial(_inproj_kernel, tm=tm, attn_layout=attn_layout),
        out_shape=out_shape,
        grid=(nt,),
        in_specs=in_specs,
        out_specs=out_specs,
        compiler_params=_cparams(("parallel",)),
        name="inproj",
    )(*ins)


def _next_row(b):
    return pltpu.roll(b, b.shape[0] - 1, axis=0)


def _compress_kernel(zk_ref, zv_ref, pos_ref, wak_ref, wbk_ref, wav_ref, wbv_ref,
                     w2k_ref, w2v_ref, gkc_ref, sm_ref, kc_ref, vct_ref):
    pos = pos_ref[...]

    def one(z_ref, pa, pb, wa_ref, wb_ref, w2_ref):
        z = z_ref[...]
        a = _dot((z + pa).astype(BF16), wa_ref[...])
        b = _dot((z + pb).astype(BF16), wb_ref[...])
        return _dot(_gelu(a + _next_row(b)).astype(BF16), w2_ref[...])

    ck = one(zk_ref, pos[0:1], pos[1:2], wak_ref, wbk_ref, w2k_ref)
    cv = one(zv_ref, pos[2:3], pos[3:4], wav_ref, wbv_ref, w2v_ref)
    kc_ref[...] = _seg_rms(ck, gkc_ref[...], sm_ref[...]).astype(BF16)
    vct_ref[...] = cv.T.astype(BF16)


def _compress(zk, zv, lw):
    b, nc, kdim = zk.shape
    zspec = pl.BlockSpec((None, nc, kdim), lambda i: (i, 0, 0))
    full = lambda a: pl.BlockSpec(a.shape, lambda i: (0,) * a.ndim)
    ws = [lw["cmp_pos"], lw["cmp_wak"], lw["cmp_wbk"], lw["cmp_wav"], lw["cmp_wbv"], lw["cmp_w2k"], lw["cmp_w2v"],
          lw["gkc"], lw["sm"]]
    return pl.pallas_call(
        _compress_kernel,
        out_shape=[jax.ShapeDtypeStruct((b, nc, LANES), BF16), jax.ShapeDtypeStruct((b, LANES, nc), BF16)],
        grid=(b,),
        in_specs=[zspec] * 2 + [full(w) for w in ws],
        out_specs=[pl.BlockSpec((None, nc, LANES), lambda i: (i, 0, 0)),
                   pl.BlockSpec((None, LANES, nc), lambda i: (i, 0, 0))],
        compiler_params=_cparams(("parallel",)),
        name="nsa_compress",
    )(zk, zv, *ws)


def _nsa_kernel(qs_ref, qrs_ref, gt_ref, kc_ref, vct_ref, cov_ref, ksel_ref, vselt_ref, kwin_ref, vwint_ref,
                o_ref, score_scr, bias_scr, sa_scr, sb_scr, *, nc, ns, kb_keys):
    ib = pl.program_id(1)
    p0 = ib * Q_BLOCK
    nl = NSA_REP * Q_BLOCK
    ng = NSA_KV_HEADS
    t_row = p0 + lax.broadcasted_iota(jnp.int32, (1, nl), 1) % Q_BLOCK
    t_q = p0 + lax.broadcasted_iota(jnp.int32, (1, Q_BLOCK), 1)
    gt = gt_ref[...]
    n_top = min(N_SEL, ns)
    per_kb = kb_keys // SEL_BLOCK
    grows = [slice(g * HEAD_DIM, (g + 1) * HEAD_DIM) for g in range(ng)]

    def q_of(ref, g):
        return ref[NSA_REP * g:NSA_REP * (g + 1)].reshape(nl, LANES)

    ones_rows = 16

    def with_ones(vt):
        return jnp.concatenate([vt, jnp.ones((ones_rows, vt.shape[1]), BF16)], axis=0)

    def online(carry, s, vt):
        m, acc = carry
        m_new = jnp.maximum(m, jnp.max(s, axis=0, keepdims=True))
        p = jnp.exp2(s - m_new)
        acc = jnp.exp2(m - m_new) * acc + _dot(with_ones(vt), p.astype(BF16))
        return m_new, acc

    def normalised(acc):
        return acc[0:HEAD_DIM] * (1.0 / jnp.maximum(acc[HEAD_DIM:HEAD_DIM + 1], 1e-30))

    init1 = (jnp.full((1, nl), NEG_BIG, F32), jnp.zeros((HEAD_DIM + ones_rows, nl), F32))
    init = tuple(init1 for _ in range(ng))

    sj = lax.broadcasted_iota(jnp.int32, (ns, 1), 0)

    def cmp_branch(rows):
        outs = []
        for g in range(ng):
            s = _dot_nt(kc_ref[0:rows, :], q_of(qs_ref, g))
            ci = lax.broadcasted_iota(jnp.int32, (rows, 1), 0)
            cmask = (ci * CMP_STRIDE + (CMP_BLOCK - 1) <= t_row) & (ci < nc - 1)
            s = jnp.where(cmask, s, -jnp.inf)
            m = jnp.max(s, axis=0, keepdims=True)
            m = jnp.where(m == -jnp.inf, 0.0, m)
            e = jnp.exp2(s - m)
            l = jnp.sum(e, axis=0, keepdims=True)
            p = e * (1.0 / jnp.maximum(l, 1e-30))
            outs.append(_dot(vct_ref[grows[g], 0:rows], p.astype(BF16)))
            psum = p[:, 0:Q_BLOCK]
            for r in range(1, NSA_REP):
                psum = psum + p[:, r * Q_BLOCK:(r + 1) * Q_BLOCK]
            imp = _dot(cov_ref[:, 0:rows], psum.astype(BF16))
            bt = t_q // SEL_BLOCK
            allowed = sj * SEL_BLOCK <= t_q
            forced = (sj == 0) | (sj == bt) | (sj == bt - 1)
            score_scr[g] = jnp.where(allowed, imp + jnp.where(forced, np.float32(SEL_BONUS), 0.0), -jnp.inf)
        return tuple(outs)

    quarter = nc // 4
    if quarter % LANES == 0:
        vis = (p0 + Q_BLOCK - CMP_BLOCK) // CMP_STRIDE
        o_cmp = lax.cond(
            vis < 2 * quarter,
            lambda: lax.cond(vis < quarter, lambda: cmp_branch(quarter), lambda: cmp_branch(2 * quarter)),
            lambda: lax.cond(vis < 3 * quarter, lambda: cmp_branch(3 * quarter), lambda: cmp_branch(nc)))
    else:
        o_cmp = cmp_branch(nc)

    sjf = sj.astype(F32)
    bt_q = t_q // SEL_BLOCK
    pre = ((sj == 0) | (sj == bt_q) | (sj == bt_q - 1)) & (sj * SEL_BLOCK <= t_q)
    left = [jnp.where(pre, -jnp.inf, score_scr[g]) for g in range(ng)]
    bias = [jnp.where(pre, 0.0, -jnp.inf) for _ in range(ng)]
    for _ in range(max(n_top - 3, 0)):
        for g in range(ng):
            mx = jnp.max(left[g], axis=0, keepdims=True)
            idx = jnp.min(jnp.where(left[g] == mx, sjf, np.float32(ns)), axis=0, keepdims=True)
            hit = sjf == jnp.where(mx > -jnp.inf, idx, -1.0)
            left[g] = jnp.where(hit, -jnp.inf, left[g])
            bias[g] = jnp.where(hit, 0.0, bias[g])
    for g in range(ng):
        bias_scr[g] = bias[g]

    def sel_qk(kb, g, dst):
        k0 = pl.multiple_of(kb * kb_keys, kb_keys)
        dst[g] = _dot_nt(ksel_ref[pl.ds(k0, kb_keys), :], q_of(qrs_ref, g))

    def sel_step(carry_g, src, kb, g, causal):
        k0 = pl.multiple_of(kb * kb_keys, kb_keys)
        parts = []
        for i in range(per_kb):
            row = bias_scr[g, pl.ds(kb * per_kb + i, 1), :]
            sl = slice(i * SEL_BLOCK, (i + 1) * SEL_BLOCK)
            parts.append(src[g, sl, :] + jnp.concatenate([row] * NSA_REP, axis=1))
        s = jnp.concatenate(parts, axis=0)
        if causal:
            kpos = k0 + lax.broadcasted_iota(jnp.int32, (kb_keys, 1), 0)
            s = jnp.where(kpos <= t_row, s, -jnp.inf)
        return online(carry_g, s, vselt_ref[grows[g], pl.ds(k0, kb_keys)])

    def sel_pair(j, carry, causal):
        a = 2 * j
        for g in range(ng):
            sel_qk(a + 1, g, sb_scr)
        carry = tuple(sel_step(carry[g], sa_scr, a, g, causal) for g in range(ng))
        if not causal:
            for g in range(ng):
                sel_qk(a + 2, g, sa_scr)
        return tuple(sel_step(carry[g], sb_scr, a + 1, g, causal) for g in range(ng))

    n_kb = (p0 + Q_BLOCK + kb_keys - 1) // kb_keys
    n_pairs = (n_kb + 1) // 2
    for g in range(ng):
        sel_qk(0, g, sa_scr)
    carry = lax.fori_loop(0, n_pairs - 1, lambda j, c: sel_pair(j, c, False), init)
    carry = sel_pair(n_pairs - 1, carry, True)
    o_sel = [normalised(carry[g][1]) for g in range(ng)]

    wkeys = WINDOW + Q_BLOCK
    w0 = pl.multiple_of(jnp.maximum(p0 - WINDOW, 0), Q_BLOCK)
    diff = t_row - (w0 + lax.broadcasted_iota(jnp.int32, (wkeys, 1), 0))
    wmask = (diff >= 0) & (diff < WINDOW)
    kblk = kwin_ref[pl.ds(w0, wkeys), :]
    o_win = []
    for g in range(ng):
        s = jnp.where(wmask, _dot_nt(kblk, q_of(qrs_ref, g)), -jnp.inf)
        o_win.append(normalised(online(init1, s, vwint_ref[grows[g], pl.ds(w0, wkeys)])[1]))

    for g in range(ng):
        o_w = o_win[g]

        def gate_row(jb):
            return jnp.concatenate(
                [gt[(NSA_REP * g + r) * 3 + jb:(NSA_REP * g + r) * 3 + jb + 1, :] for r in range(NSA_REP)], axis=1)

        o = o_cmp[g] * gate_row(0) + o_sel[g] * gate_row(1) + o_w * gate_row(2)
        for pr in range(NSA_REP // 2):
            blk = jnp.concatenate([o[:, (2 * pr) * Q_BLOCK:(2 * pr + 1) * Q_BLOCK],
                                   o[:, (2 * pr + 1) * Q_BLOCK:(2 * pr + 2) * Q_BLOCK]], axis=0)
            c0 = g * NSA_REP * HEAD_DIM + pr * LANES
            o_ref[:, c0:c0 + LANES] = blk.T


def _nsa_prompt(qs, qrs, gt, kc, vct, cov, ksel_r, vsel_t, kwin_r, vwin_t, *, batch, seq):
    nb = seq // Q_BLOCK
    nc = kc.shape[1]
    ns = cov.shape[0]
    kb_keys = min(512, seq)
    per_b3 = lambda a: pl.BlockSpec((None,) + a.shape[1:], lambda b, i: (b, 0, 0))
    return pl.pallas_call(
        functools.partial(_nsa_kernel, nc=nc, ns=ns, kb_keys=kb_keys),
        out_shape=jax.ShapeDtypeStruct((batch * seq, NSA_Q), F32),
        grid=(batch, nb),
        in_specs=[
            pl.BlockSpec((None, NSA_HEADS, Q_BLOCK, LANES), lambda b, i: (b * nb + i, 0, 0, 0)),
            pl.BlockSpec((None, NSA_HEADS, Q_BLOCK, LANES), lambda b, i: (b * nb + i, 0, 0, 0)),
            pl.BlockSpec((None, LANES, Q_BLOCK), lambda b, i: (b * nb + i, 0, 0)),
            per_b3(kc), per_b3(vct),
            pl.BlockSpec(cov.shape, lambda b, i: (0, 0)),
            per_b3(ksel_r), per_b3(vsel_t), per_b3(kwin_r), per_b3(vwin_t),
        ],
        out_specs=pl.BlockSpec((Q_BLOCK, NSA_Q), lambda b, i: (b * nb + i, 0)),
        scratch_shapes=[pltpu.VMEM((NSA_KV_HEADS, ns, Q_BLOCK), F32), pltpu.VMEM((NSA_KV_HEADS, ns, Q_BLOCK), F32),
                        pltpu.VMEM((NSA_KV_HEADS, kb_keys, NSA_REP * Q_BLOCK), F32),
                        pltpu.VMEM((NSA_KV_HEADS, kb_keys, NSA_REP * Q_BLOCK), F32)],
        compiler_params=_cparams(("parallel", "arbitrary")),
        name="nsa_prompt",
    )(qs, qrs, gt, kc, vct, cov, ksel_r, vsel_t, kwin_r, vwin_t)


def _gla_rows(c):
    offs, n = [], 0
    for s in range(c):
        t0 = (s // 8) * 8
        offs.append((n, t0))
        n += c - t0
    return offs, n


def _gla_kernel(q_ref, k_ref, la_ref, v_ref, rs_ref, gn_ref, bm_ref, bmask_ref, sm_ref, s0_ref,
                o_ref, sout_ref, s_scr, prod_scr, *, c, nchunks, bpb):
    ci = pl.program_id(1)

    @pl.when(ci == 0)
    def _():
        s_scr[...] = s0_ref[...]
        prod_scr[...] = jnp.zeros_like(prod_scr)

    for bb in range(bpb):
        _gla_one(q_ref.at[bb], k_ref.at[bb], la_ref.at[bb], v_ref.at[bb], rs_ref.at[bb], gn_ref, bm_ref, bmask_ref,
                 sm_ref, o_ref.at[bb], sout_ref.at[bb], s_scr.at[bb], prod_scr.at[bb],
                 c=c, last=ci == nchunks - 1)


def _gla_one(q_ref, k_ref, la_ref, v_ref, rs_ref, gn_ref, bm_ref, bmask_ref, sm_ref, o_ref, sout_ref,
             s_scr, prod_scr, *, c, last):
    q = q_ref[...]
    k = k_ref[...]
    v = v_ref[...]
    la = la_ref[...]
    tt = lax.broadcasted_iota(jnp.int32, (c, 1), 0)
    b = la
    sh = 1
    while sh < c:
        b = b + jnp.where(tt >= sh, pltpu.roll(b, sh, axis=0), 0.0)
        sh *= 2
    state = s_scr[...]
    inter = _dot_nt((q * jnp.exp2(b)).astype(BF16), state.astype(BF16))

    offs, npack = _gla_rows(c)
    acc = [inter[tb:min(tb + 8, c)] for tb in range(0, c, 8)]
    gsz = min(16, c)
    for g0 in range(0, c, gsz):
        for s in range(g0, g0 + gsz):
            r0, t0 = offs[s]
            d = b[t0:] - b[s:s + 1]
            head = jnp.where(tt[t0:t0 + 8] >= s, d[0:8], -jnp.inf)
            d = head if c - t0 <= 8 else jnp.concatenate([head, d[8:]], axis=0)
            prod_scr[r0:r0 + c - t0, :] = (q[t0:] * k[s:s + 1] * jnp.exp2(d)).astype(BF16)
        lo = offs[g0][0]
        hi = offs[g0 + gsz][0] if g0 + gsz < c else npack
        res = _dot(prod_scr[lo:hi, :], bm_ref[...])
        for s in range(g0, g0 + gsz):
            r0, t0 = offs[s]
            for tb in range(t0, c, 8):
                rr = r0 - lo + tb - t0
                acc[tb // 8] = acc[tb // 8] + res[rr:rr + min(8, c - tb)] * v[s:s + 1]
    o = acc[0] if len(acc) == 1 else jnp.concatenate(acc, axis=0)

    bl = b[c - 1:c]
    kd = (k * jnp.exp2(bl - b)).astype(BF16)
    if c >= 16:
        upd = _dot(v.T.astype(BF16), kd)
    else:
        upd = jnp.dot(v.T, kd.astype(F32), preferred_element_type=F32)
    new_state = jnp.exp2(bl) * state + upd * bmask_ref[...]
    s_scr[...] = new_state

    o_ref[...] = _seg_rms(o, gn_ref[...], sm_ref[...]) * rs_ref[...]

    @pl.when(last)
    def _():
        sout_ref[...] = new_state


def _gla(qg, kg, la, vg, rs, s0_bd, lw, *, batch, seq, c, bpb):
    nchunks = seq // c
    _, npack = _gla_rows(c)
    npad = -(-npack // 16) * 16
    blk = lambda w: pl.BlockSpec((bpb, c, w), lambda b, i: (b, i, 0))
    full = lambda a: pl.BlockSpec(a.shape, lambda b, i: (0,) * a.ndim)
    consts = [lw["gla_gn"], lw["gla_bm"], lw["gla_bmask"], lw["sm"]]
    st = pl.BlockSpec((bpb, GLA_V, GLA_QK), lambda b, i: (b, 0, 0))
    return pl.pallas_call(
        functools.partial(_gla_kernel, c=c, nchunks=nchunks, bpb=bpb),
        out_shape=[jax.ShapeDtypeStruct((batch, seq, GLA_V), F32), jax.ShapeDtypeStruct((batch, GLA_V, GLA_QK), F32)],
        grid=(batch // bpb, nchunks),
        in_specs=[blk(GLA_QK), blk(GLA_QK), blk(GLA_QK), blk(GLA_V), blk(GLA_V)] + [full(a) for a in consts] + [st],
        out_specs=[blk(GLA_V), st],
        scratch_shapes=[pltpu.VMEM((bpb, GLA_V, GLA_QK), F32), pltpu.VMEM((bpb, npad, GLA_QK), BF16)],
        compiler_params=_cparams(("parallel", "arbitrary")),
        name="gla",
    )(qg, kg, la, vg, rs, *consts, s0_bd)


def _outproj_kernel(x_ref, oa_ref, ob_ref, u_ref, vn_ref, ws_ref, bias_ref, wo_ref, o_ref, *, tm):
    lane = lax.broadcasted_iota(jnp.int32, (GM_CHUNK, GM_W), 1)
    tri = (lax.broadcasted_iota(jnp.int32, (GM_CHUNK, GM_CHUNK), 0)
           >= lax.broadcasted_iota(jnp.int32, (GM_CHUNK, GM_CHUNK), 1))
    zs = []
    for cb in range(tm // GM_CHUNK):
        vn = vn_ref[cb * GM_CHUNK:(cb + 1) * GM_CHUNK, :]
        z = bias_ref[...]
        for g in range(GM_GROUPS):
            wm = jnp.where(tri, ws_ref[g], 0.0).astype(BF16)
            vg = jnp.where((lane >= g * GM_CH) & (lane < (g + 1) * GM_CH), vn, 0.0).astype(BF16)
            z = z + _dot(wm, vg)
        zs.append(z)
    z = zs[0] if len(zs) == 1 else jnp.concatenate(zs, axis=0)
    oc = u_ref[...] * z
    y = _dot(oa_ref[...].astype(BF16), wo_ref[0:NSA_Q, :])
    y = y + _dot(ob_ref[...].astype(BF16), wo_ref[NSA_Q:NSA_Q + GLA_V, :])
    y = y + _dot(oc.astype(BF16), wo_ref[NSA_Q + GLA_V:MIX_OUT, :])
    o_ref[...] = x_ref[...] + y


def _outproj(x, oa, ob, u, vn, ws, bias, wo, li, *, tm):
    m = x.shape[0]
    row = lambda w: pl.BlockSpec((tm, w), lambda i: (i, 0))
    return pl.pallas_call(
        functools.partial(_outproj_kernel, tm=tm),
        out_shape=jax.ShapeDtypeStruct((m, D_MODEL), F32),
        grid=(m // tm,),
        in_specs=[row(D_MODEL), row(NSA_Q), row(GLA_V), row(GM_W), row(GM_W),
                  pl.BlockSpec(ws.shape, lambda i: (0, 0, 0)), pl.BlockSpec(bias.shape, lambda i: (0, 0)),
                  pl.BlockSpec((None, MIX_OUT, D_MODEL), lambda i: (li, 0, 0))],
        out_specs=row(D_MODEL),
        compiler_params=_cparams(("parallel",)),
        name="outproj",
    )(x, oa, ob, u, vn, ws, bias, wo)


def _pcompress_kernel(pt_ref, *refs, npg):
    pages = refs[:npg]
    perm_ref, wk_ref, wv_ref, out_ref, zk_scr, zv_scr = refs[npg:npg + 6]
    perm = perm_ref[...]
    for pp in range(npg // 2):
        xt = jnp.concatenate([pages[2 * pp][...], pages[2 * pp + 1][...]], axis=1).astype(BF16)
        y = _dot_nt(perm, xt).astype(BF16)
        for s in range(CMP_STRIDE):
            rows = slice(pp * 16, (pp + 1) * 16)
            zk_scr[rows, s * LANES:(s + 1) * LANES] = y[s * 16:(s + 1) * 16, 0:LANES]
            zv_scr[rows, s * LANES:(s + 1) * LANES] = y[s * 16:(s + 1) * 16, LANES:2 * LANES]
    q = out_ref.shape[1] // 4
    rk = _dot(zk_scr[...], wk_ref[...])
    rv = _dot(zv_scr[...], wv_ref[...])
    out_ref[:, 0:q] = rk[:, 0:q]
    out_ref[:, q:2 * q] = rv[:, 0:q]
    out_ref[:, 2 * q:3 * q] = rk[:, q:2 * q]
    out_ref[:, 3 * q:4 * q] = rv[:, q:2 * q]


def _pcompress(cache_t, page_table, li, perm, wk, wv, *, npg):
    bs, n_pages = page_table.shape
    page = cache_t.shape[-1]
    nc = n_pages * page // CMP_STRIDE
    steps = n_pages // npg
    cpp = page // CMP_STRIDE

    def page_spec(k):
        return pl.BlockSpec((None, None, 2 * LANES, page), lambda b, h, pt: (li, pt[b, h * npg + k], 0, 0))

    grid_spec = pltpu.PrefetchScalarGridSpec(
        num_scalar_prefetch=1,
        grid=(bs, steps),
        in_specs=[page_spec(k) for k in range(npg)] + [
            pl.BlockSpec(perm.shape, lambda b, h, pt: (0, 0)), pl.BlockSpec(wk.shape, lambda b, h, pt: (0, 0)),
            pl.BlockSpec(wv.shape, lambda b, h, pt: (0, 0))],
        out_specs=pl.BlockSpec((None, npg * cpp, 2 * wk.shape[1]), lambda b, h, pt: (b, h, 0)),
        scratch_shapes=[pltpu.VMEM((npg * cpp, wk.shape[0]), BF16)] * 2,
    )
    return pl.pallas_call(
        functools.partial(_pcompress_kernel, npg=npg),
        out_shape=jax.ShapeDtypeStruct((bs, nc, 2 * wk.shape[1]), F32),
        grid_spec=grid_spec,
        compiler_params=_cparams(("parallel", "arbitrary")),
        name="nsa_page_compress",
    )(page_table, *([cache_t] * npg), perm, wk, wv)


def _ctail_kernel(a_ref, b_ref, bias_ref, w2_ref, gkc_ref, sm_ref, kct_ref, vcr_ref):
    h = a_ref[...] + _next_row(b_ref[...]) + bias_ref[...]
    out = _dot(_gelu(h).astype(BF16), w2_ref[...])
    kc = _seg_rms(out[:, 0:LANES], gkc_ref[...], sm_ref[...])
    kct_ref[...] = kc.T.astype(BF16)
    vcr_ref[...] = out[:, LANES:2 * LANES].astype(BF16)


def _ctail(hid_ab, bias, w2, gkc, sm):
    bs, nc, w2x = hid_ab.shape
    half = w2x // 2
    full = lambda a: pl.BlockSpec(a.shape, lambda b: (0,) * a.ndim)
    return pl.pallas_call(
        _ctail_kernel,
        out_shape=[jax.ShapeDtypeStruct((bs, LANES, nc), BF16), jax.ShapeDtypeStruct((bs, nc, LANES), BF16)],
        grid=(bs,),
        in_specs=[pl.BlockSpec((None, nc, half), lambda b: (b, 0, 0)), pl.BlockSpec((None, nc, half), lambda b: (b, 0, 1)),
                  full(bias), full(w2), full(gkc), full(sm)],
        out_specs=[pl.BlockSpec((None, LANES, nc), lambda b: (b, 0, 0)), pl.BlockSpec((None, nc, LANES), lambda b: (b, 0, 0))],
        compiler_params=_cparams(("parallel",)),
        name="nsa_compress_tail",
    )(hid_ab, hid_ab, bias, w2, gkc, sm)


def _nsa_sample_kernel(pt_ref, qp_ref, qrp_ref, gcol_ref, kct_ref, vcr_ref, cov_ref, e_ref, *refs,
                       npg, nsteps, nc, ns_tot, past_len, ts):
    pages = refs[:npg]
    (wint_ref, newk_ref, neww_ref, newwsh_ref, o_ref, winout_ref,
     bias_scr, m_scr, l_scr, acc_scr, oc_scr, ow_scr) = refs[npg:]
    step = pl.program_id(1)
    nrow = NSA_KV_HEADS * NSA_REP * 8
    q8 = lax.broadcasted_iota(jnp.int32, (nrow, 1), 0) % 8
    t_row = past_len + q8
    qrp = qrp_ref[...]
    n_keep = wint_ref.shape[-1]
    n_top = min(N_SEL, ns_tot)
    nsp = bias_scr.shape[1]

    def expand_rows(a):
        return jnp.concatenate([a[0:8]] * NSA_REP + [a[8:16]] * NSA_REP, axis=0)

    @pl.when(step == 0)
    def _():
        s = _dot(qp_ref[...], kct_ref[...])
        ci = lax.broadcasted_iota(jnp.int32, (1, nc), 1)
        cmask = (ci * CMP_STRIDE + (CMP_BLOCK - 1) <= t_row) & (ci < nc - 1)
        s = jnp.where(cmask, s, -jnp.inf)
        m = jnp.max(s, axis=1, keepdims=True)
        m = jnp.where(m == -jnp.inf, 0.0, m)
        e = jnp.exp(s - m)
        p = e * (1.0 / jnp.maximum(jnp.sum(e, axis=1, keepdims=True), 1e-30))
        oc_scr[...] = _dot(p.astype(BF16), vcr_ref[...])
        sjl = lax.broadcasted_iota(jnp.int32, (1, nsp), 1)
        tq = past_len + lax.broadcasted_iota(jnp.int32, (8, 1), 0)
        bt = tq // SEL_BLOCK
        allowed = (sjl * SEL_BLOCK <= tq) & (sjl < ns_tot)
        forced = (sjl == 0) | (sjl == bt) | (sjl == bt - 1)
        for g in range(NSA_KV_HEADS):
            base = g * NSA_REP * 8
            psum = p[base:base + 8]
            for r in range(1, NSA_REP):
                psum = psum + p[base + r * 8:base + (r + 1) * 8]
            imp = _dot(psum.astype(BF16), cov_ref[...])
            score = jnp.where(allowed, imp + jnp.where(forced, np.float32(SEL_BONUS), 0.0), -jnp.inf)
            rank = jnp.zeros((8, nsp), jnp.int32)
            for k in range(ns_tot):
                col = score[:, k:k + 1]
                later = jnp.where(sjl > k, 1, 0)
                rank = rank + jnp.where(col > score, 1, 0) + jnp.where(col == score, later, 0)
            keep = (rank < n_top) & (score > -jnp.inf)
            bias_scr[g * 8:(g + 1) * 8, :] = jnp.where(keep, 0.0, -jnp.inf)
        m_scr[...] = jnp.full(m_scr.shape, NEG_BIG, F32)
        l_scr[...] = jnp.zeros(l_scr.shape, F32)
        acc_scr[...] = jnp.zeros(acc_scr.shape, F32)

        lane_w = lax.broadcasted_iota(jnp.int32, (1, n_keep), 1)
        diff = t_row - (past_len - n_keep + lane_w)
        s_w = jnp.where((diff >= 0) & (diff < WINDOW), _dot(qrp, wint_ref[0:LANES, :].astype(BF16)), -jnp.inf)
        lane_n = lax.broadcasted_iota(jnp.int32, (1, LANES), 1)
        diff_n = t_row - (past_len + lane_n)
        s_n = jnp.where((lane_n < ts) & (diff_n >= 0) & (diff_n < WINDOW),
                        _dot(qrp, neww_ref[0:LANES, :].astype(BF16)), -jnp.inf)
        sw = jnp.concatenate([s_w, s_n], axis=1)
        mw = jnp.max(sw, axis=1, keepdims=True)
        mw = jnp.where(mw == -jnp.inf, 0.0, mw)
        ew = jnp.exp(sw - mw)
        pw = ew * (1.0 / jnp.maximum(jnp.sum(ew, axis=1, keepdims=True), 1e-30))
        ow_scr[...] = (_dot_nt(pw[:, 0:n_keep].astype(BF16), wint_ref[LANES:2 * LANES, :].astype(BF16))
                       + _dot_nt(pw[:, n_keep:].astype(BF16), neww_ref[LANES:2 * LANES, :].astype(BF16)))
        rolled = pltpu.roll(wint_ref[...], n_keep - ts, axis=1)
        lane_o = lax.broadcasted_iota(jnp.int32, (2 * LANES, LANES), 1)
        winout_ref[:, 0:n_keep - LANES] = rolled[:, 0:n_keep - LANES]
        winout_ref[:, n_keep - LANES:n_keep] = jnp.where(lane_o >= LANES - ts, newwsh_ref[...],
                                                         rolled[:, n_keep - LANES:n_keep])

    def online(s, vt):
        m_old = m_scr[:, 0:1]
        m_new = jnp.maximum(m_old, jnp.max(s, axis=1, keepdims=True))
        p = jnp.exp(s - m_new)
        alpha = jnp.exp(m_old - m_new)
        l_new = alpha * l_scr[:, 0:1] + jnp.sum(p, axis=1, keepdims=True)
        acc_scr[...] = alpha * acc_scr[...] + _dot_nt(p.astype(BF16), vt)
        m_scr[...] = jnp.broadcast_to(m_new, m_scr.shape)
        l_scr[...] = jnp.broadcast_to(l_new, l_scr.shape)

    sel01 = jnp.where(bias_scr[...] == 0.0, 1.0, 0.0).astype(BF16)
    bexp = _dot(sel01, e_ref[...])
    bias = expand_rows(jnp.where(bexp > 0.5, 0.0, -jnp.inf))
    kt = jnp.concatenate([pg[0:LANES, :] for pg in pages], axis=1).astype(BF16)
    vt = jnp.concatenate([pg[LANES:2 * LANES, :] for pg in pages], axis=1).astype(BF16)
    online(_dot(qrp, kt) + bias, vt)

    @pl.when(step == nsteps - 1)
    def _():
        lane_n = lax.broadcasted_iota(jnp.int32, (1, LANES), 1)
        bcol = expand_rows(bias_scr[:, ns_tot - 1:ns_tot])
        ok = (lane_n < ts) & (past_len + lane_n <= t_row)
        s_n = jnp.where(ok, _dot(qrp, newk_ref[2 * LANES:3 * LANES, :].astype(BF16)) + bcol, -jnp.inf)
        online(s_n, newk_ref[3 * LANES:4 * LANES, :].astype(BF16))
        o_s = acc_scr[...] * (1.0 / jnp.maximum(l_scr[:, 0:1], 1e-30))
        gc = gcol_ref[...]
        o_ref[...] = oc_scr[...] * gc[:, 0:1] + o_s * gc[:, 1:2] + ow_scr[...] * gc[:, 2:3]


def _nsa_sample(cache_t, page_table, li, qp, qrp, gcol, kct, vcr, cov_s, emat, win_t, newk_t, neww_t, newwsh_t, *,
                npg, past_len, ts):
    bs, n_pages = page_table.shape
    page = cache_t.shape[-1]
    nsteps = n_pages // npg
    nc = kct.shape[-1]
    ns_tot = -(-(past_len + ts) // SEL_BLOCK)
    nsp = cov_s.shape[1]
    n_keep = win_t.shape[-1]
    nrow = qp.shape[1]
    per_b = lambda a: pl.BlockSpec((None,) + a.shape[1:], lambda b, h, pt: (b,) + (0,) * (a.ndim - 1))

    def page_spec(k):
        return pl.BlockSpec((None, None, 2 * LANES, page), lambda b, h, pt: (li, pt[b, h * npg + k], 1, 0))

    grid_spec = pltpu.PrefetchScalarGridSpec(
        num_scalar_prefetch=1,
        grid=(bs, nsteps),
        in_specs=[per_b(qp), per_b(qrp), per_b(gcol), per_b(kct), per_b(vcr),
                  pl.BlockSpec(cov_s.shape, lambda b, h, pt: (0, 0)),
                  pl.BlockSpec((nsp, npg * page), lambda b, h, pt: (0, h))]
                 + [page_spec(k) for k in range(npg)]
                 + [pl.BlockSpec((None, None, 2 * LANES, n_keep), lambda b, h, pt: (li, b, 0, 0)),
                    per_b(newk_t), per_b(neww_t), per_b(newwsh_t)],
        out_specs=[pl.BlockSpec((None, nrow, LANES), lambda b, h, pt: (b, 0, 0)),
                   pl.BlockSpec((None, 2 * LANES, n_keep), lambda b, h, pt: (b, 0, 0))],
        scratch_shapes=[pltpu.VMEM((2 * 8, nsp), F32)] + [pltpu.VMEM((nrow, LANES), F32)] * 5,
    )
    return pl.pallas_call(
        functools.partial(_nsa_sample_kernel, npg=npg, nsteps=nsteps, nc=nc, ns_tot=ns_tot, past_len=past_len, ts=ts),
        out_shape=[jax.ShapeDtypeStruct((bs, nrow, LANES), F32), jax.ShapeDtypeStruct((bs, 2 * LANES, n_keep), F32)],
        grid_spec=grid_spec,
        compiler_params=_cparams(("parallel", "arbitrary")),
        name="nsa_sample",
    )(page_table, qp, qrp, gcol, kct, vcr, cov_s, emat, *([cache_t] * npg), win_t, newk_t, neww_t, newwsh_t)


def _pad_cols(w):
    cuts = np.cumsum((0,) + IN_SPLITS)
    parts = []
    for i, (n, p) in enumerate(zip(IN_SPLITS, IN_PADDED)):
        seg = w[..., cuts[i]:cuts[i] + n]
        if p != n:
            seg = jnp.pad(seg, [(0, 0)] * (w.ndim - 1) + [(0, p - n)])
        parts.append(seg)
    return jnp.concatenate(parts, axis=-1)


def _rope_tables(pos):
    half = HEAD_DIM // 2
    inv = 1.0 / (ROPE_THETA ** (jnp.arange(half, dtype=F32) * (2.0 / HEAD_DIM)))
    ang = pos.astype(F32)[:, None] * inv[None, :]
    cos = jnp.cos(ang)
    sin = jnp.sin(ang)
    cos_f = jnp.concatenate([cos, cos, cos, cos], axis=1)
    sin_f = jnp.concatenate([-sin, sin, -sin, sin], axis=1)
    return cos_f, sin_f


def _cover_t(seq):
    nc = seq // CMP_STRIDE
    ns = seq // SEL_BLOCK
    ci = np.arange(nc)[None, :]
    sj = np.arange(ns)[:, None]
    cov = ((ci * CMP_STRIDE <= sj * SEL_BLOCK + SEL_BLOCK - 1)
           & (ci * CMP_STRIDE + CMP_BLOCK - 1 >= sj * SEL_BLOCK) & (ci < nc - 1))
    return jnp.asarray(cov, dtype=BF16)


def _layer_weights(l, ln_gains, w_in_p, nsa_qk_norm, nsa_cmp_pos, nsa_cmp_w1, nsa_cmp_w2, gla_gate_w, gla_gate_b,
                   gla_norm, gm_ln, gm_ws, gm_b):
    eye2 = jnp.eye(NSA_KV_HEADS, dtype=F32)
    lw = {"ln0": ln_gains[l, 0][None], "ln1": ln_gains[l, 1][None], "ln2": ln_gains[l, 2][None], "w_in": w_in_p[l]}
    lw["gq"] = jnp.tile(nsa_qk_norm[l, 0], NSA_HEADS)[None]
    lw["gkc"] = jnp.tile(nsa_qk_norm[l, 1], NSA_KV_HEADS)[None]
    lw["gks"] = jnp.tile(nsa_qk_norm[l, 2], NSA_KV_HEADS)[None]
    lw["gkw"] = jnp.tile(nsa_qk_norm[l, 3], NSA_KV_HEADS)[None]
    seg = (np.arange(LANES)[:, None] // HEAD_DIM) == (np.arange(LANES)[None, :] // HEAD_DIM)
    lw["sm"] = jnp.asarray(seg * (1.0 / HEAD_DIM), dtype=BF16)
    lw["gla_gw"] = jnp.pad(gla_gate_w[l], ((0, LANES - GLA_GATE_RANK), (0, 0))).astype(BF16)
    lw["gla_gb"] = gla_gate_b[l][None]
    lw["gm_lng"] = gm_ln[l, 0][None]
    lw["gm_lnb"] = gm_ln[l, 1][None]
    pos_rows = []
    for c, nm in ((0, "k"), (1, "v")):
        w1 = nsa_cmp_w1[l, c].reshape(CMP_BLOCK, HEAD_DIM, CMP_HIDDEN)
        for half, tag in ((w1[:CMP_STRIDE], "a"), (w1[CMP_STRIDE:], "b")):
            wx = jnp.einsum("sdh,pg->spdgh", half, eye2)
            lw["cmp_w" + tag + nm] = wx.reshape(CMP_STRIDE * LANES, NSA_KV_HEADS * CMP_HIDDEN).astype(BF16)
        lw["cmp_w2" + nm] = jnp.einsum("hd,pg->phgd", nsa_cmp_w2[l, c], eye2).reshape(
            NSA_KV_HEADS * CMP_HIDDEN, LANES).astype(BF16)
        pe = nsa_cmp_pos[l, c]
        for half in (pe[:CMP_STRIDE], pe[CMP_STRIDE:]):
            pos_rows.append(jnp.broadcast_to(half[:, None, :], (CMP_STRIDE, NSA_KV_HEADS, HEAD_DIM)).reshape(-1))
    lw["cmp_pos"] = jnp.stack(pos_rows)
    w1ab = nsa_cmp_w1[l].reshape(2, 2, CMP_STRIDE, HEAD_DIM, CMP_HIDDEN)
    for c, nm in ((0, "k"), (1, "v")):
        lw["pc_w" + nm] = jnp.einsum("asdh,gy->sgdayh", w1ab[c], eye2).reshape(
            CMP_STRIDE * LANES, 2 * NSA_KV_HEADS * CMP_HIDDEN).astype(BF16)
    pb = jnp.einsum("ck,ckh->ch", nsa_cmp_pos[l].reshape(2, -1), nsa_cmp_w1[l], precision=lax.Precision.HIGHEST)
    lw["pc_bias"] = jnp.broadcast_to(pb[:, None, :], (2, NSA_KV_HEADS, CMP_HIDDEN)).reshape(1, -1)
    lw["pc_w2"] = jnp.einsum("chd,cx,gy->cghxyd", nsa_cmp_w2[l], eye2, eye2).reshape(
        2 * NSA_KV_HEADS * CMP_HIDDEN, 2 * LANES).astype(BF16)
    lw["gla_gn"] = jnp.tile(gla_norm[l], GLA_HEADS)[None]
    hq = np.arange(GLA_QK) // GLA_DK
    hv = np.arange(GLA_V) // GLA_DV
    lw["gla_bm"] = jnp.asarray(hq[:, None] == hv[None, :], dtype=BF16)
    lw["gla_bmask"] = jnp.asarray(hv[:, None] == hq[None, :], dtype=F32)
    lw["gm_ws"] = gm_ws[l]
    lw["gm_bias"] = jnp.repeat(gm_b[l].T, GM_CH, axis=1)
    return lw


def _page_perm(page):
    cpp = page // CMP_STRIDE
    r = np.arange(2 * page)
    s_, rem = r // (2 * cpp), r % (2 * cpp)
    t = (rem // cpp) * page + CMP_STRIDE * (rem % cpp) + s_
    m = np.zeros((2 * page, 2 * page), np.float32)
    m[r, t] = 1.0
    return jnp.asarray(m, dtype=BF16)


def _cover_sample(past_len, ts, nsp):
    t_tot = past_len + ts
    n_c = (t_tot - CMP_BLOCK) // CMP_STRIDE + 1
    nc = past_len // CMP_STRIDE
    n_s = -(-t_tot // SEL_BLOCK)
    ci = np.arange(nc)[:, None]
    sj = np.arange(nsp)[None, :]
    cov = ((ci * CMP_STRIDE <= sj * SEL_BLOCK + SEL_BLOCK - 1) & (ci * CMP_STRIDE + CMP_BLOCK - 1 >= sj * SEL_BLOCK)
           & (ci < n_c) & (sj < n_s))
    emat = (np.arange(past_len)[None, :] // SEL_BLOCK) == np.arange(nsp)[:, None]
    return jnp.asarray(cov, dtype=BF16), jnp.asarray(emat, dtype=BF16)


def _rows_gr8(a, ts):
    bs = a.shape[0]
    a = a.reshape(bs, ts, NSA_KV_HEADS, NSA_REP, a.shape[-1]).transpose(0, 2, 3, 1, 4)
    return jnp.pad(a, ((0, 0), (0, 0), (0, 0), (0, 8 - ts), (0, 0)))


def _state_to_t(s):
    b = s.shape[0]
    eye = jnp.eye(GLA_HEADS, dtype=s.dtype)
    return jnp.einsum("bhkv,hg->bhvgk", s, eye).reshape(b, GLA_V, GLA_QK)


def _state_from_t(st):
    b = st.shape[0]
    s5 = st.reshape(b, GLA_HEADS, GLA_DV, GLA_HEADS, GLA_DK)
    d = jnp.stack([s5[:, h, :, h, :] for h in range(GLA_HEADS)], axis=1)
    return jnp.swapaxes(d, 2, 3)


def kernel(x_prompt, x_sample, cache_kv, cache_win_kv, state_gla, page_table, ln_gains, ffn_w_gate_up, ffn_w_down,
           w_in, w_out, nsa_qk_norm, nsa_cmp_pos, nsa_cmp_w1, nsa_cmp_w2, gla_gate_w, gla_gate_b, gla_norm, gm_ln,
           gm_ws, gm_b):
    depth = w_in.shape[0]
    bp, tp, _ = x_prompt.shape
    bs, ts, _ = x_sample.shape
    n_pages = page_table.shape[1]
    page = cache_kv.shape[2]
    past_len = n_pages * page
    tq = 8
    mp, ms = bp * tp, bs * tq
    nc = tp // CMP_STRIDE

    w_gu = ffn_w_gate_up.astype(BF16).reshape(depth * 2, D_MODEL, 2 * D_FF)
    w_d = ffn_w_down.astype(BF16).reshape(depth * 2, D_FF, D_MODEL)
    w_in_p = _pad_cols(w_in).astype(BF16)
    w_o = w_out.astype(BF16)
    cos_p, sin_p = _rope_tables(jnp.arange(tp))
    cos_s, sin_s = _rope_tables(past_len + jnp.arange(ms) % tq)
    cov = _cover_t(tp)
    tm_p = 512 if mp % 512 == 0 else Q_BLOCK
    tm_f = 1024 if mp % 1024 == 0 else tm_p
    tm_i = 512 if tp % 512 == 0 else Q_BLOCK
    tf = 1408

    nrow = NSA_HEADS * 8
    n_keep_s = cache_win_kv.shape[2]
    nsp = -(-(past_len // SEL_BLOCK + 1) // LANES) * LANES
    cov_s, emat = _cover_sample(past_len, ts, nsp)
    perm = _page_perm(page)
    eye2 = jnp.eye(NSA_KV_HEADS, dtype=F32)
    npg_c = min(32, n_pages)
    npg_a = min(32, n_pages)
    cache_t = jnp.transpose(cache_kv, (0, 1, 3, 4, 5, 2)).reshape(depth, cache_kv.shape[1], 4 * LANES, page)
    win_t = jnp.transpose(cache_win_kv, (0, 1, 3, 4, 5, 2)).reshape(depth, bs, 2 * LANES, n_keep_s)

    xp = x_prompt.reshape(mp, D_MODEL)
    xs = jnp.pad(x_sample, ((0, 0), (0, tq - ts), (0, 0))).reshape(ms, D_MODEL)
    real_row = (jnp.arange(tq) < ts).astype(F32)[None, :, None]
    kv_p, win_p, gla_p, kv_s, win_s, gla_s, gmv_s = [], [], [], [], [], [], []
    eye_b = jnp.eye(GM_CHUNK // tq, dtype=F32)
    for l in range(depth):
        lw = _layer_weights(l, ln_gains, w_in_p, nsa_qk_norm, nsa_cmp_pos, nsa_cmp_w1, nsa_cmp_w2, gla_gate_w,
                            gla_gate_b, gla_norm, gm_ln, gm_ws, gm_b)
        xp = _ffn(xp, lw["ln0"], w_gu, w_d, 2 * l, tm=tm_f, tf=tf)
        (newkv, newwin, _, qg, kg, la, vg, rs, u, vn, qs, qrs, gt, ksel_r, vsel_t, kwin_r, vwin_t, kcmp,
         vcmp) = _inproj(xp, lw, cos_p, sin_p, tm=tm_i, attn_layout=True, batch=bp, seq=tp)
        zk = kcmp.reshape(bp, nc, CMP_STRIDE * LANES)
        zv = vcmp.reshape(bp, nc, CMP_STRIDE * LANES)
        kc, vct = _compress(zk, zv, lw)
        oa = _nsa_prompt(qs, qrs, gt, kc, vct, cov, ksel_r.reshape(bp, tp, LANES), vsel_t,
                         kwin_r.reshape(bp, tp, LANES), vwin_t, batch=bp, seq=tp)
        r3 = lambda a: a.reshape(bp, tp, a.shape[-1])
        ob, st = _gla(r3(qg), r3(kg), r3(la), r3(vg), r3(rs), jnp.zeros((bp, GLA_V, GLA_QK), F32), lw,
                      batch=bp, seq=tp, c=GLA_CHUNK, bpb=1)
        xp = _outproj(xp, oa, ob.reshape(mp, GLA_V), u, vn, lw["gm_ws"], lw["gm_bias"], w_o, l, tm=tm_p)
        xp = _ffn(xp, lw["ln2"], w_gu, w_d, 2 * l + 1, tm=tm_f, tf=tf)
        kv_p.append(newkv.reshape(bp, 4, NSA_KV_HEADS, HEAD_DIM, tp).transpose(0, 4, 1, 2, 3))
        n_keep = min(WINDOW, tp)
        win_p.append(newwin[:, :, tp - n_keep:].reshape(bp, 2, NSA_KV_HEADS, HEAD_DIM, n_keep).transpose(0, 4, 1, 2, 3))
        gla_p.append(_state_from_t(st))

        xs = _ffn(xs, lw["ln0"], w_gu, w_d, 2 * l, tm=ms, tf=tf)
        (newkv, newwin, gates, qg, kg, la, vg, rs, u, vn, qn, qr) = _inproj(
            xs, lw, cos_s, sin_s, tm=ms, attn_layout=False, batch=bs, seq=ts)
        hid = _pcompress(cache_t, page_table, l, perm, lw["pc_wk"], lw["pc_wv"], npg=npg_c)
        kct, vcr = _ctail(hid, lw["pc_bias"], lw["pc_w2"], lw["gkc"], lw["sm"])
        scale = np.float32(HEAD_DIM ** -0.5)
        to_pad = lambda q: jnp.einsum("bgrqd,gx->bgrqxd", _rows_gr8(q.reshape(bs, tq, NSA_HEADS, HEAD_DIM) * scale, tq),
                                      eye2).reshape(bs, nrow, LANES).astype(BF16)
        gcol = _rows_gr8(gates[:, :3 * NSA_HEADS].reshape(bs, tq, NSA_HEADS, 3), tq).reshape(bs, nrow, 3)
        gcol = jnp.pad(gcol, ((0, 0), (0, 0), (0, LANES - 3)))
        newk_t = jnp.pad(newkv.reshape(bs, tq, -1).transpose(0, 2, 1), ((0, 0), (0, 0), (0, LANES - tq)))
        neww_c = newwin.reshape(bs, tq, -1).transpose(0, 2, 1)[:, :, :ts]
        neww_t = jnp.pad(neww_c, ((0, 0), (0, 0), (0, LANES - ts)))
        newwsh_t = jnp.pad(neww_c, ((0, 0), (0, 0), (LANES - ts, 0)))
        o64, win_o = _nsa_sample(cache_t, page_table, l, to_pad(qn), to_pad(qr), gcol, kct, vcr, cov_s, emat, win_t,
                                 newk_t, neww_t, newwsh_t, npg=npg_a, past_len=past_len, ts=ts)
        o6 = o64.reshape(bs, NSA_KV_HEADS, NSA_REP, tq, NSA_KV_HEADS, HEAD_DIM)
        oa = jnp.stack([o6[:, g, :, :, g, :] for g in range(NSA_KV_HEADS)], axis=1)
        oa = (oa.transpose(0, 3, 1, 2, 4).reshape(bs, tq, NSA_Q) * real_row).reshape(ms, NSA_Q)
        r8 = lambda a: a.reshape(bs, tq, a.shape[-1])
        ob8, st_s = _gla(r8(qg), r8(kg), r8(la) * real_row, r8(vg), r8(rs), _state_to_t(state_gla[l]), lw,
                         batch=bs, seq=tq, c=tq, bpb=4 if bs % 4 == 0 else 1)
        ob = ob8.reshape(ms, GLA_V)
        ws_s = jnp.einsum("gts,bc->gbtcs", gm_ws[l][:, :tq, :tq], eye_b).reshape(GM_GROUPS, GM_CHUNK, GM_CHUNK)
        bias_s = jnp.tile(lw["gm_bias"][:tq], (GM_CHUNK // tq, 1))
        xs = _outproj(xs, oa, ob, u, vn, ws_s, bias_s, w_o, l, tm=GM_CHUNK)
        xs = _ffn(xs, lw["ln2"], w_gu, w_d, 2 * l + 1, tm=ms, tf=tf)
        kv_s.append(newkv.reshape(bs, tq, 4, NSA_KV_HEADS, HEAD_DIM)[:, :ts])
        win_s.append(win_o.reshape(bs, 2, NSA_KV_HEADS, HEAD_DIM, n_keep_s).transpose(0, 4, 1, 2, 3))
        gla_s.append(_state_from_t(st_s))
        gmv_s.append(vn.reshape(bs, tq, GM_GROUPS, GM_CH)[:, :ts])
    return (xp.reshape(bp, tp, D_MODEL), xs.reshape(bs, tq, D_MODEL)[:, :ts], jnp.stack(kv_p), jnp.stack(win_p),
            jnp.stack(gla_p), jnp.stack(kv_s), jnp.stack(win_s), jnp.stack(gla_s), jnp.stack(gmv_s))
```

```python
import functools

import numpy as np
import jax
import jax.numpy as jnp
from jax import lax
from jax.experimental import pallas as pl
from jax.experimental.pallas import tpu as pltpu

F32 = jnp.float32
BF16 = jnp.bfloat16

D_MODEL = 1024
HEAD_DIM = 64
NSA_HEADS = 8
NSA_KV_HEADS = 2
NSA_REP = NSA_HEADS // NSA_KV_HEADS
CMP_STRIDE = 16
CMP_BLOCK = 2 * CMP_STRIDE
CMP_HIDDEN = 128
SEL_BLOCK = 64
N_SEL = 16
WINDOW = 512
Q_BLOCK = 128
SEL_BONUS = 1.0e4
GLA_HEADS = 4
GLA_DK = 32
GLA_DV = 64
GLA_GATE_RANK = 16
GLA_GATE_TEMP = 16.0
GLA_CHUNK = 64
GM_GROUPS = 4
GM_CH = 64
GM_CHUNK = 128
D_FF = 2816
ROPE_THETA = 10000.0
EPS = 1e-6

NSA_Q = NSA_HEADS * HEAD_DIM
NSA_KV = NSA_KV_HEADS * HEAD_DIM
GLA_QK = GLA_HEADS * GLA_DK
GLA_V = GLA_HEADS * GLA_DV
GM_W = GM_GROUPS * GM_CH
MIX_OUT = NSA_Q + GLA_V + GM_W
IN_SPLITS = (NSA_Q, 6 * NSA_KV, 3 * NSA_HEADS, GLA_QK, GLA_QK, GLA_V, GLA_GATE_RANK, GLA_V, GM_W, GM_W)
IN_PADDED = tuple(-(-s // 128) * 128 for s in IN_SPLITS)
IN_OFFS = tuple(int(v) for v in np.cumsum((0,) + IN_PADDED))
D_IN_PAD = IN_OFFS[-1]

LANES = 128
NEG_BIG = -1.0e30
VMEM_LIMIT = 56 * 1024 * 1024


def _cparams(sem):
    return pltpu.CompilerParams(dimension_semantics=sem, vmem_limit_bytes=VMEM_LIMIT)


def _gelu(x):
    c = np.float32(np.sqrt(2.0 / np.pi))
    return x * (0.5 * (1.0 + jnp.tanh(c * (x + 0.044715 * (x * x * x)))))


def _sigmoid(x):
    return 1.0 / (1.0 + jnp.exp(-x))


def _dot(a, b):
    return jnp.dot(a, b, preferred_element_type=F32)


def _dot_nt(a, b):
    return lax.dot_general(a, b, (((1,), (1,)), ((), ())), preferred_element_type=F32)


def _seg_mean_sq(x, sm):
    sq = x * x
    hi = sq.astype(BF16)
    lo = (sq - hi.astype(F32)).astype(BF16)
    outs = []
    for c in range(x.shape[1] // LANES):
        sl = slice(c * LANES, (c + 1) * LANES)
        outs.append(_dot(hi[:, sl], sm) + _dot(lo[:, sl], sm))
    return outs[0] if len(outs) == 1 else jnp.concatenate(outs, axis=1)


def _seg_rms(x, gain, sm):
    return x * lax.rsqrt(_seg_mean_sq(x, sm) + EPS) * gain


def _tile_lanes(a, w):
    n = w // a.shape[1]
    return a if n == 1 else jnp.concatenate([a] * n, axis=1)


def _rope(x, cos, sin_signed):
    w = x.shape[1]
    lane = lax.broadcasted_iota(jnp.int32, x.shape, 1)
    fwd = pltpu.roll(x, w - HEAD_DIM // 2, axis=1)
    bwd = pltpu.roll(x, HEAD_DIM // 2, axis=1)
    partner = jnp.where((lane % HEAD_DIM) < HEAD_DIM // 2, fwd, bwd)
    return x * _tile_lanes(cos, w) + partner * _tile_lanes(sin_signed, w)


def _ffn_kernel(x_ref, g_ref, wg_ref, wu_ref, wd_ref, o_ref, h_scr, acc_scr, *, nj):
    j = pl.program_id(1)

    @pl.when(j == 0)
    def _():
        x = x_ref[...]
        ms = jnp.mean(x * x, axis=-1, keepdims=True)
        h_scr[...] = (x * lax.rsqrt(ms + EPS) * g_ref[...]).astype(BF16)
        acc_scr[...] = jnp.zeros_like(acc_scr)

    h = h_scr[...]
    g = _dot(h, wg_ref[...])
    u = _dot(h, wu_ref[...])
    a = (g * _sigmoid(g)) * u
    acc_scr[...] += _dot(a.astype(BF16), wd_ref[...])

    @pl.when(j == nj - 1)
    def _():
        o_ref[...] = x_ref[...] + 0.5 * acc_scr[...]


def _ffn(x, gain, w_gu, w_d, li, *, tm, tf):
    m = x.shape[0]
    nj = D_FF // tf
    return pl.pallas_call(
        functools.partial(_ffn_kernel, nj=nj),
        out_shape=jax.ShapeDtypeStruct((m, D_MODEL), F32),
        grid=(m // tm, nj),
        in_specs=[
            pl.BlockSpec((tm, D_MODEL), lambda i, j: (i, 0)),
            pl.BlockSpec((1, D_MODEL), lambda i, j: (0, 0)),
            pl.BlockSpec((None, D_MODEL, tf), lambda i, j: (li, 0, j)),
            pl.BlockSpec((None, D_MODEL, tf), lambda i, j: (li, 0, j + nj)),
            pl.BlockSpec((None, tf, D_MODEL), lambda i, j: (li, j, 0)),
        ],
        out_specs=pl.BlockSpec((tm, D_MODEL), lambda i, j: (i, 0)),
        scratch_shapes=[pltpu.VMEM((tm, D_MODEL), BF16), pltpu.VMEM((tm, D_MODEL), F32)],
        compiler_params=_cparams(("parallel", "arbitrary")),
        name="ffn",
    )(x, gain, w_gu, w_gu, w_d)


def _group_padded(arr, h):
    c = arr[:, (h // 2) * LANES:(h // 2 + 1) * LANES]
    g = h // NSA_REP
    if (h % 2) != g:
        c = pltpu.roll(c, HEAD_DIM, axis=1)
    lane = lax.broadcasted_iota(jnp.int32, c.shape, 1)
    keep = (lane >= g * HEAD_DIM) & (lane < (g + 1) * HEAD_DIM)
    return jnp.where(keep, c, 0.0)


def _inproj_kernel(x_ref, ln_ref, w_ref, gq_ref, gks_ref, gkw_ref, cos_ref, sin_ref, gw_ref, gb_ref,
                   lng_ref, lnb_ref, sm_ref, *outs, tm, attn_layout):
    (newkv_ref, newwin_ref, gates_ref, qg_ref, kg_ref, la_ref, vg_ref, rs_ref, u_ref, vn_ref) = outs[:10]
    x = x_ref[...]
    ms = jnp.mean(x * x, axis=-1, keepdims=True)
    h = (x * lax.rsqrt(ms + EPS) * ln_ref[...]).astype(BF16)
    p = _dot(h, w_ref[...])
    sm = sm_ref[...]
    cos = cos_ref[...]
    sin = sin_ref[...]
    o = IN_OFFS

    def seg(i, a=0, b=None):
        b = IN_PADDED[i] if b is None else b
        return p[:, o[i] + a:o[i] + b]

    qn = _seg_rms(seg(0), gq_ref[...], sm)
    qr = _rope(qn, cos, sin)
    kv = [seg(1, LANES * j, LANES * (j + 1)) for j in range(6)]
    ksel = _rope(_seg_rms(kv[2], gks_ref[...], sm), cos, sin)
    kwin = _rope(_seg_rms(kv[4], gkw_ref[...], sm), cos, sin)
    vsel_t = kv[3].T
    vwin_t = kv[5].T
    if attn_layout:
        for j, a in enumerate((kv[0].T, kv[1].T, ksel.T, vsel_t)):
            newkv_ref[j * LANES:(j + 1) * LANES, :] = a
        newwin_ref[0:LANES, :] = kwin.T
        newwin_ref[LANES:2 * LANES, :] = vwin_t
    else:
        for j, a in enumerate((kv[0], kv[1], ksel, kv[3])):
            newkv_ref[:, j * LANES:(j + 1) * LANES] = a
        newwin_ref[:, 0:LANES] = kwin
        newwin_ref[:, LANES:2 * LANES] = kv[5]
    gates = _sigmoid(seg(2))
    gates_ref[...] = gates
    qg_ref[...] = seg(3) * np.float32(GLA_DK ** -0.5)
    kg_ref[...] = seg(4)
    vg_ref[...] = seg(5)
    logit = _dot(seg(6).astype(BF16), gw_ref[...]) + gb_ref[...]
    log_sig = jnp.minimum(logit, 0.0) - jnp.log1p(jnp.exp(-jnp.abs(logit)))
    la_ref[...] = log_sig * np.float32(np.log2(np.e) / GLA_GATE_TEMP)
    r = seg(7)
    rs_ref[...] = r * _sigmoid(r)
    u_ref[...] = _gelu(seg(8))
    v = _gelu(seg(9))
    mu = jnp.mean(v, axis=-1, keepdims=True)
    var = jnp.mean(jnp.square(v - mu), axis=-1, keepdims=True)
    vn_ref[...] = (v - mu) * lax.rsqrt(var + EPS) * lng_ref[...] + lnb_ref[...]

    if attn_layout:
        (qs_ref, qrs_ref, gt_ref, kselr_ref, vselt_ref, kwinr_ref, vwint_ref, kcmp_ref, vcmp_ref) = outs[10:]
        scale = np.float32(HEAD_DIM ** -0.5 * np.log2(np.e))
        qs = qn * scale
        qrs = qr * scale
        for hh in range(NSA_HEADS):
            a = _group_padded(qs, hh).astype(BF16)
            b = _group_padded(qrs, hh).astype(BF16)
            for rb in range(tm // Q_BLOCK):
                qs_ref[rb, hh] = a[rb * Q_BLOCK:(rb + 1) * Q_BLOCK]
                qrs_ref[rb, hh] = b[rb * Q_BLOCK:(rb + 1) * Q_BLOCK]
        for rb in range(tm // Q_BLOCK):
            gt_ref[rb] = gates[rb * Q_BLOCK:(rb + 1) * Q_BLOCK].T
        kselr_ref[...] = ksel.astype(BF16)
        vselt_ref[...] = vsel_t.astype(BF16)
        kwinr_ref[...] = kwin.astype(BF16)
        vwint_ref[...] = vwin_t.astype(BF16)
        kcmp_ref[...] = kv[0]
        vcmp_ref[...] = kv[1]
    else:
        qn_ref, qr_ref = outs[10:]
        qn_ref[...] = qn
        qr_ref[...] = qr


def _inproj(x, lw, cos_t, sin_t, *, tm, attn_layout, batch, seq):
    m = x.shape[0]
    nt = m // tm
    ntab = cos_t.shape[0] // tm
    row = lambda w: pl.BlockSpec((tm, w), lambda i: (i, 0))
    full = lambda a: pl.BlockSpec(a.shape, lambda i: (0,) * a.ndim)
    ins = [x, lw["ln1"], lw["w_in"], lw["gq"], lw["gks"], lw["gkw"], cos_t, sin_t, lw["gla_gw"], lw["gla_gb"],
           lw["gm_lng"], lw["gm_lnb"], lw["sm"]]
    in_specs = [row(D_MODEL), full(lw["ln1"]), full(lw["w_in"]), full(lw["gq"]), full(lw["gks"]), full(lw["gkw"]),
                pl.BlockSpec((tm, LANES), lambda i: (i % ntab, 0)), pl.BlockSpec((tm, LANES), lambda i: (i % ntab, 0)),
                full(lw["gla_gw"]), full(lw["gla_gb"]), full(lw["gm_lng"]), full(lw["gm_lnb"]), full(lw["sm"])]
    widths = [512, 256, 128, 128, 128, 128, 256, 256, 256, 256]
    out_shape = [jax.ShapeDtypeStruct((m, w), F32) for w in widths]
    out_specs = [row(w) for w in widths]
    if attn_layout:
        nqb = m // Q_BLOCK
        rpb = tm // Q_BLOCK
        tpb = seq // tm
        for j in range(2):
            out_shape[j] = jax.ShapeDtypeStruct((batch, widths[j], seq), F32)
            out_specs[j] = pl.BlockSpec((None, widths[j], tm), lambda i: (i // tpb, 0, i % tpb))
        out_shape += [jax.ShapeDtypeStruct((nqb, NSA_HEADS, Q_BLOCK, LANES), BF16)] * 2
        out_specs += [pl.BlockSpec((rpb, NSA_HEADS, Q_BLOCK, LANES), lambda i: (i, 0, 0, 0))] * 2
        out_shape += [jax.ShapeDtypeStruct((nqb, LANES, Q_BLOCK), F32)]
        out_specs += [pl.BlockSpec((rpb, LANES, Q_BLOCK), lambda i: (i, 0, 0))]
        rowmaj = (jax.ShapeDtypeStruct((m, LANES), BF16), row(LANES))
        trans = (jax.ShapeDtypeStruct((batch, LANES, seq), BF16),
                 pl.BlockSpec((None, LANES, tm), lambda i: (i // tpb, 0, i % tpb)))
        for sh, sp in (rowmaj, trans, rowmaj, trans):
            out_shape.append(sh)
            out_specs.append(sp)
        out_shape += [jax.ShapeDtypeStruct((m, LANES), F32)] * 2
        out_specs += [row(LANES)] * 2
    else:
        out_shape += [jax.ShapeDtypeStruct((m, NSA_Q), F32)] * 2
        out_specs += [row(NSA_Q)] * 2
    return pl.pallas_call(
        functools.partial(_inproj_kernel, tm=tm, attn_layout=attn_layout),
        out_shape=out_shape,
        grid=(nt,),
        in_specs=in_specs,
        out_specs=out_specs,
        compiler_params=_cparams(("parallel",)),
        name="inproj",
    )(*ins)


def _next_row(b):
    return pltpu.roll(b, b.shape[0] - 1, axis=0)


def _compress_kernel(zk_ref, zv_ref, pos_ref, wak_ref, wbk_ref, wav_ref, wbv_ref,
                     w2k_ref, w2v_ref, gkc_ref, sm_ref, kc_ref, vct_ref):
    pos = pos_ref[...]

    def one(z_ref, pa, pb, wa_ref, wb_ref, w2_ref):
        z = z_ref[...]
        a = _dot((z + pa).astype(BF16), wa_ref[...])
        b = _dot((z + pb).astype(BF16), wb_ref[...])
        return _dot(_gelu(a + _next_row(b)).astype(BF16), w2_ref[...])

    ck = one(zk_ref, pos[0:1], pos[1:2], wak_ref, wbk_ref, w2k_ref)
    cv = one(zv_ref, pos[2:3], pos[3:4], wav_ref, wbv_ref, w2v_ref)
    kc_ref[...] = _seg_rms(ck, gkc_ref[...], sm_ref[...]).astype(BF16)
    vct_ref[...] = cv.T.astype(BF16)


def _compress(zk, zv, lw):
    b, nc, kdim = zk.shape
    zspec = pl.BlockSpec((None, nc, kdim), lambda i: (i, 0, 0))
    full = lambda a: pl.BlockSpec(a.shape, lambda i: (0,) * a.ndim)
    ws = [lw["cmp_pos"], lw["cmp_wak"], lw["cmp_wbk"], lw["cmp_wav"], lw["cmp_wbv"], lw["cmp_w2k"], lw["cmp_w2v"],
          lw["gkc"], lw["sm"]]
    return pl.pallas_call(
        _compress_kernel,
        out_shape=[jax.ShapeDtypeStruct((b, nc, LANES), BF16), jax.ShapeDtypeStruct((b, LANES, nc), BF16)],
        grid=(b,),
        in_specs=[zspec] * 2 + [full(w) for w in ws],
        out_specs=[pl.BlockSpec((None, nc, LANES), lambda i: (i, 0, 0)),
                   pl.BlockSpec((None, LANES, nc), lambda i: (i, 0, 0))],
        compiler_params=_cparams(("parallel",)),
        name="nsa_compress",
    )(zk, zv, *ws)


def _nsa_kernel(qs_ref, qrs_ref, gt_ref, kc_ref, vct_ref, cov_ref, ksel_ref, vselt_ref, kwin_ref, vwint_ref,
                o_ref, score_scr, bias_scr, sa_scr, sb_scr, *, nc, ns, kb_keys):
    ib = pl.program_id(1)
    p0 = ib * Q_BLOCK
    nl = NSA_REP * Q_BLOCK
    ng = NSA_KV_HEADS
    t_row = p0 + lax.broadcasted_iota(jnp.int32, (1, nl), 1) % Q_BLOCK
    t_q = p0 + lax.broadcasted_iota(jnp.int32, (1, Q_BLOCK), 1)
    gt = gt_ref[...]
    n_top = min(N_SEL, ns)
    per_kb = kb_keys // SEL_BLOCK
    grows = [slice(g * HEAD_DIM, (g + 1) * HEAD_DIM) for g in range(ng)]

    def q_of(ref, g):
        return ref[NSA_REP * g:NSA_REP * (g + 1)].reshape(nl, LANES)

    ones_rows = 16

    def with_ones(vt):
        return jnp.concatenate([vt, jnp.ones((ones_rows, vt.shape[1]), BF16)], axis=0)

    def online(carry, s, vt):
        m, acc = carry
        m_new = jnp.maximum(m, jnp.max(s, axis=0, keepdims=True))
        p = jnp.exp2(s - m_new)
        acc = jnp.exp2(m - m_new) * acc + _dot(with_ones(vt), p.astype(BF16))
        return m_new, acc

    def normalised(acc):
        return acc[0:HEAD_DIM] * (1.0 / jnp.maximum(acc[HEAD_DIM:HEAD_DIM + 1], 1e-30))

    init1 = (jnp.full((1, nl), NEG_BIG, F32), jnp.zeros((HEAD_DIM + ones_rows, nl), F32))
    init = tuple(init1 for _ in range(ng))

    sj = lax.broadcasted_iota(jnp.int32, (ns, 1), 0)

    def cmp_branch(rows):
        outs = []
        for g in range(ng):
            s = _dot_nt(kc_ref[0:rows, :], q_of(qs_ref, g))
            ci = lax.broadcasted_iota(jnp.int32, (rows, 1), 0)
            cmask = (ci * CMP_STRIDE + (CMP_BLOCK - 1) <= t_row) & (ci < nc - 1)
            s = jnp.where(cmask, s, -jnp.inf)
            m = jnp.max(s, axis=0, keepdims=True)
            m = jnp.where(m == -jnp.inf, 0.0, m)
            e = jnp.exp2(s - m)
            l = jnp.sum(e, axis=0, keepdims=True)
            p = e * (1.0 / jnp.maximum(l, 1e-30))
            outs.append(_dot(vct_ref[grows[g], 0:rows], p.astype(BF16)))
            psum = p[:, 0:Q_BLOCK]
            for r in range(1, NSA_REP):
                psum = psum + p[:, r * Q_BLOCK:(r + 1) * Q_BLOCK]
            imp = _dot(cov_ref[:, 0:rows], psum.astype(BF16))
            bt = t_q // SEL_BLOCK
            allowed = sj * SEL_BLOCK <= t_q
            forced = (sj == 0) | (sj == bt) | (sj == bt - 1)
            score_scr[g] = jnp.where(allowed, imp + jnp.where(forced, np.float32(SEL_BONUS), 0.0), -jnp.inf)
        return tuple(outs)

    quarter = nc // 4
    if quarter % LANES == 0:
        vis = (p0 + Q_BLOCK - CMP_BLOCK) // CMP_STRIDE
        o_cmp = lax.cond(
            vis < 2 * quarter,
            lambda: lax.cond(vis < quarter, lambda: cmp_branch(quarter), lambda: cmp_branch(2 * quarter)),
            lambda: lax.cond(vis < 3 * quarter, lambda: cmp_branch(3 * quarter), lambda: cmp_branch(nc)))
    else:
        o_cmp = cmp_branch(nc)

    sjf = sj.astype(F32)
    bt_q = t_q // SEL_BLOCK
    pre = ((sj == 0) | (sj == bt_q) | (sj == bt_q - 1)) & (sj * SEL_BLOCK <= t_q)
    left = [jnp.where(pre, -jnp.inf, score_scr[g]) for g in range(ng)]
    bias = [jnp.where(pre, 0.0, -jnp.inf) for _ in range(ng)]
    for _ in range(max(n_top - 3, 0)):
        for g in range(ng):
            mx = jnp.max(left[g], axis=0, keepdims=True)
            idx = jnp.min(jnp.where(left[g] == mx, sjf, np.float32(ns)), axis=0, keepdims=True)
            hit = sjf == jnp.where(mx > -jnp.inf, idx, -1.0)
            left[g] = jnp.where(hit, -jnp.inf, left[g])
            bias[g] = jnp.where(hit, 0.0, bias[g])
    for g in range(ng):
        bias_scr[g] = bias[g]

    def sel_qk(kb, g, dst):
        k0 = pl.multiple_of(kb * kb_keys, kb_keys)
        dst[g] = _dot_nt(ksel_ref[pl.ds(k0, kb_keys), :], q_of(qrs_ref, g))

    def sel_step(carry_g, src, kb, g, causal):
        k0 = pl.multiple_of(kb * kb_keys, kb_keys)
        parts = []
        for i in range(per_kb):
            row = bias_scr[g, pl.ds(kb * per_kb + i, 1), :]
            sl = slice(i * SEL_BLOCK, (i + 1) * SEL_BLOCK)
            parts.append(src[g, sl, :] + jnp.concatenate([row] * NSA_REP, axis=1))
        s = jnp.concatenate(parts, axis=0)
        if causal:
            kpos = k0 + lax.broadcasted_iota(jnp.int32, (kb_keys, 1), 0)
            s = jnp.where(kpos <= t_row, s, -jnp.inf)
        return online(carry_g, s, vselt_ref[grows[g], pl.ds(k0, kb_keys)])

    def sel_pair(j, carry, causal):
        a = 2 * j
        for g in range(ng):
            sel_qk(a + 1, g, sb_scr)
        carry = tuple(sel_step(carry[g], sa_scr, a, g, causal) for g in range(ng))
        if not causal:
            for g in range(ng):
                sel_qk(a + 2, g, sa_scr)
        return tuple(sel_step(carry[g], sb_scr, a + 1, g, causal) for g in range(ng))

    n_kb = (p0 + Q_BLOCK + kb_keys - 1) // kb_keys
    n_pairs = (n_kb + 1) // 2
    for g in range(ng):
        sel_qk(0, g, sa_scr)
    carry = lax.fori_loop(0, n_pairs - 1, lambda j, c: sel_pair(j, c, False), init)
    a_last = 2 * (n_pairs - 1)
    carry = tuple(sel_step(carry[g], sa_scr, a_last, g, True) for g in range(ng))

    def second_block(c):
        for g in range(ng):
            sel_qk(a_last + 1, g, sb_scr)
        return tuple(sel_step(c[g], sb_scr, a_last + 1, g, True) for g in range(ng))

    carry = lax.cond(n_kb % 2 == 0, second_block, lambda c: c, carry)
    o_sel = [normalised(carry[g][1]) for g in range(ng)]

    wkeys = WINDOW + Q_BLOCK
    w0 = pl.multiple_of(jnp.maximum(p0 - WINDOW, 0), Q_BLOCK)
    diff = t_row - (w0 + lax.broadcasted_iota(jnp.int32, (wkeys, 1), 0))
    wmask = (diff >= 0) & (diff < WINDOW)
    kblk = kwin_ref[pl.ds(w0, wkeys), :]
    o_win = []
    for g in range(ng):
        s = jnp.where(wmask, _dot_nt(kblk, q_of(qrs_ref, g)), -jnp.inf)
        o_win.append(normalised(online(init1, s, vwint_ref[grows[g], pl.ds(w0, wkeys)])[1]))

    for g in range(ng):
        o_w = o_win[g]

        def gate_row(jb):
            return jnp.concatenate(
                [gt[(NSA_REP * g + r) * 3 + jb:(NSA_REP * g + r) * 3 + jb + 1, :] for r in range(NSA_REP)], axis=1)

        o = o_cmp[g] * gate_row(0) + o_sel[g] * gate_row(1) + o_w * gate_row(2)
        for pr in range(NSA_REP // 2):
            blk = jnp.concatenate([o[:, (2 * pr) * Q_BLOCK:(2 * pr + 1) * Q_BLOCK],
                                   o[:, (2 * pr + 1) * Q_BLOCK:(2 * pr + 2) * Q_BLOCK]], axis=0)
            c0 = g * NSA_REP * HEAD_DIM + pr * LANES
            o_ref[:, c0:c0 + LANES] = blk.T


def _nsa_prompt(qs, qrs, gt, kc, vct, cov, ksel_r, vsel_t, kwin_r, vwin_t, *, batch, seq):
    nb = seq // Q_BLOCK
    nc = kc.shape[1]
    ns = cov.shape[0]
    kb_keys = min(512, seq)
    per_b3 = lambda a: pl.BlockSpec((None,) + a.shape[1:], lambda b, i: (b, 0, 0))
    return pl.pallas_call(
        functools.partial(_nsa_kernel, nc=nc, ns=ns, kb_keys=kb_keys),
        out_shape=jax.ShapeDtypeStruct((batch * seq, NSA_Q), F32),
        grid=(batch, nb),
        in_specs=[
            pl.BlockSpec((None, NSA_HEADS, Q_BLOCK, LANES), lambda b, i: (b * nb + i, 0, 0, 0)),
            pl.BlockSpec((None, NSA_HEADS, Q_BLOCK, LANES), lambda b, i: (b * nb + i, 0, 0, 0)),
            pl.BlockSpec((None, LANES, Q_BLOCK), lambda b, i: (b * nb + i, 0, 0)),
            per_b3(kc), per_b3(vct),
            pl.BlockSpec(cov.shape, lambda b, i: (0, 0)),
            per_b3(ksel_r), per_b3(vsel_t), per_b3(kwin_r), per_b3(vwin_t),
        ],
        out_specs=pl.BlockSpec((Q_BLOCK, NSA_Q), lambda b, i: (b * nb + i, 0)),
        scratch_shapes=[pltpu.VMEM((NSA_KV_HEADS, ns, Q_BLOCK), F32), pltpu.VMEM((NSA_KV_HEADS, ns, Q_BLOCK), F32),
                        pltpu.VMEM((NSA_KV_HEADS, kb_keys, NSA_REP * Q_BLOCK), F32),
                        pltpu.VMEM((NSA_KV_HEADS, kb_keys, NSA_REP * Q_BLOCK), F32)],
        compiler_params=_cparams(("parallel", "arbitrary")),
        name="nsa_prompt",
    )(qs, qrs, gt, kc, vct, cov, ksel_r, vsel_t, kwin_r, vwin_t)


def _gla_rows(c):
    offs, n = [], 0
    for s in range(c):
        t0 = (s // 8) * 8
        offs.append((n, t0))
        n += c - t0
    return offs, n


def _gla_kernel(q_ref, k_ref, la_ref, v_ref, rs_ref, gn_ref, bm_ref, bmask_ref, sm_ref, s0_ref,
                o_ref, sout_ref, s_scr, prod_scr, *, c, nchunks, bpb):
    ci = pl.program_id(1)

    @pl.when(ci == 0)
    def _():
        s_scr[...] = s0_ref[...]
        prod_scr[...] = jnp.zeros_like(prod_scr)

    for bb in range(bpb):
        _gla_one(q_ref.at[bb], k_ref.at[bb], la_ref.at[bb], v_ref.at[bb], rs_ref.at[bb], gn_ref, bm_ref, bmask_ref,
                 sm_ref, o_ref.at[bb], sout_ref.at[bb], s_scr.at[bb], prod_scr.at[bb],
                 c=c, last=ci == nchunks - 1)


def _gla_one(q_ref, k_ref, la_ref, v_ref, rs_ref, gn_ref, bm_ref, bmask_ref, sm_ref, o_ref, sout_ref,
             s_scr, prod_scr, *, c, last):
    q = q_ref[...]
    k = k_ref[...]
    v = v_ref[...]
    la = la_ref[...]
    tt = lax.broadcasted_iota(jnp.int32, (c, 1), 0)
    b = la
    sh = 1
    while sh < c:
        b = b + jnp.where(tt >= sh, pltpu.roll(b, sh, axis=0), 0.0)
        sh *= 2
    state = s_scr[...]
    inter = _dot_nt((q * jnp.exp2(b)).astype(BF16), state.astype(BF16))

    offs, npack = _gla_rows(c)
    acc = [inter[tb:min(tb + 8, c)] for tb in range(0, c, 8)]
    gsz = min(16, c)
    for g0 in range(0, c, gsz):
        for s in range(g0, g0 + gsz):
            r0, t0 = offs[s]
            d = b[t0:] - b[s:s + 1]
            head = jnp.where(tt[t0:t0 + 8] >= s, d[0:8], -jnp.inf)
            d = head if c - t0 <= 8 else jnp.concatenate([head, d[8:]], axis=0)
            prod_scr[r0:r0 + c - t0, :] = (q[t0:] * k[s:s + 1] * jnp.exp2(d)).astype(BF16)
        lo = offs[g0][0]
        hi = offs[g0 + gsz][0] if g0 + gsz < c else npack
        res = _dot(prod_scr[lo:hi, :], bm_ref[...])
        for s in range(g0, g0 + gsz):
            r0, t0 = offs[s]
            for tb in range(t0, c, 8):
                rr = r0 - lo + tb - t0
                acc[tb // 8] = acc[tb // 8] + res[rr:rr + min(8, c - tb)] * v[s:s + 1]
    o = acc[0] if len(acc) == 1 else jnp.concatenate(acc, axis=0)

    bl = b[c - 1:c]
    kd = (k * jnp.exp2(bl - b)).astype(BF16)
    if c >= 16:
        upd = _dot(v.T.astype(BF16), kd)
    else:
        upd = jnp.dot(v.T, kd.astype(F32), preferred_element_type=F32)
    new_state = jnp.exp2(bl) * state + upd * bmask_ref[...]
    s_scr[...] = new_state

    o_ref[...] = _seg_rms(o, gn_ref[...], sm_ref[...]) * rs_ref[...]

    @pl.when(last)
    def _():
        sout_ref[...] = new_state


def _gla(qg, kg, la, vg, rs, s0_bd, lw, *, batch, seq, c, bpb):
    nchunks = seq // c
    _, npack = _gla_rows(c)
    npad = -(-npack // 16) * 16
    blk = lambda w: pl.BlockSpec((bpb, c, w), lambda b, i: (b, i, 0))
    full = lambda a: pl.BlockSpec(a.shape, lambda b, i: (0,) * a.ndim)
    consts = [lw["gla_gn"], lw["gla_bm"], lw["gla_bmask"], lw["sm"]]
    st = pl.BlockSpec((bpb, GLA_V, GLA_QK), lambda b, i: (b, 0, 0))
    return pl.pallas_call(
        functools.partial(_gla_kernel, c=c, nchunks=nchunks, bpb=bpb),
        out_shape=[jax.ShapeDtypeStruct((batch, seq, GLA_V), F32), jax.ShapeDtypeStruct((batch, GLA_V, GLA_QK), F32)],
        grid=(batch // bpb, nchunks),
        in_specs=[blk(GLA_QK), blk(GLA_QK), blk(GLA_QK), blk(GLA_V), blk(GLA_V)] + [full(a) for a in consts] + [st],
        out_specs=[blk(GLA_V), st],
        scratch_shapes=[pltpu.VMEM((bpb, GLA_V, GLA_QK), F32), pltpu.VMEM((bpb, npad, GLA_QK), BF16)],
        compiler_params=_cparams(("parallel", "arbitrary")),
        name="gla",
    )(qg, kg, la, vg, rs, *consts, s0_bd)


def _outproj_kernel(x_ref, oa_ref, ob_ref, u_ref, vn_ref, ws_ref, bias_ref, wo_ref, o_ref, *, tm):
    lane = lax.broadcasted_iota(jnp.int32, (GM_CHUNK, GM_W), 1)
    tri = (lax.broadcasted_iota(jnp.int32, (GM_CHUNK, GM_CHUNK), 0)
           >= lax.broadcasted_iota(jnp.int32, (GM_CHUNK, GM_CHUNK), 1))
    zs = []
    for cb in range(tm // GM_CHUNK):
        vn = vn_ref[cb * GM_CHUNK:(cb + 1) * GM_CHUNK, :]
        z = bias_ref[...]
        for g in range(GM_GROUPS):
            wm = jnp.where(tri, ws_ref[g], 0.0).astype(BF16)
            vg = jnp.where((lane >= g * GM_CH) & (lane < (g + 1) * GM_CH), vn, 0.0).astype(BF16)
            z = z + _dot(wm, vg)
        zs.append(z)
    z = zs[0] if len(zs) == 1 else jnp.concatenate(zs, axis=0)
    oc = u_ref[...] * z
    y = _dot(oa_ref[...].astype(BF16), wo_ref[0:NSA_Q, :])
    y = y + _dot(ob_ref[...].astype(BF16), wo_ref[NSA_Q:NSA_Q + GLA_V, :])
    y = y + _dot(oc.astype(BF16), wo_ref[NSA_Q + GLA_V:MIX_OUT, :])
    o_ref[...] = x_ref[...] + y


def _outproj(x, oa, ob, u, vn, ws, bias, wo, li, *, tm):
    m = x.shape[0]
    row = lambda w: pl.BlockSpec((tm, w), lambda i: (i, 0))
    return pl.pallas_call(
        functools.partial(_outproj_kernel, tm=tm),
        out_shape=jax.ShapeDtypeStruct((m, D_MODEL), F32),
        grid=(m // tm,),
        in_specs=[row(D_MODEL), row(NSA_Q), row(GLA_V), row(GM_W), row(GM_W),
                  pl.BlockSpec(ws.shape, lambda i: (0, 0, 0)), pl.BlockSpec(bias.shape, lambda i: (0, 0)),
                  pl.BlockSpec((None, MIX_OUT, D_MODEL), lambda i: (li, 0, 0))],
        out_specs=row(D_MODEL),
        compiler_params=_cparams(("parallel",)),
        name="outproj",
    )(x, oa, ob, u, vn, ws, bias, wo)


def _pcompress_kernel(pt_ref, *refs, npg):
    pages = refs[:npg]
    perm_ref, wk_ref, wv_ref, out_ref, zk_scr, zv_scr = refs[npg:npg + 6]
    perm = perm_ref[...]
    for pp in range(npg // 2):
        xt = jnp.concatenate([pages[2 * pp][...], pages[2 * pp + 1][...]], axis=1).astype(BF16)
        y = _dot_nt(perm, xt).astype(BF16)
        for s in range(CMP_STRIDE):
            rows = slice(pp * 16, (pp + 1) * 16)
            zk_scr[rows, s * LANES:(s + 1) * LANES] = y[s * 16:(s + 1) * 16, 0:LANES]
            zv_scr[rows, s * LANES:(s + 1) * LANES] = y[s * 16:(s + 1) * 16, LANES:2 * LANES]
    q = out_ref.shape[1] // 4
    rk = _dot(zk_scr[...], wk_ref[...])
    rv = _dot(zv_scr[...], wv_ref[...])
    out_ref[:, 0:q] = rk[:, 0:q]
    out_ref[:, q:2 * q] = rv[:, 0:q]
    out_ref[:, 2 * q:3 * q] = rk[:, q:2 * q]
    out_ref[:, 3 * q:4 * q] = rv[:, q:2 * q]


def _pcompress(cache_t, page_table, li, perm, wk, wv, *, npg):
    bs, n_pages = page_table.shape
    page = cache_t.shape[-1]
    nc = n_pages * page // CMP_STRIDE
    steps = n_pages // npg
    cpp = page // CMP_STRIDE

    def page_spec(k):
        return pl.BlockSpec((None, None, 2 * LANES, page), lambda b, h, pt: (li, pt[b, h * npg + k], 0, 0))

    grid_spec = pltpu.PrefetchScalarGridSpec(
        num_scalar_prefetch=1,
        grid=(bs, steps),
        in_specs=[page_spec(k) for k in range(npg)] + [
            pl.BlockSpec(perm.shape, lambda b, h, pt: (0, 0)), pl.BlockSpec(wk.shape, lambda b, h, pt: (0, 0)),
            pl.BlockSpec(wv.shape, lambda b, h, pt: (0, 0))],
        out_specs=pl.BlockSpec((None, npg * cpp, 2 * wk.shape[1]), lambda b, h, pt: (b, h, 0)),
        scratch_shapes=[pltpu.VMEM((npg * cpp, wk.shape[0]), BF16)] * 2,
    )
    return pl.pallas_call(
        functools.partial(_pcompress_kernel, npg=npg),
        out_shape=jax.ShapeDtypeStruct((bs, nc, 2 * wk.shape[1]), F32),
        grid_spec=grid_spec,
        compiler_params=_cparams(("parallel", "arbitrary")),
        name="nsa_page_compress",
    )(page_table, *([cache_t] * npg), perm, wk, wv)


def _ctail_kernel(a_ref, b_ref, bias_ref, w2_ref, gkc_ref, sm_ref, kct_ref, vcr_ref):
    h = a_ref[...] + _next_row(b_ref[...]) + bias_ref[...]
    out = _dot(_gelu(h).astype(BF16), w2_ref[...])
    kc = _seg_rms(out[:, 0:LANES], gkc_ref[...], sm_ref[...])
    kct_ref[...] = kc.T.astype(BF16)
    vcr_ref[...] = out[:, LANES:2 * LANES].astype(BF16)


def _ctail(hid_ab, bias, w2, gkc, sm):
    bs, nc, w2x = hid_ab.shape
    half = w2x // 2
    full = lambda a: pl.BlockSpec(a.shape, lambda b: (0,) * a.ndim)
    return pl.pallas_call(
        _ctail_kernel,
        out_shape=[jax.ShapeDtypeStruct((bs, LANES, nc), BF16), jax.ShapeDtypeStruct((bs, nc, LANES), BF16)],
        grid=(bs,),
        in_specs=[pl.BlockSpec((None, nc, half), lambda b: (b, 0, 0)), pl.BlockSpec((None, nc, half), lambda b: (b, 0, 1)),
                  full(bias), full(w2), full(gkc), full(sm)],
        out_specs=[pl.BlockSpec((None, LANES, nc), lambda b: (b, 0, 0)), pl.BlockSpec((None, nc, LANES), lambda b: (b, 0, 0))],
        compiler_params=_cparams(("parallel",)),
        name="nsa_compress_tail",
    )(hid_ab, hid_ab, bias, w2, gkc, sm)


def _nsa_sample_kernel(pt_ref, qp_ref, qrp_ref, gcol_ref, kct_ref, vcr_ref, cov_ref, e_ref, *refs,
                       npg, nsteps, nc, ns_tot, past_len, ts):
    pages = refs[:npg]
    (wint_ref, newk_ref, neww_ref, newwsh_ref, o_ref, winout_ref,
     bias_scr, m_scr, l_scr, acc_scr, oc_scr, ow_scr) = refs[npg:]
    step = pl.program_id(1)
    nrow = NSA_KV_HEADS * NSA_REP * 8
    q8 = lax.broadcasted_iota(jnp.int32, (nrow, 1), 0) % 8
    t_row = past_len + q8
    qrp = qrp_ref[...]
    n_keep = wint_ref.shape[-1]
    n_top = min(N_SEL, ns_tot)
    nsp = bias_scr.shape[1]

    def expand_rows(a):
        return jnp.concatenate([a[0:8]] * NSA_REP + [a[8:16]] * NSA_REP, axis=0)

    @pl.when(step == 0)
    def _():
        s = _dot(qp_ref[...], kct_ref[...])
        ci = lax.broadcasted_iota(jnp.int32, (1, nc), 1)
        cmask = (ci * CMP_STRIDE + (CMP_BLOCK - 1) <= t_row) & (ci < nc - 1)
        s = jnp.where(cmask, s, -jnp.inf)
        m = jnp.max(s, axis=1, keepdims=True)
        m = jnp.where(m == -jnp.inf, 0.0, m)
        e = jnp.exp(s - m)
        p = e * (1.0 / jnp.maximum(jnp.sum(e, axis=1, keepdims=True), 1e-30))
        oc_scr[...] = _dot(p.astype(BF16), vcr_ref[...])
        sjl = lax.broadcasted_iota(jnp.int32, (1, nsp), 1)
        tq = past_len + lax.broadcasted_iota(jnp.int32, (8, 1), 0)
        bt = tq // SEL_BLOCK
        allowed = (sjl * SEL_BLOCK <= tq) & (sjl < ns_tot)
        forced = (sjl == 0) | (sjl == bt) | (sjl == bt - 1)
        for g in range(NSA_KV_HEADS):
            base = g * NSA_REP * 8
            psum = p[base:base + 8]
            for r in range(1, NSA_REP):
                psum = psum + p[base + r * 8:base + (r + 1) * 8]
            imp = _dot(psum.astype(BF16), cov_ref[...])
            score = jnp.where(allowed, imp + jnp.where(forced, np.float32(SEL_BONUS), 0.0), -jnp.inf)
            rank = jnp.zeros((8, nsp), jnp.int32)
            for k in range(ns_tot):
                col = score[:, k:k + 1]
                later = jnp.where(sjl > k, 1, 0)
                rank = rank + jnp.where(col > score, 1, 0) + jnp.where(col == score, later, 0)
            keep = (rank < n_top) & (score > -jnp.inf)
            bias_scr[g * 8:(g + 1) * 8, :] = jnp.where(keep, 0.0, -jnp.inf)
        m_scr[...] = jnp.full(m_scr.shape, NEG_BIG, F32)
        l_scr[...] = jnp.zeros(l_scr.shape, F32)
        acc_scr[...] = jnp.zeros(acc_scr.shape, F32)

        lane_w = lax.broadcasted_iota(jnp.int32, (1, n_keep), 1)
        diff = t_row - (past_len - n_keep + lane_w)
        s_w = jnp.where((diff >= 0) & (diff < WINDOW), _dot(qrp, wint_ref[0:LANES, :].astype(BF16)), -jnp.inf)
        lane_n = lax.broadcasted_iota(jnp.int32, (1, LANES), 1)
        diff_n = t_row - (past_len + lane_n)
        s_n = jnp.where((lane_n < ts) & (diff_n >= 0) & (diff_n < WINDOW),
                        _dot(qrp, neww_ref[0:LANES, :].astype(BF16)), -jnp.inf)
        sw = jnp.concatenate([s_w, s_n], axis=1)
        mw = jnp.max(sw, axis=1, keepdims=True)
        mw = jnp.where(mw == -jnp.inf, 0.0, mw)
        ew = jnp.exp(sw - mw)
        pw = ew * (1.0 / jnp.maximum(jnp.sum(ew, axis=1, keepdims=True), 1e-30))
        ow_scr[...] = (_dot_nt(pw[:, 0:n_keep].astype(BF16), wint_ref[LANES:2 * LANES, :].astype(BF16))
                       + _dot_nt(pw[:, n_keep:].astype(BF16), neww_ref[LANES:2 * LANES, :].astype(BF16)))
        rolled = pltpu.roll(wint_ref[...], n_keep - ts, axis=1)
        lane_o = lax.broadcasted_iota(jnp.int32, (2 * LANES, LANES), 1)
        winout_ref[:, 0:n_keep - LANES] = rolled[:, 0:n_keep - LANES]
        winout_ref[:, n_keep - LANES:n_keep] = jnp.where(lane_o >= LANES - ts, newwsh_ref[...],
                                                         rolled[:, n_keep - LANES:n_keep])

    def online(s, vt):
        m_old = m_scr[:, 0:1]
        m_new = jnp.maximum(m_old, jnp.max(s, axis=1, keepdims=True))
        p = jnp.exp(s - m_new)
        alpha = jnp.exp(m_old - m_new)
        l_new = alpha * l_scr[:, 0:1] + jnp.sum(p, axis=1, keepdims=True)
        acc_scr[...] = alpha * acc_scr[...] + _dot_nt(p.astype(BF16), vt)
        m_scr[...] = jnp.broadcast_to(m_new, m_scr.shape)
        l_scr[...] = jnp.broadcast_to(l_new, l_scr.shape)

    sel01 = jnp.where(bias_scr[...] == 0.0, 1.0, 0.0).astype(BF16)
    bexp = _dot(sel01, e_ref[...])
    bias = expand_rows(jnp.where(bexp > 0.5, 0.0, -jnp.inf))
    kt = jnp.concatenate([pg[0:LANES, :] for pg in pages], axis=1).astype(BF16)
    vt = jnp.concatenate([pg[LANES:2 * LANES, :] for pg in pages], axis=1).astype(BF16)
    online(_dot(qrp, kt) + bias, vt)

    @pl.when(step == nsteps - 1)
    def _():
        lane_n = lax.broadcasted_iota(jnp.int32, (1, LANES), 1)
        bcol = expand_rows(bias_scr[:, ns_tot - 1:ns_tot])
        ok = (lane_n < ts) & (past_len + lane_n <= t_row)
        s_n = jnp.where(ok, _dot(qrp, newk_ref[2 * LANES:3 * LANES, :].astype(BF16)) + bcol, -jnp.inf)
        online(s_n, newk_ref[3 * LANES:4 * LANES, :].astype(BF16))
        o_s = acc_scr[...] * (1.0 / jnp.maximum(l_scr[:, 0:1], 1e-30))
        gc = gcol_ref[...]
        o_ref[...] = oc_scr[...] * gc[:, 0:1] + o_s * gc[:, 1:2] + ow_scr[...] * gc[:, 2:3]


def _nsa_sample(cache_t, page_table, li, qp, qrp, gcol, kct, vcr, cov_s, emat, win_t, newk_t, neww_t, newwsh_t, *,
                npg, past_len, ts):
    bs, n_pages = page_table.shape
    page = cache_t.shape[-1]
    nsteps = n_pages // npg
    nc = kct.shape[-1]
    ns_tot = -(-(past_len + ts) // SEL_BLOCK)
    nsp = cov_s.shape[1]
    n_keep = win_t.shape[-1]
    nrow = qp.shape[1]
    per_b = lambda a: pl.BlockSpec((None,) + a.shape[1:], lambda b, h, pt: (b,) + (0,) * (a.ndim - 1))

    def page_spec(k):
        return pl.BlockSpec((None, None, 2 * LANES, page), lambda b, h, pt: (li, pt[b, h * npg + k], 1, 0))

    grid_spec = pltpu.PrefetchScalarGridSpec(
        num_scalar_prefetch=1,
        grid=(bs, nsteps),
        in_specs=[per_b(qp), per_b(qrp), per_b(gcol), per_b(kct), per_b(vcr),
                  pl.BlockSpec(cov_s.shape, lambda b, h, pt: (0, 0)),
                  pl.BlockSpec((nsp, npg * page), lambda b, h, pt: (0, h))]
                 + [page_spec(k) for k in range(npg)]
                 + [pl.BlockSpec((None, None, 2 * LANES, n_keep), lambda b, h, pt: (li, b, 0, 0)),
                    per_b(newk_t), per_b(neww_t), per_b(newwsh_t)],
        out_specs=[pl.BlockSpec((None, nrow, LANES), lambda b, h, pt: (b, 0, 0)),
                   pl.BlockSpec((None, 2 * LANES, n_keep), lambda b, h, pt: (b, 0, 0))],
        scratch_shapes=[pltpu.VMEM((2 * 8, nsp), F32)] + [pltpu.VMEM((nrow, LANES), F32)] * 5,
    )
    return pl.pallas_call(
        functools.partial(_nsa_sample_kernel, npg=npg, nsteps=nsteps, nc=nc, ns_tot=ns_tot, past_len=past_len, ts=ts),
        out_shape=[jax.ShapeDtypeStruct((bs, nrow, LANES), F32), jax.ShapeDtypeStruct((bs, 2 * LANES, n_keep), F32)],
        grid_spec=grid_spec,
        compiler_params=_cparams(("parallel", "arbitrary")),
        name="nsa_sample",
    )(page_table, qp, qrp, gcol, kct, vcr, cov_s, emat, *([cache_t] * npg), win_t, newk_t, neww_t, newwsh_t)


def _pad_cols(w):
    cuts = np.cumsum((0,) + IN_SPLITS)
    parts = []
    for i, (n, p) in enumerate(zip(IN_SPLITS, IN_PADDED)):
        seg = w[..., cuts[i]:cuts[i] + n]
        if p != n:
            seg = jnp.pad(seg, [(0, 0)] * (w.ndim - 1) + [(0, p - n)])
        parts.append(seg)
    return jnp.concatenate(parts, axis=-1)


def _rope_tables(pos):
    half = HEAD_DIM // 2
    inv = 1.0 / (ROPE_THETA ** (jnp.arange(half, dtype=F32) * (2.0 / HEAD_DIM)))
    ang = pos.astype(F32)[:, None] * inv[None, :]
    cos = jnp.cos(ang)
    sin = jnp.sin(ang)
    cos_f = jnp.concatenate([cos, cos, cos, cos], axis=1)
    sin_f = jnp.concatenate([-sin, sin, -sin, sin], axis=1)
    return cos_f, sin_f


def _cover_t(seq):
    nc = seq // CMP_STRIDE
    ns = seq // SEL_BLOCK
    ci = np.arange(nc)[None, :]
    sj = np.arange(ns)[:, None]
    cov = ((ci * CMP_STRIDE <= sj * SEL_BLOCK + SEL_BLOCK - 1)
           & (ci * CMP_STRIDE + CMP_BLOCK - 1 >= sj * SEL_BLOCK) & (ci < nc - 1))
    return jnp.asarray(cov, dtype=BF16)


def _layer_weights(l, ln_gains, w_in_p, nsa_qk_norm, nsa_cmp_pos, nsa_cmp_w1, nsa_cmp_w2, gla_gate_w, gla_gate_b,
                   gla_norm, gm_ln, gm_ws, gm_b):
    eye2 = jnp.eye(NSA_KV_HEADS, dtype=F32)
    lw = {"ln0": ln_gains[l, 0][None], "ln1": ln_gains[l, 1][None], "ln2": ln_gains[l, 2][None], "w_in": w_in_p[l]}
    lw["gq"] = jnp.tile(nsa_qk_norm[l, 0], NSA_HEADS)[None]
    lw["gkc"] = jnp.tile(nsa_qk_norm[l, 1], NSA_KV_HEADS)[None]
    lw["gks"] = jnp.tile(nsa_qk_norm[l, 2], NSA_KV_HEADS)[None]
    lw["gkw"] = jnp.tile(nsa_qk_norm[l, 3], NSA_KV_HEADS)[None]
    seg = (np.arange(LANES)[:, None] // HEAD_DIM) == (np.arange(LANES)[None, :] // HEAD_DIM)
    lw["sm"] = jnp.asarray(seg * (1.0 / HEAD_DIM), dtype=BF16)
    lw["gla_gw"] = jnp.pad(gla_gate_w[l], ((0, LANES - GLA_GATE_RANK), (0, 0))).astype(BF16)
    lw["gla_gb"] = gla_gate_b[l][None]
    lw["gm_lng"] = gm_ln[l, 0][None]
    lw["gm_lnb"] = gm_ln[l, 1][None]
    pos_rows = []
    for c, nm in ((0, "k"), (1, "v")):
        w1 = nsa_cmp_w1[l, c].reshape(CMP_BLOCK, HEAD_DIM, CMP_HIDDEN)
        for half, tag in ((w1[:CMP_STRIDE], "a"), (w1[CMP_STRIDE:], "b")):
            wx = jnp.einsum("sdh,pg->spdgh", half, eye2)
            lw["cmp_w" + tag + nm] = wx.reshape(CMP_STRIDE * LANES, NSA_KV_HEADS * CMP_HIDDEN).astype(BF16)
        lw["cmp_w2" + nm] = jnp.einsum("hd,pg->phgd", nsa_cmp_w2[l, c], eye2).reshape(
            NSA_KV_HEADS * CMP_HIDDEN, LANES).astype(BF16)
        pe = nsa_cmp_pos[l, c]
        for half in (pe[:CMP_STRIDE], pe[CMP_STRIDE:]):
            pos_rows.append(jnp.broadcast_to(half[:, None, :], (CMP_STRIDE, NSA_KV_HEADS, HEAD_DIM)).reshape(-1))
    lw["cmp_pos"] = jnp.stack(pos_rows)
    w1ab = nsa_cmp_w1[l].reshape(2, 2, CMP_STRIDE, HEAD_DIM, CMP_HIDDEN)
    for c, nm in ((0, "k"), (1, "v")):
        lw["pc_w" + nm] = jnp.einsum("asdh,gy->sgdayh", w1ab[c], eye2).reshape(
            CMP_STRIDE * LANES, 2 * NSA_KV_HEADS * CMP_HIDDEN).astype(BF16)
    pb = jnp.einsum("ck,ckh->ch", nsa_cmp_pos[l].reshape(2, -1), nsa_cmp_w1[l], precision=lax.Precision.HIGHEST)
    lw["pc_bias"] = jnp.broadcast_to(pb[:, None, :], (2, NSA_KV_HEADS, CMP_HIDDEN)).reshape(1, -1)
    lw["pc_w2"] = jnp.einsum("chd,cx,gy->cghxyd", nsa_cmp_w2[l], eye2, eye2).reshape(
        2 * NSA_KV_HEADS * CMP_HIDDEN, 2 * LANES).astype(BF16)
    lw["gla_gn"] = jnp.tile(gla_norm[l], GLA_HEADS)[None]
    hq = np.arange(GLA_QK) // GLA_DK
    hv = np.arange(GLA_V) // GLA_DV
    lw["gla_bm"] = jnp.asarray(hq[:, None] == hv[None, :], dtype=BF16)
    lw["gla_bmask"] = jnp.asarray(hv[:, None] == hq[None, :], dtype=F32)
    lw["gm_ws"] = gm_ws[l]
    lw["gm_bias"] = jnp.repeat(gm_b[l].T, GM_CH, axis=1)
    return lw


def _page_perm(page):
    cpp = page // CMP_STRIDE
    r = np.arange(2 * page)
    s_, rem = r // (2 * cpp), r % (2 * cpp)
    t = (rem // cpp) * page + CMP_STRIDE * (rem % cpp) + s_
    m = np.zeros((2 * page, 2 * page), np.float32)
    m[r, t] = 1.0
    return jnp.asarray(m, dtype=BF16)


def _cover_sample(past_len, ts, nsp):
    t_tot = past_len + ts
    n_c = (t_tot - CMP_BLOCK) // CMP_STRIDE + 1
    nc = past_len // CMP_STRIDE
    n_s = -(-t_tot // SEL_BLOCK)
    ci = np.arange(nc)[:, None]
    sj = np.arange(nsp)[None, :]
    cov = ((ci * CMP_STRIDE <= sj * SEL_BLOCK + SEL_BLOCK - 1) & (ci * CMP_STRIDE + CMP_BLOCK - 1 >= sj * SEL_BLOCK)
           & (ci < n_c) & (sj < n_s))
    emat = (np.arange(past_len)[None, :] // SEL_BLOCK) == np.arange(nsp)[:, None]
    return jnp.asarray(cov, dtype=BF16), jnp.asarray(emat, dtype=BF16)


def _rows_gr8(a, ts):
    bs = a.shape[0]
    a = a.reshape(bs, ts, NSA_KV_HEADS, NSA_REP, a.shape[-1]).transpose(0, 2, 3, 1, 4)
    return jnp.pad(a, ((0, 0), (0, 0), (0, 0), (0, 8 - ts), (0, 0)))


def _state_to_t(s):
    b = s.shape[0]
    eye = jnp.eye(GLA_HEADS, dtype=s.dtype)
    return jnp.einsum("bhkv,hg->bhvgk", s, eye).reshape(b, GLA_V, GLA_QK)


def _state_from_t(st):
    b = st.shape[0]
    s5 = st.reshape(b, GLA_HEADS, GLA_DV, GLA_HEADS, GLA_DK)
    d = jnp.stack([s5[:, h, :, h, :] for h in range(GLA_HEADS)], axis=1)
    return jnp.swapaxes(d, 2, 3)


def kernel(x_prompt, x_sample, cache_kv, cache_win_kv, state_gla, page_table, ln_gains, ffn_w_gate_up, ffn_w_down,
           w_in, w_out, nsa_qk_norm, nsa_cmp_pos, nsa_cmp_w1, nsa_cmp_w2, gla_gate_w, gla_gate_b, gla_norm, gm_ln,
           gm_ws, gm_b):
    depth = w_in.shape[0]
    bp, tp, _ = x_prompt.shape
    bs, ts, _ = x_sample.shape
    n_pages = page_table.shape[1]
    page = cache_kv.shape[2]
    past_len = n_pages * page
    tq = 8
    mp, ms = bp * tp, bs * tq
    nc = tp // CMP_STRIDE

    w_gu = ffn_w_gate_up.astype(BF16).reshape(depth * 2, D_MODEL, 2 * D_FF)
    w_d = ffn_w_down.astype(BF16).reshape(depth * 2, D_FF, D_MODEL)
    w_in_p = _pad_cols(w_in).astype(BF16)
    w_o = w_out.astype(BF16)
    cos_p, sin_p = _rope_tables(jnp.arange(tp))
    cos_s, sin_s = _rope_tables(past_len + jnp.arange(ms) % tq)
    cov = _cover_t(tp)
    tm_p = 512 if mp % 512 == 0 else Q_BLOCK
    tm_f = 1024 if mp % 1024 == 0 else tm_p
    tm_i = 512 if tp % 512 == 0 else Q_BLOCK
    tf = 1408

    nrow = NSA_HEADS * 8
    n_keep_s = cache_win_kv.shape[2]
    nsp = -(-(past_len // SEL_BLOCK + 1) // LANES) * LANES
    cov_s, emat = _cover_sample(past_len, ts, nsp)
    perm = _page_perm(page)
    eye2 = jnp.eye(NSA_KV_HEADS, dtype=F32)
    npg_c = min(32, n_pages)
    npg_a = min(32, n_pages)
    cache_t = jnp.transpose(cache_kv, (0, 1, 3, 4, 5, 2)).reshape(depth, cache_kv.shape[1], 4 * LANES, page)
    win_t = jnp.transpose(cache_win_kv, (0, 1, 3, 4, 5, 2)).reshape(depth, bs, 2 * LANES, n_keep_s)

    xp = x_prompt.reshape(mp, D_MODEL)
    xs = jnp.pad(x_sample, ((0, 0), (0, tq - ts), (0, 0))).reshape(ms, D_MODEL)
    real_row = (jnp.arange(tq) < ts).astype(F32)[None, :, None]
    kv_p, win_p, gla_p, kv_s, win_s, gla_s, gmv_s = [], [], [], [], [], [], []
    eye_b = jnp.eye(GM_CHUNK // tq, dtype=F32)
    for l in range(depth):
        lw = _layer_weights(l, ln_gains, w_in_p, nsa_qk_norm, nsa_cmp_pos, nsa_cmp_w1, nsa_cmp_w2, gla_gate_w,
                            gla_gate_b, gla_norm, gm_ln, gm_ws, gm_b)
        xp = _ffn(xp, lw["ln0"], w_gu, w_d, 2 * l, tm=tm_f, tf=tf)
        (newkv, newwin, _, qg, kg, la, vg, rs, u, vn, qs, qrs, gt, ksel_r, vsel_t, kwin_r, vwin_t, kcmp,
         vcmp) = _inproj(xp, lw, cos_p, sin_p, tm=tm_i, attn_layout=True, batch=bp, seq=tp)
        zk = kcmp.reshape(bp, nc, CMP_STRIDE * LANES)
        zv = vcmp.reshape(bp, nc, CMP_STRIDE * LANES)
        kc, vct = _compress(zk, zv, lw)
        oa = _nsa_prompt(qs, qrs, gt, kc, vct, cov, ksel_r.reshape(bp, tp, LANES), vsel_t,
                         kwin_r.reshape(bp, tp, LANES), vwin_t, batch=bp, seq=tp)
        r3 = lambda a: a.reshape(bp, tp, a.shape[-1])
        ob, st = _gla(r3(qg), r3(kg), r3(la), r3(vg), r3(rs), jnp.zeros((bp, GLA_V, GLA_QK), F32), lw,
                      batch=bp, seq=tp, c=GLA_CHUNK, bpb=1)
        xp = _outproj(xp, oa, ob.reshape(mp, GLA_V), u, vn, lw["gm_ws"], lw["gm_bias"], w_o, l, tm=tm_p)
        xp = _ffn(xp, lw["ln2"], w_gu, w_d, 2 * l + 1, tm=tm_f, tf=tf)
        kv_p.append(newkv.reshape(bp, 4, NSA_KV_HEADS, HEAD_DIM, tp).transpose(0, 4, 1, 2, 3))
        n_keep = min(WINDOW, tp)
        win_p.append(newwin[:, :, tp - n_keep:].reshape(bp, 2, NSA_KV_HEADS, HEAD_DIM, n_keep).transpose(0, 4, 1, 2, 3))
        gla_p.append(_state_from_t(st))

        xs = _ffn(xs, lw["ln0"], w_gu, w_d, 2 * l, tm=ms, tf=tf)
        (newkv, newwin, gates, qg, kg, la, vg, rs, u, vn, qn, qr) = _inproj(
            xs, lw, cos_s, sin_s, tm=ms, attn_layout=False, batch=bs, seq=ts)
        hid = _pcompress(cache_t, page_table, l, perm, lw["pc_wk"], lw["pc_wv"], npg=npg_c)
        kct, vcr = _ctail(hid, lw["pc_bias"], lw["pc_w2"], lw["gkc"], lw["sm"])
        scale = np.float32(HEAD_DIM ** -0.5)
        to_pad = lambda q: jnp.einsum("bgrqd,gx->bgrqxd", _rows_gr8(q.reshape(bs, tq, NSA_HEADS, HEAD_DIM) * scale, tq),
                                      eye2).reshape(bs, nrow, LANES).astype(BF16)
        gcol = _rows_gr8(gates[:, :3 * NSA_HEADS].reshape(bs, tq, NSA_HEADS, 3), tq).reshape(bs, nrow, 3)
        gcol = jnp.pad(gcol, ((0, 0), (0, 0), (0, LANES - 3)))
        newk_t = jnp.pad(newkv.reshape(bs, tq, -1).transpose(0, 2, 1), ((0, 0), (0, 0), (0, LANES - tq)))
        neww_c = newwin.reshape(bs, tq, -1).transpose(0, 2, 1)[:, :, :ts]
        neww_t = jnp.pad(neww_c, ((0, 0), (0, 0), (0, LANES - ts)))
        newwsh_t = jnp.pad(neww_c, ((0, 0), (0, 0), (LANES - ts, 0)))
        o64, win_o = _nsa_sample(cache_t, page_table, l, to_pad(qn), to_pad(qr), gcol, kct, vcr, cov_s, emat, win_t,
                                 newk_t, neww_t, newwsh_t, npg=npg_a, past_len=past_len, ts=ts)
        o6 = o64.reshape(bs, NSA_KV_HEADS, NSA_REP, tq, NSA_KV_HEADS, HEAD_DIM)
        oa = jnp.stack([o6[:, g, :, :, g, :] for g in range(NSA_KV_HEADS)], axis=1)
        oa = (oa.transpose(0, 3, 1, 2, 4).reshape(bs, tq, NSA_Q) * real_row).reshape(ms, NSA_Q)
        r8 = lambda a: a.reshape(bs, tq, a.shape[-1])
        ob8, st_s = _gla(r8(qg), r8(kg), r8(la) * real_row, r8(vg), r8(rs), _state_to_t(state_gla[l]), lw,
                         batch=bs, seq=tq, c=tq, bpb=4 if bs % 4 == 0 else 1)
        ob = ob8.reshape(ms, GLA_V)
        ws_s = jnp.einsum("gts,bc->gbtcs", gm_ws[l][:, :tq, :tq], eye_b).reshape(GM_GROUPS, GM_CHUNK, GM_CHUNK)
        bias_s = jnp.tile(lw["gm_bias"][:tq], (GM_CHUNK // tq, 1))
        xs = _outproj(xs, oa, ob, u, vn, ws_s, bias_s, w_o, l, tm=GM_CHUNK)
        xs = _ffn(xs, lw["ln2"], w_gu, w_d, 2 * l + 1, tm=ms, tf=tf)
        kv_s.append(newkv.reshape(bs, tq, 4, NSA_KV_HEADS, HEAD_DIM)[:, :ts])
        win_s.append(win_o.reshape(bs, 2, NSA_KV_HEADS, HEAD_DIM, n_keep_s).transpose(0, 4, 1, 2, 3))
        gla_s.append(_state_from_t(st_s))
        gmv_s.append(vn.reshape(bs, tq, GM_GROUPS, GM_CH)[:, :ts])
    return (xp.reshape(bp, tp, D_MODEL), xs.reshape(bs, tq, D_MODEL)[:, :ts], jnp.stack(kv_p), jnp.stack(win_p),
            jnp.stack(gla_p), jnp.stack(kv_s), jnp.stack(win_s), jnp.stack(gla_s), jnp.stack(gmv_s))
```

```python
import functools

import numpy as np
import jax
import jax.numpy as jnp
from jax import lax
from jax.experimental import pallas as pl
from jax.experimental.pallas import tpu as pltpu

F32 = jnp.float32
BF16 = jnp.bfloat16

D_MODEL = 1024
HEAD_DIM = 64
NSA_HEADS = 8
NSA_KV_HEADS = 2
NSA_REP = NSA_HEADS // NSA_KV_HEADS
CMP_STRIDE = 16
CMP_BLOCK = 2 * CMP_STRIDE
CMP_HIDDEN = 128
SEL_BLOCK = 64
N_SEL = 16
WINDOW = 512
Q_BLOCK = 128
SEL_BONUS = 1.0e4
GLA_HEADS = 4
GLA_DK = 32
GLA_DV = 64
GLA_GATE_RANK = 16
GLA_GATE_TEMP = 16.0
GLA_CHUNK = 64
GM_GROUPS = 4
GM_CH = 64
GM_CHUNK = 128
D_FF = 2816
ROPE_THETA = 10000.0
EPS = 1e-6

NSA_Q = NSA_HEADS * HEAD_DIM
NSA_KV = NSA_KV_HEADS * HEAD_DIM
GLA_QK = GLA_HEADS * GLA_DK
GLA_V = GLA_HEADS * GLA_DV
GM_W = GM_GROUPS * GM_CH
MIX_OUT = NSA_Q + GLA_V + GM_W
IN_SPLITS = (NSA_Q, 6 * NSA_KV, 3 * NSA_HEADS, GLA_QK, GLA_QK, GLA_V, GLA_GATE_RANK, GLA_V, GM_W, GM_W)
IN_PADDED = tuple(-(-s // 128) * 128 for s in IN_SPLITS)
IN_OFFS = tuple(int(v) for v in np.cumsum((0,) + IN_PADDED))
D_IN_PAD = IN_OFFS[-1]

LANES = 128
NEG_BIG = -1.0e30
VMEM_LIMIT = 56 * 1024 * 1024


def _cparams(sem):
    return pltpu.CompilerParams(dimension_semantics=sem, vmem_limit_bytes=VMEM_LIMIT)


def _gelu(x):
    c = np.float32(np.sqrt(2.0 / np.pi))
    return x * (0.5 * (1.0 + jnp.tanh(c * (x + 0.044715 * (x * x * x)))))


def _sigmoid(x):
    return 1.0 / (1.0 + jnp.exp(-x))


def _dot(a, b):
    return jnp.dot(a, b, preferred_element_type=F32)


def _dot_nt(a, b):
    return lax.dot_general(a, b, (((1,), (1,)), ((), ())), preferred_element_type=F32)


def _seg_mean_sq(x, sm):
    sq = x * x
    hi = sq.astype(BF16)
    lo = (sq - hi.astype(F32)).astype(BF16)
    outs = []
    for c in range(x.shape[1] // LANES):
        sl = slice(c * LANES, (c + 1) * LANES)
        outs.append(_dot(hi[:, sl], sm) + _dot(lo[:, sl], sm))
    return outs[0] if len(outs) == 1 else jnp.concatenate(outs, axis=1)


def _seg_rms(x, gain, sm):
    return x * lax.rsqrt(_seg_mean_sq(x, sm) + EPS) * gain


def _tile_lanes(a, w):
    n = w // a.shape[1]
    return a if n == 1 else jnp.concatenate([a] * n, axis=1)


def _rope(x, cos, sin_signed):
    w = x.shape[1]
    lane = lax.broadcasted_iota(jnp.int32, x.shape, 1)
    fwd = pltpu.roll(x, w - HEAD_DIM // 2, axis=1)
    bwd = pltpu.roll(x, HEAD_DIM // 2, axis=1)
    partner = jnp.where((lane % HEAD_DIM) < HEAD_DIM // 2, fwd, bwd)
    return x * _tile_lanes(cos, w) + partner * _tile_lanes(sin_signed, w)


def _ffn_kernel(x_ref, g_ref, wg_ref, wu_ref, wd_ref, o_ref, h_scr, acc_scr, *, nj):
    j = pl.program_id(1)

    @pl.when(j == 0)
    def _():
        x = x_ref[...]
        ms = jnp.mean(x * x, axis=-1, keepdims=True)
        h_scr[...] = (x * lax.rsqrt(ms + EPS) * g_ref[...]).astype(BF16)
        acc_scr[...] = jnp.zeros_like(acc_scr)

    h = h_scr[...]
    g = _dot(h, wg_ref[...])
    u = _dot(h, wu_ref[...])
    a = (g * _sigmoid(g)) * u
    acc_scr[...] += _dot(a.astype(BF16), wd_ref[...])

    @pl.when(j == nj - 1)
    def _():
        o_ref[...] = x_ref[...] + 0.5 * acc_scr[...]


def _ffn(x, gain, w_gu, w_d, li, *, tm, tf):
    m = x.shape[0]
    nj = D_FF // tf
    return pl.pallas_call(
        functools.partial(_ffn_kernel, nj=nj),
        out_shape=jax.ShapeDtypeStruct((m, D_MODEL), F32),
        grid=(m // tm, nj),
        in_specs=[
            pl.BlockSpec((tm, D_MODEL), lambda i, j: (i, 0)),
            pl.BlockSpec((1, D_MODEL), lambda i, j: (0, 0)),
            pl.BlockSpec((None, D_MODEL, tf), lambda i, j: (li, 0, j)),
            pl.BlockSpec((None, D_MODEL, tf), lambda i, j: (li, 0, j + nj)),
            pl.BlockSpec((None, tf, D_MODEL), lambda i, j: (li, j, 0)),
        ],
        out_specs=pl.BlockSpec((tm, D_MODEL), lambda i, j: (i, 0)),
        scratch_shapes=[pltpu.VMEM((tm, D_MODEL), BF16), pltpu.VMEM((tm, D_MODEL), F32)],
        compiler_params=_cparams(("parallel", "arbitrary")),
        name="ffn",
    )(x, gain, w_gu, w_gu, w_d)


def _group_padded(arr, h):
    c = arr[:, (h // 2) * LANES:(h // 2 + 1) * LANES]
    g = h // NSA_REP
    if (h % 2) != g:
        c = pltpu.roll(c, HEAD_DIM, axis=1)
    lane = lax.broadcasted_iota(jnp.int32, c.shape, 1)
    keep = (lane >= g * HEAD_DIM) & (lane < (g + 1) * HEAD_DIM)
    return jnp.where(keep, c, 0.0)


def _inproj_kernel(x_ref, ln_ref, w_ref, gq_ref, gks_ref, gkw_ref, cos_ref, sin_ref, gw_ref, gb_ref,
                   lng_ref, lnb_ref, sm_ref, *outs, tm, attn_layout):
    (newkv_ref, newwin_ref, gates_ref, qg_ref, kg_ref, la_ref, vg_ref, rs_ref, u_ref, vn_ref) = outs[:10]
    x = x_ref[...]
    ms = jnp.mean(x * x, axis=-1, keepdims=True)
    h = (x * lax.rsqrt(ms + EPS) * ln_ref[...]).astype(BF16)
    p = _dot(h, w_ref[...])
    sm = sm_ref[...]
    cos = cos_ref[...]
    sin = sin_ref[...]
    o = IN_OFFS

    def seg(i, a=0, b=None):
        b = IN_PADDED[i] if b is None else b
        return p[:, o[i] + a:o[i] + b]

    qn = _seg_rms(seg(0), gq_ref[...], sm)
    qr = _rope(qn, cos, sin)
    kv = [seg(1, LANES * j, LANES * (j + 1)) for j in range(6)]
    ksel = _rope(_seg_rms(kv[2], gks_ref[...], sm), cos, sin)
    kwin = _rope(_seg_rms(kv[4], gkw_ref[...], sm), cos, sin)
    vsel_t = kv[3].T
    vwin_t = kv[5].T
    if attn_layout:
        for j, a in enumerate((kv[0].T, kv[1].T, ksel.T, vsel_t)):
            newkv_ref[j * LANES:(j + 1) * LANES, :] = a
        newwin_ref[0:LANES, :] = kwin.T
        newwin_ref[LANES:2 * LANES, :] = vwin_t
    else:
        for j, a in enumerate((kv[0], kv[1], ksel, kv[3])):
            newkv_ref[:, j * LANES:(j + 1) * LANES] = a
        newwin_ref[:, 0:LANES] = kwin
        newwin_ref[:, LANES:2 * LANES] = kv[5]
    gates = _sigmoid(seg(2))
    gates_ref[...] = gates
    qg_ref[...] = seg(3) * np.float32(GLA_DK ** -0.5)
    kg_ref[...] = seg(4)
    vg_ref[...] = seg(5)
    logit = _dot(seg(6).astype(BF16), gw_ref[...]) + gb_ref[...]
    log_sig = jnp.minimum(logit, 0.0) - jnp.log1p(jnp.exp(-jnp.abs(logit)))
    la_ref[...] = log_sig * np.float32(np.log2(np.e) / GLA_GATE_TEMP)
    r = seg(7)
    rs_ref[...] = r * _sigmoid(r)
    u_ref[...] = _gelu(seg(8))
    v = _gelu(seg(9))
    mu = jnp.mean(v, axis=-1, keepdims=True)
    var = jnp.mean(jnp.square(v - mu), axis=-1, keepdims=True)
    vn_ref[...] = (v - mu) * lax.rsqrt(var + EPS) * lng_ref[...] + lnb_ref[...]

    if attn_layout:
        (qs_ref, qrs_ref, gt_ref, kselr_ref, vselt_ref, kwinr_ref, vwint_ref, kcmp_ref, vcmp_ref) = outs[10:]
        scale = np.float32(HEAD_DIM ** -0.5 * np.log2(np.e))
        qs = qn * scale
        qrs = qr * scale
        for hh in range(NSA_HEADS):
            a = _group_padded(qs, hh).astype(BF16)
            b = _group_padded(qrs, hh).astype(BF16)
            for rb in range(tm // Q_BLOCK):
                qs_ref[rb, hh] = a[rb * Q_BLOCK:(rb + 1) * Q_BLOCK]
                qrs_ref[rb, hh] = b[rb * Q_BLOCK:(rb + 1) * Q_BLOCK]
        for rb in range(tm // Q_BLOCK):
            gt_ref[rb] = gates[rb * Q_BLOCK:(rb + 1) * Q_BLOCK].T
        kselr_ref[...] = ksel.astype(BF16)
        vselt_ref[...] = vsel_t.astype(BF16)
        kwinr_ref[...] = kwin.astype(BF16)
        vwint_ref[...] = vwin_t.astype(BF16)
        kcmp_ref[...] = kv[0]
        vcmp_ref[...] = kv[1]
    else:
        qn_ref, qr_ref = outs[10:]
        qn_ref[...] = qn
        qr_ref[...] = qr


def _inproj(x, lw, cos_t, sin_t, *, tm, attn_layout, batch, seq):
    m = x.shape[0]
    nt = m // tm
    ntab = cos_t.shape[0] // tm
    row = lambda w: pl.BlockSpec((tm, w), lambda i: (i, 0))
    full = lambda a: pl.BlockSpec(a.shape, lambda i: (0,) * a.ndim)
    ins = [x, lw["ln1"], lw["w_in"], lw["gq"], lw["gks"], lw["gkw"], cos_t, sin_t, lw["gla_gw"], lw["gla_gb"],
           lw["gm_lng"], lw["gm_lnb"], lw["sm"]]
    in_specs = [row(D_MODEL), full(lw["ln1"]), full(lw["w_in"]), full(lw["gq"]), full(lw["gks"]), full(lw["gkw"]),
                pl.BlockSpec((tm, LANES), lambda i: (i % ntab, 0)), pl.BlockSpec((tm, LANES), lambda i: (i % ntab, 0)),
                full(lw["gla_gw"]), full(lw["gla_gb"]), full(lw["gm_lng"]), full(lw["gm_lnb"]), full(lw["sm"])]
    widths = [512, 256, 128, 128, 128, 128, 256, 256, 256, 256]
    out_shape = [jax.ShapeDtypeStruct((m, w), F32) for w in widths]
    out_specs = [row(w) for w in widths]
    if attn_layout:
        nqb = m // Q_BLOCK
        rpb = tm // Q_BLOCK
        tpb = seq // tm
        for j in range(2):
            out_shape[j] = jax.ShapeDtypeStruct((batch, widths[j], seq), F32)
            out_specs[j] = pl.BlockSpec((None, widths[j], tm), lambda i: (i // tpb, 0, i % tpb))
        out_shape += [jax.ShapeDtypeStruct((nqb, NSA_HEADS, Q_BLOCK, LANES), BF16)] * 2
        out_specs += [pl.BlockSpec((rpb, NSA_HEADS, Q_BLOCK, LANES), lambda i: (i, 0, 0, 0))] * 2
        out_shape += [jax.ShapeDtypeStruct((nqb, LANES, Q_BLOCK), F32)]
        out_specs += [pl.BlockSpec((rpb, LANES, Q_BLOCK), lambda i: (i, 0, 0))]
        rowmaj = (jax.ShapeDtypeStruct((m, LANES), BF16), row(LANES))
        trans = (jax.ShapeDtypeStruct((batch, LANES, seq), BF16),
                 pl.BlockSpec((None, LANES, tm), lambda i: (i // tpb, 0, i % tpb)))
        for sh, sp in (rowmaj, trans, rowmaj, trans):
            out_shape.append(sh)
            out_specs.append(sp)
        out_shape += [jax.ShapeDtypeStruct((m, LANES), F32)] * 2
        out_specs += [row(LANES)] * 2
    else:
        out_shape += [jax.ShapeDtypeStruct((m, NSA_Q), F32)] * 2
        out_specs += [row(NSA_Q)] * 2
    return pl.pallas_call(
        functools.partial(_inproj_kernel, tm=tm, attn_layout=attn_layout),
        out_shape=out_shape,
        grid=(nt,),
        in_specs=in_specs,
        out_specs=out_specs,
        compiler_params=_cparams(("parallel",)),
        name="inproj",
    )(*ins)


def _next_row(b):
    return pltpu.roll(b, b.shape[0] - 1, axis=0)


def _compress_kernel(zk_ref, zv_ref, pos_ref, wak_ref, wbk_ref, wav_ref, wbv_ref,
                     w2k_ref, w2v_ref, gkc_ref, sm_ref, kc_ref, vct_ref):
    pos = pos_ref[...]

    def one(z_ref, pa, pb, wa_ref, wb_ref, w2_ref):
        z = z_ref[...]
        a = _dot((z + pa).astype(BF16), wa_ref[...])
        b = _dot((z + pb).astype(BF16), wb_ref[...])
        return _dot(_gelu(a + _next_row(b)).astype(BF16), w2_ref[...])

    ck = one(zk_ref, pos[0:1], pos[1:2], wak_ref, wbk_ref, w2k_ref)
    cv = one(zv_ref, pos[2:3], pos[3:4], wav_ref, wbv_ref, w2v_ref)
    kc_ref[...] = _seg_rms(ck, gkc_ref[...], sm_ref[...]).astype(BF16)
    vct_ref[...] = cv.T.astype(BF16)


def _compress(zk, zv, lw):
    b, nc, kdim = zk.shape
    zspec = pl.BlockSpec((None, nc, kdim), lambda i: (i, 0, 0))
    full = lambda a: pl.BlockSpec(a.shape, lambda i: (0,) * a.ndim)
    ws = [lw["cmp_pos"], lw["cmp_wak"], lw["cmp_wbk"], lw["cmp_wav"], lw["cmp_wbv"], lw["cmp_w2k"], lw["cmp_w2v"],
          lw["gkc"], lw["sm"]]
    return pl.pallas_call(
        _compress_kernel,
        out_shape=[jax.ShapeDtypeStruct((b, nc, LANES), BF16), jax.ShapeDtypeStruct((b, LANES, nc), BF16)],
        grid=(b,),
        in_specs=[zspec] * 2 + [full(w) for w in ws],
        out_specs=[pl.BlockSpec((None, nc, LANES), lambda i: (i, 0, 0)),
                   pl.BlockSpec((None, LANES, nc), lambda i: (i, 0, 0))],
        compiler_params=_cparams(("parallel",)),
        name="nsa_compress",
    )(zk, zv, *ws)


def _nsa_kernel(qs_ref, qrs_ref, gt_ref, kc_ref, vct_ref, cov_ref, ksel_ref, vselt_ref, kwin_ref, vwint_ref,
                o_ref, score_scr, bias_scr, sa_scr, sb_scr, *, nc, ns, kb_keys):
    ib = pl.program_id(1)
    p0 = ib * Q_BLOCK
    nl = NSA_REP * Q_BLOCK
    ng = NSA_KV_HEADS
    t_row = p0 + lax.broadcasted_iota(jnp.int32, (1, nl), 1) % Q_BLOCK
    t_q = p0 + lax.broadcasted_iota(jnp.int32, (1, Q_BLOCK), 1)
    gt = gt_ref[...]
    n_top = min(N_SEL, ns)
    per_kb = kb_keys // SEL_BLOCK
    grows = [slice(g * HEAD_DIM, (g + 1) * HEAD_DIM) for g in range(ng)]

    def q_of(ref, g):
        return ref[NSA_REP * g:NSA_REP * (g + 1)].reshape(nl, LANES)

    ones_rows = 16

    def with_ones(vt):
        return jnp.concatenate([vt, jnp.ones((ones_rows, vt.shape[1]), BF16)], axis=0)

    def online(carry, s, vt):
        m, acc = carry
        m_new = jnp.maximum(m, jnp.max(s, axis=0, keepdims=True))
        p = jnp.exp2(s - m_new)
        acc = jnp.exp2(m - m_new) * acc + _dot(with_ones(vt), p.astype(BF16))
        return m_new, acc

    def normalised(acc):
        return acc[0:HEAD_DIM] * (1.0 / jnp.maximum(acc[HEAD_DIM:HEAD_DIM + 1], 1e-30))

    init1 = (jnp.full((1, nl), NEG_BIG, F32), jnp.zeros((HEAD_DIM + ones_rows, nl), F32))
    init = tuple(init1 for _ in range(ng))

    sj = lax.broadcasted_iota(jnp.int32, (ns, 1), 0)

    def cmp_branch(rows):
        outs = []
        for g in range(ng):
            s = _dot_nt(kc_ref[0:rows, :], q_of(qs_ref, g))
            ci = lax.broadcasted_iota(jnp.int32, (rows, 1), 0)
            cmask = (ci * CMP_STRIDE + (CMP_BLOCK - 1) <= t_row) & (ci < nc - 1)
            s = jnp.where(cmask, s, -jnp.inf)
            m = jnp.max(s, axis=0, keepdims=True)
            m = jnp.where(m == -jnp.inf, 0.0, m)
            e = jnp.exp2(s - m)
            l = jnp.sum(e, axis=0, keepdims=True)
            p = e * (1.0 / jnp.maximum(l, 1e-30))
            outs.append(_dot(vct_ref[grows[g], 0:rows], p.astype(BF16)))
            psum = p[:, 0:Q_BLOCK]
            for r in range(1, NSA_REP):
                psum = psum + p[:, r * Q_BLOCK:(r + 1) * Q_BLOCK]
            imp = _dot(cov_ref[:, 0:rows], psum.astype(BF16))
            bt = t_q // SEL_BLOCK
            allowed = sj * SEL_BLOCK <= t_q
            forced = (sj == 0) | (sj == bt) | (sj == bt - 1)
            score_scr[g] = jnp.where(allowed, imp + jnp.where(forced, np.float32(SEL_BONUS), 0.0), -jnp.inf)
        return tuple(outs)

    quarter = nc // 4
    if quarter % LANES == 0:
        vis = (p0 + Q_BLOCK - CMP_BLOCK) // CMP_STRIDE
        o_cmp = lax.cond(
            vis < 2 * quarter,
            lambda: lax.cond(vis < quarter, lambda: cmp_branch(quarter), lambda: cmp_branch(2 * quarter)),
            lambda: lax.cond(vis < 3 * quarter, lambda: cmp_branch(3 * quarter), lambda: cmp_branch(nc)))
    else:
        o_cmp = cmp_branch(nc)

    sjf = sj.astype(F32)
    bt_q = t_q // SEL_BLOCK
    pre = ((sj == 0) | (sj == bt_q) | (sj == bt_q - 1)) & (sj * SEL_BLOCK <= t_q)
    left = [jnp.where(pre, -jnp.inf, score_scr[g]) for g in range(ng)]
    bias = [jnp.where(pre, 0.0, -jnp.inf) for _ in range(ng)]
    for _ in range(max(n_top - 3, 0)):
        for g in range(ng):
            mx = jnp.max(left[g], axis=0, keepdims=True)
            idx = jnp.min(jnp.where(left[g] == mx, sjf, np.float32(ns)), axis=0, keepdims=True)
            hit = sjf == jnp.where(mx > -jnp.inf, idx, -1.0)
            left[g] = jnp.where(hit, -jnp.inf, left[g])
            bias[g] = jnp.where(hit, 0.0, bias[g])
    for g in range(ng):
        bias_scr[g] = bias[g]

    def sel_qk(kb, g, dst):
        k0 = pl.multiple_of(kb * kb_keys, kb_keys)
        dst[g] = _dot_nt(ksel_ref[pl.ds(k0, kb_keys), :], q_of(qrs_ref, g))

    def sel_step(carry_g, src, kb, g, causal):
        k0 = pl.multiple_of(kb * kb_keys, kb_keys)
        parts = []
        for i in range(per_kb):
            row = bias_scr[g, pl.ds(kb * per_kb + i, 1), :]
            sl = slice(i * SEL_BLOCK, (i + 1) * SEL_BLOCK)
            parts.append(src[g, sl, :] + jnp.concatenate([row] * NSA_REP, axis=1))
        s = jnp.concatenate(parts, axis=0)
        if causal:
            kpos = k0 + lax.broadcasted_iota(jnp.int32, (kb_keys, 1), 0)
            s = jnp.where(kpos <= t_row, s, -jnp.inf)
        return online(carry_g, s, vselt_ref[grows[g], pl.ds(k0, kb_keys)])

    def sel_pair(j, carry, causal):
        a = 2 * j
        for g in range(ng):
            sel_qk(a + 1, g, sb_scr)
        carry = tuple(sel_step(carry[g], sa_scr, a, g, causal) for g in range(ng))
        if not causal:
            for g in range(ng):
                sel_qk(a + 2, g, sa_scr)
        return tuple(sel_step(carry[g], sb_scr, a + 1, g, causal) for g in range(ng))

    n_kb = (p0 + Q_BLOCK + kb_keys - 1) // kb_keys
    n_pairs = (n_kb + 1) // 2
    for g in range(ng):
        sel_qk(0, g, sa_scr)
    carry = lax.fori_loop(0, n_pairs - 1, lambda j, c: sel_pair(j, c, False), init)
    a_last = 2 * (n_pairs - 1)
    carry = tuple(sel_step(carry[g], sa_scr, a_last, g, True) for g in range(ng))

    def second_block(c):
        for g in range(ng):
            sel_qk(a_last + 1, g, sb_scr)
        return tuple(sel_step(c[g], sb_scr, a_last + 1, g, True) for g in range(ng))

    carry = lax.cond(n_kb % 2 == 0, second_block, lambda c: c, carry)
    o_sel = [normalised(carry[g][1]) for g in range(ng)]

    wkeys = WINDOW + Q_BLOCK
    w0 = pl.multiple_of(jnp.maximum(p0 - WINDOW, 0), Q_BLOCK)
    diff = t_row - (w0 + lax.broadcasted_iota(jnp.int32, (wkeys, 1), 0))
    wmask = (diff >= 0) & (diff < WINDOW)
    kblk = kwin_ref[pl.ds(w0, wkeys), :]
    o_win = []
    for g in range(ng):
        s = jnp.where(wmask, _dot_nt(kblk, q_of(qrs_ref, g)), -jnp.inf)
        o_win.append(normalised(online(init1, s, vwint_ref[grows[g], pl.ds(w0, wkeys)])[1]))

    for g in range(ng):
        o_w = o_win[g]

        def gate_row(jb):
            return jnp.concatenate(
                [gt[(NSA_REP * g + r) * 3 + jb:(NSA_REP * g + r) * 3 + jb + 1, :] for r in range(NSA_REP)], axis=1)

        o = o_cmp[g] * gate_row(0) + o_sel[g] * gate_row(1) + o_w * gate_row(2)
        for pr in range(NSA_REP // 2):
            blk = jnp.concatenate([o[:, (2 * pr) * Q_BLOCK:(2 * pr + 1) * Q_BLOCK],
                                   o[:, (2 * pr + 1) * Q_BLOCK:(2 * pr + 2) * Q_BLOCK]], axis=0)
            c0 = g * NSA_REP * HEAD_DIM + pr * LANES
            o_ref[:, c0:c0 + LANES] = blk.T


def _nsa_prompt(qs, qrs, gt, kc, vct, cov, ksel_r, vsel_t, kwin_r, vwin_t, *, batch, seq):
    nb = seq // Q_BLOCK
    nc = kc.shape[1]
    ns = cov.shape[0]
    kb_keys = min(512, seq)
    per_b3 = lambda a: pl.BlockSpec((None,) + a.shape[1:], lambda b, i: (b, 0, 0))
    return pl.pallas_call(
        functools.partial(_nsa_kernel, nc=nc, ns=ns, kb_keys=kb_keys),
        out_shape=jax.ShapeDtypeStruct((batch * seq, NSA_Q), F32),
        grid=(batch, nb),
        in_specs=[
            pl.BlockSpec((None, NSA_HEADS, Q_BLOCK, LANES), lambda b, i: (b * nb + i, 0, 0, 0)),
            pl.BlockSpec((None, NSA_HEADS, Q_BLOCK, LANES), lambda b, i: (b * nb + i, 0, 0, 0)),
            pl.BlockSpec((None, LANES, Q_BLOCK), lambda b, i: (b * nb + i, 0, 0)),
            per_b3(kc), per_b3(vct),
            pl.BlockSpec(cov.shape, lambda b, i: (0, 0)),
            per_b3(ksel_r), per_b3(vsel_t), per_b3(kwin_r), per_b3(vwin_t),
        ],
        out_specs=pl.BlockSpec((Q_BLOCK, NSA_Q), lambda b, i: (b * nb + i, 0)),
        scratch_shapes=[pltpu.VMEM((NSA_KV_HEADS, ns, Q_BLOCK), F32), pltpu.VMEM((NSA_KV_HEADS, ns, Q_BLOCK), F32),
                        pltpu.VMEM((NSA_KV_HEADS, kb_keys, NSA_REP * Q_BLOCK), F32),
                        pltpu.VMEM((NSA_KV_HEADS, kb_keys, NSA_REP * Q_BLOCK), F32)],
        compiler_params=_cparams(("parallel", "arbitrary")),
        name="nsa_prompt",
    )(qs, qrs, gt, kc, vct, cov, ksel_r, vsel_t, kwin_r, vwin_t)


def _gla_rows(c):
    offs, n = [], 0
    for s in range(c):
        t0 = (s // 8) * 8
        offs.append((n, t0))
        n += c - t0
    return offs, n


def _gla_kernel(q_ref, k_ref, la_ref, v_ref, rs_ref, gn_ref, bm_ref, bmask_ref, sm_ref, s0_ref,
                o_ref, sout_ref, s_scr, prod_scr, *, c, nchunks, bpb):
    ci = pl.program_id(1)

    @pl.when(ci == 0)
    def _():
        s_scr[...] = s0_ref[...]
        prod_scr[...] = jnp.zeros_like(prod_scr)

    for bb in range(bpb):
        _gla_one(q_ref.at[bb], k_ref.at[bb], la_ref.at[bb], v_ref.at[bb], rs_ref.at[bb], gn_ref, bm_ref, bmask_ref,
                 sm_ref, o_ref.at[bb], sout_ref.at[bb], s_scr.at[bb], prod_scr.at[bb],
                 c=c, last=ci == nchunks - 1)


def _gla_one(q_ref, k_ref, la_ref, v_ref, rs_ref, gn_ref, bm_ref, bmask_ref, sm_ref, o_ref, sout_ref,
             s_scr, prod_scr, *, c, last):
    q = q_ref[...]
    k = k_ref[...]
    v = v_ref[...]
    la = la_ref[...]
    tt = lax.broadcasted_iota(jnp.int32, (c, 1), 0)
    b = la
    sh = 1
    while sh < c:
        b = b + jnp.where(tt >= sh, pltpu.roll(b, sh, axis=0), 0.0)
        sh *= 2
    state = s_scr[...]
    inter = _dot_nt((q * jnp.exp2(b)).astype(BF16), state.astype(BF16))

    offs, npack = _gla_rows(c)
    acc = [inter[tb:min(tb + 8, c)] for tb in range(0, c, 8)]
    gsz = min(16, c)
    for g0 in range(0, c, gsz):
        for s in range(g0, g0 + gsz):
            r0, t0 = offs[s]
            d = b[t0:] - b[s:s + 1]
            head = jnp.where(tt[t0:t0 + 8] >= s, d[0:8], -jnp.inf)
            d = head if c - t0 <= 8 else jnp.concatenate([head, d[8:]], axis=0)
            prod_scr[r0:r0 + c - t0, :] = (q[t0:] * k[s:s + 1] * jnp.exp2(d)).astype(BF16)
        lo = offs[g0][0]
        hi = offs[g0 + gsz][0] if g0 + gsz < c else npack
        res = _dot(prod_scr[lo:hi, :], bm_ref[...])
        for s in range(g0, g0 + gsz):
            r0, t0 = offs[s]
            for tb in range(t0, c, 8):
                rr = r0 - lo + tb - t0
                acc[tb // 8] = acc[tb // 8] + res[rr:rr + min(8, c - tb)] * v[s:s + 1]
    o = acc[0] if len(acc) == 1 else jnp.concatenate(acc, axis=0)

    bl = b[c - 1:c]
    kd = (k * jnp.exp2(bl - b)).astype(BF16)
    if c >= 16:
        upd = _dot(v.T.astype(BF16), kd)
    else:
        upd = jnp.dot(v.T, kd.astype(F32), preferred_element_type=F32)
    new_state = jnp.exp2(bl) * state + upd * bmask_ref[...]
    s_scr[...] = new_state

    o_ref[...] = _seg_rms(o, gn_ref[...], sm_ref[...]) * rs_ref[...]

    @pl.when(last)
    def _():
        sout_ref[...] = new_state


def _gla(qg, kg, la, vg, rs, s0_bd, lw, *, batch, seq, c, bpb):
    nchunks = seq // c
    _, npack = _gla_rows(c)
    npad = -(-npack // 16) * 16
    blk = lambda w: pl.BlockSpec((bpb, c, w), lambda b, i: (b, i, 0))
    full = lambda a: pl.BlockSpec(a.shape, lambda b, i: (0,) * a.ndim)
    consts = [lw["gla_gn"], lw["gla_bm"], lw["gla_bmask"], lw["sm"]]
    st = pl.BlockSpec((bpb, GLA_V, GLA_QK), lambda b, i: (b, 0, 0))
    return pl.pallas_call(
        functools.partial(_gla_kernel, c=c, nchunks=nchunks, bpb=bpb),
        out_shape=[jax.ShapeDtypeStruct((batch, seq, GLA_V), F32), jax.ShapeDtypeStruct((batch, GLA_V, GLA_QK), F32)],
        grid=(batch // bpb, nchunks),
        in_specs=[blk(GLA_QK), blk(GLA_QK), blk(GLA_QK), blk(GLA_V), blk(GLA_V)] + [full(a) for a in consts] + [st],
        out_specs=[blk(GLA_V), st],
        scratch_shapes=[pltpu.VMEM((bpb, GLA_V, GLA_QK), F32), pltpu.VMEM((bpb, npad, GLA_QK), BF16)],
        compiler_params=_cparams(("parallel", "arbitrary")),
        name="gla",
    )(qg, kg, la, vg, rs, *consts, s0_bd)


def _outproj_kernel(x_ref, oa_ref, ob_ref, u_ref, vn_ref, ws_ref, bias_ref, wo_ref, o_ref, *, tm):
    lane = lax.broadcasted_iota(jnp.int32, (GM_CHUNK, GM_W), 1)
    tri = (lax.broadcasted_iota(jnp.int32, (GM_CHUNK, GM_CHUNK), 0)
           >= lax.broadcasted_iota(jnp.int32, (GM_CHUNK, GM_CHUNK), 1))
    zs = []
    for cb in range(tm // GM_CHUNK):
        vn = vn_ref[cb * GM_CHUNK:(cb + 1) * GM_CHUNK, :]
        z = bias_ref[...]
        for g in range(GM_GROUPS):
            wm = jnp.where(tri, ws_ref[g], 0.0).astype(BF16)
            vg = jnp.where((lane >= g * GM_CH) & (lane < (g + 1) * GM_CH), vn, 0.0).astype(BF16)
            z = z + _dot(wm, vg)
        zs.append(z)
    z = zs[0] if len(zs) == 1 else jnp.concatenate(zs, axis=0)
    oc = u_ref[...] * z
    y = _dot(oa_ref[...].astype(BF16), wo_ref[0:NSA_Q, :])
    y = y + _dot(ob_ref[...].astype(BF16), wo_ref[NSA_Q:NSA_Q + GLA_V, :])
    y = y + _dot(oc.astype(BF16), wo_ref[NSA_Q + GLA_V:MIX_OUT, :])
    o_ref[...] = x_ref[...] + y


def _outproj(x, oa, ob, u, vn, ws, bias, wo, li, *, tm):
    m = x.shape[0]
    row = lambda w: pl.BlockSpec((tm, w), lambda i: (i, 0))
    return pl.pallas_call(
        functools.partial(_outproj_kernel, tm=tm),
        out_shape=jax.ShapeDtypeStruct((m, D_MODEL), F32),
        grid=(m // tm,),
        in_specs=[row(D_MODEL), row(NSA_Q), row(GLA_V), row(GM_W), row(GM_W),
                  pl.BlockSpec(ws.shape, lambda i: (0, 0, 0)), pl.BlockSpec(bias.shape, lambda i: (0, 0)),
                  pl.BlockSpec((None, MIX_OUT, D_MODEL), lambda i: (li, 0, 0))],
        out_specs=row(D_MODEL),
        compiler_params=_cparams(("parallel",)),
        name="outproj",
    )(x, oa, ob, u, vn, ws, bias, wo)


def _pcompress_kernel(pt_ref, *refs, npg):
    pages = refs[:npg]
    perm_ref, wk_ref, wv_ref, out_ref, zk_scr, zv_scr = refs[npg:npg + 6]
    perm = perm_ref[...]
    for pp in range(npg // 2):
        xt = jnp.concatenate([pages[2 * pp][...], pages[2 * pp + 1][...]], axis=1).astype(BF16)
        y = _dot_nt(perm, xt).astype(BF16)
        for s in range(CMP_STRIDE):
            rows = slice(pp * 16, (pp + 1) * 16)
            zk_scr[rows, s * LANES:(s + 1) * LANES] = y[s * 16:(s + 1) * 16, 0:LANES]
            zv_scr[rows, s * LANES:(s + 1) * LANES] = y[s * 16:(s + 1) * 16, LANES:2 * LANES]
    q = out_ref.shape[1] // 4
    rk = _dot(zk_scr[...], wk_ref[...])
    rv = _dot(zv_scr[...], wv_ref[...])
    out_ref[:, 0:q] = rk[:, 0:q]
    out_ref[:, q:2 * q] = rv[:, 0:q]
    out_ref[:, 2 * q:3 * q] = rk[:, q:2 * q]
    out_ref[:, 3 * q:4 * q] = rv[:, q:2 * q]


def _pcompress(cache_t, page_table, li, perm, wk, wv, *, npg):
    bs, n_pages = page_table.shape
    page = cache_t.shape[-1]
    nc = n_pages * page // CMP_STRIDE
    steps = n_pages // npg
    cpp = page // CMP_STRIDE

    def page_spec(k):
        return pl.BlockSpec((None, None, 2 * LANES, page), lambda b, h, pt: (li, pt[b, h * npg + k], 0, 0))

    grid_spec = pltpu.PrefetchScalarGridSpec(
        num_scalar_prefetch=1,
        grid=(bs, steps),
        in_specs=[page_spec(k) for k in range(npg)] + [
            pl.BlockSpec(perm.shape, lambda b, h, pt: (0, 0)), pl.BlockSpec(wk.shape, lambda b, h, pt: (0, 0)),
            pl.BlockSpec(wv.shape, lambda b, h, pt: (0, 0))],
        out_specs=pl.BlockSpec((None, npg * cpp, 2 * wk.shape[1]), lambda b, h, pt: (b, h, 0)),
        scratch_shapes=[pltpu.VMEM((npg * cpp, wk.shape[0]), BF16)] * 2,
    )
    return pl.pallas_call(
        functools.partial(_pcompress_kernel, npg=npg),
        out_shape=jax.ShapeDtypeStruct((bs, nc, 2 * wk.shape[1]), F32),
        grid_spec=grid_spec,
        compiler_params=_cparams(("parallel", "arbitrary")),
        name="nsa_page_compress",
    )(page_table, *([cache_t] * npg), perm, wk, wv)


def _ctail_kernel(a_ref, b_ref, bias_ref, w2_ref, gkc_ref, sm_ref, kct_ref, vcr_ref):
    h = a_ref[...] + _next_row(b_ref[...]) + bias_ref[...]
    out = _dot(_gelu(h).astype(BF16), w2_ref[...])
    kc = _seg_rms(out[:, 0:LANES], gkc_ref[...], sm_ref[...])
    kct_ref[...] = kc.T.astype(BF16)
    vcr_ref[...] = out[:, LANES:2 * LANES].astype(BF16)


def _ctail(hid_ab, bias, w2, gkc, sm):
    bs, nc, w2x = hid_ab.shape
    half = w2x // 2
    full = lambda a: pl.BlockSpec(a.shape, lambda b: (0,) * a.ndim)
    return pl.pallas_call(
        _ctail_kernel,
        out_shape=[jax.ShapeDtypeStruct((bs, LANES, nc), BF16), jax.ShapeDtypeStruct((bs, nc, LANES), BF16)],
        grid=(bs,),
        in_specs=[pl.BlockSpec((None, nc, half), lambda b: (b, 0, 0)), pl.BlockSpec((None, nc, half), lambda b: (b, 0, 1)),
                  full(bias), full(w2), full(gkc), full(sm)],
        out_specs=[pl.BlockSpec((None, LANES, nc), lambda b: (b, 0, 0)), pl.BlockSpec((None, nc, LANES), lambda b: (b, 0, 0))],
        compiler_params=_cparams(("parallel",)),
        name="nsa_compress_tail",
    )(hid_ab, hid_ab, bias, w2, gkc, sm)


def _nsa_sample_kernel(pt_ref, qp_ref, qrp_ref, gcol_ref, kct_ref, vcr_ref, cov_ref, e_ref, *refs,
                       npg, nsteps, nc, ns_tot, past_len, ts):
    pages = refs[:npg]
    (wint_ref, newk_ref, neww_ref, newwsh_ref, o_ref, winout_ref,
     bias_scr, m_scr, l_scr, acc_scr, oc_scr, ow_scr) = refs[npg:]
    step = pl.program_id(1)
    nrow = NSA_KV_HEADS * NSA_REP * 8
    q8 = lax.broadcasted_iota(jnp.int32, (nrow, 1), 0) % 8
    t_row = past_len + q8
    qrp = qrp_ref[...]
    n_keep = wint_ref.shape[-1]
    n_top = min(N_SEL, ns_tot)
    nsp = bias_scr.shape[1]

    def expand_rows(a):
        return jnp.concatenate([a[0:8]] * NSA_REP + [a[8:16]] * NSA_REP, axis=0)

    @pl.when(step == 0)
    def _():
        s = _dot(qp_ref[...], kct_ref[...])
        ci = lax.broadcasted_iota(jnp.int32, (1, nc), 1)
        cmask = (ci * CMP_STRIDE + (CMP_BLOCK - 1) <= t_row) & (ci < nc - 1)
        s = jnp.where(cmask, s, -jnp.inf)
        m = jnp.max(s, axis=1, keepdims=True)
        m = jnp.where(m == -jnp.inf, 0.0, m)
        e = jnp.exp(s - m)
        p = e * (1.0 / jnp.maximum(jnp.sum(e, axis=1, keepdims=True), 1e-30))
        oc_scr[...] = _dot(p.astype(BF16), vcr_ref[...])
        sjl = lax.broadcasted_iota(jnp.int32, (1, nsp), 1)
        tq = past_len + lax.broadcasted_iota(jnp.int32, (8, 1), 0)
        bt = tq // SEL_BLOCK
        allowed = (sjl * SEL_BLOCK <= tq) & (sjl < ns_tot)
        forced = (sjl == 0) | (sjl == bt) | (sjl == bt - 1)
        for g in range(NSA_KV_HEADS):
            base = g * NSA_REP * 8
            psum = p[base:base + 8]
            for r in range(1, NSA_REP):
                psum = psum + p[base + r * 8:base + (r + 1) * 8]
            imp = _dot(psum.astype(BF16), cov_ref[...])
            score = jnp.where(allowed, imp + jnp.where(forced, np.float32(SEL_BONUS), 0.0), -jnp.inf)
            rank = jnp.zeros((8, nsp), jnp.int32)
            for k in range(ns_tot):
                col = score[:, k:k + 1]
                later = jnp.where(sjl > k, 1, 0)
                rank = rank + jnp.where(col > score, 1, 0) + jnp.where(col == score, later, 0)
            keep = (rank < n_top) & (score > -jnp.inf)
            bias_scr[g * 8:(g + 1) * 8, :] = jnp.where(keep, 0.0, -jnp.inf)
        m_scr[...] = jnp.full(m_scr.shape, NEG_BIG, F32)
        l_scr[...] = jnp.zeros(l_scr.shape, F32)
        acc_scr[...] = jnp.zeros(acc_scr.shape, F32)

        lane_w = lax.broadcasted_iota(jnp.int32, (1, n_keep), 1)
        diff = t_row - (past_len - n_keep + lane_w)
        s_w = jnp.where((diff >= 0) & (diff < WINDOW), _dot(qrp, wint_ref[0:LANES, :].astype(BF16)), -jnp.inf)
        lane_n = lax.broadcasted_iota(jnp.int32, (1, LANES), 1)
        diff_n = t_row - (past_len + lane_n)
        s_n = jnp.where((lane_n < ts) & (diff_n >= 0) & (diff_n < WINDOW),
                        _dot(qrp, neww_ref[0:LANES, :].astype(BF16)), -jnp.inf)
        sw = jnp.concatenate([s_w, s_n], axis=1)
        mw = jnp.max(sw, axis=1, keepdims=True)
        mw = jnp.where(mw == -jnp.inf, 0.0, mw)
        ew = jnp.exp(sw - mw)
        pw = ew * (1.0 / jnp.maximum(jnp.sum(ew, axis=1, keepdims=True), 1e-30))
        ow_scr[...] = (_dot_nt(pw[:, 0:n_keep].astype(BF16), wint_ref[LANES:2 * LANES, :].astype(BF16))
                       + _dot_nt(pw[:, n_keep:].astype(BF16), neww_ref[LANES:2 * LANES, :].astype(BF16)))
        rolled = pltpu.roll(wint_ref[...], n_keep - ts, axis=1)
        lane_o = lax.broadcasted_iota(jnp.int32, (2 * LANES, LANES), 1)
        winout_ref[:, 0:n_keep - LANES] = rolled[:, 0:n_keep - LANES]
        winout_ref[:, n_keep - LANES:n_keep] = jnp.where(lane_o >= LANES - ts, newwsh_ref[...],
                                                         rolled[:, n_keep - LANES:n_keep])

    def online(s, vt):
        m_old = m_scr[:, 0:1]
        m_new = jnp.maximum(m_old, jnp.max(s, axis=1, keepdims=True))
        p = jnp.exp(s - m_new)
        alpha = jnp.exp(m_old - m_new)
        l_new = alpha * l_scr[:, 0:1] + jnp.sum(p, axis=1, keepdims=True)
        acc_scr[...] = alpha * acc_scr[...] + _dot_nt(p.astype(BF16), vt)
        m_scr[...] = jnp.broadcast_to(m_new, m_scr.shape)
        l_scr[...] = jnp.broadcast_to(l_new, l_scr.shape)

    sel01 = jnp.where(bias_scr[...] == 0.0, 1.0, 0.0).astype(BF16)
    bexp = _dot(sel01, e_ref[...])
    bias = expand_rows(jnp.where(bexp > 0.5, 0.0, -jnp.inf))
    kt = jnp.concatenate([pg[0:LANES, :] for pg in pages], axis=1).astype(BF16)
    vt = jnp.concatenate([pg[LANES:2 * LANES, :] for pg in pages], axis=1).astype(BF16)
    online(_dot(qrp, kt) + bias, vt)

    @pl.when(step == nsteps - 1)
    def _():
        lane_n = lax.broadcasted_iota(jnp.int32, (1, LANES), 1)
        bcol = expand_rows(bias_scr[:, ns_tot - 1:ns_tot])
        ok = (lane_n < ts) & (past_len + lane_n <= t_row)
        s_n = jnp.where(ok, _dot(qrp, newk_ref[2 * LANES:3 * LANES, :].astype(BF16)) + bcol, -jnp.inf)
        online(s_n, newk_ref[3 * LANES:4 * LANES, :].astype(BF16))
        o_s = acc_scr[...] * (1.0 / jnp.maximum(l_scr[:, 0:1], 1e-30))
        gc = gcol_ref[...]
        o_ref[...] = oc_scr[...] * gc[:, 0:1] + o_s * gc[:, 1:2] + ow_scr[...] * gc[:, 2:3]


def _nsa_sample(cache_t, page_table, li, qp, qrp, gcol, kct, vcr, cov_s, emat, win_t, newk_t, neww_t, newwsh_t, *,
                npg, past_len, ts):
    bs, n_pages = page_table.shape
    page = cache_t.shape[-1]
    nsteps = n_pages // npg
    nc = kct.shape[-1]
    ns_tot = -(-(past_len + ts) // SEL_BLOCK)
    nsp = cov_s.shape[1]
    n_keep = win_t.shape[-1]
    nrow = qp.shape[1]
    per_b = lambda a: pl.BlockSpec((None,) + a.shape[1:], lambda b, h, pt: (b,) + (0,) * (a.ndim - 1))

    def page_spec(k):
        return pl.BlockSpec((None, None, 2 * LANES, page), lambda b, h, pt: (li, pt[b, h * npg + k], 1, 0))

    grid_spec = pltpu.PrefetchScalarGridSpec(
        num_scalar_prefetch=1,
        grid=(bs, nsteps),
        in_specs=[per_b(qp), per_b(qrp), per_b(gcol), per_b(kct), per_b(vcr),
                  pl.BlockSpec(cov_s.shape, lambda b, h, pt: (0, 0)),
                  pl.BlockSpec((nsp, npg * page), lambda b, h, pt: (0, h))]
                 + [page_spec(k) for k in range(npg)]
                 + [pl.BlockSpec((None, None, 2 * LANES, n_keep), lambda b, h, pt: (li, b, 0, 0)),
                    per_b(newk_t), per_b(neww_t), per_b(newwsh_t)],
        out_specs=[pl.BlockSpec((None, nrow, LANES), lambda b, h, pt: (b, 0, 0)),
                   pl.BlockSpec((None, 2 * LANES, n_keep), lambda b, h, pt: (b, 0, 0))],
        scratch_shapes=[pltpu.VMEM((2 * 8, nsp), F32)] + [pltpu.VMEM((nrow, LANES), F32)] * 5,
    )
    return pl.pallas_call(
        functools.partial(_nsa_sample_kernel, npg=npg, nsteps=nsteps, nc=nc, ns_tot=ns_tot, past_len=past_len, ts=ts),
        out_shape=[jax.ShapeDtypeStruct((bs, nrow, LANES), F32), jax.ShapeDtypeStruct((bs, 2 * LANES, n_keep), F32)],
        grid_spec=grid_spec,
        compiler_params=_cparams(("parallel", "arbitrary")),
        name="nsa_sample",
    )(page_table, qp, qrp, gcol, kct, vcr, cov_s, emat, *([cache_t] * npg), win_t, newk_t, neww_t, newwsh_t)


def _pad_cols(w):
    cuts = np.cumsum((0,) + IN_SPLITS)
    parts = []
    for i, (n, p) in enumerate(zip(IN_SPLITS, IN_PADDED)):
        seg = w[..., cuts[i]:cuts[i] + n]
        if p != n:
            seg = jnp.pad(seg, [(0, 0)] * (w.ndim - 1) + [(0, p - n)])
        parts.append(seg)
    return jnp.concatenate(parts, axis=-1)


def _rope_tables(pos):
    half = HEAD_DIM // 2
    inv = 1.0 / (ROPE_THETA ** (jnp.arange(half, dtype=F32) * (2.0 / HEAD_DIM)))
    ang = pos.astype(F32)[:, None] * inv[None, :]
    cos = jnp.cos(ang)
    sin = jnp.sin(ang)
    cos_f = jnp.concatenate([cos, cos, cos, cos], axis=1)
    sin_f = jnp.concatenate([-sin, sin, -sin, sin], axis=1)
    return cos_f, sin_f


def _cover_t(seq):
    nc = seq // CMP_STRIDE
    ns = seq // SEL_BLOCK
    ci = np.arange(nc)[None, :]
    sj = np.arange(ns)[:, None]
    cov = ((ci * CMP_STRIDE <= sj * SEL_BLOCK + SEL_BLOCK - 1)
           & (ci * CMP_STRIDE + CMP_BLOCK - 1 >= sj * SEL_BLOCK) & (ci < nc - 1))
    return jnp.asarray(cov, dtype=BF16)


def _layer_weights(l, ln_gains, w_in_p, nsa_qk_norm, nsa_cmp_pos, nsa_cmp_w1, nsa_cmp_w2, gla_gate_w, gla_gate_b,
                   gla_norm, gm_ln, gm_ws, gm_b):
    eye2 = jnp.eye(NSA_KV_HEADS, dtype=F32)
    lw = {"ln0": ln_gains[l, 0][None], "ln1": ln_gains[l, 1][None], "ln2": ln_gains[l, 2][None], "w_in": w_in_p[l]}
    lw["gq"] = jnp.tile(nsa_qk_norm[l, 0], NSA_HEADS)[None]
    lw["gkc"] = jnp.tile(nsa_qk_norm[l, 1], NSA_KV_HEADS)[None]
    lw["gks"] = jnp.tile(nsa_qk_norm[l, 2], NSA_KV_HEADS)[None]
    lw["gkw"] = jnp.tile(nsa_qk_norm[l, 3], NSA_KV_HEADS)[None]
    seg = (np.arange(LANES)[:, None] // HEAD_DIM) == (np.arange(LANES)[None, :] // HEAD_DIM)
    lw["sm"] = jnp.asarray(seg * (1.0 / HEAD_DIM), dtype=BF16)
    lw["gla_gw"] = jnp.pad(gla_gate_w[l], ((0, LANES - GLA_GATE_RANK), (0, 0))).astype(BF16)
    lw["gla_gb"] = gla_gate_b[l][None]
    lw["gm_lng"] = gm_ln[l, 0][None]
    lw["gm_lnb"] = gm_ln[l, 1][None]
    pos_rows = []
    for c, nm in ((0, "k"), (1, "v")):
        w1 = nsa_cmp_w1[l, c].reshape(CMP_BLOCK, HEAD_DIM, CMP_HIDDEN)
        for half, tag in ((w1[:CMP_STRIDE], "a"), (w1[CMP_STRIDE:], "b")):
            wx = jnp.einsum("sdh,pg->spdgh", half, eye2)
            lw["cmp_w" + tag + nm] = wx.reshape(CMP_STRIDE * LANES, NSA_KV_HEADS * CMP_HIDDEN).astype(BF16)
        lw["cmp_w2" + nm] = jnp.einsum("hd,pg->phgd", nsa_cmp_w2[l, c], eye2).reshape(
            NSA_KV_HEADS * CMP_HIDDEN, LANES).astype(BF16)
        pe = nsa_cmp_pos[l, c]
        for half in (pe[:CMP_STRIDE], pe[CMP_STRIDE:]):
            pos_rows.append(jnp.broadcast_to(half[:, None, :], (CMP_STRIDE, NSA_KV_HEADS, HEAD_DIM)).reshape(-1))
    lw["cmp_pos"] = jnp.stack(pos_rows)
    w1ab = nsa_cmp_w1[l].reshape(2, 2, CMP_STRIDE, HEAD_DIM, CMP_HIDDEN)
    for c, nm in ((0, "k"), (1, "v")):
        lw["pc_w" + nm] = jnp.einsum("asdh,gy->sgdayh", w1ab[c], eye2).reshape(
            CMP_STRIDE * LANES, 2 * NSA_KV_HEADS * CMP_HIDDEN).astype(BF16)
    pb = jnp.einsum("ck,ckh->ch", nsa_cmp_pos[l].reshape(2, -1), nsa_cmp_w1[l], precision=lax.Precision.HIGHEST)
    lw["pc_bias"] = jnp.broadcast_to(pb[:, None, :], (2, NSA_KV_HEADS, CMP_HIDDEN)).reshape(1, -1)
    lw["pc_w2"] = jnp.einsum("chd,cx,gy->cghxyd", nsa_cmp_w2[l], eye2, eye2).reshape(
        2 * NSA_KV_HEADS * CMP_HIDDEN, 2 * LANES).astype(BF16)
    lw["gla_gn"] = jnp.tile(gla_norm[l], GLA_HEADS)[None]
    hq = np.arange(GLA_QK) // GLA_DK
    hv = np.arange(GLA_V) // GLA_DV
    lw["gla_bm"] = jnp.asarray(hq[:, None] == hv[None, :], dtype=BF16)
    lw["gla_bmask"] = jnp.asarray(hv[:, None] == hq[None, :], dtype=F32)
    lw["gm_ws"] = gm_ws[l]
    lw["gm_bias"] = jnp.repeat(gm_b[l].T, GM_CH, axis=1)
    return lw


def _page_perm(page):
    cpp = page // CMP_STRIDE
    r = np.arange(2 * page)
    s_, rem = r // (2 * cpp), r % (2 * cpp)
    t = (rem // cpp) * page + CMP_STRIDE * (rem % cpp) + s_
    m = np.zeros((2 * page, 2 * page), np.float32)
    m[r, t] = 1.0
    return jnp.asarray(m, dtype=BF16)


def _cover_sample(past_len, ts, nsp):
    t_tot = past_len + ts
    n_c = (t_tot - CMP_BLOCK) // CMP_STRIDE + 1
    nc = past_len // CMP_STRIDE
    n_s = -(-t_tot // SEL_BLOCK)
    ci = np.arange(nc)[:, None]
    sj = np.arange(nsp)[None, :]
    cov = ((ci * CMP_STRIDE <= sj * SEL_BLOCK + SEL_BLOCK - 1) & (ci * CMP_STRIDE + CMP_BLOCK - 1 >= sj * SEL_BLOCK)
           & (ci < n_c) & (sj < n_s))
    emat = (np.arange(past_len)[None, :] // SEL_BLOCK) == np.arange(nsp)[:, None]
    return jnp.asarray(cov, dtype=BF16), jnp.asarray(emat, dtype=BF16)


def _rows_gr8(a, ts):
    bs = a.shape[0]
    a = a.reshape(bs, ts, NSA_KV_HEADS, NSA_REP, a.shape[-1]).transpose(0, 2, 3, 1, 4)
    return jnp.pad(a, ((0, 0), (0, 0), (0, 0), (0, 8 - ts), (0, 0)))


def _state_to_t(s):
    b = s.shape[0]
    eye = jnp.eye(GLA_HEADS, dtype=s.dtype)
    return jnp.einsum("bhkv,hg->bhvgk", s, eye).reshape(b, GLA_V, GLA_QK)


def _state_from_t(st):
    b = st.shape[0]
    s5 = st.reshape(b, GLA_HEADS, GLA_DV, GLA_HEADS, GLA_DK)
    d = jnp.stack([s5[:, h, :, h, :] for h in range(GLA_HEADS)], axis=1)
    return jnp.swapaxes(d, 2, 3)


def kernel(x_prompt, x_sample, cache_kv, cache_win_kv, state_gla, page_table, ln_gains, ffn_w_gate_up, ffn_w_down,
           w_in, w_out, nsa_qk_norm, nsa_cmp_pos, nsa_cmp_w1, nsa_cmp_w2, gla_gate_w, gla_gate_b, gla_norm, gm_ln,
           gm_ws, gm_b):
    depth = w_in.shape[0]
    bp, tp, _ = x_prompt.shape
    bs, ts, _ = x_sample.shape
    n_pages = page_table.shape[1]
    page = cache_kv.shape[2]
    past_len = n_pages * page
    tq = 8
    mp, ms = bp * tp, bs * tq
    nc = tp // CMP_STRIDE

    w_gu = ffn_w_gate_up.astype(BF16).reshape(depth * 2, D_MODEL, 2 * D_FF)
    w_d = ffn_w_down.astype(BF16).reshape(depth * 2, D_FF, D_MODEL)
    w_in_p = _pad_cols(w_in).astype(BF16)
    w_o = w_out.astype(BF16)
    cos_p, sin_p = _rope_tables(jnp.arange(tp))
    cos_s, sin_s = _rope_tables(past_len + jnp.arange(ms) % tq)
    cov = _cover_t(tp)
    tm_p = 512 if mp % 512 == 0 else Q_BLOCK
    tm_f = 1024 if mp % 1024 == 0 else tm_p
    tm_i = 512 if tp % 512 == 0 else Q_BLOCK
    tf = 1408

    nrow = NSA_HEADS * 8
    n_keep_s = cache_win_kv.shape[2]
    nsp = -(-(past_len // SEL_BLOCK + 1) // LANES) * LANES
    cov_s, emat = _cover_sample(past_len, ts, nsp)
    perm = _page_perm(page)
    eye2 = jnp.eye(NSA_KV_HEADS, dtype=F32)
    npg_c = min(64, n_pages)
    npg_a = min(64, n_pages)
    cache_t = jnp.transpose(cache_kv, (0, 1, 3, 4, 5, 2)).reshape(depth, cache_kv.shape[1], 4 * LANES, page)
    win_t = jnp.transpose(cache_win_kv, (0, 1, 3, 4, 5, 2)).reshape(depth, bs, 2 * LANES, n_keep_s)

    xp = x_prompt.reshape(mp, D_MODEL)
    xs = jnp.pad(x_sample, ((0, 0), (0, tq - ts), (0, 0))).reshape(ms, D_MODEL)
    real_row = (jnp.arange(tq) < ts).astype(F32)[None, :, None]
    kv_p, win_p, gla_p, kv_s, win_s, gla_s, gmv_s = [], [], [], [], [], [], []
    eye_b = jnp.eye(GM_CHUNK // tq, dtype=F32)
    for l in range(depth):
        lw = _layer_weights(l, ln_gains, w_in_p, nsa_qk_norm, nsa_cmp_pos, nsa_cmp_w1, nsa_cmp_w2, gla_gate_w,
                            gla_gate_b, gla_norm, gm_ln, gm_ws, gm_b)
        xp = _ffn(xp, lw["ln0"], w_gu, w_d, 2 * l, tm=tm_f, tf=tf)
        (newkv, newwin, _, qg, kg, la, vg, rs, u, vn, qs, qrs, gt, ksel_r, vsel_t, kwin_r, vwin_t, kcmp,
         vcmp) = _inproj(xp, lw, cos_p, sin_p, tm=tm_i, attn_layout=True, batch=bp, seq=tp)
        zk = kcmp.reshape(bp, nc, CMP_STRIDE * LANES)
        zv = vcmp.reshape(bp, nc, CMP_STRIDE * LANES)
        kc, vct = _compress(zk, zv, lw)
        oa = _nsa_prompt(qs, qrs, gt, kc, vct, cov, ksel_r.reshape(bp, tp, LANES), vsel_t,
                         kwin_r.reshape(bp, tp, LANES), vwin_t, batch=bp, seq=tp)
        r3 = lambda a: a.reshape(bp, tp, a.shape[-1])
        ob, st = _gla(r3(qg), r3(kg), r3(la), r3(vg), r3(rs), jnp.zeros((bp, GLA_V, GLA_QK), F32), lw,
                      batch=bp, seq=tp, c=GLA_CHUNK, bpb=1)
        xp = _outproj(xp, oa, ob.reshape(mp, GLA_V), u, vn, lw["gm_ws"], lw["gm_bias"], w_o, l, tm=tm_p)
        xp = _ffn(xp, lw["ln2"], w_gu, w_d, 2 * l + 1, tm=tm_f, tf=tf)
        kv_p.append(newkv.reshape(bp, 4, NSA_KV_HEADS, HEAD_DIM, tp).transpose(0, 4, 1, 2, 3))
        n_keep = min(WINDOW, tp)
        win_p.append(newwin[:, :, tp - n_keep:].reshape(bp, 2, NSA_KV_HEADS, HEAD_DIM, n_keep).transpose(0, 4, 1, 2, 3))
        gla_p.append(_state_from_t(st))

        xs = _ffn(xs, lw["ln0"], w_gu, w_d, 2 * l, tm=ms, tf=tf)
        (newkv, newwin, gates, qg, kg, la, vg, rs, u, vn, qn, qr) = _inproj(
            xs, lw, cos_s, sin_s, tm=ms, attn_layout=False, batch=bs, seq=ts)
        hid = _pcompress(cache_t, page_table, l, perm, lw["pc_wk"], lw["pc_wv"], npg=npg_c)
        kct, vcr = _ctail(hid, lw["pc_bias"], lw["pc_w2"], lw["gkc"], lw["sm"])
        scale = np.float32(HEAD_DIM ** -0.5)
        to_pad = lambda q: jnp.einsum("bgrqd,gx->bgrqxd", _rows_gr8(q.reshape(bs, tq, NSA_HEADS, HEAD_DIM) * scale, tq),
                                      eye2).reshape(bs, nrow, LANES).astype(BF16)
        gcol = _rows_gr8(gates[:, :3 * NSA_HEADS].reshape(bs, tq, NSA_HEADS, 3), tq).reshape(bs, nrow, 3)
        gcol = jnp.pad(gcol, ((0, 0), (0, 0), (0, LANES - 3)))
        newk_t = jnp.pad(newkv.reshape(bs, tq, -1).transpose(0, 2, 1), ((0, 0), (0, 0), (0, LANES - tq)))
        neww_c = newwin.reshape(bs, tq, -1).transpose(0, 2, 1)[:, :, :ts]
        neww_t = jnp.pad(neww_c, ((0, 0), (0, 0), (0, LANES - ts)))
        newwsh_t = jnp.pad(neww_c, ((0, 0), (0, 0), (LANES - ts, 0)))
        o64, win_o = _nsa_sample(cache_t, page_table, l, to_pad(qn), to_pad(qr), gcol, kct, vcr, cov_s, emat, win_t,
                                 newk_t, neww_t, newwsh_t, npg=npg_a, past_len=past_len, ts=ts)
        o6 = o64.reshape(bs, NSA_KV_HEADS, NSA_REP, tq, NSA_KV_HEADS, HEAD_DIM)
        oa = jnp.stack([o6[:, g, :, :, g, :] for g in range(NSA_KV_HEADS)], axis=1)
        oa = (oa.transpose(0, 3, 1, 2, 4).reshape(bs, tq, NSA_Q) * real_row).reshape(ms, NSA_Q)
        r8 = lambda a: a.reshape(bs, tq, a.shape[-1])
        ob8, st_s = _gla(r8(qg), r8(kg), r8(la) * real_row, r8(vg), r8(rs), _state_to_t(state_gla[l]), lw,
                         batch=bs, seq=tq, c=tq, bpb=4 if bs % 4 == 0 else 1)
        ob = ob8.reshape(ms, GLA_V)
        ws_s = jnp.einsum("gts,bc->gbtcs", gm_ws[l][:, :tq, :tq], eye_b).reshape(GM_GROUPS, GM_CHUNK, GM_CHUNK)
        bias_s = jnp.tile(lw["gm_bias"][:tq], (GM_CHUNK // tq, 1))
        xs = _outproj(xs, oa, ob, u, vn, ws_s, bias_s, w_o, l, tm=GM_CHUNK)
        xs = _ffn(xs, lw["ln2"], w_gu, w_d, 2 * l + 1, tm=ms, tf=tf)
        kv_s.append(newkv.reshape(bs, tq, 4, NSA_KV_HEADS, HEAD_DIM)[:, :ts])
        win_s.append(win_o.reshape(bs, 2, NSA_KV_HEADS, HEAD_DIM, n_keep_s).transpose(0, 4, 1, 2, 3))
        gla_s.append(_state_from_t(st_s))
        gmv_s.append(vn.reshape(bs, tq, GM_GROUPS, GM_CH)[:, :ts])
    return (xp.reshape(bp, tp, D_MODEL), xs.reshape(bs, tq, D_MODEL)[:, :ts], jnp.stack(kv_p), jnp.stack(win_p),
            jnp.stack(gla_p), jnp.stack(kv_s), jnp.stack(win_s), jnp.stack(gla_s), jnp.stack(gmv_s))
```
